```python
import math
import jax, jax.numpy as jnp
from jax import lax
import numpy as np

D_MODEL = 1024
BATCH = 8
SEQ = 16384
DEPTH = 4

N_MIXERS = 2
EXPAND = 2
D_INNER = EXPAND * D_MODEL
CONF_KERNEL = 31
HEADDIM = 64
SSD_HEADS = D_INNER // HEADDIM
SSD_GROUPS = 4
D_STATE = 128
SSD_CONV = 5
CHUNK = 128
SSD_CONV_DIM = D_INNER + 2 * SSD_GROUPS * D_STATE
SSD_IN = D_INNER + SSD_CONV_DIM + 2 * SSD_HEADS
N_CONF_LAYERS = (DEPTH + 1) // 2
N_SSD_LAYERS = DEPTH // 2
EPS = 1e-5

kernel_name = "bidir_conformer_ssd_hybrid"


def rmsnorm(x, w):
    xf = x.astype(jnp.float32)
    y = xf * lax.rsqrt(jnp.mean(xf * xf, axis=-1, keepdims=True) + EPS)
    return (y * w.astype(jnp.float32)).astype(x.dtype)


def layernorm(x, w, b):
    xf = x.astype(jnp.float32)
    mu = jnp.mean(xf, axis=-1, keepdims=True)
    xc = xf - mu
    var = jnp.mean(xc * xc, axis=-1, keepdims=True)
    y = xc * lax.rsqrt(var + EPS)
    return (y * w.astype(jnp.float32) + b.astype(jnp.float32)).astype(x.dtype)


def dwconv_centred(x, w, b):
    width, ch = w.shape
    pad = (width - 1) // 2
    y = lax.conv_general_dilated(
        x, w[:, None, :].astype(x.dtype), window_strides=(1,), padding=[(pad, pad)],
        dimension_numbers=("NWC", "WIO", "NWC"), feature_group_count=ch)
    return y + b.astype(x.dtype)


def conformer_mixer(h, w_in, dw_w, dw_b, ln_w, ln_b, w_out):
    proj = h @ w_in
    v, g, z = jnp.split(proj, 3, axis=-1)
    u = v * jax.nn.sigmoid(g)
    u = dwconv_centred(u, dw_w, dw_b)
    u = layernorm(u, ln_w, ln_b)
    u = jax.nn.silu(u) * jax.nn.silu(z)
    return u @ w_out


def ssd_chunked(x, dt, A, B, C):
    in_dtype = x.dtype
    f32 = jnp.float32
    x, dt, A, B, C = (t.astype(f32) for t in (x, dt, A, B, C))
    b, s, H, P = x.shape
    G, N = B.shape[2], B.shape[3]
    hpg = H // G
    nc = s // CHUNK
    xdt = (x * dt[..., None]).reshape(b, nc, CHUNK, G, hpg, P)
    a = (dt * A).reshape(b, nc, CHUNK, G, hpg)
    a = jnp.transpose(a, (0, 3, 4, 1, 2))
    Bc = B.reshape(b, nc, CHUNK, G, N)
    Cc = C.reshape(b, nc, CHUNK, G, N)
    acs = jnp.cumsum(a, axis=-1)

    diff = acs[..., :, None] - acs[..., None, :]
    mask = jnp.tril(jnp.ones((CHUNK, CHUNK), dtype=bool))
    Lmat = jnp.where(mask, jnp.exp(jnp.where(mask, diff, 0.0)), 0.0)
    CB = jnp.einsum("bclgn,bcsgn->bgcls", Cc, Bc)
    M = CB[:, :, None] * Lmat
    y_diag = jnp.einsum("bghcls,bcsghp->bclghp", M, xdt)

    decay_states = jnp.exp(acs[..., -1:] - acs)
    states = jnp.einsum("bclgn,bghcl,bclghp->bcghpn", Bc, decay_states, xdt)

    chunk_decay = jnp.exp(acs[..., -1])

    def step(hstate, inp):
        dec, st = inp
        return hstate * dec[..., None, None] + st, hstate

    h0 = jnp.zeros((b, G, hpg, P, N), f32)
    _, state_in = lax.scan(step, h0, (jnp.moveaxis(chunk_decay, 3, 0), jnp.moveaxis(states, 1, 0)))
    state_in = jnp.moveaxis(state_in, 0, 1)

    y_off = jnp.einsum("bclgn,bcghpn,bghcl->bclghp", Cc, state_in, jnp.exp(acs))
    y = (y_diag + y_off).reshape(b, s, H, P)
    return y.astype(in_dtype)


def ssd_mixer(h, w_in, conv_w, conv_b, dt_bias, A_log, D_skip, norm_w, w_out):
    b, s, _ = h.shape
    proj = h @ w_in
    z = proj[..., :D_INNER]
    xBC = proj[..., D_INNER:D_INNER + SSD_CONV_DIM]
    dt_raw = proj[..., D_INNER + SSD_CONV_DIM:]
    xBC = jax.nn.silu(dwconv_centred(xBC, conv_w, conv_b))
    xs = xBC[..., :D_INNER].reshape(b, s, SSD_HEADS, HEADDIM)
    Bm = xBC[..., D_INNER:D_INNER + SSD_GROUPS * D_STATE].reshape(b, s, SSD_GROUPS, D_STATE)
    Cm = xBC[..., D_INNER + SSD_GROUPS * D_STATE:].reshape(b, s, SSD_GROUPS, D_STATE)
    dt = jax.nn.softplus(dt_raw.reshape(b, s, 2, SSD_HEADS).astype(jnp.float32)
                         + dt_bias.astype(jnp.float32))
    A = -jnp.exp(A_log.astype(jnp.float32))
    y_fwd = ssd_chunked(xs, dt[:, :, 0], A[0], Bm, Cm)
    flip = lambda t: jnp.flip(t, axis=1)
    y_bwd = flip(ssd_chunked(flip(xs), flip(dt[:, :, 1]), A[1], flip(Bm), flip(Cm)))
    y = y_fwd + y_bwd + xs * D_skip[:, None].astype(xs.dtype)
    y = y.reshape(b, s, D_INNER)
    y = rmsnorm(y * jax.nn.silu(z), norm_w)
    return y @ w_out


def _fwd_setup_inputs(seed: int = 0) -> dict:
    key = jax.random.key(seed)
    ks = jax.random.split(key, 20)
    f32 = jnp.float32
    NC, NS = N_CONF_LAYERS, N_SSD_LAYERS
    x = jax.random.normal(ks[0], (BATCH, SEQ, D_MODEL), f32)
    norm_w = 1.0 + 0.02 * jax.random.normal(ks[1], (DEPTH, D_MODEL), f32)
    final_norm_w = 1.0 + 0.02 * jax.random.normal(ks[2], (D_MODEL,), f32)
    cm_w_in = jax.random.normal(ks[3], (NC, D_MODEL, 3 * D_INNER), f32) * D_MODEL ** -0.5
    cm_dw_w = jax.random.normal(ks[4], (NC, CONF_KERNEL, D_INNER), f32) * CONF_KERNEL ** -0.5
    cm_dw_b = 0.02 * jax.random.normal(ks[5], (NC, D_INNER), f32)
    cm_ln_w = 1.0 + 0.02 * jax.random.normal(ks[6], (NC, D_INNER), f32)
    cm_ln_b = 0.02 * jax.random.normal(ks[7], (NC, D_INNER), f32)
    cm_w_out = jax.random.normal(ks[8], (NC, D_INNER, D_MODEL), f32) * D_INNER ** -0.5
    ssd_w_in = jax.random.normal(ks[9], (NS, D_MODEL, SSD_IN), f32) * D_MODEL ** -0.5
    ssd_conv_w = jax.random.normal(ks[10], (NS, SSD_CONV, SSD_CONV_DIM), f32) * SSD_CONV ** -0.5
    ssd_conv_b = 0.02 * jax.random.normal(ks[11], (NS, SSD_CONV_DIM), f32)
    u = jax.random.uniform(ks[12], (NS, 2, SSD_HEADS), f32)
    dt0 = jnp.exp(u * (math.log(0.1) - math.log(0.001)) + math.log(0.001))
    ssd_dt_bias = dt0 + jnp.log(-jnp.expm1(-dt0))
    ssd_A_log = jnp.log(jax.random.uniform(ks[13], (NS, 2, SSD_HEADS), f32, 1.0, 16.0))
    ssd_D = 1.0 + 0.1 * jax.random.normal(ks[14], (NS, SSD_HEADS), f32)
    ssd_norm_w = 1.0 + 0.02 * jax.random.normal(ks[15], (NS, D_INNER), f32)
    ssd_w_out = jax.random.normal(ks[16], (NS, D_INNER, D_MODEL), f32) * D_INNER ** -0.5
    return {"x": x, "norm_w": norm_w, "final_norm_w": final_norm_w,
            "cm_w_in": cm_w_in, "cm_dw_w": cm_dw_w, "cm_dw_b": cm_dw_b,
            "cm_ln_w": cm_ln_w, "cm_ln_b": cm_ln_b, "cm_w_out": cm_w_out,
            "ssd_w_in": ssd_w_in, "ssd_conv_w": ssd_conv_w, "ssd_conv_b": ssd_conv_b,
            "ssd_dt_bias": ssd_dt_bias, "ssd_A_log": ssd_A_log, "ssd_D": ssd_D,
            "ssd_norm_w": ssd_norm_w, "ssd_w_out": ssd_w_out}


def _fwd_reference(x, norm_w, final_norm_w, cm_w_in, cm_dw_w, cm_dw_b, cm_ln_w, cm_ln_b, cm_w_out,
              ssd_w_in, ssd_conv_w, ssd_conv_b, ssd_dt_bias, ssd_A_log, ssd_D, ssd_norm_w, ssd_w_out):
    h = x
    for i in range(DEPTH):
        hn = rmsnorm(h, norm_w[i])
        j = i // N_MIXERS
        if i % N_MIXERS == 0:
            out = conformer_mixer(hn, cm_w_in[j], cm_dw_w[j], cm_dw_b[j], cm_ln_w[j], cm_ln_b[j], cm_w_out[j])
        else:
            out = ssd_mixer(hn, ssd_w_in[j], ssd_conv_w[j], ssd_conv_b[j], ssd_dt_bias[j], ssd_A_log[j],
                            ssd_D[j], ssd_norm_w[j], ssd_w_out[j])
        h = h + out
    return rmsnorm(h, final_norm_w)


import jax as _jax
import jax.numpy as _jnp

TWIN_FORMAT = 'train_step'
FWD_PARAMS = ['x', 'norm_w', 'final_norm_w', 'cm_w_in', 'cm_dw_w', 'cm_dw_b', 'cm_ln_w', 'cm_ln_b', 'cm_w_out', 'ssd_w_in', 'ssd_conv_w', 'ssd_conv_b', 'ssd_dt_bias', 'ssd_A_log', 'ssd_D', 'ssd_norm_w', 'ssd_w_out']
TWIN_WEIGHTS = ['norm_w', 'final_norm_w', 'cm_w_in', 'cm_dw_w', 'cm_dw_b', 'cm_ln_w', 'cm_ln_b', 'cm_w_out', 'ssd_w_in', 'ssd_conv_w', 'ssd_conv_b', 'ssd_dt_bias', 'ssd_A_log', 'ssd_D', 'ssd_norm_w', 'ssd_w_out']
TWIN_DIFF_INPUT = 'x'
TWIN_INPUTS = ['x', 'norm_w', 'final_norm_w', 'cm_w_in', 'cm_dw_w', 'cm_dw_b', 'cm_ln_w', 'cm_ln_b', 'cm_w_out', 'ssd_w_in', 'ssd_conv_w', 'ssd_conv_b', 'ssd_dt_bias', 'ssd_A_log', 'ssd_D', 'ssd_norm_w', 'ssd_w_out', 'loss_target', 'm_norm_w', 'm_final_norm_w', 'm_cm_w_in', 'm_cm_dw_w', 'm_cm_dw_b', 'm_cm_ln_w', 'm_cm_ln_b', 'm_cm_w_out', 'm_ssd_w_in', 'm_ssd_conv_w', 'm_ssd_conv_b', 'm_ssd_dt_bias', 'm_ssd_A_log', 'm_ssd_D', 'm_ssd_norm_w', 'm_ssd_w_out', 'v_norm_w', 'v_final_norm_w', 'v_cm_w_in', 'v_cm_dw_w', 'v_cm_dw_b', 'v_cm_ln_w', 'v_cm_ln_b', 'v_cm_w_out', 'v_ssd_w_in', 'v_ssd_conv_w', 'v_ssd_conv_b', 'v_ssd_dt_bias', 'v_ssd_A_log', 'v_ssd_D', 'v_ssd_norm_w', 'v_ssd_w_out']
TWIN_OUTPUTS = ['loss', 'grad_x', 'grad_norm_w', 'grad_final_norm_w', 'grad_cm_w_in', 'grad_cm_dw_w', 'grad_cm_dw_b', 'grad_cm_ln_w', 'grad_cm_ln_b', 'grad_cm_w_out', 'grad_ssd_w_in', 'grad_ssd_conv_w', 'grad_ssd_conv_b', 'grad_ssd_dt_bias', 'grad_ssd_A_log', 'grad_ssd_D', 'grad_ssd_norm_w', 'grad_ssd_w_out', 'delta_norm_w', 'delta_final_norm_w', 'delta_cm_w_in', 'delta_cm_dw_w', 'delta_cm_dw_b', 'delta_cm_ln_w', 'delta_cm_ln_b', 'delta_cm_w_out', 'delta_ssd_w_in', 'delta_ssd_conv_w', 'delta_ssd_conv_b', 'delta_ssd_dt_bias', 'delta_ssd_A_log', 'delta_ssd_D', 'delta_ssd_norm_w', 'delta_ssd_w_out', 'new_m_norm_w', 'new_m_final_norm_w', 'new_m_cm_w_in', 'new_m_cm_dw_w', 'new_m_cm_dw_b', 'new_m_cm_ln_w', 'new_m_cm_ln_b', 'new_m_cm_w_out', 'new_m_ssd_w_in', 'new_m_ssd_conv_w', 'new_m_ssd_conv_b', 'new_m_ssd_dt_bias', 'new_m_ssd_A_log', 'new_m_ssd_D', 'new_m_ssd_norm_w', 'new_m_ssd_w_out', 'new_v_norm_w', 'new_v_final_norm_w', 'new_v_cm_w_in', 'new_v_cm_dw_w', 'new_v_cm_dw_b', 'new_v_cm_ln_w', 'new_v_cm_ln_b', 'new_v_cm_w_out', 'new_v_ssd_w_in', 'new_v_ssd_conv_w', 'new_v_ssd_conv_b', 'new_v_ssd_dt_bias', 'new_v_ssd_A_log', 'new_v_ssd_D', 'new_v_ssd_norm_w', 'new_v_ssd_w_out']
TWIN_LEAF_KINDS = {'loss': 'loss', 'grad_x': 'grad_x', 'grad_norm_w': 'grad_w', 'grad_final_norm_w': 'grad_w', 'grad_cm_w_in': 'grad_w', 'grad_cm_dw_w': 'grad_w', 'grad_cm_dw_b': 'grad_w', 'grad_cm_ln_w': 'grad_w', 'grad_cm_ln_b': 'grad_w', 'grad_cm_w_out': 'grad_w', 'grad_ssd_w_in': 'grad_w', 'grad_ssd_conv_w': 'grad_w', 'grad_ssd_conv_b': 'grad_w', 'grad_ssd_dt_bias': 'grad_w', 'grad_ssd_A_log': 'grad_w', 'grad_ssd_D': 'grad_w', 'grad_ssd_norm_w': 'grad_w', 'grad_ssd_w_out': 'grad_w', 'delta_norm_w': 'delta_w', 'delta_final_norm_w': 'delta_w', 'delta_cm_w_in': 'delta_w', 'delta_cm_dw_w': 'delta_w', 'delta_cm_dw_b': 'delta_w', 'delta_cm_ln_w': 'delta_w', 'delta_cm_ln_b': 'delta_w', 'delta_cm_w_out': 'delta_w', 'delta_ssd_w_in': 'delta_w', 'delta_ssd_conv_w': 'delta_w', 'delta_ssd_conv_b': 'delta_w', 'delta_ssd_dt_bias': 'delta_w', 'delta_ssd_A_log': 'delta_w', 'delta_ssd_D': 'delta_w', 'delta_ssd_norm_w': 'delta_w', 'delta_ssd_w_out': 'delta_w', 'new_m_norm_w': 'new_m', 'new_m_final_norm_w': 'new_m', 'new_m_cm_w_in': 'new_m', 'new_m_cm_dw_w': 'new_m', 'new_m_cm_dw_b': 'new_m', 'new_m_cm_ln_w': 'new_m', 'new_m_cm_ln_b': 'new_m', 'new_m_cm_w_out': 'new_m', 'new_m_ssd_w_in': 'new_m', 'new_m_ssd_conv_w': 'new_m', 'new_m_ssd_conv_b': 'new_m', 'new_m_ssd_dt_bias': 'new_m', 'new_m_ssd_A_log': 'new_m', 'new_m_ssd_D': 'new_m', 'new_m_ssd_norm_w': 'new_m', 'new_m_ssd_w_out': 'new_m', 'new_v_norm_w': 'new_v', 'new_v_final_norm_w': 'new_v', 'new_v_cm_w_in': 'new_v', 'new_v_cm_dw_w': 'new_v', 'new_v_cm_dw_b': 'new_v', 'new_v_cm_ln_w': 'new_v', 'new_v_cm_ln_b': 'new_v', 'new_v_cm_w_out': 'new_v', 'new_v_ssd_w_in': 'new_v', 'new_v_ssd_conv_w': 'new_v', 'new_v_ssd_conv_b': 'new_v', 'new_v_ssd_dt_bias': 'new_v', 'new_v_ssd_A_log': 'new_v', 'new_v_ssd_D': 'new_v', 'new_v_ssd_norm_w': 'new_v', 'new_v_ssd_w_out': 'new_v'}


def _forward(args):
    return _fwd_reference(*[args[k] for k in FWD_PARAMS])


def _output_shape():
    def fwd():
        inp = _fwd_setup_inputs(0)
        return _fwd_reference(*[inp[k] for k in FWD_PARAMS])
    out = _jax.eval_shape(fwd)
    return out.shape, out.dtype

N_MICROBATCH = 1
ADAM_LR = 0.001
ADAM_B1 = 0.9
ADAM_B2 = 0.999
ADAM_EPS = 1e-08
ADAM_WD = 0.01
ADAM_STEP = 10
PER_EXAMPLE_BATCH_AXIS = {'x': 0, 'loss_target': 0}
SHARED_INPUTS = []
_WEIGHT_DTYPES = {'norm_w': _jnp.float32, 'final_norm_w': _jnp.float32, 'cm_w_in': _jnp.float32, 'cm_dw_w': _jnp.float32, 'cm_dw_b': _jnp.float32, 'cm_ln_w': _jnp.float32, 'cm_ln_b': _jnp.float32, 'cm_w_out': _jnp.float32, 'ssd_w_in': _jnp.float32, 'ssd_conv_w': _jnp.float32, 'ssd_conv_b': _jnp.float32, 'ssd_dt_bias': _jnp.float32, 'ssd_A_log': _jnp.float32, 'ssd_D': _jnp.float32, 'ssd_norm_w': _jnp.float32, 'ssd_w_out': _jnp.float32}
MOMENT_SCALE = {'norm_w': 3.162429e-01, 'final_norm_w': 1.281228e+02, 'cm_w_in': 9.238122e-02, 'cm_dw_w': 1.086039e-01, 'cm_dw_b': 2.086282e-01, 'cm_ln_w': 1.287215e-01, 'cm_ln_b': 1.106745e-01, 'cm_w_out': 1.496563e-01, 'ssd_w_in': 1.704549e-01, 'ssd_conv_w': 1.552605e-01, 'ssd_conv_b': 2.409217e-01, 'ssd_dt_bias': 5.213085e-01, 'ssd_A_log': 4.067954e-01, 'ssd_D': 1.284114e+00, 'ssd_norm_w': 1.785866e-01, 'ssd_w_out': 2.578564e-01}


def _to_microbatches(a, axis):
    t = _jnp.moveaxis(a, axis, 0)
    t = t.reshape((N_MICROBATCH, t.shape[0] // N_MICROBATCH) + t.shape[1:])
    return _jnp.moveaxis(t, 1, axis + 1)


def setup_inputs(seed: int = 0) -> dict:
    inp = _fwd_setup_inputs(seed)
    key = _jax.random.fold_in(_jax.random.key(seed), 7919)
    shape, _ = _output_shape()
    out = dict(inp)
    out["loss_target"] = _jax.random.normal(_jax.random.fold_in(key, 0), shape, _jnp.float32)
    for i, name in enumerate(TWIN_WEIGHTS):
        w = inp[name].astype(_jnp.float32)
        if MOMENT_SCALE is None:
            s = _jnp.sqrt(_jnp.mean(_jnp.square(w)) + 1e-30)
        else:
            s = MOMENT_SCALE[name]
        km, kv = _jax.random.split(_jax.random.fold_in(key, i + 1))
        out[name] = w
        out["m_" + name] = s * _jax.random.normal(km, w.shape, _jnp.float32)
        out["v_" + name] = (s * s) * _jax.random.uniform(kv, w.shape, _jnp.float32, 0.5, 1.5)
    if N_MICROBATCH > 1:
        for name, axis in PER_EXAMPLE_BATCH_AXIS.items():
            out[name] = _to_microbatches(out[name], axis)
    return {'x': out['x'], 'norm_w': out['norm_w'], 'final_norm_w': out['final_norm_w'], 'cm_w_in': out['cm_w_in'], 'cm_dw_w': out['cm_dw_w'], 'cm_dw_b': out['cm_dw_b'], 'cm_ln_w': out['cm_ln_w'], 'cm_ln_b': out['cm_ln_b'], 'cm_w_out': out['cm_w_out'], 'ssd_w_in': out['ssd_w_in'], 'ssd_conv_w': out['ssd_conv_w'], 'ssd_conv_b': out['ssd_conv_b'], 'ssd_dt_bias': out['ssd_dt_bias'], 'ssd_A_log': out['ssd_A_log'], 'ssd_D': out['ssd_D'], 'ssd_norm_w': out['ssd_norm_w'], 'ssd_w_out': out['ssd_w_out'], 'loss_target': out['loss_target'], 'm_norm_w': out['m_norm_w'], 'm_final_norm_w': out['m_final_norm_w'], 'm_cm_w_in': out['m_cm_w_in'], 'm_cm_dw_w': out['m_cm_dw_w'], 'm_cm_dw_b': out['m_cm_dw_b'], 'm_cm_ln_w': out['m_cm_ln_w'], 'm_cm_ln_b': out['m_cm_ln_b'], 'm_cm_w_out': out['m_cm_w_out'], 'm_ssd_w_in': out['m_ssd_w_in'], 'm_ssd_conv_w': out['m_ssd_conv_w'], 'm_ssd_conv_b': out['m_ssd_conv_b'], 'm_ssd_dt_bias': out['m_ssd_dt_bias'], 'm_ssd_A_log': out['m_ssd_A_log'], 'm_ssd_D': out['m_ssd_D'], 'm_ssd_norm_w': out['m_ssd_norm_w'], 'm_ssd_w_out': out['m_ssd_w_out'], 'v_norm_w': out['v_norm_w'], 'v_final_norm_w': out['v_final_norm_w'], 'v_cm_w_in': out['v_cm_w_in'], 'v_cm_dw_w': out['v_cm_dw_w'], 'v_cm_dw_b': out['v_cm_dw_b'], 'v_cm_ln_w': out['v_cm_ln_w'], 'v_cm_ln_b': out['v_cm_ln_b'], 'v_cm_w_out': out['v_cm_w_out'], 'v_ssd_w_in': out['v_ssd_w_in'], 'v_ssd_conv_w': out['v_ssd_conv_w'], 'v_ssd_conv_b': out['v_ssd_conv_b'], 'v_ssd_dt_bias': out['v_ssd_dt_bias'], 'v_ssd_A_log': out['v_ssd_A_log'], 'v_ssd_D': out['v_ssd_D'], 'v_ssd_norm_w': out['v_ssd_norm_w'], 'v_ssd_w_out': out['v_ssd_w_out']}


def _loss(weights, diff, rest, loss_target):
    with _jax.named_scope("forward"):
        args = {**rest, TWIN_DIFF_INPUT: diff, **{k: w.astype(_WEIGHT_DTYPES[k]) for k, w in weights.items()}}
        y = _forward(args)
    with _jax.named_scope("loss_head"):
        err = _jnp.square(y.astype(_jnp.float32) - loss_target)
        return 0.5 * _jnp.sum(_jnp.mean(err, axis=-1)) if err.ndim else 0.5 * err


def _adamw(w, g, m, v):
    m = ADAM_B1 * m + (1.0 - ADAM_B1) * g
    v = ADAM_B2 * v + (1.0 - ADAM_B2) * _jnp.square(g)
    m_hat = m / (1.0 - ADAM_B1 ** ADAM_STEP)
    v_hat = v / (1.0 - ADAM_B2 ** ADAM_STEP)
    delta = -ADAM_LR * (m_hat / (_jnp.sqrt(v_hat) + ADAM_EPS) + ADAM_WD * w)
    return delta, m, v


def reference(x, norm_w, final_norm_w, cm_w_in, cm_dw_w, cm_dw_b, cm_ln_w, cm_ln_b, cm_w_out, ssd_w_in, ssd_conv_w, ssd_conv_b, ssd_dt_bias, ssd_A_log, ssd_D, ssd_norm_w, ssd_w_out, loss_target, m_norm_w, m_final_norm_w, m_cm_w_in, m_cm_dw_w, m_cm_dw_b, m_cm_ln_w, m_cm_ln_b, m_cm_w_out, m_ssd_w_in, m_ssd_conv_w, m_ssd_conv_b, m_ssd_dt_bias, m_ssd_A_log, m_ssd_D, m_ssd_norm_w, m_ssd_w_out, v_norm_w, v_final_norm_w, v_cm_w_in, v_cm_dw_w, v_cm_dw_b, v_cm_ln_w, v_cm_ln_b, v_cm_w_out, v_ssd_w_in, v_ssd_conv_w, v_ssd_conv_b, v_ssd_dt_bias, v_ssd_A_log, v_ssd_D, v_ssd_norm_w, v_ssd_w_out):
    given = dict(x=x, norm_w=norm_w, final_norm_w=final_norm_w, cm_w_in=cm_w_in, cm_dw_w=cm_dw_w, cm_dw_b=cm_dw_b, cm_ln_w=cm_ln_w, cm_ln_b=cm_ln_b, cm_w_out=cm_w_out, ssd_w_in=ssd_w_in, ssd_conv_w=ssd_conv_w, ssd_conv_b=ssd_conv_b, ssd_dt_bias=ssd_dt_bias, ssd_A_log=ssd_A_log, ssd_D=ssd_D, ssd_norm_w=ssd_norm_w, ssd_w_out=ssd_w_out, loss_target=loss_target, m_norm_w=m_norm_w, m_final_norm_w=m_final_norm_w, m_cm_w_in=m_cm_w_in, m_cm_dw_w=m_cm_dw_w, m_cm_dw_b=m_cm_dw_b, m_cm_ln_w=m_cm_ln_w, m_cm_ln_b=m_cm_ln_b, m_cm_w_out=m_cm_w_out, m_ssd_w_in=m_ssd_w_in, m_ssd_conv_w=m_ssd_conv_w, m_ssd_conv_b=m_ssd_conv_b, m_ssd_dt_bias=m_ssd_dt_bias, m_ssd_A_log=m_ssd_A_log, m_ssd_D=m_ssd_D, m_ssd_norm_w=m_ssd_norm_w, m_ssd_w_out=m_ssd_w_out, v_norm_w=v_norm_w, v_final_norm_w=v_final_norm_w, v_cm_w_in=v_cm_w_in, v_cm_dw_w=v_cm_dw_w, v_cm_dw_b=v_cm_dw_b, v_cm_ln_w=v_cm_ln_w, v_cm_ln_b=v_cm_ln_b, v_cm_w_out=v_cm_w_out, v_ssd_w_in=v_ssd_w_in, v_ssd_conv_w=v_ssd_conv_w, v_ssd_conv_b=v_ssd_conv_b, v_ssd_dt_bias=v_ssd_dt_bias, v_ssd_A_log=v_ssd_A_log, v_ssd_D=v_ssd_D, v_ssd_norm_w=v_ssd_norm_w, v_ssd_w_out=v_ssd_w_out)
    weights = {n: given[n] for n in TWIN_WEIGHTS}
    shared = {n: given[n] for n in SHARED_INPUTS}
    per_example = {n: given[n] for n in ['x']}
    grad_fn = _jax.value_and_grad(_loss, argnums=(0, 1))

    def one_microbatch(ex, loss_target):
        ex = dict(ex)
        diff = ex.pop(TWIN_DIFF_INPUT)
        return grad_fn(weights, diff, {**shared, **ex}, loss_target)

    if N_MICROBATCH == 1:
        loss, (grad_w, grad_x) = one_microbatch(per_example, given["loss_target"])
    else:
        def body(carry, xs):
            loss_sum, grad_sum = carry
            l_k, (gw_k, gx_k) = one_microbatch(xs[0], xs[1])
            with _jax.named_scope("update"):
                return (loss_sum + l_k, _jax.tree.map(_jnp.add, grad_sum, gw_k)), gx_k

        init = (_jnp.zeros((), _jnp.float32), _jax.tree.map(_jnp.zeros_like, weights))
        (loss, grad_w), grad_x = _jax.lax.scan(body, init, (per_example, given["loss_target"]))
    with _jax.named_scope("update"):
        delta_w, new_m, new_v = {}, {}, {}
        for n in TWIN_WEIGHTS:
            delta_w[n], new_m[n], new_v[n] = _adamw(weights[n], grad_w[n], given["m_" + n], given["v_" + n])
    return (loss, grad_x, *[grad_w[n] for n in TWIN_WEIGHTS], *[delta_w[n] for n in TWIN_WEIGHTS],
            *[new_m[n] for n in TWIN_WEIGHTS], *[new_v[n] for n in TWIN_WEIGHTS])
```

```python
import functools
import math

import jax
import jax.numpy as jnp
from jax import lax
from jax.experimental import pallas as pl
from jax.experimental.pallas import tpu as pltpu

F32, BF16 = jnp.float32, jnp.bfloat16
SDS = jax.ShapeDtypeStruct

D_MODEL = 1024
D_INNER = 2048
CONF_KERNEL = 31
HEADDIM = 64
SSD_HEADS = 32
SSD_GROUPS = 4
HPG = SSD_HEADS // SSD_GROUPS
D_STATE = 128
SSD_CONV = 5
CHUNK = 128
GN = SSD_GROUPS * D_STATE
SSD_CONV_DIM = D_INNER + 2 * GN
SSD_IN = D_INNER + SSD_CONV_DIM + 2 * SSD_HEADS
SSD_IN_PAD = SSD_IN + 64
EPS = 1e-5
N_DEV = 8

ADAM_LR, ADAM_B1, ADAM_B2, ADAM_EPS, ADAM_WD, ADAM_STEP = 0.001, 0.9, 0.999, 1e-08, 0.01, 10

VMEM_LIMIT = 56 * 1024 * 1024
ROW_TILE = 512
GATE_TILE = 256
CONV_ROWS = 512
CONV_LANES = 256
CONV_HALO = 16
CONV_RB = 32
NEG = -1e30


def _cparams(sem):
    return pltpu.CompilerParams(dimension_semantics=sem, vmem_limit_bytes=VMEM_LIMIT)


def _resident(shape):
    nd = len(shape)
    return pl.BlockSpec(shape, lambda *_: (0,) * nd, pipeline_mode=pl.Buffered(1))


def _sig(x):
    return 1.0 / (1.0 + jnp.exp(-x))


def _silu(x):
    return x * _sig(x)


def _dsilu(x):
    s = _sig(x)
    return s * (1.0 + x * (1.0 - s))


def _softplus(x):
    return jnp.maximum(x, 0.0) + jnp.log(1.0 + jnp.exp(-jnp.abs(x)))


def _col_chunks(n, width):
    out, c = [], 0
    while c < n:
        out.append((c, min(c + width, n)))
        c += width
    return out


def _rowsum8(v):
    acc = v[0:8]
    for j in range(1, v.shape[0] // 8):
        acc = acc + v[8 * j:8 * j + 8]
    return acc


def _inproj_fwd(h, nw, w, *, tail, name):
    S, D = h.shape
    N = w.shape[1]
    tm = min(ROW_TILE, S)
    chunks = _col_chunks(N, 1024)

    def body(h_ref, nw_ref, w_ref, o_ref, *rest):
        x = h_ref[...]
        r = lax.rsqrt(jnp.mean(x * x, axis=-1, keepdims=True) + EPS)
        hn = (x * r * nw_ref[...]).astype(BF16)
        for c0, c1 in chunks:
            acc = jnp.dot(hn, w_ref[:, c0:c1], preferred_element_type=F32)
            o_ref[:, c0:c1] = acc.astype(BF16)
            if tail and c1 == N:
                rest[0][...] = acc[:, acc.shape[1] - 128:]

    out_shape = [SDS((S, N), BF16)]
    out_specs = [pl.BlockSpec((tm, N), lambda i: (i, 0))]
    if tail:
        out_shape.append(SDS((S, 128), F32))
        out_specs.append(pl.BlockSpec((tm, 128), lambda i: (i, 0)))
    res = pl.pallas_call(
        body, grid=(S // tm,), out_shape=out_shape,
        in_specs=[pl.BlockSpec((tm, D), lambda i: (i, 0)), _resident((1, D)), _resident((D, N))],
        out_specs=out_specs, compiler_params=_cparams(("parallel",)), name=name)(h, nw, w)
    return res if tail else res[0]


def _tap_plan(offsets):
    rs = sorted({s % 8 for s in offsets})
    return rs, [(k, rs.index(s % 8), s // 8) for k, s in enumerate(offsets)]


def _halo_specs(S, rows, lanes, col0):
    per, last = rows // CONV_HALO, S // CONV_HALO - 1
    cb = col0 // lanes
    return [pl.BlockSpec((rows, lanes), lambda j, i: (i, cb + j)),
            pl.BlockSpec((CONV_HALO, lanes), lambda j, i: (jnp.maximum(i * per - 1, 0), cb + j)),
            pl.BlockSpec((CONV_HALO, lanes), lambda j, i: (jnp.minimum((i + 1) * per, last), cb + j))]


def _fill_window(win, prev, main, nxt, i, nt, rows):
    hb = CONV_HALO
    win[0:hb, :] = jnp.where(i > 0, prev, 0.0)
    win[hb:hb + rows, :] = main
    win[hb + rows:hb + rows + hb, :] = jnp.where(i < nt - 1, nxt, 0.0)


def _fill_shifts(win, sh, rs, n_u):
    for idx, r in enumerate(rs):
        sh[idx] = win[pl.ds(r, n_u), :]


def _dwconv_fwd(src, w, b, *, col0, glu_col0, C, name):
    S = src.shape[0]
    K = w.shape[0]
    pad = (K - 1) // 2
    rows, lanes, hb, rb = min(CONV_ROWS, S), CONV_LANES, CONV_HALO, CONV_RB
    nt = S // rows
    rs, taps = _tap_plan([k - pad + hb for k in range(K)])
    n_u = rows + 2 * hb - 8
    glu = glu_col0 is not None

    def body(*refs):
        if glu:
            vm, vp, vn, gm, gp, gn, w_ref, b_ref, o_ref, win, sh = refs
        else:
            vm, vp, vn, w_ref, b_ref, o_ref, win, sh = refs
        i = pl.program_id(1)

        def act(k):
            v = (vm, vp, vn)[k][...].astype(F32)
            return v * _sig((gm, gp, gn)[k][...].astype(F32)) if glu else v

        _fill_window(win, act(1), act(0), act(2), i, nt, rows)
        _fill_shifts(win, sh, rs, n_u)

        def rowblk(t, carry):
            r0 = pl.multiple_of(t * rb, rb)
            acc = jnp.zeros((rb, lanes), F32) + b_ref[...]
            for k, ri, q in taps:
                acc = acc + w_ref[k:k + 1, :] * sh[ri, pl.ds(pl.multiple_of(r0 + 8 * q, 8), rb), :]
            o_ref[pl.ds(r0, rb), :] = acc.astype(BF16)
            return carry

        lax.fori_loop(0, rows // rb, rowblk, 0)

    in_specs = _halo_specs(S, rows, lanes, col0)
    args = [src, src, src]
    if glu:
        in_specs += _halo_specs(S, rows, lanes, glu_col0)
        args += [src, src, src]
    in_specs += [pl.BlockSpec((K, lanes), lambda j, i: (0, j)), pl.BlockSpec((1, lanes), lambda j, i: (0, j))]
    return pl.pallas_call(
        body, grid=(C // lanes, nt), out_shape=SDS((S, C), BF16), in_specs=in_specs,
        out_specs=pl.BlockSpec((rows, lanes), lambda j, i: (i, j)),
        scratch_shapes=[pltpu.VMEM((rows + 2 * hb, lanes), F32), pltpu.VMEM((len(rs), n_u, lanes), F32)],
        compiler_params=_cparams(("parallel", "parallel")), name=name)(*args, w, b)


def _dwconv_bwd(dsrcs, src, w, *, col0, glu_col0, C, name):
    S = src.shape[0]
    K = w.shape[0]
    pad = (K - 1) // 2
    rows, lanes, hb, rb = min(CONV_ROWS, S), CONV_LANES, CONV_HALO, CONV_RB
    nt = S // rows
    rs_u, taps_u = _tap_plan([k - pad + hb for k in range(K)])
    rs_d, taps_d = _tap_plan([hb + pad - k for k in range(K)])
    n_u = rows + 2 * hb - 8
    glu = glu_col0 is not None
    nd = len(dsrcs)

    def body(*refs):
        d_refs, refs = refs[:3 * nd], refs[3 * nd:]
        if glu:
            vm, vp, vn, gm, gp, gn, w_ref = refs[:7]
            dv_ref, dg_ref, dw_ref, db_ref, dwin, dsh, uwin, ush, accw, accb = refs[7:]
        else:
            vm, vp, vn, w_ref = refs[:4]
            du_ref, dw_ref, db_ref, dwin, dsh, uwin, ush, accw, accb = refs[4:]
        i = pl.program_id(1)

        def dpiece(k):
            v = d_refs[k][...].astype(F32)
            for e in range(1, nd):
                v = v + d_refs[3 * e + k][...].astype(F32)
            return v

        def act(k):
            v = (vm, vp, vn)[k][...].astype(F32)
            return v * _sig((gm, gp, gn)[k][...].astype(F32)) if glu else v

        _fill_window(dwin, dpiece(1), dpiece(0), dpiece(2), i, nt, rows)
        _fill_shifts(dwin, dsh, rs_d, n_u)
        _fill_window(uwin, act(1), act(0), act(2), i, nt, rows)
        _fill_shifts(uwin, ush, rs_u, n_u)

        @pl.when(i == 0)
        def _():
            accw[...] = jnp.zeros_like(accw)
            accb[...] = jnp.zeros_like(accb)

        def rowblk(t, carry):
            r0 = pl.multiple_of(t * rb, rb)
            du = jnp.zeros((rb, lanes), F32)
            for k, ri, q in taps_d:
                du = du + w_ref[k:k + 1, :] * dsh[ri, pl.ds(pl.multiple_of(r0 + 8 * q, 8), rb), :]
            if glu:
                v = vm[pl.ds(r0, rb), :].astype(F32)
                s = _sig(gm[pl.ds(r0, rb), :].astype(F32))
                dv_ref[pl.ds(r0, rb), :] = (du * s).astype(BF16)
                dg_ref[pl.ds(r0, rb), :] = (du * v * s * (1.0 - s)).astype(BF16)
            else:
                du_ref[pl.ds(r0, rb), :] = du.astype(BF16)
            dmain = dwin[pl.ds(pl.multiple_of(r0 + hb, 8), rb), :]
            accb[...] += _rowsum8(dmain)
            for k, ri, q in taps_u:
                accw[k] += _rowsum8(dmain * ush[ri, pl.ds(pl.multiple_of(r0 + 8 * q, 8), rb), :])
            return carry

        lax.fori_loop(0, rows // rb, rowblk, 0)

        @pl.when(i == nt - 1)
        def _():
            for k in range(K):
                dw_ref[k:k + 1, :] = jnp.sum(accw[k], axis=0, keepdims=True)
            db_ref[...] = jnp.sum(accb[...], axis=0, keepdims=True)

    in_specs, args = [], []
    for d in dsrcs:
        in_specs += _halo_specs(S, rows, lanes, 0)
        args += [d, d, d]
    in_specs += _halo_specs(S, rows, lanes, col0)
    args += [src, src, src]
    if glu:
        in_specs += _halo_specs(S, rows, lanes, glu_col0)
        args += [src, src, src]
    in_specs += [pl.BlockSpec((K, lanes), lambda j, i: (0, j))]
    tile = pl.BlockSpec((rows, lanes), lambda j, i: (i, j))
    n_act = 2 if glu else 1
    out_shape = [SDS((S, C), BF16)] * n_act + [SDS((K, C), F32), SDS((1, C), F32)]
    out_specs = [tile] * n_act + [pl.BlockSpec((K, lanes), lambda j, i: (0, j)),
                                  pl.BlockSpec((1, lanes), lambda j, i: (0, j))]
    return pl.pallas_call(
        body, grid=(C // lanes, nt), out_shape=out_shape, in_specs=in_specs, out_specs=out_specs,
        scratch_shapes=[pltpu.VMEM((rows + 2 * hb, lanes), F32), pltpu.VMEM((len(rs_d), n_u, lanes), F32),
                        pltpu.VMEM((rows + 2 * hb, lanes), F32), pltpu.VMEM((len(rs_u), n_u, lanes), F32),
                        pltpu.VMEM((K, 8, lanes), F32), pltpu.VMEM((8, lanes), F32)],
        compiler_params=_cparams(("parallel", "arbitrary")), name=name)(*args, w)


def _ln_parts(u, ln_w, ln_b):
    mu = jnp.mean(u, axis=-1, keepdims=True)
    xc = u - mu
    rstd = lax.rsqrt(jnp.mean(xc * xc, axis=-1, keepdims=True) + EPS)
    un = xc * rstd
    return un, rstd, un * ln_w + ln_b


def _conf_out_fwd(uc, proj, h, ln_w, ln_b, wout, *, name):
    S, E = uc.shape
    D = h.shape[1]
    tm = min(GATE_TILE, S)

    def body(uc_ref, z_ref, h_ref, lw_ref, lb_ref, w_ref, o_ref):
        _, _, ul = _ln_parts(uc_ref[...].astype(F32), lw_ref[...], lb_ref[...])
        u3 = (_silu(ul) * _silu(z_ref[...].astype(F32))).astype(BF16)
        o_ref[...] = h_ref[...] + jnp.dot(u3, w_ref[...], preferred_element_type=F32)

    return pl.pallas_call(
        body, grid=(S // tm,), out_shape=SDS((S, D), F32),
        in_specs=[pl.BlockSpec((tm, E), lambda i: (i, 0)), pl.BlockSpec((tm, E), lambda i: (i, 2)),
                  pl.BlockSpec((tm, D), lambda i: (i, 0)), _resident((1, E)), _resident((1, E)), _resident((E, D))],
        out_specs=pl.BlockSpec((tm, D), lambda i: (i, 0)),
        compiler_params=_cparams(("parallel",)), name=name)(uc, proj, h, ln_w, ln_b, wout)


def _conf_out_bwd(dh, uc, proj, ln_w, ln_b, wout, *, name):
    S, E = uc.shape
    D = dh.shape[1]
    tm = min(GATE_TILE, S)
    nt = S // tm

    def body(dh_ref, uc_ref, z_ref, lw_ref, lb_ref, w_ref, u3_ref, duc_ref, dz_ref, dlw_ref, dlb_ref, alw, alb):
        i = pl.program_id(0)

        @pl.when(i == 0)
        def _():
            alw[...] = jnp.zeros_like(alw)
            alb[...] = jnp.zeros_like(alb)

        un, rstd, ul = _ln_parts(uc_ref[...].astype(F32), lw_ref[...], lb_ref[...])
        z = z_ref[...].astype(F32)
        su, sz = _silu(ul), _silu(z)
        u3_ref[...] = (su * sz).astype(BF16)
        du3 = lax.dot_general(dh_ref[...].astype(BF16), w_ref[...], (((1,), (1,)), ((), ())),
                              preferred_element_type=F32)
        dz_ref[...] = (du3 * su * _dsilu(z)).astype(BF16)
        dul = du3 * sz * _dsilu(ul)
        alw[...] += _rowsum8(dul * un)
        alb[...] += _rowsum8(dul)
        dun = dul * lw_ref[...]
        m1 = jnp.mean(dun, axis=-1, keepdims=True)
        m2 = jnp.mean(dun * un, axis=-1, keepdims=True)
        duc_ref[...] = (rstd * (dun - m1 - un * m2)).astype(BF16)

        @pl.when(i == nt - 1)
        def _():
            dlw_ref[...] = jnp.sum(alw[...], axis=0, keepdims=True)
            dlb_ref[...] = jnp.sum(alb[...], axis=0, keepdims=True)

    tile = pl.BlockSpec((tm, E), lambda i: (i, 0))
    vec = pl.BlockSpec((1, E), lambda i: (0, 0))
    return pl.pallas_call(
        body, grid=(nt,),
        out_shape=[SDS((S, E), BF16)] * 3 + [SDS((1, E), F32)] * 2,
        in_specs=[pl.BlockSpec((tm, D), lambda i: (i, 0)), tile, pl.BlockSpec((tm, E), lambda i: (i, 2)),
                  _resident((1, E)), _resident((1, E)), _resident((E, D))],
        out_specs=[tile, tile, tile, vec, vec],
        scratch_shapes=[pltpu.VMEM((8, E), F32), pltpu.VMEM((8, E), F32)],
        compiler_params=_cparams(("arbitrary",)), name=name)(dh, uc, proj, ln_w, ln_b, wout)


def _inproj_bwd(parts, offs, w, h, nw, dh_in, *, name):
    S, D = h.shape
    N = w.shape[1]
    tm = min(ROW_TILE, S)
    nt = S // tm
    widths = [p.shape[1] for p in parts]
    npart = len(parts)

    def body(*refs):
        p_refs = refs[:npart]
        w_ref, h_ref, nw_ref, dh_ref, o_ref, hn_ref, dnw_ref, anw = refs[npart:]
        i = pl.program_id(0)

        @pl.when(i == 0)
        def _():
            anw[...] = jnp.zeros_like(anw)

        x = h_ref[...]
        r = lax.rsqrt(jnp.mean(x * x, axis=-1, keepdims=True) + EPS)
        n = x * r
        hn_ref[...] = (n * nw_ref[...]).astype(BF16)
        dhn = jnp.zeros((tm, D), F32)
        for p_ref, off, wd in zip(p_refs, offs, widths):
            for c0, c1 in _col_chunks(wd, 1024):
                dhn = dhn + lax.dot_general(p_ref[:, c0:c1], w_ref[:, off + c0:off + c1],
                                            (((1,), (1,)), ((), ())), preferred_element_type=F32)
        anw[...] += _rowsum8(dhn * n)
        dn = dhn * nw_ref[...]
        o_ref[...] = dh_ref[...] + r * (dn - n * jnp.mean(dn * n, axis=-1, keepdims=True))

        @pl.when(i == nt - 1)
        def _():
            dnw_ref[...] = jnp.sum(anw[...], axis=0, keepdims=True)

    tile = pl.BlockSpec((tm, D), lambda i: (i, 0))
    return pl.pallas_call(
        body, grid=(nt,),
        out_shape=[SDS((S, D), F32), SDS((S, D), BF16), SDS((1, D), F32)],
        in_specs=[pl.BlockSpec((tm, wd), lambda i: (i, 0)) for wd in widths]
        + [_resident((D, N)), tile, _resident((1, D)), tile],
        out_specs=[tile, tile, pl.BlockSpec((1, D), lambda i: (0, 0))],
        scratch_shapes=[pltpu.VMEM((8, D), F32)],
        compiler_params=_cparams(("arbitrary",)), name=name)(*parts, w, h, nw, dh_in)


def _tn_matmul(a, b, *, name):
    S, Ka = a.shape
    N = b.shape[1]
    tm = min(ROW_TILE, S)
    bn = min(1024, N)
    nt = S // tm

    def body(a_ref, b_ref, o_ref):
        @pl.when(pl.program_id(1) == 0)
        def _():
            o_ref[...] = jnp.zeros_like(o_ref)

        o_ref[...] += lax.dot_general(a_ref[...], b_ref[...].astype(BF16), (((0,), (0,)), ((), ())),
                                      preferred_element_type=F32)

    return pl.pallas_call(
        body, grid=(N // bn, nt), out_shape=SDS((Ka, N), F32),
        in_specs=[pl.BlockSpec((tm, Ka), lambda j, i: (i, 0)), pl.BlockSpec((tm, bn), lambda j, i: (i, j))],
        out_specs=pl.BlockSpec((Ka, bn), lambda j, i: (0, j)),
        compiler_params=_cparams(("parallel", "arbitrary")), name=name)(a, b)


def _loss_head(h, fnw, target, *, name):
    S, D = h.shape
    tm = min(ROW_TILE, S)
    nt = S // tm

    def body(h_ref, w_ref, t_ref, loss_ref, dh_ref, dw_ref, aw, al):
        i = pl.program_id(0)

        @pl.when(i == 0)
        def _():
            aw[...] = jnp.zeros_like(aw)
            al[...] = jnp.zeros_like(al)

        x = h_ref[...]
        r = lax.rsqrt(jnp.mean(x * x, axis=-1, keepdims=True) + EPS)
        n = x * r
        err = n * w_ref[...] - t_ref[...]
        al[...] += _rowsum8(err * err)
        dy = err * (1.0 / D)
        aw[...] += _rowsum8(dy * n)
        dn = dy * w_ref[...]
        dh_ref[...] = r * (dn - n * jnp.mean(dn * n, axis=-1, keepdims=True))

        @pl.when(i == nt - 1)
        def _():
            dw_ref[...] = jnp.sum(aw[...], axis=0, keepdims=True)
            tot = jnp.sum(jnp.sum(al[...], axis=0, keepdims=True), axis=1, keepdims=True) * (0.5 / D)
            loss_ref[...] = jnp.zeros((8, 128), F32) + tot

    tile = pl.BlockSpec((tm, D), lambda i: (i, 0))
    loss, dh, dw = pl.pallas_call(
        body, grid=(nt,), out_shape=[SDS((8, 128), F32), SDS((S, D), F32), SDS((1, D), F32)],
        in_specs=[tile, _resident((1, D)), tile],
        out_specs=[pl.BlockSpec((8, 128), lambda i: (0, 0)), tile, pl.BlockSpec((1, D), lambda i: (0, 0))],
        scratch_shapes=[pltpu.VMEM((8, D), F32), pltpu.VMEM((8, D), F32)],
        compiler_params=_cparams(("arbitrary",)), name=name)(h, fnw, target)
    return loss[0, 0], dh, dw


def _head_mask(shape, head_axis):
    hd = lax.broadcasted_iota(jnp.int32, shape, head_axis)
    ch = lax.broadcasted_iota(jnp.int32, shape, 1 - head_axis)
    return ((ch >= hd * HEADDIM) & (ch < hd * HEADDIM + HEADDIM)).astype(BF16)


def _split_dot(vals, mat):
    parts = []
    for v in vals:
        hi = v.astype(BF16)
        parts += [hi, (v - hi.astype(F32)).astype(BF16)]
    out = jnp.dot(jnp.concatenate(parts, axis=0), mat, preferred_element_type=F32)
    R = vals[0].shape[0]
    return [out[2 * R * i:2 * R * i + R] + out[2 * R * i + R:2 * R * i + 2 * R] for i in range(len(vals))]


def _nt(a, b):
    return lax.dot_general(a, b, (((1,), (1,)), ((), ())), preferred_element_type=F32)


def _tn(a, b):
    return lax.dot_general(a, b, (((0,), (0,)), ((), ())), preferred_element_type=F32)


def _ssd_setup(c_val, dtraw, bias, alog, fwd):
    xbc = _silu(c_val.astype(F32))
    pre = dtraw + bias
    dt = _softplus(pre)
    A = -jnp.exp(alog)
    row = lax.broadcasted_iota(jnp.int32, (CHUNK, CHUNK), 0)
    col = lax.broadcasted_iota(jnp.int32, (CHUNK, CHUNK), 1)
    T = (col <= row) if fwd else (col >= row)
    cum = jnp.dot(T.astype(F32), dt * A, precision=lax.Precision.HIGHEST, preferred_element_type=F32)
    total = cum[CHUNK - 1:CHUNK] if fwd else cum[0:1]
    return dict(x=xbc[:, :D_INNER], B=xbc[:, D_INNER:D_INNER + GN].astype(BF16),
                C=xbc[:, D_INNER + GN:].astype(BF16), pre=pre, dt=dt, A=A, T=T, Tt=(col >= row) if fwd else (col <= row),
                cum=cum, e=jnp.exp(cum), dtds=dt * jnp.exp(total - cum))


def _ssd_fwd_dir(c_val, dtraw, bias, alog, state, y_ref, hin_ref, fwd, ex):
    s = _ssd_setup(c_val, dtraw, bias, alog, fwd)
    x, cum, T = s["x"], s["cum"], s["T"]
    e_x, dtds_x = _split_dot([s["e"], s["dtds"]], ex)
    dec_x = e_x[CHUNK - 1:CHUNK] if fwd else e_x[0:1]
    cumT, dtT = cum.T, s["dt"].T
    xb = x.astype(BF16)
    for g in range(SSD_GROUPS):
        g0, g1 = g * HPG * HEADDIM, (g + 1) * HPG * HEADDIM
        Bg, Cg = s["B"][:, g * D_STATE:(g + 1) * D_STATE], s["C"][:, g * D_STATE:(g + 1) * D_STATE]
        CBg = _nt(Cg, Bg)
        Hg = state[:, g0:g1]
        Hb = Hg.astype(BF16)
        hin_ref[0, :, g0:g1] = Hb
        yoff = jnp.dot(Cg, Hb, preferred_element_type=F32) * e_x[:, g0:g1]
        ys = []
        for hh in range(HPG):
            h = g * HPG + hh
            Lh = jnp.exp(jnp.where(T, cum[:, h:h + 1] - cumT[h:h + 1, :], NEG))
            Mh = (CBg * Lh * dtT[h:h + 1, :]).astype(BF16)
            ys.append(jnp.dot(Mh, xb[:, h * HEADDIM:(h + 1) * HEADDIM], preferred_element_type=F32))
        y_ref[:, g0:g1] = jnp.concatenate(ys, axis=1) + yoff
        xds = (x[:, g0:g1] * dtds_x[:, g0:g1]).astype(BF16)
        state[:, g0:g1] = Hg * dec_x[:, g0:g1] + _tn(Bg, xds)


def _ssd_scan_fwd(cact, tail, bias, alog, *, name):
    S = cact.shape[0]
    nc = S // CHUNK
    E = D_INNER

    def body(cf_ref, cb_ref, tf_ref, tb_ref, bias_ref, alog_ref, yf_ref, yb_ref, hf_ref, hb_ref, state):
        @pl.when(pl.program_id(0) == 0)
        def _():
            state[...] = jnp.zeros_like(state)

        ex = _head_mask((SSD_HEADS, D_INNER), 0)
        H = SSD_HEADS
        _ssd_fwd_dir(cf_ref[...], tf_ref[:, 0:H], bias_ref[0:1, :], alog_ref[0:1, :], state.at[0], yf_ref, hf_ref, True, ex)
        _ssd_fwd_dir(cb_ref[...], tb_ref[:, H:2 * H], bias_ref[1:2, :], alog_ref[1:2, :], state.at[1], yb_ref, hb_ref, False, ex)

    up, down = (lambda i: (i, 0)), (lambda i: (nc - 1 - i, 0))
    up3, down3 = (lambda i: (i, 0, 0)), (lambda i: (nc - 1 - i, 0, 0))
    W = cact.shape[1]
    return pl.pallas_call(
        body, grid=(nc,),
        out_shape=[SDS((S, E), F32), SDS((S, E), F32), SDS((nc, D_STATE, E), BF16), SDS((nc, D_STATE, E), BF16)],
        in_specs=[pl.BlockSpec((CHUNK, W), up), pl.BlockSpec((CHUNK, W), down),
                  pl.BlockSpec((CHUNK, 128), up), pl.BlockSpec((CHUNK, 128), down),
                  _resident((2, SSD_HEADS)), _resident((2, SSD_HEADS))],
        out_specs=[pl.BlockSpec((CHUNK, E), up), pl.BlockSpec((CHUNK, E), down),
                   pl.BlockSpec((1, D_STATE, E), up3), pl.BlockSpec((1, D_STATE, E), down3)],
        scratch_shapes=[pltpu.VMEM((2, D_STATE, E), F32)],
        compiler_params=_cparams(("arbitrary",)), name=name)(cact, cact, tail, tail, bias, alog)


def _ssd_bwd_dir(c_val, dtraw, bias, alog, dy, hin_ref, dskip_x, G, dc_ref, ddt_ref, accA, accb, fwd, ex, ext):
    s = _ssd_setup(c_val, dtraw, bias, alog, fwd)
    x, cum, T, dt = s["x"], s["cum"], s["T"], s["dt"]
    e_x, dtds_x = _split_dot([s["e"], s["dtds"]], ex)
    dec_x = e_x[CHUNK - 1:CHUNK] if fwd else e_x[0:1]
    cumT, dtT = cum.T, dt.T
    xb = x.astype(BF16)
    dyf = dy.astype(F32)
    dx_diag, dx_st, dBs, dCs, gh, yoff_dy = [], [], [], [], [], []
    lane_h = lax.broadcasted_iota(jnp.int32, (CHUNK, SSD_HEADS), 1)
    sub_h = lax.broadcasted_iota(jnp.int32, (SSD_HEADS, CHUNK), 0)
    rows = jnp.zeros((CHUNK, SSD_HEADS), F32)
    colsT = jnp.zeros((SSD_HEADS, CHUNK), F32)
    for g in range(SSD_GROUPS):
        g0, g1 = g * HPG * HEADDIM, (g + 1) * HPG * HEADDIM
        Bg, Cg = s["B"][:, g * D_STATE:(g + 1) * D_STATE], s["C"][:, g * D_STATE:(g + 1) * D_STATE]
        CBg = _nt(Cg, Bg)
        Hb = hin_ref[0, :, g0:g1]
        Gg = G[:, g0:g1]
        Gb = Gg.astype(BF16)
        edyf = dyf[:, g0:g1] * e_x[:, g0:g1]
        edy = edyf.astype(BF16)
        yoff_dy.append(edyf * jnp.dot(Cg, Hb, preferred_element_type=F32))
        dC_g = _nt(edy, Hb)
        dH_g = _tn(Cg, edy)
        xds = (x[:, g0:g1] * dtds_x[:, g0:g1]).astype(BF16)
        dB_g = _nt(xds, Gb)
        dx_st.append(dtds_x[:, g0:g1] * jnp.dot(Bg, Gb, preferred_element_type=F32))
        dCB = jnp.zeros((CHUNK, CHUNK), F32)
        for hh in range(HPG):
            h = g * HPG + hh
            hs = slice(h * HEADDIM, (h + 1) * HEADDIM)
            Lh = jnp.exp(jnp.where(T, cum[:, h:h + 1] - cumT[h:h + 1, :], NEG))
            Mh = (CBg * Lh * dtT[h:h + 1, :]).astype(BF16)
            term = _nt(dy[:, hs], xb[:, hs]) * dtT[h:h + 1, :] * Lh
            dCB = dCB + term
            Wh = term * CBg
            rows = rows + jnp.where(lane_h == h, jnp.sum(Wh, axis=1, keepdims=True), 0.0)
            colsT = colsT + jnp.where(sub_h == h, jnp.sum(Wh, axis=0, keepdims=True), 0.0)
            dx_diag.append(_tn(Mh, dy[:, hs]))
        dCBb = dCB.astype(BF16)
        dCs.append(dC_g + jnp.dot(dCBb, Bg, preferred_element_type=F32))
        dBs.append(dB_g + _tn(dCBb, Cg))
        gh.append(_rowsum8(Gg * Hb.astype(F32)))
        G[:, g0:g1] = Gg * dec_x[:, g0:g1] + dH_g
    dxn = jnp.concatenate(dx_diag, axis=1) + jnp.concatenate(dx_st, axis=1)
    xst = x * jnp.concatenate(dx_st, axis=1)
    R1, Zx, Zs = _split_dot([jnp.concatenate(yoff_dy, axis=1), x * dxn, xst], ext)
    ddec = jnp.sum(_split_dot([jnp.concatenate(gh, axis=1)], ext)[0], axis=0, keepdims=True)
    dec = s["e"][CHUNK - 1:CHUNK] if fwd else s["e"][0:1]
    dtotal = jnp.sum(Zs, axis=0, keepdims=True) + dec * ddec
    rowi = lax.broadcasted_iota(jnp.int32, (CHUNK, SSD_HEADS), 0)
    dacs = rows - colsT.T + R1 - Zs + jnp.where(rowi == (CHUNK - 1 if fwd else 0), dtotal, 0.0)
    da = jnp.dot(s["Tt"].astype(F32), dacs, precision=lax.Precision.HIGHEST, preferred_element_type=F32)
    accA[...] += _rowsum8(da * dt)
    sg = _sig(s["pre"])
    ratio = jnp.where(dt > 1e-30, sg / jnp.maximum(dt, 1e-30), 1.0)
    ddtraw = da * s["A"] * sg + Zx * ratio
    ddt_ref[...] = ddtraw
    accb[...] += _rowsum8(ddtraw)
    dx = dxn + dyf * dskip_x if fwd else dxn
    dact = jnp.concatenate([dx] + dBs + dCs, axis=1)
    dc_ref[...] = (dact * _dsilu(c_val.astype(F32))).astype(BF16)


def _ssd_scan_bwd(cact, tail, bias, alog, dy, hf, hb, dskip_x, *, name):
    S = cact.shape[0]
    nc = S // CHUNK
    E, H, W = D_INNER, SSD_HEADS, cact.shape[1]

    def body(cf_ref, cb_ref, tf_ref, tb_ref, bias_ref, alog_ref, dyf_ref, dyb_ref, hf_ref, hb_ref,
             dsk_ref, dcf_ref, dcb_ref, ddtf_ref, ddtb_ref, dalog_ref, dbias_ref, G, accA, accb):
        i = pl.program_id(0)

        @pl.when(i == 0)
        def _():
            G[...] = jnp.zeros_like(G)
            accA[...] = jnp.zeros_like(accA)
            accb[...] = jnp.zeros_like(accb)

        ex = _head_mask((H, E), 0)
        ext = _head_mask((E, H), 1)
        _ssd_bwd_dir(cf_ref[...], tf_ref[:, 0:H], bias_ref[0:1, :], alog_ref[0:1, :], dyf_ref[...], hf_ref,
                     dsk_ref[...], G.at[0], dcf_ref, ddtf_ref, accA.at[0], accb.at[0], True, ex, ext)
        _ssd_bwd_dir(cb_ref[...], tb_ref[:, H:2 * H], bias_ref[1:2, :], alog_ref[1:2, :], dyb_ref[...], hb_ref,
                     dsk_ref[...], G.at[1], dcb_ref, ddtb_ref, accA.at[1], accb.at[1], False, ex, ext)

        @pl.when(i == nc - 1)
        def _():
            for d in range(2):
                A = -jnp.exp(alog_ref[d:d + 1, :])
                dalog_ref[d:d + 1, :] = jnp.sum(accA[d], axis=0, keepdims=True) * A
                dbias_ref[d:d + 1, :] = jnp.sum(accb[d], axis=0, keepdims=True)

    up, down = (lambda i: (i, 0)), (lambda i: (nc - 1 - i, 0))
    up3, down3 = (lambda i: (i, 0, 0)), (lambda i: (nc - 1 - i, 0, 0))
    small = pl.BlockSpec((2, H), lambda i: (0, 0))
    return pl.pallas_call(
        body, grid=(nc,),
        out_shape=[SDS((S, W), BF16), SDS((S, W), BF16), SDS((S, H), F32), SDS((S, H), F32),
                   SDS((2, H), F32), SDS((2, H), F32)],
        in_specs=[pl.BlockSpec((CHUNK, W), down), pl.BlockSpec((CHUNK, W), up),
                  pl.BlockSpec((CHUNK, 128), down), pl.BlockSpec((CHUNK, 128), up),
                  _resident((2, H)), _resident((2, H)),
                  pl.BlockSpec((CHUNK, E), down), pl.BlockSpec((CHUNK, E), up),
                  pl.BlockSpec((1, D_STATE, E), down3), pl.BlockSpec((1, D_STATE, E), up3),
                  _resident((1, E))],
        out_specs=[pl.BlockSpec((CHUNK, W), down), pl.BlockSpec((CHUNK, W), up),
                   pl.BlockSpec((CHUNK, H), down), pl.BlockSpec((CHUNK, H), up), small, small],
        scratch_shapes=[pltpu.VMEM((2, D_STATE, E), F32), pltpu.VMEM((2, 8, H), F32), pltpu.VMEM((2, 8, H), F32)],
        compiler_params=_cparams(("arbitrary",)), name=name)(
            cact, cact, tail, tail, bias, alog, dy, dy, hf, hb, dskip_x)


def _ssd_gate(yf, yb, cx, z, dsk, nw):
    xs = _silu(cx.astype(F32))
    y = yf + yb + xs * dsk
    sz = _silu(z)
    gt = y * sz
    r = lax.rsqrt(jnp.mean(gt * gt, axis=-1, keepdims=True) + EPS)
    return xs, y, sz, r, gt * r


def _ssd_out_fwd(yf, yb, cact, proj, dskip_x, nw, h, wout, *, name):
    S, E = yf.shape
    D = h.shape[1]
    tm = min(GATE_TILE, S)

    def body(yf_ref, yb_ref, cx_ref, z_ref, dsk_ref, nw_ref, h_ref, w_ref, o_ref):
        _, _, _, _, n = _ssd_gate(yf_ref[...], yb_ref[...], cx_ref[...], z_ref[...].astype(F32), dsk_ref[...], nw_ref[...])
        o_ref[...] = h_ref[...] + jnp.dot((n * nw_ref[...]).astype(BF16), w_ref[...], preferred_element_type=F32)

    tile = pl.BlockSpec((tm, E), lambda i: (i, 0))
    return pl.pallas_call(
        body, grid=(S // tm,), out_shape=SDS((S, D), F32),
        in_specs=[tile, tile, tile, tile, _resident((1, E)), _resident((1, E)),
                  pl.BlockSpec((tm, D), lambda i: (i, 0)), _resident((E, D))],
        out_specs=pl.BlockSpec((tm, D), lambda i: (i, 0)),
        compiler_params=_cparams(("parallel",)), name=name)(yf, yb, cact, proj, dskip_x, nw, h, wout)


def _ssd_out_bwd(dh, yf, yb, cact, proj, dskip_x, nw, wout, *, name):
    S, E = yf.shape
    D = dh.shape[1]
    tm = min(GATE_TILE, S)
    nt = S // tm

    def body(dh_ref, yf_ref, yb_ref, cx_ref, z_ref, dsk_ref, nw_ref, w_ref, yn_ref, dy_ref, dz_ref, dnw_ref, dd_ref, anw, ad):
        i = pl.program_id(0)

        @pl.when(i == 0)
        def _():
            anw[...] = jnp.zeros_like(anw)
            ad[...] = jnp.zeros_like(ad)

        z = z_ref[...].astype(F32)
        xs, y, sz, r, n = _ssd_gate(yf_ref[...], yb_ref[...], cx_ref[...], z, dsk_ref[...], nw_ref[...])
        yn_ref[...] = (n * nw_ref[...]).astype(BF16)
        dyn = _nt(dh_ref[...].astype(BF16), w_ref[...])
        anw[...] += _rowsum8(dyn * n)
        dn = dyn * nw_ref[...]
        dgt = r * (dn - n * jnp.mean(dn * n, axis=-1, keepdims=True))
        dy = dgt * sz
        dy_ref[...] = dy.astype(BF16)
        dz_ref[...] = (dgt * y * _dsilu(z)).astype(BF16)
        ad[...] += _rowsum8(dy * xs)

        @pl.when(i == nt - 1)
        def _():
            dnw_ref[...] = jnp.sum(anw[...], axis=0, keepdims=True)
            dd_ref[...] = jnp.sum(ad[...], axis=0, keepdims=True)

    tile = pl.BlockSpec((tm, E), lambda i: (i, 0))
    vec = pl.BlockSpec((1, E), lambda i: (0, 0))
    return pl.pallas_call(
        body, grid=(nt,), out_shape=[SDS((S, E), BF16)] * 3 + [SDS((1, E), F32)] * 2,
        in_specs=[pl.BlockSpec((tm, D), lambda i: (i, 0)), tile, tile, tile, tile,
                  _resident((1, E)), _resident((1, E)), _resident((E, D))],
        out_specs=[tile, tile, tile, vec, vec],
        scratch_shapes=[pltpu.VMEM((8, E), F32), pltpu.VMEM((8, E), F32)],
        compiler_params=_cparams(("arbitrary",)), name=name)(dh, yf, yb, cact, proj, dskip_x, nw, wout)


def _local_step(x, target, p):
    E, S = D_INNER, x.shape[0]
    depth = p["norm_w"].shape[0]
    h, saved = x, []
    for i in range(depth):
        j, nw = i // 2, p["norm_w"][i][None]
        if i % 2 == 0:
            proj = _inproj_fwd(h, nw, p["cm_w_in"][j], tail=False, name="cm_in_fwd")
            uc = _dwconv_fwd(proj, p["cm_dw_w"][j], p["cm_dw_b"][j][None], col0=0, glu_col0=E, C=E, name="cm_conv_fwd")
            hn = _conf_out_fwd(uc, proj, h, p["cm_ln_w"][j][None], p["cm_ln_b"][j][None], p["cm_w_out"][j], name="cm_out_fwd")
            saved.append((h, proj, uc))
        else:
            proj, tail = _inproj_fwd(h, nw, p["ssd_w_in"][j], tail=True, name="ssd_in_fwd")
            cact = _dwconv_fwd(proj, p["ssd_conv_w"][j], p["ssd_conv_b"][j][None], col0=E, glu_col0=None,
                               C=SSD_CONV_DIM, name="ssd_conv_fwd")
            yf, yb, hf, hb = _ssd_scan_fwd(cact, tail, p["ssd_dt_bias"][j], p["ssd_A_log"][j], name="ssd_scan_fwd")
            dsk = jnp.repeat(p["ssd_D"][j], HEADDIM)[None]
            hn = _ssd_out_fwd(yf, yb, cact, proj, dsk, p["ssd_norm_w"][j][None], h, p["ssd_w_out"][j], name="ssd_out_fwd")
            saved.append((h, proj, tail, cact, yf, yb, hf, hb, dsk))
        h = hn
    loss, dh, g_fnw = _loss_head(h, p["final_norm_w"][None], target, name="loss_head")

    names = ["norm_w", "cm_w_in", "cm_dw_w", "cm_dw_b", "cm_ln_w", "cm_ln_b", "cm_w_out", "ssd_w_in", "ssd_conv_w",
             "ssd_conv_b", "ssd_dt_bias", "ssd_A_log", "ssd_D", "ssd_norm_w", "ssd_w_out"]
    g = {n: [None] * p[n].shape[0] for n in names}
    for i in reversed(range(depth)):
        j, nw = i // 2, p["norm_w"][i][None]
        if i % 2 == 0:
            h_in, proj, uc = saved[i]
            u3, duc, dz, g_lw, g_lb = _conf_out_bwd(dh, uc, proj, p["cm_ln_w"][j][None], p["cm_ln_b"][j][None],
                                                    p["cm_w_out"][j], name="cm_out_bwd")
            g["cm_w_out"][j] = _tn_matmul(u3, dh, name="cm_wout_grad")
            dv, dg, g_dw, g_db = _dwconv_bwd([duc], proj, p["cm_dw_w"][j], col0=0, glu_col0=E, C=E, name="cm_conv_bwd")
            parts = [dv, dg, dz]
            dh, hnb, g_nw = _inproj_bwd(parts, [0, E, 2 * E], p["cm_w_in"][j], h_in, nw, dh, name="cm_in_bwd")
            g["cm_w_in"][j] = jnp.concatenate([_tn_matmul(hnb, q, name="cm_win_grad") for q in parts], axis=1)
            g["cm_dw_w"][j], g["cm_dw_b"][j], g["cm_ln_w"][j], g["cm_ln_b"][j] = g_dw, g_db[0], g_lw[0], g_lb[0]
        else:
            h_in, proj, tail, cact, yf, yb, hf, hb, dsk = saved[i]
            yn, dy, dz, g_sn, g_dx = _ssd_out_bwd(dh, yf, yb, cact, proj, dsk, p["ssd_norm_w"][j][None],
                                                  p["ssd_w_out"][j], name="ssd_out_bwd")
            g["ssd_w_out"][j] = _tn_matmul(yn, dh, name="ssd_wout_grad")
            dcf, dcb, ddtf, ddtb, g_al, g_bias = _ssd_scan_bwd(cact, tail, p["ssd_dt_bias"][j], p["ssd_A_log"][j],
                                                               dy, hf, hb, dsk, name="ssd_scan_bwd")
            dxbc, g_cw, g_cb = _dwconv_bwd([dcf, dcb], proj, p["ssd_conv_w"][j], col0=E, glu_col0=None,
                                           C=SSD_CONV_DIM, name="ssd_conv_bwd")
            ddt = jnp.concatenate([ddtf, ddtb, jnp.zeros((S, 128 - 2 * SSD_HEADS), F32)], axis=1).astype(BF16)
            parts = [dz, dxbc, ddt]
            dh, hnb, g_nw = _inproj_bwd(parts, [0, E, E + SSD_CONV_DIM], p["ssd_w_in"][j], h_in, nw, dh, name="ssd_in_bwd")
            gw = [_tn_matmul(hnb, q, name="ssd_win_grad") for q in parts]
            g["ssd_w_in"][j] = jnp.concatenate([gw[0], gw[1], gw[2][:, :2 * SSD_HEADS]], axis=1)
            g["ssd_conv_w"][j], g["ssd_conv_b"][j], g["ssd_norm_w"][j] = g_cw, g_cb[0], g_sn[0]
            g["ssd_dt_bias"][j], g["ssd_A_log"][j] = g_bias, g_al
            g["ssd_D"][j] = jnp.sum(g_dx.reshape(SSD_HEADS, HEADDIM), axis=1)
        g["norm_w"][i] = g_nw[0]
    grads = {n: jnp.stack(v) for n, v in g.items()}
    grads["final_norm_w"] = g_fnw[0]
    return loss, dh, grads


def _exchange(items, *, name):
    n = len(items)
    flips = [(fx, fy, fc) for fx in (0, 1) for fy in (0, 1) for fc in (0, 1)][1:]

    def body(*refs):
        ins, outs = refs[:n], refs[n:2 * n]
        send_sems, recv_sems, local_sems = refs[2 * n:]
        x, y, c = lax.axis_index("x"), lax.axis_index("y"), lax.axis_index("c")
        me = 4 * x + 2 * y + c
        started = []
        for b, (_, scatter) in enumerate(items):
            own = ins[b].at[me] if scatter else ins[b]
            loc = pltpu.make_async_copy(own, outs[b].at[me], local_sems.at[b])
            loc.start()
            started.append(loc)
        copies = []
        for b, (_, scatter) in enumerate(items):
            for k, (fx, fy, fc) in enumerate(flips):
                px, py, pc = x ^ fx, y ^ fy, c ^ fc
                peer = 4 * px + 2 * py + pc
                cp = pltpu.make_async_remote_copy(
                    src_ref=ins[b].at[peer] if scatter else ins[b], dst_ref=outs[b].at[me],
                    send_sem=send_sems.at[b * 7 + k], recv_sem=recv_sems.at[b * 7 + k],
                    device_id=(px, py, pc), device_id_type=pl.DeviceIdType.MESH)
                cp.start()
                arrival = pltpu.make_async_remote_copy(
                    src_ref=ins[b].at[peer] if scatter else ins[b], dst_ref=outs[b].at[peer],
                    send_sem=send_sems.at[b * 7 + k], recv_sem=recv_sems.at[b * 7 + k],
                    device_id=(px, py, pc), device_id_type=pl.DeviceIdType.MESH)
                copies.append((cp, arrival))
        for cp, arrival in copies:
            arrival.wait_recv()
        for cp, arrival in copies:
            cp.wait_send()
        for loc in started:
            loc.wait()

    hbm = pl.BlockSpec(memory_space=pltpu.HBM)
    out_shape = [SDS((N_DEV,) + tuple(a.shape[-2:]), a.dtype) for a, _ in items]
    return pl.pallas_call(
        body, out_shape=out_shape, in_specs=[hbm] * n, out_specs=[hbm] * n,
        scratch_shapes=[pltpu.SemaphoreType.DMA((7 * n,)), pltpu.SemaphoreType.DMA((7 * n,)),
                        pltpu.SemaphoreType.DMA((n,))],
        name=name)(*[a for a, _ in items])


def _reduce_adamw(recv, w, m, v, *, rows, name):
    R, C = w.shape
    c1 = 1.0 / (1.0 - ADAM_B1 ** ADAM_STEP)
    c2 = 1.0 / (1.0 - ADAM_B2 ** ADAM_STEP)

    def body(r_ref, w_ref, m_ref, v_ref, g_ref, d_ref, mo_ref, vo_ref):
        g = r_ref[0].astype(F32)
        for s in range(1, N_DEV):
            g = g + r_ref[s].astype(F32)
        mn = ADAM_B1 * m_ref[...] + (1.0 - ADAM_B1) * g
        vn = ADAM_B2 * v_ref[...] + (1.0 - ADAM_B2) * (g * g)
        g_ref[...] = g
        mo_ref[...] = mn
        vo_ref[...] = vn
        d_ref[...] = -ADAM_LR * ((mn * c1) / (jnp.sqrt(vn * c2) + ADAM_EPS) + ADAM_WD * w_ref[...])

    tile = pl.BlockSpec((rows, C), lambda i: (i, 0))
    return pl.pallas_call(
        body, grid=(R // rows,), out_shape=[SDS((R, C), F32)] * 4,
        in_specs=[pl.BlockSpec((N_DEV, rows, C), lambda i: (0, i, 0)), tile, tile, tile],
        out_specs=[tile] * 4, compiler_params=_cparams(("parallel",)), name=name)(recv, w, m, v)


_ORDER = ["norm_w", "final_norm_w", "cm_w_in", "cm_dw_w", "cm_dw_b", "cm_ln_w", "cm_ln_b", "cm_w_out", "ssd_w_in",
          "ssd_conv_w", "ssd_conv_b", "ssd_dt_bias", "ssd_A_log", "ssd_D", "ssd_norm_w", "ssd_w_out"]
_BIG = (["cm_w_in", "cm_w_out", "ssd_w_in", "ssd_w_out"], 3872, 1024, 352)
_SMALL = (["cm_dw_w", "ssd_conv_w", "ssd_conv_b", "ssd_norm_w"], 168, 128, 168)
_REP = (["norm_w", "final_norm_w", "cm_dw_b", "cm_ln_w", "cm_ln_b", "ssd_dt_bias", "ssd_A_log", "ssd_D"], 144, 128, 144)
_SHARD_AXIS = {"cm_w_in": 2, "cm_dw_w": 2, "cm_w_out": 1, "ssd_w_in": 2, "ssd_conv_w": 2, "ssd_conv_b": 1,
               "ssd_norm_w": 1, "ssd_w_out": 1}


def _pack(arrs, rows, lanes, dtype, lead=()):
    nl = len(lead)
    flat = jnp.concatenate([a.reshape(lead + (-1,)).astype(dtype) for a in arrs], axis=nl)
    flat = jnp.pad(flat, [(0, 0)] * nl + [(0, rows * lanes - flat.shape[nl])])
    return flat.reshape(lead + (rows, lanes))


def _unpack(buf, shapes, lead=()):
    nl = len(lead)
    flat = buf.reshape(lead + (-1,))
    out, off = [], 0
    for s in shapes:
        sz = math.prod(s)
        out.append(flat[..., off:off + sz].reshape(lead + tuple(s)))
        off += sz
    return out


def _join_shards(parts, axis):
    full = jnp.moveaxis(parts, 0, axis)
    sh = full.shape
    return full.reshape(sh[:axis] + (sh[axis] * sh[axis + 1],) + sh[axis + 2:])


def _split_shards(full, axis):
    sh = full.shape
    parts = full.reshape(sh[:axis] + (N_DEV, sh[axis] // N_DEV) + sh[axis + 1:])
    return jnp.moveaxis(parts, axis, 0)


def kernel(x, norm_w, final_norm_w, cm_w_in, cm_dw_w, cm_dw_b, cm_ln_w, cm_ln_b, cm_w_out, ssd_w_in, ssd_conv_w, ssd_conv_b, ssd_dt_bias, ssd_A_log, ssd_D, ssd_norm_w, ssd_w_out, loss_target, m_norm_w, m_final_norm_w, m_cm_w_in, m_cm_dw_w, m_cm_dw_b, m_cm_ln_w, m_cm_ln_b, m_cm_w_out, m_ssd_w_in, m_ssd_conv_w, m_ssd_conv_b, m_ssd_dt_bias, m_ssd_A_log, m_ssd_D, m_ssd_norm_w, m_ssd_w_out, v_norm_w, v_final_norm_w, v_cm_w_in, v_cm_dw_w, v_cm_dw_b, v_cm_ln_w, v_cm_ln_b, v_cm_w_out, v_ssd_w_in, v_ssd_conv_w, v_ssd_conv_b, v_ssd_dt_bias, v_ssd_A_log, v_ssd_D, v_ssd_norm_w, v_ssd_w_out):
    w = dict(zip(_ORDER, (norm_w, final_norm_w, cm_w_in, cm_dw_w, cm_dw_b, cm_ln_w, cm_ln_b, cm_w_out, ssd_w_in,
                          ssd_conv_w, ssd_conv_b, ssd_dt_bias, ssd_A_log, ssd_D, ssd_norm_w, ssd_w_out)))
    m = dict(zip(_ORDER, (m_norm_w, m_final_norm_w, m_cm_w_in, m_cm_dw_w, m_cm_dw_b, m_cm_ln_w, m_cm_ln_b, m_cm_w_out,
                          m_ssd_w_in, m_ssd_conv_w, m_ssd_conv_b, m_ssd_dt_bias, m_ssd_A_log, m_ssd_D, m_ssd_norm_w,
                          m_ssd_w_out)))
    v = dict(zip(_ORDER, (v_norm_w, v_final_norm_w, v_cm_w_in, v_cm_dw_w, v_cm_dw_b, v_cm_ln_w, v_cm_ln_b, v_cm_w_out,
                          v_ssd_w_in, v_ssd_conv_w, v_ssd_conv_b, v_ssd_dt_bias, v_ssd_A_log, v_ssd_D, v_ssd_norm_w,
                          v_ssd_w_out)))

    (bn, br, bl, bt), (sn, sr, sl, st), (rn, rr, rl, rt) = _BIG, _SMALL, _REP
    big_all, small_all = _exchange([(_pack([w[k] for k in bn], br, bl, BF16), False),
                                    (_pack([w[k] for k in sn], sr, sl, F32), False)], name="gather_weights")
    p = {k: w[k] for k in rn}
    for names, buf in ((bn, big_all), (sn, small_all)):
        for k, parts in zip(names, _unpack(buf, [w[k].shape for k in names], lead=(N_DEV,))):
            p[k] = _join_shards(parts, _SHARD_AXIS[k])
    p["ssd_w_in"] = jnp.pad(p["ssd_w_in"], ((0, 0), (0, 0), (0, SSD_IN_PAD - SSD_IN)))

    loss, dx, g = _local_step(x[0], loss_target[0], p)
    loss = lax.psum(loss, ("x", "y", "c"))

    g_big = _pack([_split_shards(g[k], _SHARD_AXIS[k]) for k in bn], br, bl, BF16, lead=(N_DEV,))
    g_small = _pack([_split_shards(g[k], _SHARD_AXIS[k]) for k in sn], sr, sl, F32, lead=(N_DEV,))
    g_rep = _pack([g[k] for k in rn], rr, rl, F32)
    r_big, r_small, r_rep = _exchange([(g_big, True), (g_small, True), (g_rep, False)], name="exchange_grads")

    outs = {}
    for (names, rows, lanes, tile), recv, nm in ((_BIG, r_big, "adamw_big"), (_SMALL, r_small, "adamw_small"),
                                                 (_REP, r_rep, "adamw_rep")):
        res = _reduce_adamw(recv, *[_pack([d[k] for k in names], rows, lanes, F32) for d in (w, m, v)],
                            rows=tile, name=nm)
        for kind, buf in zip(("g", "d", "m", "v"), res):
            for k, a in zip(names, _unpack(buf, [w[k].shape for k in names])):
                outs[kind, k] = a
    return (loss, dx[None], *[outs[kind, k] for kind in ("g", "d", "m", "v") for k in _ORDER])
```

```python
import functools
import math

import jax
import jax.numpy as jnp
from jax import lax
from jax.experimental import pallas as pl
from jax.experimental.pallas import tpu as pltpu

F32, BF16 = jnp.float32, jnp.bfloat16
SDS = jax.ShapeDtypeStruct

D_MODEL = 1024
D_INNER = 2048
CONF_KERNEL = 31
HEADDIM = 64
SSD_HEADS = 32
SSD_GROUPS = 4
HPG = SSD_HEADS // SSD_GROUPS
D_STATE = 128
SSD_CONV = 5
CHUNK = 128
GN = SSD_GROUPS * D_STATE
SSD_CONV_DIM = D_INNER + 2 * GN
SSD_IN = D_INNER + SSD_CONV_DIM + 2 * SSD_HEADS
SSD_IN_PAD = SSD_IN + 64
EPS = 1e-5
N_DEV = 8

ADAM_LR, ADAM_B1, ADAM_B2, ADAM_EPS, ADAM_WD, ADAM_STEP = 0.001, 0.9, 0.999, 1e-08, 0.01, 10

VMEM_LIMIT = 56 * 1024 * 1024
ROW_TILE = 512
GATE_TILE = 256
CONV_ROWS = 512
CONV_LANES = 256
CONV_HALO = 16
CONV_RB = 32
CONV_DIRECT_TAPS = 8
NEG = -1e30


def _cparams(sem):
    return pltpu.CompilerParams(dimension_semantics=sem, vmem_limit_bytes=VMEM_LIMIT)


def _resident(shape):
    nd = len(shape)
    return pl.BlockSpec(shape, lambda *_: (0,) * nd, pipeline_mode=pl.Buffered(1))


def _sig(x):
    return 1.0 / (1.0 + jnp.exp(-x))


def _silu(x):
    return x * _sig(x)


def _dsilu(x):
    s = _sig(x)
    return s * (1.0 + x * (1.0 - s))


def _softplus(x):
    return jnp.maximum(x, 0.0) + jnp.log(1.0 + jnp.exp(-jnp.abs(x)))


def _col_chunks(n, width):
    out, c = [], 0
    while c < n:
        out.append((c, min(c + width, n)))
        c += width
    return out


def _rowsum8(v):
    acc = v[0:8]
    for j in range(1, v.shape[0] // 8):
        acc = acc + v[8 * j:8 * j + 8]
    return acc


def _inproj_fwd(h, nw, w, *, tail, wt, name):
    S, D = h.shape
    N = w.shape[0] if wt else w.shape[1]
    tm = min(ROW_TILE, S)
    chunks = _col_chunks(N, 1024)

    def body(h_ref, nw_ref, w_ref, o_ref, *rest):
        x = h_ref[...]
        r = lax.rsqrt(jnp.mean(x * x, axis=-1, keepdims=True) + EPS)
        hn = (x * r * nw_ref[...]).astype(BF16)
        for c0, c1 in chunks:
            if wt:
                acc = lax.dot_general(hn, w_ref[c0:c1, :], (((1,), (1,)), ((), ())), preferred_element_type=F32)
            else:
                acc = jnp.dot(hn, w_ref[:, c0:c1], preferred_element_type=F32)
            o_ref[:, c0:c1] = acc.astype(BF16)
            if tail and c1 == N:
                rest[0][...] = acc[:, acc.shape[1] - 128:]

    out_shape = [SDS((S, N), BF16)]
    out_specs = [pl.BlockSpec((tm, N), lambda i: (i, 0))]
    if tail:
        out_shape.append(SDS((S, 128), F32))
        out_specs.append(pl.BlockSpec((tm, 128), lambda i: (i, 0)))
    res = pl.pallas_call(
        body, grid=(S // tm,), out_shape=out_shape,
        in_specs=[pl.BlockSpec((tm, D), lambda i: (i, 0)), _resident((1, D)), _resident(w.shape)],
        out_specs=out_specs, compiler_params=_cparams(("parallel",)), name=name)(h, nw, w)
    return res if tail else res[0]


def _tap_plan(offsets):
    rs = sorted({s % 8 for s in offsets})
    return rs, [(k, rs.index(s % 8), s // 8) for k, s in enumerate(offsets)]


def _halo_specs(S, rows, lanes, col0):
    per, last = rows // CONV_HALO, S // CONV_HALO - 1
    cb = col0 // lanes
    return [pl.BlockSpec((rows, lanes), lambda j, i: (i, cb + j)),
            pl.BlockSpec((CONV_HALO, lanes), lambda j, i: (jnp.maximum(i * per - 1, 0), cb + j)),
            pl.BlockSpec((CONV_HALO, lanes), lambda j, i: (jnp.minimum((i + 1) * per, last), cb + j))]


def _fill_window(win, prev, main, nxt, i, nt, rows):
    hb = CONV_HALO
    win[0:hb, :] = jnp.where(i > 0, prev, 0.0)
    win[hb:hb + rows, :] = main
    win[hb + rows:hb + rows + hb, :] = jnp.where(i < nt - 1, nxt, 0.0)


def _fill_shifts(win, sh, rs, n_u):
    for idx, r in enumerate(rs):
        sh[idx] = win[pl.ds(r, n_u), :]


def _tap_rows(win, sh, rs, ri, q, r0, rb, direct):
    if direct:
        return win[pl.ds(r0 + 8 * q + rs[ri], rb), :]
    return sh[ri, pl.ds(pl.multiple_of(r0 + 8 * q, 8), rb), :]


def _row_loop(n, rb, fn, direct):
    if direct:
        for t in range(n):
            fn(t * rb)
    else:
        def step(t, carry):
            fn(pl.multiple_of(t * rb, rb))
            return carry

        lax.fori_loop(0, n, step, 0)


def _dwconv_fwd(src, w, b, *, col0, glu_col0, C, name):
    S = src.shape[0]
    K = w.shape[0]
    pad = (K - 1) // 2
    rows, lanes, hb, rb = min(CONV_ROWS, S), CONV_LANES, CONV_HALO, CONV_RB
    nt = S // rows
    rs, taps = _tap_plan([k - pad + hb for k in range(K)])
    direct = K <= CONV_DIRECT_TAPS
    n_u = 8 if direct else rows + 2 * hb - 8
    glu = glu_col0 is not None

    def body(*refs):
        if glu:
            vm, vp, vn, gm, gp, gn, w_ref, b_ref, o_ref, win, sh = refs
        else:
            vm, vp, vn, w_ref, b_ref, o_ref, win, sh = refs
        i = pl.program_id(1)

        def act(k):
            v = (vm, vp, vn)[k][...].astype(F32)
            return v * _sig((gm, gp, gn)[k][...].astype(F32)) if glu else v

        _fill_window(win, act(1), act(0), act(2), i, nt, rows)
        if not direct:
            _fill_shifts(win, sh, rs, n_u)

        def rowblk(r0):
            acc = jnp.zeros((rb, lanes), F32) + b_ref[...]
            for k, ri, q in taps:
                acc = acc + w_ref[k:k + 1, :] * _tap_rows(win, sh, rs, ri, q, r0, rb, direct)
            o_ref[pl.ds(r0, rb), :] = acc.astype(BF16)

        _row_loop(rows // rb, rb, rowblk, direct)

    in_specs = _halo_specs(S, rows, lanes, col0)
    args = [src, src, src]
    if glu:
        in_specs += _halo_specs(S, rows, lanes, glu_col0)
        args += [src, src, src]
    in_specs += [pl.BlockSpec((K, lanes), lambda j, i: (0, j)), pl.BlockSpec((1, lanes), lambda j, i: (0, j))]
    return pl.pallas_call(
        body, grid=(C // lanes, nt), out_shape=SDS((S, C), BF16), in_specs=in_specs,
        out_specs=pl.BlockSpec((rows, lanes), lambda j, i: (i, j)),
        scratch_shapes=[pltpu.VMEM((rows + 2 * hb, lanes), F32), pltpu.VMEM((len(rs), n_u, lanes), F32)],
        compiler_params=_cparams(("parallel", "parallel")), name=name)(*args, w, b)


def _dwconv_bwd(dsrcs, src, w, *, col0, glu_col0, C, name):
    S = src.shape[0]
    K = w.shape[0]
    pad = (K - 1) // 2
    rows, lanes, hb, rb = min(CONV_ROWS, S), CONV_LANES, CONV_HALO, CONV_RB
    nt = S // rows
    rs_u, taps_u = _tap_plan([k - pad + hb for k in range(K)])
    rs_d, taps_d = _tap_plan([hb + pad - k for k in range(K)])
    direct = K <= CONV_DIRECT_TAPS
    n_u = 8 if direct else rows + 2 * hb - 8
    glu = glu_col0 is not None
    nd = len(dsrcs)

    def body(*refs):
        d_refs, refs = refs[:3 * nd], refs[3 * nd:]
        if glu:
            vm, vp, vn, gm, gp, gn, w_ref = refs[:7]
            dv_ref, dg_ref, dw_ref, db_ref, dwin, dsh, uwin, ush, accw, accb = refs[7:]
        else:
            vm, vp, vn, w_ref = refs[:4]
            du_ref, dw_ref, db_ref, dwin, dsh, uwin, ush, accw, accb = refs[4:]
        i = pl.program_id(1)

        def dpiece(k):
            v = d_refs[k][...].astype(F32)
            for e in range(1, nd):
                v = v + d_refs[3 * e + k][...].astype(F32)
            return v

        def act(k):
            v = (vm, vp, vn)[k][...].astype(F32)
            return v * _sig((gm, gp, gn)[k][...].astype(F32)) if glu else v

        _fill_window(dwin, dpiece(1), dpiece(0), dpiece(2), i, nt, rows)
        _fill_window(uwin, act(1), act(0), act(2), i, nt, rows)
        if not direct:
            _fill_shifts(dwin, dsh, rs_d, n_u)
            _fill_shifts(uwin, ush, rs_u, n_u)

        @pl.when(i == 0)
        def _():
            accw[...] = jnp.zeros_like(accw)
            accb[...] = jnp.zeros_like(accb)

        def rowblk(r0):
            du = jnp.zeros((rb, lanes), F32)
            for k, ri, q in taps_d:
                du = du + w_ref[k:k + 1, :] * _tap_rows(dwin, dsh, rs_d, ri, q, r0, rb, direct)
            if glu:
                v = vm[pl.ds(r0, rb), :].astype(F32)
                s = _sig(gm[pl.ds(r0, rb), :].astype(F32))
                dv_ref[pl.ds(r0, rb), :] = (du * s).astype(BF16)
                dg_ref[pl.ds(r0, rb), :] = (du * v * s * (1.0 - s)).astype(BF16)
            else:
                du_ref[pl.ds(r0, rb), :] = du.astype(BF16)
            dmain = dwin[pl.ds(r0 + hb, rb), :] if direct else dwin[pl.ds(pl.multiple_of(r0 + hb, 8), rb), :]
            accb[...] += _rowsum8(dmain)
            for k, ri, q in taps_u:
                accw[k] += _rowsum8(dmain * _tap_rows(uwin, ush, rs_u, ri, q, r0, rb, direct))

        _row_loop(rows // rb, rb, rowblk, direct)

        @pl.when(i == nt - 1)
        def _():
            for k in range(K):
                dw_ref[k:k + 1, :] = jnp.sum(accw[k], axis=0, keepdims=True)
            db_ref[...] = jnp.sum(accb[...], axis=0, keepdims=True)

    in_specs, args = [], []
    for d in dsrcs:
        in_specs += _halo_specs(S, rows, lanes, 0)
        args += [d, d, d]
    in_specs += _halo_specs(S, rows, lanes, col0)
    args += [src, src, src]
    if glu:
        in_specs += _halo_specs(S, rows, lanes, glu_col0)
        args += [src, src, src]
    in_specs += [pl.BlockSpec((K, lanes), lambda j, i: (0, j))]
    tile = pl.BlockSpec((rows, lanes), lambda j, i: (i, j))
    n_act = 2 if glu else 1
    out_shape = [SDS((S, C), BF16)] * n_act + [SDS((K, C), F32), SDS((1, C), F32)]
    out_specs = [tile] * n_act + [pl.BlockSpec((K, lanes), lambda j, i: (0, j)),
                                  pl.BlockSpec((1, lanes), lambda j, i: (0, j))]
    return pl.pallas_call(
        body, grid=(C // lanes, nt), out_shape=out_shape, in_specs=in_specs, out_specs=out_specs,
        scratch_shapes=[pltpu.VMEM((rows + 2 * hb, lanes), F32), pltpu.VMEM((len(rs_d), n_u, lanes), F32),
                        pltpu.VMEM((rows + 2 * hb, lanes), F32), pltpu.VMEM((len(rs_u), n_u, lanes), F32),
                        pltpu.VMEM((K, 8, lanes), F32), pltpu.VMEM((8, lanes), F32)],
        compiler_params=_cparams(("parallel", "arbitrary")), name=name)(*args, w)


def _ln_parts(u, ln_w, ln_b):
    mu = jnp.mean(u, axis=-1, keepdims=True)
    xc = u - mu
    rstd = lax.rsqrt(jnp.mean(xc * xc, axis=-1, keepdims=True) + EPS)
    un = xc * rstd
    return un, rstd, un * ln_w + ln_b


def _conf_out_fwd(uc, proj, h, ln_w, ln_b, wout, *, name):
    S, E = uc.shape
    D = h.shape[1]
    tm = min(GATE_TILE, S)

    def body(uc_ref, z_ref, h_ref, lw_ref, lb_ref, w_ref, o_ref):
        _, _, ul = _ln_parts(uc_ref[...].astype(F32), lw_ref[...], lb_ref[...])
        u3 = (_silu(ul) * _silu(z_ref[...].astype(F32))).astype(BF16)
        o_ref[...] = h_ref[...] + jnp.dot(u3, w_ref[...], preferred_element_type=F32)

    return pl.pallas_call(
        body, grid=(S // tm,), out_shape=SDS((S, D), F32),
        in_specs=[pl.BlockSpec((tm, E), lambda i: (i, 0)), pl.BlockSpec((tm, E), lambda i: (i, 2)),
                  pl.BlockSpec((tm, D), lambda i: (i, 0)), _resident((1, E)), _resident((1, E)), _resident((E, D))],
        out_specs=pl.BlockSpec((tm, D), lambda i: (i, 0)),
        compiler_params=_cparams(("parallel",)), name=name)(uc, proj, h, ln_w, ln_b, wout)


def _conf_out_bwd(dh, uc, proj, ln_w, ln_b, wout, *, name):
    S, E = uc.shape
    D = dh.shape[1]
    tm = min(GATE_TILE, S)
    nt = S // tm

    def body(dh_ref, uc_ref, z_ref, lw_ref, lb_ref, w_ref, u3_ref, duc_ref, dz_ref, dlw_ref, dlb_ref, alw, alb):
        i = pl.program_id(0)

        @pl.when(i == 0)
        def _():
            alw[...] = jnp.zeros_like(alw)
            alb[...] = jnp.zeros_like(alb)

        un, rstd, ul = _ln_parts(uc_ref[...].astype(F32), lw_ref[...], lb_ref[...])
        z = z_ref[...].astype(F32)
        su, sz = _silu(ul), _silu(z)
        u3_ref[...] = (su * sz).astype(BF16)
        du3 = lax.dot_general(dh_ref[...].astype(BF16), w_ref[...], (((1,), (1,)), ((), ())),
                              preferred_element_type=F32)
        dz_ref[...] = (du3 * su * _dsilu(z)).astype(BF16)
        dul = du3 * sz * _dsilu(ul)
        alw[...] += _rowsum8(dul * un)
        alb[...] += _rowsum8(dul)
        dun = dul * lw_ref[...]
        m1 = jnp.mean(dun, axis=-1, keepdims=True)
        m2 = jnp.mean(dun * un, axis=-1, keepdims=True)
        duc_ref[...] = (rstd * (dun - m1 - un * m2)).astype(BF16)

        @pl.when(i == nt - 1)
        def _():
            dlw_ref[...] = jnp.sum(alw[...], axis=0, keepdims=True)
            dlb_ref[...] = jnp.sum(alb[...], axis=0, keepdims=True)

    tile = pl.BlockSpec((tm, E), lambda i: (i, 0))
    vec = pl.BlockSpec((1, E), lambda i: (0, 0))
    return pl.pallas_call(
        body, grid=(nt,),
        out_shape=[SDS((S, E), BF16)] * 3 + [SDS((1, E), F32)] * 2,
        in_specs=[pl.BlockSpec((tm, D), lambda i: (i, 0)), tile, pl.BlockSpec((tm, E), lambda i: (i, 2)),
                  _resident((1, E)), _resident((1, E)), _resident((E, D))],
        out_specs=[tile, tile, tile, vec, vec],
        scratch_shapes=[pltpu.VMEM((8, E), F32), pltpu.VMEM((8, E), F32)],
        compiler_params=_cparams(("arbitrary",)), name=name)(dh, uc, proj, ln_w, ln_b, wout)


def _inproj_bwd(parts, offs, w, h, nw, dh_in, *, wt, name):
    S, D = h.shape
    tm = min(ROW_TILE, S)
    nt = S // tm
    widths = [p.shape[1] for p in parts]
    npart = len(parts)

    def body(*refs):
        p_refs = refs[:npart]
        w_ref, h_ref, nw_ref, dh_ref, o_ref, hn_ref, dnw_ref, anw = refs[npart:]
        i = pl.program_id(0)

        @pl.when(i == 0)
        def _():
            anw[...] = jnp.zeros_like(anw)

        x = h_ref[...]
        r = lax.rsqrt(jnp.mean(x * x, axis=-1, keepdims=True) + EPS)
        n = x * r
        hn_ref[...] = (n * nw_ref[...]).astype(BF16)
        dhn = jnp.zeros((tm, D), F32)
        for p_ref, off, wd in zip(p_refs, offs, widths):
            for c0, c1 in _col_chunks(wd, 1024):
                if wt:
                    dhn = dhn + jnp.dot(p_ref[:, c0:c1], w_ref[off + c0:off + c1, :], preferred_element_type=F32)
                else:
                    dhn = dhn + lax.dot_general(p_ref[:, c0:c1], w_ref[:, off + c0:off + c1],
                                                (((1,), (1,)), ((), ())), preferred_element_type=F32)
        anw[...] += _rowsum8(dhn * n)
        dn = dhn * nw_ref[...]
        o_ref[...] = dh_ref[...] + r * (dn - n * jnp.mean(dn * n, axis=-1, keepdims=True))

        @pl.when(i == nt - 1)
        def _():
            dnw_ref[...] = jnp.sum(anw[...], axis=0, keepdims=True)

    tile = pl.BlockSpec((tm, D), lambda i: (i, 0))
    return pl.pallas_call(
        body, grid=(nt,),
        out_shape=[SDS((S, D), F32), SDS((S, D), BF16), SDS((1, D), F32)],
        in_specs=[pl.BlockSpec((tm, wd), lambda i: (i, 0)) for wd in widths]
        + [_resident(w.shape), tile, _resident((1, D)), tile],
        out_specs=[tile, tile, pl.BlockSpec((1, D), lambda i: (0, 0))],
        scratch_shapes=[pltpu.VMEM((8, D), F32)],
        compiler_params=_cparams(("arbitrary",)), name=name)(*parts, w, h, nw, dh_in)


def _tn_matmul(a, b, *, transpose_out, name):
    S, Ka = a.shape
    N = b.shape[1]
    tm = min(ROW_TILE, S)
    bn = min(1024, N)
    nt = S // tm

    def body(a_ref, b_ref, o_ref, acc):
        i = pl.program_id(1)

        @pl.when(i == 0)
        def _():
            acc[...] = jnp.zeros_like(acc)

        acc[...] += lax.dot_general(a_ref[...], b_ref[...].astype(BF16), (((0,), (0,)), ((), ())),
                                    preferred_element_type=F32)

        @pl.when(i == nt - 1)
        def _():
            o_ref[...] = acc[...].T if transpose_out else acc[...]

    if transpose_out:
        out_shape, out_spec = SDS((N, Ka), F32), pl.BlockSpec((bn, Ka), lambda j, i: (j, 0))
    else:
        out_shape, out_spec = SDS((Ka, N), F32), pl.BlockSpec((Ka, bn), lambda j, i: (0, j))
    return pl.pallas_call(
        body, grid=(N // bn, nt), out_shape=out_shape,
        in_specs=[pl.BlockSpec((tm, Ka), lambda j, i: (i, 0)), pl.BlockSpec((tm, bn), lambda j, i: (i, j))],
        out_specs=out_spec, scratch_shapes=[pltpu.VMEM((Ka, bn), F32)],
        compiler_params=_cparams(("parallel", "arbitrary")), name=name)(a, b)


def _loss_head(h, fnw, target, *, name):
    S, D = h.shape
    tm = min(ROW_TILE, S)
    nt = S // tm

    def body(h_ref, w_ref, t_ref, loss_ref, dh_ref, dw_ref, aw, al):
        i = pl.program_id(0)

        @pl.when(i == 0)
        def _():
            aw[...] = jnp.zeros_like(aw)
            al[...] = jnp.zeros_like(al)

        x = h_ref[...]
        r = lax.rsqrt(jnp.mean(x * x, axis=-1, keepdims=True) + EPS)
        n = x * r
        err = n * w_ref[...] - t_ref[...]
        al[...] += _rowsum8(err * err)
        dy = err * (1.0 / D)
        aw[...] += _rowsum8(dy * n)
        dn = dy * w_ref[...]
        dh_ref[...] = r * (dn - n * jnp.mean(dn * n, axis=-1, keepdims=True))

        @pl.when(i == nt - 1)
        def _():
            dw_ref[...] = jnp.sum(aw[...], axis=0, keepdims=True)
            tot = jnp.sum(jnp.sum(al[...], axis=0, keepdims=True), axis=1, keepdims=True) * (0.5 / D)
            loss_ref[...] = jnp.zeros((8, 128), F32) + tot

    tile = pl.BlockSpec((tm, D), lambda i: (i, 0))
    loss, dh, dw = pl.pallas_call(
        body, grid=(nt,), out_shape=[SDS((8, 128), F32), SDS((S, D), F32), SDS((1, D), F32)],
        in_specs=[tile, _resident((1, D)), tile],
        out_specs=[pl.BlockSpec((8, 128), lambda i: (0, 0)), tile, pl.BlockSpec((1, D), lambda i: (0, 0))],
        scratch_shapes=[pltpu.VMEM((8, D), F32), pltpu.VMEM((8, D), F32)],
        compiler_params=_cparams(("arbitrary",)), name=name)(h, fnw, target)
    return loss[0, 0], dh, dw


def _head_mask(shape, head_axis):
    hd = lax.broadcasted_iota(jnp.int32, shape, head_axis)
    ch = lax.broadcasted_iota(jnp.int32, shape, 1 - head_axis)
    return ((ch >= hd * HEADDIM) & (ch < hd * HEADDIM + HEADDIM)).astype(BF16)


def _split_dot(vals, mat):
    parts = []
    for v in vals:
        hi = v.astype(BF16)
        parts += [hi, (v - hi.astype(F32)).astype(BF16)]
    out = jnp.dot(jnp.concatenate(parts, axis=0), mat, preferred_element_type=F32)
    R = vals[0].shape[0]
    return [out[2 * R * i:2 * R * i + R] + out[2 * R * i + R:2 * R * i + 2 * R] for i in range(len(vals))]


def _nt(a, b):
    return lax.dot_general(a, b, (((1,), (1,)), ((), ())), preferred_element_type=F32)


def _tn(a, b):
    return lax.dot_general(a, b, (((0,), (0,)), ((), ())), preferred_element_type=F32)


def _ssd_setup(c_val, dtraw, bias, alog, fwd):
    xbc = _silu(c_val.astype(F32))
    pre = dtraw + bias
    dt = _softplus(pre)
    A = -jnp.exp(alog)
    row = lax.broadcasted_iota(jnp.int32, (CHUNK, CHUNK), 0)
    col = lax.broadcasted_iota(jnp.int32, (CHUNK, CHUNK), 1)
    T = (col <= row) if fwd else (col >= row)
    cum = jnp.dot(T.astype(F32), dt * A, precision=lax.Precision.HIGHEST, preferred_element_type=F32)
    total = cum[CHUNK - 1:CHUNK] if fwd else cum[0:1]
    return dict(x=xbc[:, :D_INNER], B=xbc[:, D_INNER:D_INNER + GN].astype(BF16),
                C=xbc[:, D_INNER + GN:].astype(BF16), pre=pre, dt=dt, A=A, T=T, Tt=(col >= row) if fwd else (col <= row),
                cum=cum, e=jnp.exp(cum), dtds=dt * jnp.exp(total - cum))


def _ssd_fwd_dir(c_val, dtraw, bias, alog, state, y_ref, hin_ref, fwd, ex):
    s = _ssd_setup(c_val, dtraw, bias, alog, fwd)
    x, cum, T = s["x"], s["cum"], s["T"]
    e_x, dtds_x = _split_dot([s["e"], s["dtds"]], ex)
    dec_x = e_x[CHUNK - 1:CHUNK] if fwd else e_x[0:1]
    cumT, dtT = cum.T, s["dt"].T
    xb = x.astype(BF16)
    for g in range(SSD_GROUPS):
        g0, g1 = g * HPG * HEADDIM, (g + 1) * HPG * HEADDIM
        Bg, Cg = s["B"][:, g * D_STATE:(g + 1) * D_STATE], s["C"][:, g * D_STATE:(g + 1) * D_STATE]
        CBg = _nt(Cg, Bg)
        Hg = state[:, g0:g1]
        Hb = Hg.astype(BF16)
        hin_ref[0, :, g0:g1] = Hb
        yoff = jnp.dot(Cg, Hb, preferred_element_type=F32) * e_x[:, g0:g1]
        ys = []
        for hh in range(HPG):
            h = g * HPG + hh
            Lh = jnp.exp(jnp.where(T, cum[:, h:h + 1] - cumT[h:h + 1, :], NEG))
            Mh = (CBg * Lh * dtT[h:h + 1, :]).astype(BF16)
            ys.append(jnp.dot(Mh, xb[:, h * HEADDIM:(h + 1) * HEADDIM], preferred_element_type=F32))
        y_ref[:, g0:g1] = jnp.concatenate(ys, axis=1) + yoff
        xds = (x[:, g0:g1] * dtds_x[:, g0:g1]).astype(BF16)
        state[:, g0:g1] = Hg * dec_x[:, g0:g1] + _tn(Bg, xds)


def _ssd_scan_fwd(cact, tail, bias, alog, *, name):
    S = cact.shape[0]
    nc = S // CHUNK
    E = D_INNER

    def body(cf_ref, cb_ref, tf_ref, tb_ref, bias_ref, alog_ref, yf_ref, yb_ref, hf_ref, hb_ref, state):
        @pl.when(pl.program_id(0) == 0)
        def _():
            state[...] = jnp.zeros_like(state)

        ex = _head_mask((SSD_HEADS, D_INNER), 0)
        H = SSD_HEADS
        _ssd_fwd_dir(cf_ref[...], tf_ref[:, 0:H], bias_ref[0:1, :], alog_ref[0:1, :], state.at[0], yf_ref, hf_ref, True, ex)
        _ssd_fwd_dir(cb_ref[...], tb_ref[:, H:2 * H], bias_ref[1:2, :], alog_ref[1:2, :], state.at[1], yb_ref, hb_ref, False, ex)

    up, down = (lambda i: (i, 0)), (lambda i: (nc - 1 - i, 0))
    up3, down3 = (lambda i: (i, 0, 0)), (lambda i: (nc - 1 - i, 0, 0))
    W = cact.shape[1]
    return pl.pallas_call(
        body, grid=(nc,),
        out_shape=[SDS((S, E), F32), SDS((S, E), F32), SDS((nc, D_STATE, E), BF16), SDS((nc, D_STATE, E), BF16)],
        in_specs=[pl.BlockSpec((CHUNK, W), up), pl.BlockSpec((CHUNK, W), down),
                  pl.BlockSpec((CHUNK, 128), up), pl.BlockSpec((CHUNK, 128), down),
                  _resident((2, SSD_HEADS)), _resident((2, SSD_HEADS))],
        out_specs=[pl.BlockSpec((CHUNK, E), up), pl.BlockSpec((CHUNK, E), down),
                   pl.BlockSpec((1, D_STATE, E), up3), pl.BlockSpec((1, D_STATE, E), down3)],
        scratch_shapes=[pltpu.VMEM((2, D_STATE, E), F32)],
        compiler_params=_cparams(("arbitrary",)), name=name)(cact, cact, tail, tail, bias, alog)


def _ssd_bwd_dir(c_val, dtraw, bias, alog, dy, hin_ref, dskip_x, G, dc_ref, ddt_ref, accA, accb, fwd, ex, ext):
    s = _ssd_setup(c_val, dtraw, bias, alog, fwd)
    x, cum, T, dt = s["x"], s["cum"], s["T"], s["dt"]
    e_x, dtds_x = _split_dot([s["e"], s["dtds"]], ex)
    dec_x = e_x[CHUNK - 1:CHUNK] if fwd else e_x[0:1]
    cumT, dtT = cum.T, dt.T
    xb = x.astype(BF16)
    dyf = dy.astype(F32)
    dx_diag, dx_st, dBs, dCs, gh, yoff_dy = [], [], [], [], [], []
    lane_h = lax.broadcasted_iota(jnp.int32, (CHUNK, SSD_HEADS), 1)
    sub_h = lax.broadcasted_iota(jnp.int32, (SSD_HEADS, CHUNK), 0)
    rows = jnp.zeros((CHUNK, SSD_HEADS), F32)
    colsT = jnp.zeros((SSD_HEADS, CHUNK), F32)
    for g in range(SSD_GROUPS):
        g0, g1 = g * HPG * HEADDIM, (g + 1) * HPG * HEADDIM
        Bg, Cg = s["B"][:, g * D_STATE:(g + 1) * D_STATE], s["C"][:, g * D_STATE:(g + 1) * D_STATE]
        CBg = _nt(Cg, Bg)
        Hb = hin_ref[0, :, g0:g1]
        Gg = G[:, g0:g1]
        Gb = Gg.astype(BF16)
        edyf = dyf[:, g0:g1] * e_x[:, g0:g1]
        edy = edyf.astype(BF16)
        yoff_dy.append(edyf * jnp.dot(Cg, Hb, preferred_element_type=F32))
        dC_g = _nt(edy, Hb)
        dH_g = _tn(Cg, edy)
        xds = (x[:, g0:g1] * dtds_x[:, g0:g1]).astype(BF16)
        dB_g = _nt(xds, Gb)
        dx_st.append(dtds_x[:, g0:g1] * jnp.dot(Bg, Gb, preferred_element_type=F32))
        dCB = jnp.zeros((CHUNK, CHUNK), F32)
        for hh in range(HPG):
            h = g * HPG + hh
            hs = slice(h * HEADDIM, (h + 1) * HEADDIM)
            Lh = jnp.exp(jnp.where(T, cum[:, h:h + 1] - cumT[h:h + 1, :], NEG))
            Mh = (CBg * Lh * dtT[h:h + 1, :]).astype(BF16)
            term = _nt(dy[:, hs], xb[:, hs]) * dtT[h:h + 1, :] * Lh
            dCB = dCB + term
            Wh = term * CBg
            rows = rows + jnp.where(lane_h == h, jnp.sum(Wh, axis=1, keepdims=True), 0.0)
            colsT = colsT + jnp.where(sub_h == h, jnp.sum(Wh, axis=0, keepdims=True), 0.0)
            dx_diag.append(_tn(Mh, dy[:, hs]))
        dCBb = dCB.astype(BF16)
        dCs.append(dC_g + jnp.dot(dCBb, Bg, preferred_element_type=F32))
        dBs.append(dB_g + _tn(dCBb, Cg))
        gh.append(_rowsum8(Gg * Hb.astype(F32)))
        G[:, g0:g1] = Gg * dec_x[:, g0:g1] + dH_g
    dxn = jnp.concatenate(dx_diag, axis=1) + jnp.concatenate(dx_st, axis=1)
    xst = x * jnp.concatenate(dx_st, axis=1)
    R1, Zx, Zs = _split_dot([jnp.concatenate(yoff_dy, axis=1), x * dxn, xst], ext)
    ddec = jnp.sum(_split_dot([jnp.concatenate(gh, axis=1)], ext)[0], axis=0, keepdims=True)
    dec = s["e"][CHUNK - 1:CHUNK] if fwd else s["e"][0:1]
    dtotal = jnp.sum(Zs, axis=0, keepdims=True) + dec * ddec
    rowi = lax.broadcasted_iota(jnp.int32, (CHUNK, SSD_HEADS), 0)
    dacs = rows - colsT.T + R1 - Zs + jnp.where(rowi == (CHUNK - 1 if fwd else 0), dtotal, 0.0)
    da = jnp.dot(s["Tt"].astype(F32), dacs, precision=lax.Precision.HIGHEST, preferred_element_type=F32)
    accA[...] += _rowsum8(da * dt)
    sg = _sig(s["pre"])
    ratio = jnp.where(dt > 1e-30, sg / jnp.maximum(dt, 1e-30), 1.0)
    ddtraw = da * s["A"] * sg + Zx * ratio
    ddt_ref[...] = ddtraw
    accb[...] += _rowsum8(ddtraw)
    dx = dxn + dyf * dskip_x if fwd else dxn
    dact = jnp.concatenate([dx] + dBs + dCs, axis=1)
    dc_ref[...] = (dact * _dsilu(c_val.astype(F32))).astype(BF16)


def _ssd_scan_bwd(cact, tail, bias, alog, dy, hf, hb, dskip_x, *, name):
    S = cact.shape[0]
    nc = S // CHUNK
    E, H, W = D_INNER, SSD_HEADS, cact.shape[1]

    def body(cf_ref, cb_ref, tf_ref, tb_ref, bias_ref, alog_ref, dyf_ref, dyb_ref, hf_ref, hb_ref,
             dsk_ref, dcf_ref, dcb_ref, ddtf_ref, ddtb_ref, dalog_ref, dbias_ref, G, accA, accb):
        i = pl.program_id(0)

        @pl.when(i == 0)
        def _():
            G[...] = jnp.zeros_like(G)
            accA[...] = jnp.zeros_like(accA)
            accb[...] = jnp.zeros_like(accb)

        ex = _head_mask((H, E), 0)
        ext = _head_mask((E, H), 1)
        _ssd_bwd_dir(cf_ref[...], tf_ref[:, 0:H], bias_ref[0:1, :], alog_ref[0:1, :], dyf_ref[...], hf_ref,
                     dsk_ref[...], G.at[0], dcf_ref, ddtf_ref, accA.at[0], accb.at[0], True, ex, ext)
        _ssd_bwd_dir(cb_ref[...], tb_ref[:, H:2 * H], bias_ref[1:2, :], alog_ref[1:2, :], dyb_ref[...], hb_ref,
                     dsk_ref[...], G.at[1], dcb_ref, ddtb_ref, accA.at[1], accb.at[1], False, ex, ext)

        @pl.when(i == nc - 1)
        def _():
            for d in range(2):
                A = -jnp.exp(alog_ref[d:d + 1, :])
                dalog_ref[d:d + 1, :] = jnp.sum(accA[d], axis=0, keepdims=True) * A
                dbias_ref[d:d + 1, :] = jnp.sum(accb[d], axis=0, keepdims=True)

    up, down = (lambda i: (i, 0)), (lambda i: (nc - 1 - i, 0))
    up3, down3 = (lambda i: (i, 0, 0)), (lambda i: (nc - 1 - i, 0, 0))
    small = pl.BlockSpec((2, H), lambda i: (0, 0))
    return pl.pallas_call(
        body, grid=(nc,),
        out_shape=[SDS((S, W), BF16), SDS((S, W), BF16), SDS((S, H), F32), SDS((S, H), F32),
                   SDS((2, H), F32), SDS((2, H), F32)],
        in_specs=[pl.BlockSpec((CHUNK, W), down), pl.BlockSpec((CHUNK, W), up),
                  pl.BlockSpec((CHUNK, 128), down), pl.BlockSpec((CHUNK, 128), up),
                  _resident((2, H)), _resident((2, H)),
                  pl.BlockSpec((CHUNK, E), down), pl.BlockSpec((CHUNK, E), up),
                  pl.BlockSpec((1, D_STATE, E), down3), pl.BlockSpec((1, D_STATE, E), up3),
                  _resident((1, E))],
        out_specs=[pl.BlockSpec((CHUNK, W), down), pl.BlockSpec((CHUNK, W), up),
                   pl.BlockSpec((CHUNK, H), down), pl.BlockSpec((CHUNK, H), up), small, small],
        scratch_shapes=[pltpu.VMEM((2, D_STATE, E), F32), pltpu.VMEM((2, 8, H), F32), pltpu.VMEM((2, 8, H), F32)],
        compiler_params=_cparams(("arbitrary",)), name=name)(
            cact, cact, tail, tail, bias, alog, dy, dy, hf, hb, dskip_x)


def _ssd_gate(yf, yb, cx, z, dsk, nw):
    xs = _silu(cx.astype(F32))
    y = yf + yb + xs * dsk
    sz = _silu(z)
    gt = y * sz
    r = lax.rsqrt(jnp.mean(gt * gt, axis=-1, keepdims=True) + EPS)
    return xs, y, sz, r, gt * r


def _ssd_out_fwd(yf, yb, cact, proj, dskip_x, nw, h, wout, *, name):
    S, E = yf.shape
    D = h.shape[1]
    tm = min(GATE_TILE, S)

    def body(yf_ref, yb_ref, cx_ref, z_ref, dsk_ref, nw_ref, h_ref, w_ref, o_ref):
        _, _, _, _, n = _ssd_gate(yf_ref[...], yb_ref[...], cx_ref[...], z_ref[...].astype(F32), dsk_ref[...], nw_ref[...])
        o_ref[...] = h_ref[...] + jnp.dot((n * nw_ref[...]).astype(BF16), w_ref[...], preferred_element_type=F32)

    tile = pl.BlockSpec((tm, E), lambda i: (i, 0))
    return pl.pallas_call(
        body, grid=(S // tm,), out_shape=SDS((S, D), F32),
        in_specs=[tile, tile, tile, tile, _resident((1, E)), _resident((1, E)),
                  pl.BlockSpec((tm, D), lambda i: (i, 0)), _resident((E, D))],
        out_specs=pl.BlockSpec((tm, D), lambda i: (i, 0)),
        compiler_params=_cparams(("parallel",)), name=name)(yf, yb, cact, proj, dskip_x, nw, h, wout)


def _ssd_out_bwd(dh, yf, yb, cact, proj, dskip_x, nw, wout, *, name):
    S, E = yf.shape
    D = dh.shape[1]
    tm = min(GATE_TILE, S)
    nt = S // tm

    def body(dh_ref, yf_ref, yb_ref, cx_ref, z_ref, dsk_ref, nw_ref, w_ref, yn_ref, dy_ref, dz_ref, dnw_ref, dd_ref, anw, ad):
        i = pl.program_id(0)

        @pl.when(i == 0)
        def _():
            anw[...] = jnp.zeros_like(anw)
            ad[...] = jnp.zeros_like(ad)

        z = z_ref[...].astype(F32)
        xs, y, sz, r, n = _ssd_gate(yf_ref[...], yb_ref[...], cx_ref[...], z, dsk_ref[...], nw_ref[...])
        yn_ref[...] = (n * nw_ref[...]).astype(BF16)
        dyn = _nt(dh_ref[...].astype(BF16), w_ref[...])
        anw[...] += _rowsum8(dyn * n)
        dn = dyn * nw_ref[...]
        dgt = r * (dn - n * jnp.mean(dn * n, axis=-1, keepdims=True))
        dy = dgt * sz
        dy_ref[...] = dy.astype(BF16)
        dz_ref[...] = (dgt * y * _dsilu(z)).astype(BF16)
        ad[...] += _rowsum8(dy * xs)

        @pl.when(i == nt - 1)
        def _():
            dnw_ref[...] = jnp.sum(anw[...], axis=0, keepdims=True)
            dd_ref[...] = jnp.sum(ad[...], axis=0, keepdims=True)

    tile = pl.BlockSpec((tm, E), lambda i: (i, 0))
    vec = pl.BlockSpec((1, E), lambda i: (0, 0))
    return pl.pallas_call(
        body, grid=(nt,), out_shape=[SDS((S, E), BF16)] * 3 + [SDS((1, E), F32)] * 2,
        in_specs=[pl.BlockSpec((tm, D), lambda i: (i, 0)), tile, tile, tile, tile,
                  _resident((1, E)), _resident((1, E)), _resident((E, D))],
        out_specs=[tile, tile, tile, vec, vec],
        scratch_shapes=[pltpu.VMEM((8, E), F32), pltpu.VMEM((8, E), F32)],
        compiler_params=_cparams(("arbitrary",)), name=name)(dh, yf, yb, cact, proj, dskip_x, nw, wout)


def _local_step(x, target, p):
    E, S = D_INNER, x.shape[0]
    depth = p["norm_w"].shape[0]
    h, saved = x, []
    for i in range(depth):
        j, nw = i // 2, p["norm_w"][i][None]
        if i % 2 == 0:
            proj = _inproj_fwd(h, nw, p["cm_w_in"][j], tail=False, wt=False, name="cm_in_fwd")
            uc = _dwconv_fwd(proj, p["cm_dw_w"][j], p["cm_dw_b"][j][None], col0=0, glu_col0=E, C=E, name="cm_conv_fwd")
            hn = _conf_out_fwd(uc, proj, h, p["cm_ln_w"][j][None], p["cm_ln_b"][j][None], p["cm_w_out"][j], name="cm_out_fwd")
            saved.append((h, proj, uc))
        else:
            proj, tail = _inproj_fwd(h, nw, p["ssd_w_in"][j], tail=True, wt=True, name="ssd_in_fwd")
            cact = _dwconv_fwd(proj, p["ssd_conv_w"][j], p["ssd_conv_b"][j][None], col0=E, glu_col0=None,
                               C=SSD_CONV_DIM, name="ssd_conv_fwd")
            yf, yb, hf, hb = _ssd_scan_fwd(cact, tail, p["ssd_dt_bias"][j], p["ssd_A_log"][j], name="ssd_scan_fwd")
            dsk = jnp.repeat(p["ssd_D"][j], HEADDIM)[None]
            hn = _ssd_out_fwd(yf, yb, cact, proj, dsk, p["ssd_norm_w"][j][None], h, p["ssd_w_out"][j], name="ssd_out_fwd")
            saved.append((h, proj, tail, cact, yf, yb, hf, hb, dsk))
        h = hn
    loss, dh, g_fnw = _loss_head(h, p["final_norm_w"][None], target, name="loss_head")

    names = ["norm_w", "cm_w_in", "cm_dw_w", "cm_dw_b", "cm_ln_w", "cm_ln_b", "cm_w_out", "ssd_w_in", "ssd_conv_w",
             "ssd_conv_b", "ssd_dt_bias", "ssd_A_log", "ssd_D", "ssd_norm_w", "ssd_w_out"]
    g = {n: [None] * p[n].shape[0] for n in names}
    for i in reversed(range(depth)):
        j, nw = i // 2, p["norm_w"][i][None]
        if i % 2 == 0:
            h_in, proj, uc = saved[i]
            u3, duc, dz, g_lw, g_lb = _conf_out_bwd(dh, uc, proj, p["cm_ln_w"][j][None], p["cm_ln_b"][j][None],
                                                    p["cm_w_out"][j], name="cm_out_bwd")
            g["cm_w_out"][j] = _tn_matmul(u3, dh, transpose_out=False, name="cm_wout_grad")
            dv, dg, g_dw, g_db = _dwconv_bwd([duc], proj, p["cm_dw_w"][j], col0=0, glu_col0=E, C=E, name="cm_conv_bwd")
            parts = [dv, dg, dz]
            dh, hnb, g_nw = _inproj_bwd(parts, [0, E, 2 * E], p["cm_w_in"][j], h_in, nw, dh, wt=False, name="cm_in_bwd")
            g["cm_w_in"][j] = jnp.concatenate(
                [_tn_matmul(hnb, q, transpose_out=False, name="cm_win_grad") for q in parts], axis=1)
            g["cm_dw_w"][j], g["cm_dw_b"][j], g["cm_ln_w"][j], g["cm_ln_b"][j] = g_dw, g_db[0], g_lw[0], g_lb[0]
        else:
            h_in, proj, tail, cact, yf, yb, hf, hb, dsk = saved[i]
            yn, dy, dz, g_sn, g_dx = _ssd_out_bwd(dh, yf, yb, cact, proj, dsk, p["ssd_norm_w"][j][None],
                                                  p["ssd_w_out"][j], name="ssd_out_bwd")
            g["ssd_w_out"][j] = _tn_matmul(yn, dh, transpose_out=False, name="ssd_wout_grad")
            dcf, dcb, ddtf, ddtb, g_al, g_bias = _ssd_scan_bwd(cact, tail, p["ssd_dt_bias"][j], p["ssd_A_log"][j],
                                                               dy, hf, hb, dsk, name="ssd_scan_bwd")
            dxbc, g_cw, g_cb = _dwconv_bwd([dcf, dcb], proj, p["ssd_conv_w"][j], col0=E, glu_col0=None,
                                           C=SSD_CONV_DIM, name="ssd_conv_bwd")
            ddt = jnp.concatenate([ddtf, ddtb, jnp.zeros((S, 128 - 2 * SSD_HEADS), F32)], axis=1).astype(BF16)
            parts = [dz, dxbc, ddt]
            dh, hnb, g_nw = _inproj_bwd(parts, [0, E, E + SSD_CONV_DIM], p["ssd_w_in"][j], h_in, nw, dh, wt=True,
                                        name="ssd_in_bwd")
            gw = [_tn_matmul(hnb, q, transpose_out=True, name="ssd_win_grad") for q in parts]
            g["ssd_w_in"][j] = jnp.concatenate([gw[0], gw[1], gw[2][:2 * SSD_HEADS]], axis=0)
            g["ssd_conv_w"][j], g["ssd_conv_b"][j], g["ssd_norm_w"][j] = g_cw, g_cb[0], g_sn[0]
            g["ssd_dt_bias"][j], g["ssd_A_log"][j] = g_bias, g_al
            g["ssd_D"][j] = jnp.sum(g_dx.reshape(SSD_HEADS, HEADDIM), axis=1)
        g["norm_w"][i] = g_nw[0]
    grads = {n: jnp.stack(v) for n, v in g.items()}
    grads["final_norm_w"] = g_fnw[0]
    return loss, dh, grads


def _exchange(items, *, name):
    n = len(items)
    flips = [(fx, fy, fc) for fx in (0, 1) for fy in (0, 1) for fc in (0, 1)][1:]

    def body(*refs):
        ins, outs = refs[:n], refs[n:2 * n]
        send_sems, recv_sems, local_sems = refs[2 * n:]
        x, y, c = lax.axis_index("x"), lax.axis_index("y"), lax.axis_index("c")
        me = 4 * x + 2 * y + c
        started = []
        for b, (_, scatter) in enumerate(items):
            own = ins[b].at[me] if scatter else ins[b]
            loc = pltpu.make_async_copy(own, outs[b].at[me], local_sems.at[b])
            loc.start()
            started.append(loc)
        copies = []
        for b, (_, scatter) in enumerate(items):
            for k, (fx, fy, fc) in enumerate(flips):
                px, py, pc = x ^ fx, y ^ fy, c ^ fc
                peer = 4 * px + 2 * py + pc
                cp = pltpu.make_async_remote_copy(
                    src_ref=ins[b].at[peer] if scatter else ins[b], dst_ref=outs[b].at[me],
                    send_sem=send_sems.at[b * 7 + k], recv_sem=recv_sems.at[b * 7 + k],
                    device_id=(px, py, pc), device_id_type=pl.DeviceIdType.MESH)
                cp.start()
                arrival = pltpu.make_async_remote_copy(
                    src_ref=ins[b].at[peer] if scatter else ins[b], dst_ref=outs[b].at[peer],
                    send_sem=send_sems.at[b * 7 + k], recv_sem=recv_sems.at[b * 7 + k],
                    device_id=(px, py, pc), device_id_type=pl.DeviceIdType.MESH)
                copies.append((cp, arrival))
        for cp, arrival in copies:
            arrival.wait_recv()
        for cp, arrival in copies:
            cp.wait_send()
        for loc in started:
            loc.wait()

    hbm = pl.BlockSpec(memory_space=pltpu.HBM)
    out_shape = [SDS((N_DEV,) + tuple(a.shape[-2:]), a.dtype) for a, _ in items]
    return pl.pallas_call(
        body, out_shape=out_shape, in_specs=[hbm] * n, out_specs=[hbm] * n,
        scratch_shapes=[pltpu.SemaphoreType.DMA((7 * n,)), pltpu.SemaphoreType.DMA((7 * n,)),
                        pltpu.SemaphoreType.DMA((n,))],
        name=name)(*[a for a, _ in items])


def _reduce_slots(recv, *, rows, name):
    n, R, C = recv.shape

    def body(r_ref, g_ref):
        g = r_ref[0].astype(F32)
        for s in range(1, n):
            g = g + r_ref[s].astype(F32)
        g_ref[...] = g

    return pl.pallas_call(
        body, grid=(R // rows,), out_shape=SDS((R, C), F32),
        in_specs=[pl.BlockSpec((n, rows, C), lambda i: (0, i, 0))],
        out_specs=pl.BlockSpec((rows, C), lambda i: (i, 0)),
        compiler_params=_cparams(("parallel",)), name=name)(recv)


def _reduce_adamw(recv, w, m, v, *, rows, name):
    R, C = w.shape
    n = recv.shape[0]
    c1 = 1.0 / (1.0 - ADAM_B1 ** ADAM_STEP)
    c2 = 1.0 / (1.0 - ADAM_B2 ** ADAM_STEP)

    def body(r_ref, w_ref, m_ref, v_ref, g_ref, d_ref, mo_ref, vo_ref):
        g = r_ref[0].astype(F32)
        for s in range(1, n):
            g = g + r_ref[s].astype(F32)
        mn = ADAM_B1 * m_ref[...] + (1.0 - ADAM_B1) * g
        vn = ADAM_B2 * v_ref[...] + (1.0 - ADAM_B2) * (g * g)
        g_ref[...] = g
        mo_ref[...] = mn
        vo_ref[...] = vn
        d_ref[...] = -ADAM_LR * ((mn * c1) / (jnp.sqrt(vn * c2) + ADAM_EPS) + ADAM_WD * w_ref[...])

    tile = pl.BlockSpec((rows, C), lambda i: (i, 0))
    return pl.pallas_call(
        body, grid=(R // rows,), out_shape=[SDS((R, C), F32)] * 4,
        in_specs=[pl.BlockSpec((n, rows, C), lambda i: (0, i, 0)), tile, tile, tile],
        out_specs=[tile] * 4, compiler_params=_cparams(("parallel",)), name=name)(recv, w, m, v)


_ORDER = ["norm_w", "final_norm_w", "cm_w_in", "cm_dw_w", "cm_dw_b", "cm_ln_w", "cm_ln_b", "cm_w_out", "ssd_w_in",
          "ssd_conv_w", "ssd_conv_b", "ssd_dt_bias", "ssd_A_log", "ssd_D", "ssd_norm_w", "ssd_w_out"]
_SMALL = (["cm_dw_w", "ssd_conv_w", "ssd_conv_b", "ssd_norm_w"], 168)
_REP = (["norm_w", "final_norm_w", "cm_dw_b", "cm_ln_w", "cm_ln_b", "ssd_dt_bias", "ssd_A_log", "ssd_D"], 144)
_SHARD_AXIS = {"cm_w_in": 2, "cm_dw_w": 2, "cm_w_out": 1, "ssd_w_in": 2, "ssd_conv_w": 2, "ssd_conv_b": 1,
               "ssd_norm_w": 1, "ssd_w_out": 1}
OPT_ROWS = 256
OPT_ROWS_T = 432


def _pack(arrs, rows, lanes, dtype, lead=()):
    nl = len(lead)
    flat = jnp.concatenate([a.reshape(lead + (-1,)).astype(dtype) for a in arrs], axis=nl)
    flat = jnp.pad(flat, [(0, 0)] * nl + [(0, rows * lanes - flat.shape[nl])])
    return flat.reshape(lead + (rows, lanes))


def _unpack(buf, shapes, lead=()):
    nl = len(lead)
    flat = buf.reshape(lead + (-1,))
    out, off = [], 0
    for s in shapes:
        sz = math.prod(s)
        out.append(flat[..., off:off + sz].reshape(lead + tuple(s)))
        off += sz
    return out


def _join_shards(parts, axis):
    full = jnp.moveaxis(parts, 0, axis)
    sh = full.shape
    return full.reshape(sh[:axis] + (sh[axis] * sh[axis + 1],) + sh[axis + 2:])


def _split_shards(full, axis):
    sh = full.shape
    parts = full.reshape(sh[:axis] + (N_DEV, sh[axis] // N_DEV) + sh[axis + 1:])
    return jnp.moveaxis(parts, axis, 0)


def kernel(x, norm_w, final_norm_w, cm_w_in, cm_dw_w, cm_dw_b, cm_ln_w, cm_ln_b, cm_w_out, ssd_w_in, ssd_conv_w, ssd_conv_b, ssd_dt_bias, ssd_A_log, ssd_D, ssd_norm_w, ssd_w_out, loss_target, m_norm_w, m_final_norm_w, m_cm_w_in, m_cm_dw_w, m_cm_dw_b, m_cm_ln_w, m_cm_ln_b, m_cm_w_out, m_ssd_w_in, m_ssd_conv_w, m_ssd_conv_b, m_ssd_dt_bias, m_ssd_A_log, m_ssd_D, m_ssd_norm_w, m_ssd_w_out, v_norm_w, v_final_norm_w, v_cm_w_in, v_cm_dw_w, v_cm_dw_b, v_cm_ln_w, v_cm_ln_b, v_cm_w_out, v_ssd_w_in, v_ssd_conv_w, v_ssd_conv_b, v_ssd_dt_bias, v_ssd_A_log, v_ssd_D, v_ssd_norm_w, v_ssd_w_out):
    w = dict(zip(_ORDER, (norm_w, final_norm_w, cm_w_in, cm_dw_w, cm_dw_b, cm_ln_w, cm_ln_b, cm_w_out, ssd_w_in,
                          ssd_conv_w, ssd_conv_b, ssd_dt_bias, ssd_A_log, ssd_D, ssd_norm_w, ssd_w_out)))
    m = dict(zip(_ORDER, (m_norm_w, m_final_norm_w, m_cm_w_in, m_cm_dw_w, m_cm_dw_b, m_cm_ln_w, m_cm_ln_b, m_cm_w_out,
                          m_ssd_w_in, m_ssd_conv_w, m_ssd_conv_b, m_ssd_dt_bias, m_ssd_A_log, m_ssd_D, m_ssd_norm_w,
                          m_ssd_w_out)))
    v = dict(zip(_ORDER, (v_norm_w, v_final_norm_w, v_cm_w_in, v_cm_dw_w, v_cm_dw_b, v_cm_ln_w, v_cm_ln_b, v_cm_w_out,
                          v_ssd_w_in, v_ssd_conv_w, v_ssd_conv_b, v_ssd_dt_bias, v_ssd_A_log, v_ssd_D, v_ssd_norm_w,
                          v_ssd_w_out)))

    D, nl = D_MODEL, cm_w_in.shape[0]
    (sn, sr), (rn, rr) = _SMALL, _REP
    mats = ["cm_w_in", "cm_w_out", "ssd_w_in", "ssd_w_out"]
    lanes = {"cm_w_in": cm_w_in.shape[2], "cm_w_out": D, "ssd_w_in": D, "ssd_w_out": D}

    own = dict(w)
    own["ssd_w_in"] = jnp.swapaxes(ssd_w_in, 1, 2)
    gathered = _exchange([(own[k].reshape(-1, lanes[k]).astype(BF16), False) for k in mats]
                         + [(_pack([w[k] for k in sn], sr, 128, F32), False)], name="gather_weights")
    p = {k: w[k] for k in rn}
    p["cm_w_in"] = _join_shards(gathered[0].reshape(N_DEV, nl, D, -1), 2)
    p["cm_w_out"] = _join_shards(gathered[1].reshape(N_DEV, nl, -1, D), 1)
    wt_in = _join_shards(gathered[2].astype(F32).reshape(N_DEV, nl, -1, D), 1)
    p["ssd_w_in"] = jnp.pad(wt_in, ((0, 0), (0, SSD_IN_PAD - SSD_IN), (0, 0))).astype(BF16)
    p["ssd_w_out"] = _join_shards(gathered[3].reshape(N_DEV, nl, -1, D), 1)
    for k, parts in zip(sn, _unpack(gathered[4], [w[k].shape for k in sn], lead=(N_DEV,))):
        p[k] = _join_shards(parts, _SHARD_AXIS[k])

    loss, dx, g = _local_step(x[0], loss_target[0], p)
    loss = lax.psum(loss, ("x", "y", "c"))

    axis = dict(_SHARD_AXIS, ssd_w_in=1)
    send = [(_split_shards(g[k], axis[k]).reshape(N_DEV, -1, lanes[k]).astype(BF16), True) for k in mats]
    send.append((_pack([_split_shards(g[k], _SHARD_AXIS[k]) for k in sn], sr, 128, F32, lead=(N_DEV,)), True))
    send.append((_pack([g[k] for k in rn], rr, 128, F32), False))
    recv = _exchange(send, name="exchange_grads")

    outs = {}

    def optimizer(name, r, rows):
        shp = w[name].shape
        res = _reduce_adamw(r, *[d[name].reshape(-1, shp[-1]) for d in (w, m, v)], rows=rows, name="adamw_" + name)
        for kind, a in zip(("g", "d", "m", "v"), res):
            outs[kind, name] = a.reshape(shp)

    for k, r in zip(mats, recv):
        if k == "ssd_w_in":
            gt = _reduce_slots(r, rows=min(OPT_ROWS_T, r.shape[1]), name="sum_ssd_w_in")
            r = jnp.swapaxes(gt.reshape(nl, -1, D), 1, 2).reshape(1, nl * D, -1)
        optimizer(k, r, min(OPT_ROWS, r.shape[1]))
    for (names, rows), r, nm in ((_SMALL, recv[4], "adamw_small"), (_REP, recv[5], "adamw_rep")):
        res = _reduce_adamw(r, *[_pack([d[k] for k in names], rows, 128, F32) for d in (w, m, v)], rows=rows, name=nm)
        for kind, buf in zip(("g", "d", "m", "v"), res):
            for k, a in zip(names, _unpack(buf, [w[k].shape for k in names])):
                outs[kind, k] = a
    return (loss, dx[None], *[outs[kind, k] for kind in ("g", "d", "m", "v") for k in _ORDER])
```

```python
import functools
import math

import jax
import jax.numpy as jnp
from jax import lax
from jax.experimental import pallas as pl
from jax.experimental.pallas import tpu as pltpu

F32, BF16 = jnp.float32, jnp.bfloat16
SDS = jax.ShapeDtypeStruct

D_MODEL = 1024
D_INNER = 2048
CONF_KERNEL = 31
HEADDIM = 64
SSD_HEADS = 32
SSD_GROUPS = 4
HPG = SSD_HEADS // SSD_GROUPS
D_STATE = 128
SSD_CONV = 5
CHUNK = 128
GN = SSD_GROUPS * D_STATE
SSD_CONV_DIM = D_INNER + 2 * GN
SSD_IN = D_INNER + SSD_CONV_DIM + 2 * SSD_HEADS
SSD_IN_PAD = SSD_IN + 64
EPS = 1e-5
N_DEV = 8

ADAM_LR, ADAM_B1, ADAM_B2, ADAM_EPS, ADAM_WD, ADAM_STEP = 0.001, 0.9, 0.999, 1e-08, 0.01, 10

VMEM_LIMIT = 56 * 1024 * 1024
ROW_TILE = 512
GATE_TILE = 256
CONV_ROWS = 512
CONV_LANES = 512
CONV_HALO = 16
CONV_RB = 16
CONV_LANES_FEW = 1024
CONV_RB_FEW = 16
CONV_DIRECT_TAPS = 8
NEG = -1e30


def _cparams(sem):
    return pltpu.CompilerParams(dimension_semantics=sem, vmem_limit_bytes=VMEM_LIMIT)


def _resident(shape):
    nd = len(shape)
    return pl.BlockSpec(shape, lambda *_: (0,) * nd, pipeline_mode=pl.Buffered(1))


def _sig(x):
    return 1.0 / (1.0 + jnp.exp(-x))


def _silu(x):
    return x * _sig(x)


def _dsilu(x):
    s = _sig(x)
    return s * (1.0 + x * (1.0 - s))


def _softplus(x):
    return jnp.maximum(x, 0.0) + jnp.log(1.0 + jnp.exp(-jnp.abs(x)))


def _col_chunks(n, width):
    out, c = [], 0
    while c < n:
        out.append((c, min(c + width, n)))
        c += width
    return out


def _rowsum8(v):
    acc = v[0:8]
    for j in range(1, v.shape[0] // 8):
        acc = acc + v[8 * j:8 * j + 8]
    return acc


def _inproj_fwd(h, nw, w, *, tail, wt, name, carry=None):
    S, D = h.shape
    N = w.shape[0] if wt else w.shape[1]
    tm = min(ROW_TILE, S)
    chunks = _col_chunks(N, 1024)

    def body(h_ref, nw_ref, w_ref, o_ref, *rest):
        x = h_ref[...]
        r = lax.rsqrt(jnp.mean(x * x, axis=-1, keepdims=True) + EPS)
        hn = (x * r * nw_ref[...]).astype(BF16)
        for c0, c1 in chunks:
            if wt:
                acc = lax.dot_general(hn, w_ref[c0:c1, :], (((1,), (1,)), ((), ())), preferred_element_type=F32)
            else:
                acc = jnp.dot(hn, w_ref[:, c0:c1], preferred_element_type=F32)
            o_ref[:, c0:c1] = acc.astype(BF16)
            if tail and c1 == N:
                rest[0][...] = acc[:, acc.shape[1] - 128:]

    out_shape = [SDS((S, N), BF16)]
    out_specs = [pl.BlockSpec((tm, N), lambda i: (i, 0))]
    if tail:
        out_shape.append(SDS((S, 128), F32))
        out_specs.append(pl.BlockSpec((tm, 128), lambda i: (i, 0)))
    res, got = _call(
        body, grid=(S // tm,), out_shape=out_shape,
        in_specs=[pl.BlockSpec((tm, D), lambda i: (i, 0)), _resident((1, D)), _resident(w.shape)],
        out_specs=out_specs, scratch_shapes=[], sem=("parallel",), name=name, args=(h, nw, w), carry=carry)
    return (res if tail else res[0]), got


def _tap_plan(offsets):
    rs = sorted({s % 8 for s in offsets})
    return rs, [(k, rs.index(s % 8), s // 8) for k, s in enumerate(offsets)]


def _halo_specs(S, rows, lanes, col0):
    per, last = rows // CONV_HALO, S // CONV_HALO - 1
    cb = col0 // lanes
    return [pl.BlockSpec((rows, lanes), lambda j, i: (i, cb + j)),
            pl.BlockSpec((CONV_HALO, lanes), lambda j, i: (jnp.maximum(i * per - 1, 0), cb + j)),
            pl.BlockSpec((CONV_HALO, lanes), lambda j, i: (jnp.minimum((i + 1) * per, last), cb + j))]


def _fill_window(win, prev, main, nxt, i, nt, rows):
    hb = CONV_HALO
    win[0:hb, :] = jnp.where(i > 0, prev, 0.0)
    win[hb:hb + rows, :] = main
    win[hb + rows:hb + rows + hb, :] = jnp.where(i < nt - 1, nxt, 0.0)


def _fill_shifts(win, sh, rs, n_u):
    for idx, r in enumerate(rs):
        sh[idx] = win[pl.ds(r, n_u), :]


def _tap_rows(win, sh, rs, ri, q, r0, rb, direct):
    if direct:
        return win[pl.ds(r0 + 8 * q + rs[ri], rb), :]
    return sh[ri, pl.ds(pl.multiple_of(r0 + 8 * q, 8), rb), :]


def _row_loop(n, rb, fn, direct):
    if direct:
        for t in range(n):
            fn(t * rb)
    else:
        def step(t, carry):
            fn(pl.multiple_of(t * rb, rb))
            return carry

        lax.fori_loop(0, n, step, 0)


def _conv_blocking(K):
    return (CONV_LANES_FEW, CONV_RB_FEW) if K <= CONV_DIRECT_TAPS else (CONV_LANES, CONV_RB)


def _dwconv_fwd(src, w, b, *, col0, glu_col0, C, name, carry=None):
    S = src.shape[0]
    K = w.shape[0]
    pad = (K - 1) // 2
    rows, hb = min(CONV_ROWS, S), CONV_HALO
    lanes, rb = _conv_blocking(K)
    nt = S // rows
    rs, taps = _tap_plan([k - pad + hb for k in range(K)])
    direct = K <= CONV_DIRECT_TAPS
    n_u = 8 if direct else rows + 2 * hb - 8
    glu = glu_col0 is not None

    def body(*refs):
        if glu:
            vm, vp, vn, gm, gp, gn, w_ref, b_ref, o_ref, win, sh = refs
        else:
            vm, vp, vn, w_ref, b_ref, o_ref, win, sh = refs
        i = pl.program_id(1)

        def act(k):
            v = (vm, vp, vn)[k][...].astype(F32)
            return v * _sig((gm, gp, gn)[k][...].astype(F32)) if glu else v

        _fill_window(win, act(1), act(0), act(2), i, nt, rows)
        if not direct:
            _fill_shifts(win, sh, rs, n_u)

        def rowblk(r0):
            acc = jnp.zeros((rb, lanes), F32) + b_ref[...]
            for k, ri, q in taps:
                acc = acc + w_ref[k:k + 1, :] * _tap_rows(win, sh, rs, ri, q, r0, rb, direct)
            o_ref[pl.ds(r0, rb), :] = acc.astype(BF16)

        _row_loop(rows // rb, rb, rowblk, direct)

    in_specs = _halo_specs(S, rows, lanes, col0)
    args = [src, src, src]
    if glu:
        in_specs += _halo_specs(S, rows, lanes, glu_col0)
        args += [src, src, src]
    in_specs += [pl.BlockSpec((K, lanes), lambda j, i: (0, j)), pl.BlockSpec((1, lanes), lambda j, i: (0, j))]
    res, got = _call(
        body, grid=(C // lanes, nt), out_shape=[SDS((S, C), BF16)], in_specs=in_specs,
        out_specs=[pl.BlockSpec((rows, lanes), lambda j, i: (i, j))],
        scratch_shapes=[pltpu.VMEM((rows + 2 * hb, lanes), F32), pltpu.VMEM((len(rs), n_u, lanes), F32)],
        sem=("parallel", "parallel"), name=name, args=(*args, w, b), carry=carry)
    return res[0], got


def _dwconv_bwd(dsrcs, src, w, *, col0, glu_col0, C, name, carry=None):
    S = src.shape[0]
    K = w.shape[0]
    pad = (K - 1) // 2
    rows, hb = min(CONV_ROWS, S), CONV_HALO
    lanes, rb = _conv_blocking(K)
    nt = S // rows
    rs_u, taps_u = _tap_plan([k - pad + hb for k in range(K)])
    rs_d, taps_d = _tap_plan([hb + pad - k for k in range(K)])
    direct = K <= CONV_DIRECT_TAPS
    n_u = 8 if direct else rows + 2 * hb - 8
    glu = glu_col0 is not None
    nd = len(dsrcs)

    def body(*refs):
        d_refs, refs = refs[:3 * nd], refs[3 * nd:]
        if glu:
            vm, vp, vn, gm, gp, gn, w_ref = refs[:7]
            dv_ref, dg_ref, dw_ref, db_ref, dwin, dsh, uwin, ush, accw, accb = refs[7:]
        else:
            vm, vp, vn, w_ref = refs[:4]
            du_ref, dw_ref, db_ref, dwin, dsh, uwin, ush, accw, accb = refs[4:]
        i = pl.program_id(1)

        def dpiece(k):
            v = d_refs[k][...].astype(F32)
            for e in range(1, nd):
                v = v + d_refs[3 * e + k][...].astype(F32)
            return v

        def act(k):
            v = (vm, vp, vn)[k][...].astype(F32)
            return v * _sig((gm, gp, gn)[k][...].astype(F32)) if glu else v

        _fill_window(dwin, dpiece(1), dpiece(0), dpiece(2), i, nt, rows)
        _fill_window(uwin, act(1), act(0), act(2), i, nt, rows)
        if not direct:
            _fill_shifts(dwin, dsh, rs_d, n_u)
            _fill_shifts(uwin, ush, rs_u, n_u)

        @pl.when(i == 0)
        def _():
            accw[...] = jnp.zeros_like(accw)
            accb[...] = jnp.zeros_like(accb)

        def rowblk(r0):
            du = jnp.zeros((rb, lanes), F32)
            for k, ri, q in taps_d:
                du = du + w_ref[k:k + 1, :] * _tap_rows(dwin, dsh, rs_d, ri, q, r0, rb, direct)
            if glu:
                v = vm[pl.ds(r0, rb), :].astype(F32)
                s = _sig(gm[pl.ds(r0, rb), :].astype(F32))
                dv_ref[pl.ds(r0, rb), :] = (du * s).astype(BF16)
                dg_ref[pl.ds(r0, rb), :] = (du * v * s * (1.0 - s)).astype(BF16)
            else:
                du_ref[pl.ds(r0, rb), :] = du.astype(BF16)
            dmain = dwin[pl.ds(r0 + hb, rb), :] if direct else dwin[pl.ds(pl.multiple_of(r0 + hb, 8), rb), :]
            accb[...] += _rowsum8(dmain)
            for k, ri, q in taps_u:
                accw[k] += _rowsum8(dmain * _tap_rows(uwin, ush, rs_u, ri, q, r0, rb, direct))

        _row_loop(rows // rb, rb, rowblk, direct)

        @pl.when(i == nt - 1)
        def _():
            for k in range(K):
                dw_ref[k:k + 1, :] = jnp.sum(accw[k], axis=0, keepdims=True)
            db_ref[...] = jnp.sum(accb[...], axis=0, keepdims=True)

    in_specs, args = [], []
    for d in dsrcs:
        in_specs += _halo_specs(S, rows, lanes, 0)
        args += [d, d, d]
    in_specs += _halo_specs(S, rows, lanes, col0)
    args += [src, src, src]
    if glu:
        in_specs += _halo_specs(S, rows, lanes, glu_col0)
        args += [src, src, src]
    in_specs += [pl.BlockSpec((K, lanes), lambda j, i: (0, j))]
    tile = pl.BlockSpec((rows, lanes), lambda j, i: (i, j))
    n_act = 2 if glu else 1
    out_shape = [SDS((S, C), BF16)] * n_act + [SDS((K, C), F32), SDS((1, C), F32)]
    out_specs = [tile] * n_act + [pl.BlockSpec((K, lanes), lambda j, i: (0, j)),
                                  pl.BlockSpec((1, lanes), lambda j, i: (0, j))]
    return _call(
        body, grid=(C // lanes, nt), out_shape=out_shape, in_specs=in_specs, out_specs=out_specs,
        scratch_shapes=[pltpu.VMEM((rows + 2 * hb, lanes), F32), pltpu.VMEM((len(rs_d), n_u, lanes), F32),
                        pltpu.VMEM((rows + 2 * hb, lanes), F32), pltpu.VMEM((len(rs_u), n_u, lanes), F32),
                        pltpu.VMEM((K, 8, lanes), F32), pltpu.VMEM((8, lanes), F32)],
        sem=("parallel", "arbitrary"), name=name, args=(*args, w), carry=carry)


def _ln_parts(u, ln_w, ln_b):
    mu = jnp.mean(u, axis=-1, keepdims=True)
    xc = u - mu
    rstd = lax.rsqrt(jnp.mean(xc * xc, axis=-1, keepdims=True) + EPS)
    un = xc * rstd
    return un, rstd, un * ln_w + ln_b


def _conf_out_fwd(uc, proj, h, ln_w, ln_b, wout, *, name):
    S, E = uc.shape
    D = h.shape[1]
    tm = min(GATE_TILE, S)

    def body(uc_ref, z_ref, h_ref, lw_ref, lb_ref, w_ref, o_ref):
        _, _, ul = _ln_parts(uc_ref[...].astype(F32), lw_ref[...], lb_ref[...])
        u3 = (_silu(ul) * _silu(z_ref[...].astype(F32))).astype(BF16)
        o_ref[...] = h_ref[...] + jnp.dot(u3, w_ref[...], preferred_element_type=F32)

    return pl.pallas_call(
        body, grid=(S // tm,), out_shape=SDS((S, D), F32),
        in_specs=[pl.BlockSpec((tm, E), lambda i: (i, 0)), pl.BlockSpec((tm, E), lambda i: (i, 2)),
                  pl.BlockSpec((tm, D), lambda i: (i, 0)), _resident((1, E)), _resident((1, E)), _resident((E, D))],
        out_specs=pl.BlockSpec((tm, D), lambda i: (i, 0)),
        compiler_params=_cparams(("parallel",)), name=name)(uc, proj, h, ln_w, ln_b, wout)


def _conf_out_bwd(dh, uc, proj, ln_w, ln_b, wout, *, name):
    S, E = uc.shape
    D = dh.shape[1]
    tm = min(GATE_TILE, S)
    nt = S // tm

    def body(dh_ref, uc_ref, z_ref, lw_ref, lb_ref, w_ref, u3_ref, duc_ref, dz_ref, dlw_ref, dlb_ref, alw, alb):
        i = pl.program_id(0)

        @pl.when(i == 0)
        def _():
            alw[...] = jnp.zeros_like(alw)
            alb[...] = jnp.zeros_like(alb)

        un, rstd, ul = _ln_parts(uc_ref[...].astype(F32), lw_ref[...], lb_ref[...])
        z = z_ref[...].astype(F32)
        su, sz = _silu(ul), _silu(z)
        u3_ref[...] = (su * sz).astype(BF16)
        du3 = lax.dot_general(dh_ref[...].astype(BF16), w_ref[...], (((1,), (1,)), ((), ())),
                              preferred_element_type=F32)
        dz_ref[...] = (du3 * su * _dsilu(z)).astype(BF16)
        dul = du3 * sz * _dsilu(ul)
        alw[...] += _rowsum8(dul * un)
        alb[...] += _rowsum8(dul)
        dun = dul * lw_ref[...]
        m1 = jnp.mean(dun, axis=-1, keepdims=True)
        m2 = jnp.mean(dun * un, axis=-1, keepdims=True)
        duc_ref[...] = (rstd * (dun - m1 - un * m2)).astype(BF16)

        @pl.when(i == nt - 1)
        def _():
            dlw_ref[...] = jnp.sum(alw[...], axis=0, keepdims=True)
            dlb_ref[...] = jnp.sum(alb[...], axis=0, keepdims=True)

    tile = pl.BlockSpec((tm, E), lambda i: (i, 0))
    vec = pl.BlockSpec((1, E), lambda i: (0, 0))
    return pl.pallas_call(
        body, grid=(nt,),
        out_shape=[SDS((S, E), BF16)] * 3 + [SDS((1, E), F32)] * 2,
        in_specs=[pl.BlockSpec((tm, D), lambda i: (i, 0)), tile, pl.BlockSpec((tm, E), lambda i: (i, 2)),
                  _resident((1, E)), _resident((1, E)), _resident((E, D))],
        out_specs=[tile, tile, tile, vec, vec],
        scratch_shapes=[pltpu.VMEM((8, E), F32), pltpu.VMEM((8, E), F32)],
        compiler_params=_cparams(("arbitrary",)), name=name)(dh, uc, proj, ln_w, ln_b, wout)


def _inproj_bwd(parts, offs, w, h, nw, dh_in, *, wt, name):
    S, D = h.shape
    tm = min(ROW_TILE, S)
    nt = S // tm
    widths = [p.shape[1] for p in parts]
    npart = len(parts)

    def body(*refs):
        p_refs = refs[:npart]
        w_ref, h_ref, nw_ref, dh_ref, o_ref, hn_ref, dnw_ref, anw = refs[npart:]
        i = pl.program_id(0)

        @pl.when(i == 0)
        def _():
            anw[...] = jnp.zeros_like(anw)

        x = h_ref[...]
        r = lax.rsqrt(jnp.mean(x * x, axis=-1, keepdims=True) + EPS)
        n = x * r
        hn_ref[...] = (n * nw_ref[...]).astype(BF16)
        dhn = jnp.zeros((tm, D), F32)
        for p_ref, off, wd in zip(p_refs, offs, widths):
            for c0, c1 in _col_chunks(wd, 1024):
                if wt:
                    dhn = dhn + jnp.dot(p_ref[:, c0:c1], w_ref[off + c0:off + c1, :], preferred_element_type=F32)
                else:
                    dhn = dhn + lax.dot_general(p_ref[:, c0:c1], w_ref[:, off + c0:off + c1],
                                                (((1,), (1,)), ((), ())), preferred_element_type=F32)
        anw[...] += _rowsum8(dhn * n)
        dn = dhn * nw_ref[...]
        o_ref[...] = dh_ref[...] + r * (dn - n * jnp.mean(dn * n, axis=-1, keepdims=True))

        @pl.when(i == nt - 1)
        def _():
            dnw_ref[...] = jnp.sum(anw[...], axis=0, keepdims=True)

    tile = pl.BlockSpec((tm, D), lambda i: (i, 0))
    return pl.pallas_call(
        body, grid=(nt,),
        out_shape=[SDS((S, D), F32), SDS((S, D), BF16), SDS((1, D), F32)],
        in_specs=[pl.BlockSpec((tm, wd), lambda i: (i, 0)) for wd in widths]
        + [_resident(w.shape), tile, _resident((1, D)), tile],
        out_specs=[tile, tile, pl.BlockSpec((1, D), lambda i: (0, 0))],
        scratch_shapes=[pltpu.VMEM((8, D), F32)],
        compiler_params=_cparams(("arbitrary",)), name=name)(*parts, w, h, nw, dh_in)


def _tn_matmul(a, b, *, transpose_out, name):
    S, Ka = a.shape
    N = b.shape[1]
    tm = min(ROW_TILE, S)
    bn = min(1024, N)
    nt = S // tm

    def body(a_ref, b_ref, o_ref, acc):
        i = pl.program_id(1)

        @pl.when(i == 0)
        def _():
            acc[...] = jnp.zeros_like(acc)

        acc[...] += lax.dot_general(a_ref[...], b_ref[...].astype(BF16), (((0,), (0,)), ((), ())),
                                    preferred_element_type=F32)

        @pl.when(i == nt - 1)
        def _():
            o_ref[...] = acc[...].T if transpose_out else acc[...]

    if transpose_out:
        out_shape, out_spec = SDS((N, Ka), F32), pl.BlockSpec((bn, Ka), lambda j, i: (j, 0))
    else:
        out_shape, out_spec = SDS((Ka, N), F32), pl.BlockSpec((Ka, bn), lambda j, i: (0, j))
    return pl.pallas_call(
        body, grid=(N // bn, nt), out_shape=out_shape,
        in_specs=[pl.BlockSpec((tm, Ka), lambda j, i: (i, 0)), pl.BlockSpec((tm, bn), lambda j, i: (i, j))],
        out_specs=out_spec, scratch_shapes=[pltpu.VMEM((Ka, bn), F32)],
        compiler_params=_cparams(("parallel", "arbitrary")), name=name)(a, b)


def _loss_head(h, fnw, target, *, name):
    S, D = h.shape
    tm = min(ROW_TILE, S)
    nt = S // tm

    def body(h_ref, w_ref, t_ref, loss_ref, dh_ref, dw_ref, aw, al):
        i = pl.program_id(0)

        @pl.when(i == 0)
        def _():
            aw[...] = jnp.zeros_like(aw)
            al[...] = jnp.zeros_like(al)

        x = h_ref[...]
        r = lax.rsqrt(jnp.mean(x * x, axis=-1, keepdims=True) + EPS)
        n = x * r
        err = n * w_ref[...] - t_ref[...]
        al[...] += _rowsum8(err * err)
        dy = err * (1.0 / D)
        aw[...] += _rowsum8(dy * n)
        dn = dy * w_ref[...]
        dh_ref[...] = r * (dn - n * jnp.mean(dn * n, axis=-1, keepdims=True))

        @pl.when(i == nt - 1)
        def _():
            dw_ref[...] = jnp.sum(aw[...], axis=0, keepdims=True)
            tot = jnp.sum(jnp.sum(al[...], axis=0, keepdims=True), axis=1, keepdims=True) * (0.5 / D)
            loss_ref[...] = jnp.zeros((8, 128), F32) + tot

    tile = pl.BlockSpec((tm, D), lambda i: (i, 0))
    loss, dh, dw = pl.pallas_call(
        body, grid=(nt,), out_shape=[SDS((8, 128), F32), SDS((S, D), F32), SDS((1, D), F32)],
        in_specs=[tile, _resident((1, D)), tile],
        out_specs=[pl.BlockSpec((8, 128), lambda i: (0, 0)), tile, pl.BlockSpec((1, D), lambda i: (0, 0))],
        scratch_shapes=[pltpu.VMEM((8, D), F32), pltpu.VMEM((8, D), F32)],
        compiler_params=_cparams(("arbitrary",)), name=name)(h, fnw, target)
    return loss[0, 0], dh, dw


def _head_mask(shape, head_axis):
    hd = lax.broadcasted_iota(jnp.int32, shape, head_axis)
    ch = lax.broadcasted_iota(jnp.int32, shape, 1 - head_axis)
    return ((ch >= hd * HEADDIM) & (ch < hd * HEADDIM + HEADDIM)).astype(BF16)


def _split_dot(vals, mat):
    parts = []
    for v in vals:
        hi = v.astype(BF16)
        parts += [hi, (v - hi.astype(F32)).astype(BF16)]
    out = jnp.dot(jnp.concatenate(parts, axis=0), mat, preferred_element_type=F32)
    R = vals[0].shape[0]
    return [out[2 * R * i:2 * R * i + R] + out[2 * R * i + R:2 * R * i + 2 * R] for i in range(len(vals))]


def _nt(a, b):
    return lax.dot_general(a, b, (((1,), (1,)), ((), ())), preferred_element_type=F32)


def _tn(a, b):
    return lax.dot_general(a, b, (((0,), (0,)), ((), ())), preferred_element_type=F32)


def _ssd_setup(c_val, dtraw, bias, alog, fwd):
    xbc = _silu(c_val.astype(F32))
    pre = dtraw + bias
    dt = _softplus(pre)
    A = -jnp.exp(alog)
    row = lax.broadcasted_iota(jnp.int32, (CHUNK, CHUNK), 0)
    col = lax.broadcasted_iota(jnp.int32, (CHUNK, CHUNK), 1)
    T = (col <= row) if fwd else (col >= row)
    cum = jnp.dot(T.astype(F32), dt * A, precision=lax.Precision.HIGHEST, preferred_element_type=F32)
    total = cum[CHUNK - 1:CHUNK] if fwd else cum[0:1]
    return dict(x=xbc[:, :D_INNER], B=xbc[:, D_INNER:D_INNER + GN].astype(BF16),
                C=xbc[:, D_INNER + GN:].astype(BF16), pre=pre, dt=dt, A=A, T=T, Tt=(col >= row) if fwd else (col <= row),
                cum=cum, e=jnp.exp(cum), dtds=dt * jnp.exp(total - cum))


def _ssd_fwd_dir(c_val, dtraw, bias, alog, state, y_ref, hin_ref, fwd, ex):
    s = _ssd_setup(c_val, dtraw, bias, alog, fwd)
    x, cum, T = s["x"], s["cum"], s["T"]
    e_x, dtds_x = _split_dot([s["e"], s["dtds"]], ex)
    dec_x = e_x[CHUNK - 1:CHUNK] if fwd else e_x[0:1]
    cumT, dtT = cum.T, s["dt"].T
    xb = x.astype(BF16)
    for g in range(SSD_GROUPS):
        g0, g1 = g * HPG * HEADDIM, (g + 1) * HPG * HEADDIM
        Bg, Cg = s["B"][:, g * D_STATE:(g + 1) * D_STATE], s["C"][:, g * D_STATE:(g + 1) * D_STATE]
        CBg = _nt(Cg, Bg)
        Hg = state[:, g0:g1]
        Hb = Hg.astype(BF16)
        hin_ref[0, :, g0:g1] = Hb
        yoff = jnp.dot(Cg, Hb, preferred_element_type=F32) * e_x[:, g0:g1]
        ys = []
        for hh in range(HPG):
            h = g * HPG + hh
            Lh = jnp.exp(jnp.where(T, cum[:, h:h + 1] - cumT[h:h + 1, :], NEG))
            Mh = (CBg * Lh * dtT[h:h + 1, :]).astype(BF16)
            ys.append(jnp.dot(Mh, xb[:, h * HEADDIM:(h + 1) * HEADDIM], preferred_element_type=F32))
        y_ref[:, g0:g1] = jnp.concatenate(ys, axis=1) + yoff
        xds = (x[:, g0:g1] * dtds_x[:, g0:g1]).astype(BF16)
        state[:, g0:g1] = Hg * dec_x[:, g0:g1] + _tn(Bg, xds)


def _ssd_scan_fwd(cact, tail, bias, alog, *, name):
    S = cact.shape[0]
    nc = S // CHUNK
    E = D_INNER

    def body(cf_ref, cb_ref, tf_ref, tb_ref, bias_ref, alog_ref, yf_ref, yb_ref, hf_ref, hb_ref, state):
        @pl.when(pl.program_id(0) == 0)
        def _():
            state[...] = jnp.zeros_like(state)

        ex = _head_mask((SSD_HEADS, D_INNER), 0)
        H = SSD_HEADS
        _ssd_fwd_dir(cf_ref[...], tf_ref[:, 0:H], bias_ref[0:1, :], alog_ref[0:1, :], state.at[0], yf_ref, hf_ref, True, ex)
        _ssd_fwd_dir(cb_ref[...], tb_ref[:, H:2 * H], bias_ref[1:2, :], alog_ref[1:2, :], state.at[1], yb_ref, hb_ref, False, ex)

    up, down = (lambda i: (i, 0)), (lambda i: (nc - 1 - i, 0))
    up3, down3 = (lambda i: (i, 0, 0)), (lambda i: (nc - 1 - i, 0, 0))
    W = cact.shape[1]
    return pl.pallas_call(
        body, grid=(nc,),
        out_shape=[SDS((S, E), F32), SDS((S, E), F32), SDS((nc, D_STATE, E), BF16), SDS((nc, D_STATE, E), BF16)],
        in_specs=[pl.BlockSpec((CHUNK, W), up), pl.BlockSpec((CHUNK, W), down),
                  pl.BlockSpec((CHUNK, 128), up), pl.BlockSpec((CHUNK, 128), down),
                  _resident((2, SSD_HEADS)), _resident((2, SSD_HEADS))],
        out_specs=[pl.BlockSpec((CHUNK, E), up), pl.BlockSpec((CHUNK, E), down),
                   pl.BlockSpec((1, D_STATE, E), up3), pl.BlockSpec((1, D_STATE, E), down3)],
        scratch_shapes=[pltpu.VMEM((2, D_STATE, E), F32)],
        compiler_params=_cparams(("arbitrary",)), name=name)(cact, cact, tail, tail, bias, alog)


def _ssd_bwd_dir(c_val, dtraw, bias, alog, dy, hin_ref, dskip_x, G, dc_ref, ddt_ref, accA, accb, fwd, ex, ext):
    s = _ssd_setup(c_val, dtraw, bias, alog, fwd)
    x, cum, T, dt = s["x"], s["cum"], s["T"], s["dt"]
    e_x, dtds_x = _split_dot([s["e"], s["dtds"]], ex)
    dec_x = e_x[CHUNK - 1:CHUNK] if fwd else e_x[0:1]
    cumT, dtT = cum.T, dt.T
    xb = x.astype(BF16)
    dyf = dy.astype(F32)
    dx_diag, dx_st, dBs, dCs, gh, yoff_dy = [], [], [], [], [], []
    lane_h = lax.broadcasted_iota(jnp.int32, (CHUNK, SSD_HEADS), 1)
    sub_h = lax.broadcasted_iota(jnp.int32, (SSD_HEADS, CHUNK), 0)
    rows = jnp.zeros((CHUNK, SSD_HEADS), F32)
    colsT = jnp.zeros((SSD_HEADS, CHUNK), F32)
    for g in range(SSD_GROUPS):
        g0, g1 = g * HPG * HEADDIM, (g + 1) * HPG * HEADDIM
        Bg, Cg = s["B"][:, g * D_STATE:(g + 1) * D_STATE], s["C"][:, g * D_STATE:(g + 1) * D_STATE]
        CBg = _nt(Cg, Bg)
        Hb = hin_ref[0, :, g0:g1]
        Gg = G[:, g0:g1]
        Gb = Gg.astype(BF16)
        edyf = dyf[:, g0:g1] * e_x[:, g0:g1]
        edy = edyf.astype(BF16)
        yoff_dy.append(edyf * jnp.dot(Cg, Hb, preferred_element_type=F32))
        dC_g = _nt(edy, Hb)
        dH_g = _tn(Cg, edy)
        xds = (x[:, g0:g1] * dtds_x[:, g0:g1]).astype(BF16)
        dB_g = _nt(xds, Gb)
        dx_st.append(dtds_x[:, g0:g1] * jnp.dot(Bg, Gb, preferred_element_type=F32))
        dCB = jnp.zeros((CHUNK, CHUNK), F32)
        for hh in range(HPG):
            h = g * HPG + hh
            hs = slice(h * HEADDIM, (h + 1) * HEADDIM)
            Lh = jnp.exp(jnp.where(T, cum[:, h:h + 1] - cumT[h:h + 1, :], NEG))
            Mh = (CBg * Lh * dtT[h:h + 1, :]).astype(BF16)
            term = _nt(dy[:, hs], xb[:, hs]) * dtT[h:h + 1, :] * Lh
            dCB = dCB + term
            Wh = term * CBg
            rows = rows + jnp.where(lane_h == h, jnp.sum(Wh, axis=1, keepdims=True), 0.0)
            colsT = colsT + jnp.where(sub_h == h, jnp.sum(Wh, axis=0, keepdims=True), 0.0)
            dx_diag.append(_tn(Mh, dy[:, hs]))
        dCBb = dCB.astype(BF16)
        dCs.append(dC_g + jnp.dot(dCBb, Bg, preferred_element_type=F32))
        dBs.append(dB_g + _tn(dCBb, Cg))
        gh.append(_rowsum8(Gg * Hb.astype(F32)))
        G[:, g0:g1] = Gg * dec_x[:, g0:g1] + dH_g
    dxn = jnp.concatenate(dx_diag, axis=1) + jnp.concatenate(dx_st, axis=1)
    xst = x * jnp.concatenate(dx_st, axis=1)
    R1, Zx, Zs = _split_dot([jnp.concatenate(yoff_dy, axis=1), x * dxn, xst], ext)
    ddec = jnp.sum(_split_dot([jnp.concatenate(gh, axis=1)], ext)[0], axis=0, keepdims=True)
    dec = s["e"][CHUNK - 1:CHUNK] if fwd else s["e"][0:1]
    dtotal = jnp.sum(Zs, axis=0, keepdims=True) + dec * ddec
    rowi = lax.broadcasted_iota(jnp.int32, (CHUNK, SSD_HEADS), 0)
    dacs = rows - colsT.T + R1 - Zs + jnp.where(rowi == (CHUNK - 1 if fwd else 0), dtotal, 0.0)
    da = jnp.dot(s["Tt"].astype(F32), dacs, precision=lax.Precision.HIGHEST, preferred_element_type=F32)
    accA[...] += _rowsum8(da * dt)
    sg = _sig(s["pre"])
    ratio = jnp.where(dt > 1e-30, sg / jnp.maximum(dt, 1e-30), 1.0)
    ddtraw = da * s["A"] * sg + Zx * ratio
    ddt_ref[...] = ddtraw
    accb[...] += _rowsum8(ddtraw)
    dx = dxn + dyf * dskip_x if fwd else dxn
    dact = jnp.concatenate([dx] + dBs + dCs, axis=1)
    dc_ref[...] = (dact * _dsilu(c_val.astype(F32))).astype(BF16)


def _ssd_scan_bwd(cact, tail, bias, alog, dy, hf, hb, dskip_x, *, name, carry=None):
    S = cact.shape[0]
    nc = S // CHUNK
    E, H, W = D_INNER, SSD_HEADS, cact.shape[1]

    def body(cf_ref, cb_ref, tf_ref, tb_ref, bias_ref, alog_ref, dyf_ref, dyb_ref, hf_ref, hb_ref,
             dsk_ref, dcf_ref, dcb_ref, ddtf_ref, ddtb_ref, dalog_ref, dbias_ref, G, accA, accb):
        i = pl.program_id(0)

        @pl.when(i == 0)
        def _():
            G[...] = jnp.zeros_like(G)
            accA[...] = jnp.zeros_like(accA)
            accb[...] = jnp.zeros_like(accb)

        ex = _head_mask((H, E), 0)
        ext = _head_mask((E, H), 1)
        _ssd_bwd_dir(cf_ref[...], tf_ref[:, 0:H], bias_ref[0:1, :], alog_ref[0:1, :], dyf_ref[...], hf_ref,
                     dsk_ref[...], G.at[0], dcf_ref, ddtf_ref, accA.at[0], accb.at[0], True, ex, ext)
        _ssd_bwd_dir(cb_ref[...], tb_ref[:, H:2 * H], bias_ref[1:2, :], alog_ref[1:2, :], dyb_ref[...], hb_ref,
                     dsk_ref[...], G.at[1], dcb_ref, ddtb_ref, accA.at[1], accb.at[1], False, ex, ext)

        @pl.when(i == nc - 1)
        def _():
            for d in range(2):
                A = -jnp.exp(alog_ref[d:d + 1, :])
                dalog_ref[d:d + 1, :] = jnp.sum(accA[d], axis=0, keepdims=True) * A
                dbias_ref[d:d + 1, :] = jnp.sum(accb[d], axis=0, keepdims=True)

    up, down = (lambda i: (i, 0)), (lambda i: (nc - 1 - i, 0))
    up3, down3 = (lambda i: (i, 0, 0)), (lambda i: (nc - 1 - i, 0, 0))
    small = pl.BlockSpec((2, H), lambda i: (0, 0))
    return _call(
        body, grid=(nc,),
        out_shape=[SDS((S, W), BF16), SDS((S, W), BF16), SDS((S, H), F32), SDS((S, H), F32),
                   SDS((2, H), F32), SDS((2, H), F32)],
        in_specs=[pl.BlockSpec((CHUNK, W), down), pl.BlockSpec((CHUNK, W), up),
                  pl.BlockSpec((CHUNK, 128), down), pl.BlockSpec((CHUNK, 128), up),
                  _resident((2, H)), _resident((2, H)),
                  pl.BlockSpec((CHUNK, E), down), pl.BlockSpec((CHUNK, E), up),
                  pl.BlockSpec((1, D_STATE, E), down3), pl.BlockSpec((1, D_STATE, E), up3),
                  _resident((1, E))],
        out_specs=[pl.BlockSpec((CHUNK, W), down), pl.BlockSpec((CHUNK, W), up),
                   pl.BlockSpec((CHUNK, H), down), pl.BlockSpec((CHUNK, H), up), small, small],
        scratch_shapes=[pltpu.VMEM((2, D_STATE, E), F32), pltpu.VMEM((2, 8, H), F32), pltpu.VMEM((2, 8, H), F32)],
        sem=("arbitrary",), name=name, args=(cact, cact, tail, tail, bias, alog, dy, dy, hf, hb, dskip_x), carry=carry)


def _ssd_gate(yf, yb, cx, z, dsk, nw):
    xs = _silu(cx.astype(F32))
    y = yf + yb + xs * dsk
    sz = _silu(z)
    gt = y * sz
    r = lax.rsqrt(jnp.mean(gt * gt, axis=-1, keepdims=True) + EPS)
    return xs, y, sz, r, gt * r


def _ssd_out_fwd(yf, yb, cact, proj, dskip_x, nw, h, wout, *, name):
    S, E = yf.shape
    D = h.shape[1]
    tm = min(GATE_TILE, S)

    def body(yf_ref, yb_ref, cx_ref, z_ref, dsk_ref, nw_ref, h_ref, w_ref, o_ref):
        _, _, _, _, n = _ssd_gate(yf_ref[...], yb_ref[...], cx_ref[...], z_ref[...].astype(F32), dsk_ref[...], nw_ref[...])
        o_ref[...] = h_ref[...] + jnp.dot((n * nw_ref[...]).astype(BF16), w_ref[...], preferred_element_type=F32)

    tile = pl.BlockSpec((tm, E), lambda i: (i, 0))
    return pl.pallas_call(
        body, grid=(S // tm,), out_shape=SDS((S, D), F32),
        in_specs=[tile, tile, tile, tile, _resident((1, E)), _resident((1, E)),
                  pl.BlockSpec((tm, D), lambda i: (i, 0)), _resident((E, D))],
        out_specs=pl.BlockSpec((tm, D), lambda i: (i, 0)),
        compiler_params=_cparams(("parallel",)), name=name)(yf, yb, cact, proj, dskip_x, nw, h, wout)


def _ssd_out_bwd(dh, yf, yb, cact, proj, dskip_x, nw, wout, *, name):
    S, E = yf.shape
    D = dh.shape[1]
    tm = min(GATE_TILE, S)
    nt = S // tm

    def body(dh_ref, yf_ref, yb_ref, cx_ref, z_ref, dsk_ref, nw_ref, w_ref, yn_ref, dy_ref, dz_ref, dnw_ref, dd_ref, anw, ad):
        i = pl.program_id(0)

        @pl.when(i == 0)
        def _():
            anw[...] = jnp.zeros_like(anw)
            ad[...] = jnp.zeros_like(ad)

        z = z_ref[...].astype(F32)
        xs, y, sz, r, n = _ssd_gate(yf_ref[...], yb_ref[...], cx_ref[...], z, dsk_ref[...], nw_ref[...])
        yn_ref[...] = (n * nw_ref[...]).astype(BF16)
        dyn = _nt(dh_ref[...].astype(BF16), w_ref[...])
        anw[...] += _rowsum8(dyn * n)
        dn = dyn * nw_ref[...]
        dgt = r * (dn - n * jnp.mean(dn * n, axis=-1, keepdims=True))
        dy = dgt * sz
        dy_ref[...] = dy.astype(BF16)
        dz_ref[...] = (dgt * y * _dsilu(z)).astype(BF16)
        ad[...] += _rowsum8(dy * xs)

        @pl.when(i == nt - 1)
        def _():
            dnw_ref[...] = jnp.sum(anw[...], axis=0, keepdims=True)
            dd_ref[...] = jnp.sum(ad[...], axis=0, keepdims=True)

    tile = pl.BlockSpec((tm, E), lambda i: (i, 0))
    vec = pl.BlockSpec((1, E), lambda i: (0, 0))
    return pl.pallas_call(
        body, grid=(nt,), out_shape=[SDS((S, E), BF16)] * 3 + [SDS((1, E), F32)] * 2,
        in_specs=[pl.BlockSpec((tm, D), lambda i: (i, 0)), tile, tile, tile, tile,
                  _resident((1, E)), _resident((1, E)), _resident((E, D))],
        out_specs=[tile, tile, tile, vec, vec],
        scratch_shapes=[pltpu.VMEM((8, E), F32), pltpu.VMEM((8, E), F32)],
        compiler_params=_cparams(("arbitrary",)), name=name)(dh, yf, yb, cact, proj, dskip_x, nw, wout)


def _local_step(x, target, p, hooks):
    E, S = D_INNER, x.shape[0]
    depth = p["norm_w"].shape[0]
    names = ["norm_w", "cm_w_in", "cm_dw_w", "cm_dw_b", "cm_ln_w", "cm_ln_b", "cm_w_out", "ssd_w_in", "ssd_conv_w",
             "ssd_conv_b", "ssd_dt_bias", "ssd_A_log", "ssd_D", "ssd_norm_w", "ssd_w_out"]
    g = {n: [None] * (depth if n == "norm_w" else depth // 2) for n in names}

    def carried(key):
        items, done = hooks.get(key, (None, None))
        return (items(g) if items else None), (done if done else lambda got: None)

    h, saved = x, []
    for i in range(depth):
        j, nw = i // 2, p["norm_w"][i][None]
        if i % 2 == 0:
            carry, done = carried(("cm_in_fwd", i))
            proj, got = _inproj_fwd(h, nw, p["cm_w_in"][j], tail=False, wt=False, name="cm_in_fwd", carry=carry)
            done(got)
            carry, done = carried(("cm_conv_fwd", i))
            uc, got = _dwconv_fwd(proj, p["cm_dw_w"][j], p["cm_dw_b"][j][None], col0=0, glu_col0=E, C=E,
                                  name="cm_conv_fwd", carry=carry)
            done(got)
            hn = _conf_out_fwd(uc, proj, h, p["cm_ln_w"][j][None], p["cm_ln_b"][j][None], p["cm_w_out"][j], name="cm_out_fwd")
            saved.append((h, proj, uc))
        else:
            (proj, tail), _ = _inproj_fwd(h, nw, p["ssd_w_in"][j], tail=True, wt=True, name="ssd_in_fwd")
            cact, _ = _dwconv_fwd(proj, p["ssd_conv_w"][j], p["ssd_conv_b"][j][None], col0=E, glu_col0=None,
                                  C=SSD_CONV_DIM, name="ssd_conv_fwd")
            yf, yb, hf, hb = _ssd_scan_fwd(cact, tail, p["ssd_dt_bias"][j], p["ssd_A_log"][j], name="ssd_scan_fwd")
            dsk = jnp.repeat(p["ssd_D"][j], HEADDIM)[None]
            hn = _ssd_out_fwd(yf, yb, cact, proj, dsk, p["ssd_norm_w"][j][None], h, p["ssd_w_out"][j], name="ssd_out_fwd")
            saved.append((h, proj, tail, cact, yf, yb, hf, hb, dsk))
        h = hn
    loss, dh, g_fnw = _loss_head(h, p["final_norm_w"][None], target, name="loss_head")

    for i in reversed(range(depth)):
        j, nw = i // 2, p["norm_w"][i][None]
        if i % 2 == 0:
            h_in, proj, uc = saved[i]
            u3, duc, dz, g_lw, g_lb = _conf_out_bwd(dh, uc, proj, p["cm_ln_w"][j][None], p["cm_ln_b"][j][None],
                                                    p["cm_w_out"][j], name="cm_out_bwd")
            g["cm_w_out"][j] = _tn_matmul(u3, dh, transpose_out=False, name="cm_wout_grad")
            carry, done = carried(("cm_conv_bwd", i))
            (dv, dg, g_dw, g_db), got = _dwconv_bwd([duc], proj, p["cm_dw_w"][j], col0=0, glu_col0=E, C=E,
                                                    name="cm_conv_bwd", carry=carry)
            done(got)
            parts = [dv, dg, dz]
            dh, hnb, g_nw = _inproj_bwd(parts, [0, E, 2 * E], p["cm_w_in"][j], h_in, nw, dh, wt=False, name="cm_in_bwd")
            g["cm_w_in"][j] = jnp.concatenate(
                [_tn_matmul(hnb, q, transpose_out=False, name="cm_win_grad") for q in parts], axis=1)
            g["cm_dw_w"][j], g["cm_dw_b"][j], g["cm_ln_w"][j], g["cm_ln_b"][j] = g_dw, g_db[0], g_lw[0], g_lb[0]
        else:
            h_in, proj, tail, cact, yf, yb, hf, hb, dsk = saved[i]
            yn, dy, dz, g_sn, g_dx = _ssd_out_bwd(dh, yf, yb, cact, proj, dsk, p["ssd_norm_w"][j][None],
                                                  p["ssd_w_out"][j], name="ssd_out_bwd")
            g["ssd_w_out"][j] = _tn_matmul(yn, dh, transpose_out=False, name="ssd_wout_grad")
            carry, done = carried(("ssd_scan_bwd", i))
            (dcf, dcb, ddtf, ddtb, g_al, g_bias), got = _ssd_scan_bwd(
                cact, tail, p["ssd_dt_bias"][j], p["ssd_A_log"][j], dy, hf, hb, dsk, name="ssd_scan_bwd", carry=carry)
            done(got)
            (dxbc, g_cw, g_cb), _ = _dwconv_bwd([dcf, dcb], proj, p["ssd_conv_w"][j], col0=E, glu_col0=None,
                                                C=SSD_CONV_DIM, name="ssd_conv_bwd")
            ddt = jnp.concatenate([ddtf, ddtb, jnp.zeros((S, 128 - 2 * SSD_HEADS), F32)], axis=1).astype(BF16)
            parts = [dz, dxbc, ddt]
            dh, hnb, g_nw = _inproj_bwd(parts, [0, E, E + SSD_CONV_DIM], p["ssd_w_in"][j], h_in, nw, dh, wt=True,
                                        name="ssd_in_bwd")
            gw = [_tn_matmul(hnb, q, transpose_out=True, name="ssd_win_grad") for q in parts]
            g["ssd_w_in"][j] = jnp.concatenate([gw[0], gw[1], gw[2][:2 * SSD_HEADS]], axis=0)
            g["ssd_conv_w"][j], g["ssd_conv_b"][j], g["ssd_norm_w"][j] = g_cw, g_cb[0], g_sn[0]
            g["ssd_dt_bias"][j], g["ssd_A_log"][j] = g_bias, g_al
            g["ssd_D"][j] = jnp.sum(g_dx.reshape(SSD_HEADS, HEADDIM), axis=1)
        g["norm_w"][i] = g_nw[0]
    g["final_norm_w"] = g_fnw[0]
    return loss, dh, g


_FLIPS = [(fx, fy, fc) for fx in (0, 1) for fy in (0, 1) for fc in (0, 1)][1:]


def _exchange_copies(items, ins, outs, send_sems, recv_sems, local_sems, arrivals):
    x, y, c = lax.axis_index("x"), lax.axis_index("y"), lax.axis_index("c")
    me = 4 * x + 2 * y + c
    local, pairs = [], []
    for b, (_, scatter) in enumerate(items):
        local.append(pltpu.make_async_copy(ins[b].at[me] if scatter else ins[b], outs[b].at[me], local_sems.at[b]))
        for k, (fx, fy, fc) in enumerate(_FLIPS):
            px, py, pc = x ^ fx, y ^ fy, c ^ fc
            peer = 4 * px + 2 * py + pc
            src = ins[b].at[peer] if scatter else ins[b]
            sems = dict(send_sem=send_sems.at[b * 7 + k], recv_sem=recv_sems.at[b * 7 + k],
                        device_id=(px, py, pc), device_id_type=pl.DeviceIdType.MESH)
            pairs.append((pltpu.make_async_remote_copy(src_ref=src, dst_ref=outs[b].at[me], **sems),
                          pltpu.make_async_remote_copy(src_ref=src, dst_ref=outs[b].at[peer], **sems)
                          if arrivals else None))
    return local, pairs


def _exchange_start(items, ins, outs, *sems):
    local, pairs = _exchange_copies(items, ins, outs, *sems, arrivals=False)
    for cp in local:
        cp.start()
    for cp, _ in pairs:
        cp.start()


def _exchange_wait(items, ins, outs, *sems):
    local, pairs = _exchange_copies(items, ins, outs, *sems, arrivals=True)
    for _, arrival in pairs:
        arrival.wait_recv()
    for cp, _ in pairs:
        cp.wait_send()
    for cp in local:
        cp.wait()


def _exchange_shapes(items):
    n = len(items)
    out_shape = [SDS((N_DEV,) + tuple(a.shape[-2:]), a.dtype) for a, _ in items]
    sems = [pltpu.SemaphoreType.DMA((7 * n,)), pltpu.SemaphoreType.DMA((7 * n,)), pltpu.SemaphoreType.DMA((n,))]
    return out_shape, sems


def _exchange(items, *, name):
    n = len(items)

    def body(*refs):
        _exchange_start(items, refs[:n], refs[n:2 * n], *refs[2 * n:])
        _exchange_wait(items, refs[:n], refs[n:2 * n], *refs[2 * n:])

    hbm = pl.BlockSpec(memory_space=pltpu.HBM)
    out_shape, sems = _exchange_shapes(items)
    return pl.pallas_call(body, out_shape=out_shape, in_specs=[hbm] * n, out_specs=[hbm] * n,
                          scratch_shapes=sems, name=name)(*[a for a, _ in items])


def _call(body, *, grid, out_shape, in_specs, out_specs, scratch_shapes, sem, name, args, carry):
    if not carry:
        return pl.pallas_call(body, grid=grid, out_shape=out_shape, in_specs=in_specs, out_specs=out_specs,
                              scratch_shapes=scratch_shapes, compiler_params=_cparams(sem), name=name)(*args), None
    n, n_in, n_out, n_scr = len(carry), len(in_specs), len(out_specs), len(scratch_shapes)
    last_step = tuple(g - 1 for g in grid)

    def wrapped(*refs):
        ins, cin = refs[:n_in], refs[n_in:n_in + n]
        outs, cout = refs[n_in + n:n_in + n + n_out], refs[n_in + n + n_out:n_in + 2 * n + n_out]
        scr, sems = refs[n_in + 2 * n + n_out:n_in + 2 * n + n_out + n_scr], refs[n_in + 2 * n + n_out + n_scr:]
        first = last = None
        for d, top in enumerate(last_step):
            i = pl.program_id(d)
            first = (i == 0) if first is None else first & (i == 0)
            last = (i == top) if last is None else last & (i == top)

        @pl.when(first)
        def _():
            _exchange_start(carry, cin, cout, *sems)

        body(*ins, *outs, *scr)

        @pl.when(last)
        def _():
            _exchange_wait(carry, cin, cout, *sems)

    hbm = pl.BlockSpec(memory_space=pltpu.HBM)
    x_shape, x_sems = _exchange_shapes(carry)
    res = pl.pallas_call(
        wrapped, grid=grid, out_shape=list(out_shape) + x_shape, in_specs=list(in_specs) + [hbm] * n,
        out_specs=list(out_specs) + [hbm] * n, scratch_shapes=list(scratch_shapes) + x_sems,
        compiler_params=_cparams(("arbitrary",) * len(grid)), name=name)(*args, *[a for a, _ in carry])
    return res[:n_out], res[n_out:]


def _reduce_slots(recv, *, rows, name):
    n, R, C = recv.shape

    def body(r_ref, g_ref):
        g = r_ref[0].astype(F32)
        for s in range(1, n):
            g = g + r_ref[s].astype(F32)
        g_ref[...] = g

    return pl.pallas_call(
        body, grid=(R // rows,), out_shape=SDS((R, C), F32),
        in_specs=[pl.BlockSpec((n, rows, C), lambda i: (0, i, 0))],
        out_specs=pl.BlockSpec((rows, C), lambda i: (i, 0)),
        compiler_params=_cparams(("parallel",)), name=name)(recv)


def _reduce_adamw(recv, w, m, v, *, layer, rows, name):
    _, R, C = w.shape
    n = recv.shape[0]
    c1 = 1.0 / (1.0 - ADAM_B1 ** ADAM_STEP)
    c2 = 1.0 / (1.0 - ADAM_B2 ** ADAM_STEP)

    def body(r_ref, w_ref, m_ref, v_ref, g_ref, d_ref, mo_ref, vo_ref):
        g = r_ref[0].astype(F32)
        for s in range(1, n):
            g = g + r_ref[s].astype(F32)
        mn = ADAM_B1 * m_ref[...] + (1.0 - ADAM_B1) * g
        vn = ADAM_B2 * v_ref[...] + (1.0 - ADAM_B2) * (g * g)
        g_ref[...] = g
        mo_ref[...] = mn
        vo_ref[...] = vn
        d_ref[...] = -ADAM_LR * ((mn * c1) / (jnp.sqrt(vn * c2) + ADAM_EPS) + ADAM_WD * w_ref[...])

    tile = pl.BlockSpec((rows, C), lambda i: (i, 0))
    mine = pl.BlockSpec((None, rows, C), lambda i: (layer, i, 0))
    return pl.pallas_call(
        body, grid=(R // rows,), out_shape=[SDS((R, C), F32)] * 4,
        in_specs=[pl.BlockSpec((n, rows, C), lambda i: (0, i, 0)), mine, mine, mine],
        out_specs=[tile] * 4, compiler_params=_cparams(("parallel",)), name=name)(recv, w, m, v)


_ORDER = ["norm_w", "final_norm_w", "cm_w_in", "cm_dw_w", "cm_dw_b", "cm_ln_w", "cm_ln_b", "cm_w_out", "ssd_w_in",
          "ssd_conv_w", "ssd_conv_b", "ssd_dt_bias", "ssd_A_log", "ssd_D", "ssd_norm_w", "ssd_w_out"]
_SMALL = (["cm_dw_w", "ssd_conv_w", "ssd_conv_b", "ssd_norm_w"], 168)
_REP = (["norm_w", "final_norm_w", "cm_dw_b", "cm_ln_w", "cm_ln_b", "ssd_dt_bias", "ssd_A_log", "ssd_D"], 144)
_SHARD_AXIS = {"cm_w_in": 2, "cm_dw_w": 2, "cm_w_out": 1, "ssd_w_in": 2, "ssd_conv_w": 2, "ssd_conv_b": 1,
               "ssd_norm_w": 1, "ssd_w_out": 1}
OPT_ROWS = 256


def _pack(arrs, rows, lanes, dtype, lead=()):
    nl = len(lead)
    flat = jnp.concatenate([a.reshape(lead + (-1,)).astype(dtype) for a in arrs], axis=nl)
    flat = jnp.pad(flat, [(0, 0)] * nl + [(0, rows * lanes - flat.shape[nl])])
    return flat.reshape(lead + (rows, lanes))


def _unpack(buf, shapes, lead=()):
    nl = len(lead)
    flat = buf.reshape(lead + (-1,))
    out, off = [], 0
    for s in shapes:
        sz = math.prod(s)
        out.append(flat[..., off:off + sz].reshape(lead + tuple(s)))
        off += sz
    return out


def _join_shards(parts, axis):
    full = jnp.moveaxis(parts, 0, axis)
    sh = full.shape
    return full.reshape(sh[:axis] + (sh[axis] * sh[axis + 1],) + sh[axis + 2:])


def _split_shards(full, axis):
    sh = full.shape
    parts = full.reshape(sh[:axis] + (N_DEV, sh[axis] // N_DEV) + sh[axis + 1:])
    return jnp.moveaxis(parts, axis, 0)


def kernel(x, norm_w, final_norm_w, cm_w_in, cm_dw_w, cm_dw_b, cm_ln_w, cm_ln_b, cm_w_out, ssd_w_in, ssd_conv_w, ssd_conv_b, ssd_dt_bias, ssd_A_log, ssd_D, ssd_norm_w, ssd_w_out, loss_target, m_norm_w, m_final_norm_w, m_cm_w_in, m_cm_dw_w, m_cm_dw_b, m_cm_ln_w, m_cm_ln_b, m_cm_w_out, m_ssd_w_in, m_ssd_conv_w, m_ssd_conv_b, m_ssd_dt_bias, m_ssd_A_log, m_ssd_D, m_ssd_norm_w, m_ssd_w_out, v_norm_w, v_final_norm_w, v_cm_w_in, v_cm_dw_w, v_cm_dw_b, v_cm_ln_w, v_cm_ln_b, v_cm_w_out, v_ssd_w_in, v_ssd_conv_w, v_ssd_conv_b, v_ssd_dt_bias, v_ssd_A_log, v_ssd_D, v_ssd_norm_w, v_ssd_w_out):
    w = dict(zip(_ORDER, (norm_w, final_norm_w, cm_w_in, cm_dw_w, cm_dw_b, cm_ln_w, cm_ln_b, cm_w_out, ssd_w_in,
                          ssd_conv_w, ssd_conv_b, ssd_dt_bias, ssd_A_log, ssd_D, ssd_norm_w, ssd_w_out)))
    m = dict(zip(_ORDER, (m_norm_w, m_final_norm_w, m_cm_w_in, m_cm_dw_w, m_cm_dw_b, m_cm_ln_w, m_cm_ln_b, m_cm_w_out,
                          m_ssd_w_in, m_ssd_conv_w, m_ssd_conv_b, m_ssd_dt_bias, m_ssd_A_log, m_ssd_D, m_ssd_norm_w,
                          m_ssd_w_out)))
    v = dict(zip(_ORDER, (v_norm_w, v_final_norm_w, v_cm_w_in, v_cm_dw_w, v_cm_dw_b, v_cm_ln_w, v_cm_ln_b, v_cm_w_out,
                          v_ssd_w_in, v_ssd_conv_w, v_ssd_conv_b, v_ssd_dt_bias, v_ssd_A_log, v_ssd_D, v_ssd_norm_w,
                          v_ssd_w_out)))

    D, nl = D_MODEL, cm_w_in.shape[0]
    (sn, sr), (rn, rr) = _SMALL, _REP
    mats = ["cm_w_in", "cm_w_out", "ssd_w_in", "ssd_w_out"]
    bf = lambda a: a.astype(BF16)
    wt_in = jnp.swapaxes(ssd_w_in, 1, 2)
    t_rows = wt_in.shape[1]
    t_pad = -t_rows % 16

    p = {k: w[k] for k in rn}
    p.update({k: [None] * nl for k in mats})
    first = _exchange([(bf(cm_w_in[0]), False), (_pack([w[k] for k in sn], sr, 128, F32), False)], name="gather_first")
    p["cm_w_in"][0] = _join_shards(first[0], 1)
    for k, parts in zip(sn, _unpack(first[1], [w[k].shape for k in sn], lead=(N_DEV,))):
        p[k] = _join_shards(parts, _SHARD_AXIS[k])

    def rest_in(got):
        p["cm_w_out"][0] = got[0].reshape(-1, D)

    def rest_conv(got):
        p["cm_w_in"][1] = _join_shards(got[0], 1)
        p["cm_w_out"][1] = got[1].reshape(-1, D)
        wt = _join_shards(got[2].astype(F32).reshape(N_DEV, nl, -1, D), 1)
        wt = bf(jnp.pad(wt, ((0, 0), (0, SSD_IN_PAD - SSD_IN), (0, 0))))
        wo = got[3].reshape(N_DEV, nl, -1, D)
        for j in range(nl):
            p["ssd_w_in"][j] = wt[j]
            p["ssd_w_out"][j] = wo[:, j].reshape(-1, D)

    hooks = {("cm_in_fwd", 0): (lambda g: [(bf(cm_w_out[0]), False)], rest_in),
             ("cm_conv_fwd", 0): (lambda g: [(bf(cm_w_in[1]), False), (bf(cm_w_out[1]), False),
                                             (bf(wt_in.reshape(-1, D)), False), (bf(ssd_w_out.reshape(-1, D)), False)],
                                  rest_conv)}

    recv = {}

    def blocks(name, j, g):
        if name == "cm_w_in":
            return bf(_split_shards(g[name][j], 1))
        b = _split_shards(g[name][j], 0)
        if name == "ssd_w_in":
            b = jnp.pad(b, ((0, 0), (0, t_pad), (0, 0)))
        return bf(b)

    def sender(keys):
        def items(g):
            return [(blocks(n, j, g), True) for n, j in keys]

        def done(got):
            recv.update(zip(keys, got))

        return items, done

    hooks["ssd_scan_bwd", 1] = sender([(n, 1) for n in mats])
    hooks["cm_conv_bwd", 0] = sender([("ssd_w_in", 0), ("ssd_w_out", 0)])

    loss, dx, g = _local_step(x[0], loss_target[0], p, hooks)
    loss = lax.psum(loss, ("x", "y", "c"))

    last_keys = [("cm_w_in", 0), ("cm_w_out", 0)]
    stacked = {k: jnp.stack(g[k]) for k in sn + rn if k != "final_norm_w"}
    stacked["final_norm_w"] = g["final_norm_w"]
    last = _exchange([(blocks(n, j, g), True) for n, j in last_keys]
                     + [(_pack([_split_shards(stacked[k], _SHARD_AXIS[k]) for k in sn], sr, 128, F32, lead=(N_DEV,)), True),
                        (_pack([stacked[k] for k in rn], rr, 128, F32), False)], name="exchange_last")
    recv.update(zip(last_keys, last))

    outs = {}
    for name in mats:
        shp = w[name].shape
        w3, m3, v3 = [d[name].reshape(nl, -1, shp[-1]) for d in (w, m, v)]
        per_layer = []
        for j in range(nl):
            r = recv[name, j]
            if name == "ssd_w_in":
                gt = _reduce_slots(r, rows=r.shape[1], name="sum_ssd_w_in")
                r = jnp.swapaxes(gt[:t_rows], 0, 1)[None]
            per_layer.append(_reduce_adamw(r, w3, m3, v3, layer=j, rows=min(OPT_ROWS, r.shape[1]), name="adamw_" + name))
        for kind, parts in zip(("g", "d", "m", "v"), zip(*per_layer)):
            outs[kind, name] = jnp.stack(parts).reshape(shp)
    for (names, rows), r, nm in ((_SMALL, last[2], "adamw_small"), (_REP, last[3], "adamw_rep")):
        res = _reduce_adamw(r, *[_pack([d[k] for k in names], rows, 128, F32)[None] for d in (w, m, v)], layer=0,
                            rows=rows, name=nm)
        for kind, buf in zip(("g", "d", "m", "v"), res):
            for k, a in zip(names, _unpack(buf, [w[k].shape for k in names])):
                outs[kind, k] = a
    return (loss, dx[None], *[outs[kind, k] for kind in ("g", "d", "m", "v") for k in _ORDER])
```

```python
import functools
import math

import jax
import jax.numpy as jnp
from jax import lax
from jax.experimental import pallas as pl
from jax.experimental.pallas import tpu as pltpu

F32, BF16 = jnp.float32, jnp.bfloat16
SDS = jax.ShapeDtypeStruct

D_MODEL = 1024
D_INNER = 2048
CONF_KERNEL = 31
HEADDIM = 64
SSD_HEADS = 32
SSD_GROUPS = 4
HPG = SSD_HEADS // SSD_GROUPS
D_STATE = 128
SSD_CONV = 5
CHUNK = 128
GN = SSD_GROUPS * D_STATE
SSD_CONV_DIM = D_INNER + 2 * GN
SSD_IN = D_INNER + SSD_CONV_DIM + 2 * SSD_HEADS
SSD_IN_PAD = SSD_IN + 64
EPS = 1e-5
N_DEV = 8

ADAM_LR, ADAM_B1, ADAM_B2, ADAM_EPS, ADAM_WD, ADAM_STEP = 0.001, 0.9, 0.999, 1e-08, 0.01, 10

VMEM_LIMIT = 56 * 1024 * 1024
ROW_TILE = 512
GATE_TILE = 256
CONV_ROWS = 512
CONV_LANES = 512
CONV_HALO = 16
CONV_RB = 16
CONV_LANES_FEW = 1024
CONV_RB_FEW = 16
CONV_DIRECT_TAPS = 8
NEG = -1e30


def _cparams(sem):
    return pltpu.CompilerParams(dimension_semantics=sem, vmem_limit_bytes=VMEM_LIMIT)


def _resident(shape):
    nd = len(shape)
    return pl.BlockSpec(shape, lambda *_: (0,) * nd, pipeline_mode=pl.Buffered(1))


def _sig(x):
    return 1.0 / (1.0 + jnp.exp(-x))


def _silu(x):
    return x * _sig(x)


def _dsilu(x):
    s = _sig(x)
    return s * (1.0 + x * (1.0 - s))


def _softplus(x):
    return jnp.maximum(x, 0.0) + jnp.log(1.0 + jnp.exp(-jnp.abs(x)))


def _col_chunks(n, width):
    out, c = [], 0
    while c < n:
        out.append((c, min(c + width, n)))
        c += width
    return out


def _rowsum8(v):
    acc = v[0:8]
    for j in range(1, v.shape[0] // 8):
        acc = acc + v[8 * j:8 * j + 8]
    return acc


def _inproj_fwd(h, nw, w, *, tail, wt, name, carry=None):
    S, D = h.shape
    N = w.shape[0] if wt else w.shape[1]
    tm = min(ROW_TILE, S)
    chunks = _col_chunks(N, 1024)

    def body(h_ref, nw_ref, w_ref, o_ref, *rest):
        x = h_ref[...]
        r = lax.rsqrt(jnp.mean(x * x, axis=-1, keepdims=True) + EPS)
        hn = (x * r * nw_ref[...]).astype(BF16)
        for c0, c1 in chunks:
            if wt:
                acc = lax.dot_general(hn, w_ref[c0:c1, :], (((1,), (1,)), ((), ())), preferred_element_type=F32)
            else:
                acc = jnp.dot(hn, w_ref[:, c0:c1], preferred_element_type=F32)
            o_ref[:, c0:c1] = acc.astype(BF16)
            if tail and c1 == N:
                rest[0][...] = acc[:, acc.shape[1] - 128:]

    out_shape = [SDS((S, N), BF16)]
    out_specs = [pl.BlockSpec((tm, N), lambda i: (i, 0))]
    if tail:
        out_shape.append(SDS((S, 128), F32))
        out_specs.append(pl.BlockSpec((tm, 128), lambda i: (i, 0)))
    res, got = _call(
        body, grid=(S // tm,), out_shape=out_shape,
        in_specs=[pl.BlockSpec((tm, D), lambda i: (i, 0)), _resident((1, D)), _resident(w.shape)],
        out_specs=out_specs, scratch_shapes=[], sem=("parallel",), name=name, args=(h, nw, w), carry=carry)
    return (res if tail else res[0]), got


def _tap_plan(offsets):
    rs = sorted({s % 8 for s in offsets})
    return rs, [(k, rs.index(s % 8), s // 8) for k, s in enumerate(offsets)]


def _halo_specs(S, rows, lanes, col0):
    per, last = rows // CONV_HALO, S // CONV_HALO - 1
    cb = col0 // lanes
    return [pl.BlockSpec((rows, lanes), lambda j, i: (i, cb + j)),
            pl.BlockSpec((CONV_HALO, lanes), lambda j, i: (jnp.maximum(i * per - 1, 0), cb + j)),
            pl.BlockSpec((CONV_HALO, lanes), lambda j, i: (jnp.minimum((i + 1) * per, last), cb + j))]


def _fill_window(win, prev, main, nxt, i, nt, rows):
    hb = CONV_HALO
    win[0:hb, :] = jnp.where(i > 0, prev, 0.0)
    win[hb:hb + rows, :] = main
    win[hb + rows:hb + rows + hb, :] = jnp.where(i < nt - 1, nxt, 0.0)


def _fill_shifts(win, sh, rs, n_u):
    for idx, r in enumerate(rs):
        sh[idx] = win[pl.ds(r, n_u), :]


def _tap_rows(win, sh, rs, ri, q, r0, rb, direct):
    if direct:
        return win[pl.ds(r0 + 8 * q + rs[ri], rb), :]
    return sh[ri, pl.ds(pl.multiple_of(r0 + 8 * q, 8), rb), :]


def _row_loop(n, rb, fn, direct):
    if direct:
        for t in range(n):
            fn(t * rb)
    else:
        def step(t, carry):
            fn(pl.multiple_of(t * rb, rb))
            return carry

        lax.fori_loop(0, n, step, 0)


def _conv_blocking(K):
    return (CONV_LANES_FEW, CONV_RB_FEW) if K <= CONV_DIRECT_TAPS else (CONV_LANES, CONV_RB)


def _dwconv_fwd(src, w, b, *, col0, glu_col0, C, name, carry=None):
    S = src.shape[0]
    K = w.shape[0]
    pad = (K - 1) // 2
    rows, hb = min(CONV_ROWS, S), CONV_HALO
    lanes, rb = _conv_blocking(K)
    nt = S // rows
    rs, taps = _tap_plan([k - pad + hb for k in range(K)])
    direct = K <= CONV_DIRECT_TAPS
    n_u = 8 if direct else rows + 2 * hb - 8
    glu = glu_col0 is not None

    def body(*refs):
        if glu:
            vm, vp, vn, gm, gp, gn, w_ref, b_ref, o_ref, win, sh = refs
        else:
            vm, vp, vn, w_ref, b_ref, o_ref, win, sh = refs
        i = pl.program_id(1)

        def act(k):
            v = (vm, vp, vn)[k][...].astype(F32)
            return v * _sig((gm, gp, gn)[k][...].astype(F32)) if glu else v

        _fill_window(win, act(1), act(0), act(2), i, nt, rows)
        if not direct:
            _fill_shifts(win, sh, rs, n_u)

        def rowblk(r0):
            acc = jnp.zeros((rb, lanes), F32) + b_ref[...]
            for k, ri, q in taps:
                acc = acc + w_ref[k:k + 1, :] * _tap_rows(win, sh, rs, ri, q, r0, rb, direct)
            o_ref[pl.ds(r0, rb), :] = acc.astype(BF16)

        _row_loop(rows // rb, rb, rowblk, direct)

    in_specs = _halo_specs(S, rows, lanes, col0)
    args = [src, src, src]
    if glu:
        in_specs += _halo_specs(S, rows, lanes, glu_col0)
        args += [src, src, src]
    in_specs += [pl.BlockSpec((K, lanes), lambda j, i: (0, j)), pl.BlockSpec((1, lanes), lambda j, i: (0, j))]
    res, got = _call(
        body, grid=(C // lanes, nt), out_shape=[SDS((S, C), BF16)], in_specs=in_specs,
        out_specs=[pl.BlockSpec((rows, lanes), lambda j, i: (i, j))],
        scratch_shapes=[pltpu.VMEM((rows + 2 * hb, lanes), F32), pltpu.VMEM((len(rs), n_u, lanes), F32)],
        sem=("parallel", "parallel"), name=name, args=(*args, w, b), carry=carry)
    return res[0], got


def _dwconv_bwd(dsrcs, src, w, *, col0, glu_col0, C, name, carry=None):
    S = src.shape[0]
    K = w.shape[0]
    pad = (K - 1) // 2
    rows, hb = min(CONV_ROWS, S), CONV_HALO
    lanes, rb = _conv_blocking(K)
    nt = S // rows
    rs_u, taps_u = _tap_plan([k - pad + hb for k in range(K)])
    rs_d, taps_d = _tap_plan([hb + pad - k for k in range(K)])
    direct = K <= CONV_DIRECT_TAPS
    n_u = 8 if direct else rows + 2 * hb - 8
    glu = glu_col0 is not None
    nd = len(dsrcs)

    def body(*refs):
        d_refs, refs = refs[:3 * nd], refs[3 * nd:]
        if glu:
            vm, vp, vn, gm, gp, gn, w_ref = refs[:7]
            dv_ref, dg_ref, dw_ref, db_ref, dwin, dsh, uwin, ush, accw, accb = refs[7:]
        else:
            vm, vp, vn, w_ref = refs[:4]
            du_ref, dw_ref, db_ref, dwin, dsh, uwin, ush, accw, accb = refs[4:]
        i = pl.program_id(1)

        def dpiece(k):
            v = d_refs[k][...].astype(F32)
            for e in range(1, nd):
                v = v + d_refs[3 * e + k][...].astype(F32)
            return v

        def act(k):
            v = (vm, vp, vn)[k][...].astype(F32)
            return v * _sig((gm, gp, gn)[k][...].astype(F32)) if glu else v

        _fill_window(dwin, dpiece(1), dpiece(0), dpiece(2), i, nt, rows)
        _fill_window(uwin, act(1), act(0), act(2), i, nt, rows)
        if not direct:
            _fill_shifts(dwin, dsh, rs_d, n_u)
            _fill_shifts(uwin, ush, rs_u, n_u)

        @pl.when(i == 0)
        def _():
            accw[...] = jnp.zeros_like(accw)
            accb[...] = jnp.zeros_like(accb)

        def rowblk(r0):
            du = jnp.zeros((rb, lanes), F32)
            for k, ri, q in taps_d:
                du = du + w_ref[k:k + 1, :] * _tap_rows(dwin, dsh, rs_d, ri, q, r0, rb, direct)
            if glu:
                v = vm[pl.ds(r0, rb), :].astype(F32)
                s = _sig(gm[pl.ds(r0, rb), :].astype(F32))
                dv_ref[pl.ds(r0, rb), :] = (du * s).astype(BF16)
                dg_ref[pl.ds(r0, rb), :] = (du * v * s * (1.0 - s)).astype(BF16)
            else:
                du_ref[pl.ds(r0, rb), :] = du.astype(BF16)
            dmain = dwin[pl.ds(r0 + hb, rb), :] if direct else dwin[pl.ds(pl.multiple_of(r0 + hb, 8), rb), :]
            accb[...] += _rowsum8(dmain)
            for k, ri, q in taps_u:
                accw[k] += _rowsum8(dmain * _tap_rows(uwin, ush, rs_u, ri, q, r0, rb, direct))

        _row_loop(rows // rb, rb, rowblk, direct)

        @pl.when(i == nt - 1)
        def _():
            for k in range(K):
                dw_ref[k:k + 1, :] = jnp.sum(accw[k], axis=0, keepdims=True)
            db_ref[...] = jnp.sum(accb[...], axis=0, keepdims=True)

    in_specs, args = [], []
    for d in dsrcs:
        in_specs += _halo_specs(S, rows, lanes, 0)
        args += [d, d, d]
    in_specs += _halo_specs(S, rows, lanes, col0)
    args += [src, src, src]
    if glu:
        in_specs += _halo_specs(S, rows, lanes, glu_col0)
        args += [src, src, src]
    in_specs += [pl.BlockSpec((K, lanes), lambda j, i: (0, j))]
    tile = pl.BlockSpec((rows, lanes), lambda j, i: (i, j))
    n_act = 2 if glu else 1
    out_shape = [SDS((S, C), BF16)] * n_act + [SDS((K, C), F32), SDS((1, C), F32)]
    out_specs = [tile] * n_act + [pl.BlockSpec((K, lanes), lambda j, i: (0, j)),
                                  pl.BlockSpec((1, lanes), lambda j, i: (0, j))]
    return _call(
        body, grid=(C // lanes, nt), out_shape=out_shape, in_specs=in_specs, out_specs=out_specs,
        scratch_shapes=[pltpu.VMEM((rows + 2 * hb, lanes), F32), pltpu.VMEM((len(rs_d), n_u, lanes), F32),
                        pltpu.VMEM((rows + 2 * hb, lanes), F32), pltpu.VMEM((len(rs_u), n_u, lanes), F32),
                        pltpu.VMEM((K, 8, lanes), F32), pltpu.VMEM((8, lanes), F32)],
        sem=("parallel", "arbitrary"), name=name, args=(*args, w), carry=carry)


def _ln_parts(u, ln_w, ln_b):
    mu = jnp.mean(u, axis=-1, keepdims=True)
    xc = u - mu
    rstd = lax.rsqrt(jnp.mean(xc * xc, axis=-1, keepdims=True) + EPS)
    un = xc * rstd
    return un, rstd, un * ln_w + ln_b


def _conf_out_fwd(uc, proj, h, ln_w, ln_b, wout, *, name):
    S, E = uc.shape
    D = h.shape[1]
    tm = min(GATE_TILE, S)

    def body(uc_ref, z_ref, h_ref, lw_ref, lb_ref, w_ref, o_ref):
        _, _, ul = _ln_parts(uc_ref[...].astype(F32), lw_ref[...], lb_ref[...])
        u3 = (_silu(ul) * _silu(z_ref[...].astype(F32))).astype(BF16)
        o_ref[...] = h_ref[...] + jnp.dot(u3, w_ref[...], preferred_element_type=F32)

    return pl.pallas_call(
        body, grid=(S // tm,), out_shape=SDS((S, D), F32),
        in_specs=[pl.BlockSpec((tm, E), lambda i: (i, 0)), pl.BlockSpec((tm, E), lambda i: (i, 2)),
                  pl.BlockSpec((tm, D), lambda i: (i, 0)), _resident((1, E)), _resident((1, E)), _resident((E, D))],
        out_specs=pl.BlockSpec((tm, D), lambda i: (i, 0)),
        compiler_params=_cparams(("parallel",)), name=name)(uc, proj, h, ln_w, ln_b, wout)


def _conf_out_bwd(dh, uc, proj, ln_w, ln_b, wout, *, name):
    S, E = uc.shape
    D = dh.shape[1]
    tm = min(GATE_TILE, S)
    nt = S // tm

    def body(dh_ref, uc_ref, z_ref, lw_ref, lb_ref, w_ref, dw_ref, duc_ref, dz_ref, dlw_ref, dlb_ref, alw, alb):
        i = pl.program_id(0)

        @pl.when(i == 0)
        def _():
            alw[...] = jnp.zeros_like(alw)
            alb[...] = jnp.zeros_like(alb)
            dw_ref[...] = jnp.zeros_like(dw_ref)

        un, rstd, ul = _ln_parts(uc_ref[...].astype(F32), lw_ref[...], lb_ref[...])
        z = z_ref[...].astype(F32)
        su, sz = _silu(ul), _silu(z)
        dhb = dh_ref[...].astype(BF16)
        dw_ref[...] += lax.dot_general((su * sz).astype(BF16), dhb, (((0,), (0,)), ((), ())),
                                       preferred_element_type=F32)
        du3 = lax.dot_general(dhb, w_ref[...], (((1,), (1,)), ((), ())), preferred_element_type=F32)
        dz_ref[...] = (du3 * su * _dsilu(z)).astype(BF16)
        dul = du3 * sz * _dsilu(ul)
        alw[...] += _rowsum8(dul * un)
        alb[...] += _rowsum8(dul)
        dun = dul * lw_ref[...]
        m1 = jnp.mean(dun, axis=-1, keepdims=True)
        m2 = jnp.mean(dun * un, axis=-1, keepdims=True)
        duc_ref[...] = (rstd * (dun - m1 - un * m2)).astype(BF16)

        @pl.when(i == nt - 1)
        def _():
            dlw_ref[...] = jnp.sum(alw[...], axis=0, keepdims=True)
            dlb_ref[...] = jnp.sum(alb[...], axis=0, keepdims=True)

    tile = pl.BlockSpec((tm, E), lambda i: (i, 0))
    vec = pl.BlockSpec((1, E), lambda i: (0, 0))
    return pl.pallas_call(
        body, grid=(nt,),
        out_shape=[SDS((E, D), F32)] + [SDS((S, E), BF16)] * 2 + [SDS((1, E), F32)] * 2,
        in_specs=[pl.BlockSpec((tm, D), lambda i: (i, 0)), tile, pl.BlockSpec((tm, E), lambda i: (i, 2)),
                  _resident((1, E)), _resident((1, E)), _resident((E, D))],
        out_specs=[pl.BlockSpec((E, D), lambda i: (0, 0)), tile, tile, vec, vec],
        scratch_shapes=[pltpu.VMEM((8, E), F32), pltpu.VMEM((8, E), F32)],
        compiler_params=_cparams(("arbitrary",)), name=name)(dh, uc, proj, ln_w, ln_b, wout)


def _inproj_bwd(parts, offs, w, h, nw, dh_in, *, wt, name):
    S, D = h.shape
    tm = min(ROW_TILE, S)
    nt = S // tm
    widths = [p.shape[1] for p in parts]
    npart = len(parts)

    def body(*refs):
        p_refs = refs[:npart]
        w_ref, h_ref, nw_ref, dh_ref, o_ref, hn_ref, dnw_ref, anw = refs[npart:]
        i = pl.program_id(0)

        @pl.when(i == 0)
        def _():
            anw[...] = jnp.zeros_like(anw)

        x = h_ref[...]
        r = lax.rsqrt(jnp.mean(x * x, axis=-1, keepdims=True) + EPS)
        n = x * r
        hn_ref[...] = (n * nw_ref[...]).astype(BF16)
        dhn = jnp.zeros((tm, D), F32)
        for p_ref, off, wd in zip(p_refs, offs, widths):
            for c0, c1 in _col_chunks(wd, 1024):
                if wt:
                    dhn = dhn + jnp.dot(p_ref[:, c0:c1], w_ref[off + c0:off + c1, :], preferred_element_type=F32)
                else:
                    dhn = dhn + lax.dot_general(p_ref[:, c0:c1], w_ref[:, off + c0:off + c1],
                                                (((1,), (1,)), ((), ())), preferred_element_type=F32)
        anw[...] += _rowsum8(dhn * n)
        dn = dhn * nw_ref[...]
        o_ref[...] = dh_ref[...] + r * (dn - n * jnp.mean(dn * n, axis=-1, keepdims=True))

        @pl.when(i == nt - 1)
        def _():
            dnw_ref[...] = jnp.sum(anw[...], axis=0, keepdims=True)

    tile = pl.BlockSpec((tm, D), lambda i: (i, 0))
    return pl.pallas_call(
        body, grid=(nt,),
        out_shape=[SDS((S, D), F32), SDS((S, D), BF16), SDS((1, D), F32)],
        in_specs=[pl.BlockSpec((tm, wd), lambda i: (i, 0)) for wd in widths]
        + [_resident(w.shape), tile, _resident((1, D)), tile],
        out_specs=[tile, tile, pl.BlockSpec((1, D), lambda i: (0, 0))],
        scratch_shapes=[pltpu.VMEM((8, D), F32)],
        compiler_params=_cparams(("arbitrary",)), name=name)(*parts, w, h, nw, dh_in)


def _tn_matmul(a, b, *, transpose_out, name):
    S, Ka = a.shape
    N = b.shape[1]
    tm = min(ROW_TILE, S)
    bn = min(1024, N)
    nt = S // tm

    def body(a_ref, b_ref, o_ref, acc):
        i = pl.program_id(1)

        @pl.when(i == 0)
        def _():
            acc[...] = jnp.zeros_like(acc)

        acc[...] += lax.dot_general(a_ref[...], b_ref[...].astype(BF16), (((0,), (0,)), ((), ())),
                                    preferred_element_type=F32)

        @pl.when(i == nt - 1)
        def _():
            o_ref[...] = acc[...].T if transpose_out else acc[...]

    if transpose_out:
        out_shape, out_spec = SDS((N, Ka), F32), pl.BlockSpec((bn, Ka), lambda j, i: (j, 0))
    else:
        out_shape, out_spec = SDS((Ka, N), F32), pl.BlockSpec((Ka, bn), lambda j, i: (0, j))
    return pl.pallas_call(
        body, grid=(N // bn, nt), out_shape=out_shape,
        in_specs=[pl.BlockSpec((tm, Ka), lambda j, i: (i, 0)), pl.BlockSpec((tm, bn), lambda j, i: (i, j))],
        out_specs=out_spec, scratch_shapes=[pltpu.VMEM((Ka, bn), F32)],
        compiler_params=_cparams(("parallel", "arbitrary")), name=name)(a, b)


def _loss_head(h, fnw, target, *, name):
    S, D = h.shape
    tm = min(ROW_TILE, S)
    nt = S // tm

    def body(h_ref, w_ref, t_ref, loss_ref, dh_ref, dw_ref, aw, al):
        i = pl.program_id(0)

        @pl.when(i == 0)
        def _():
            aw[...] = jnp.zeros_like(aw)
            al[...] = jnp.zeros_like(al)

        x = h_ref[...]
        r = lax.rsqrt(jnp.mean(x * x, axis=-1, keepdims=True) + EPS)
        n = x * r
        err = n * w_ref[...] - t_ref[...]
        al[...] += _rowsum8(err * err)
        dy = err * (1.0 / D)
        aw[...] += _rowsum8(dy * n)
        dn = dy * w_ref[...]
        dh_ref[...] = r * (dn - n * jnp.mean(dn * n, axis=-1, keepdims=True))

        @pl.when(i == nt - 1)
        def _():
            dw_ref[...] = jnp.sum(aw[...], axis=0, keepdims=True)
            tot = jnp.sum(jnp.sum(al[...], axis=0, keepdims=True), axis=1, keepdims=True) * (0.5 / D)
            loss_ref[...] = jnp.zeros((8, 128), F32) + tot

    tile = pl.BlockSpec((tm, D), lambda i: (i, 0))
    loss, dh, dw = pl.pallas_call(
        body, grid=(nt,), out_shape=[SDS((8, 128), F32), SDS((S, D), F32), SDS((1, D), F32)],
        in_specs=[tile, _resident((1, D)), tile],
        out_specs=[pl.BlockSpec((8, 128), lambda i: (0, 0)), tile, pl.BlockSpec((1, D), lambda i: (0, 0))],
        scratch_shapes=[pltpu.VMEM((8, D), F32), pltpu.VMEM((8, D), F32)],
        compiler_params=_cparams(("arbitrary",)), name=name)(h, fnw, target)
    return loss[0, 0], dh, dw


def _head_mask(shape, head_axis):
    hd = lax.broadcasted_iota(jnp.int32, shape, head_axis)
    ch = lax.broadcasted_iota(jnp.int32, shape, 1 - head_axis)
    return ((ch >= hd * HEADDIM) & (ch < hd * HEADDIM + HEADDIM)).astype(BF16)


def _split_dot(vals, mat, fine):
    R = vals[0].shape[0]
    parts, where = [], []
    for v, f in zip(vals, fine):
        hi = v.astype(BF16)
        where.append((len(parts), f))
        parts += [hi, (v - hi.astype(F32)).astype(BF16)] if f else [hi]
    out = jnp.dot(jnp.concatenate(parts, axis=0), mat, preferred_element_type=F32)
    return [out[R * k:R * k + R] + out[R * k + R:R * k + 2 * R] if f else out[R * k:R * k + R] for k, f in where]


def _nt(a, b):
    return lax.dot_general(a, b, (((1,), (1,)), ((), ())), preferred_element_type=F32)


def _tn(a, b):
    return lax.dot_general(a, b, (((0,), (0,)), ((), ())), preferred_element_type=F32)


def _ssd_setup(c_val, dtraw, bias, alog, fwd):
    xbc = _silu(c_val.astype(F32))
    pre = dtraw + bias
    dt = _softplus(pre)
    A = -jnp.exp(alog)
    row = lax.broadcasted_iota(jnp.int32, (CHUNK, CHUNK), 0)
    col = lax.broadcasted_iota(jnp.int32, (CHUNK, CHUNK), 1)
    T = (col <= row) if fwd else (col >= row)
    cum = jnp.dot(T.astype(F32), dt * A, precision=lax.Precision.HIGHEST, preferred_element_type=F32)
    total = cum[CHUNK - 1:CHUNK] if fwd else cum[0:1]
    return dict(x=xbc[:, :D_INNER], B=xbc[:, D_INNER:D_INNER + GN].astype(BF16),
                C=xbc[:, D_INNER + GN:].astype(BF16), pre=pre, dt=dt, A=A, T=T, Tt=(col >= row) if fwd else (col <= row),
                cum=cum, e=jnp.exp(cum), dtds=dt * jnp.exp(total - cum))


def _ssd_fwd_dir(c_val, dtraw, bias, alog, state, y_ref, hin_ref, fwd, ex):
    s = _ssd_setup(c_val, dtraw, bias, alog, fwd)
    x, cum, T = s["x"], s["cum"], s["T"]
    e_x, dtds_x = _split_dot([s["e"], s["dtds"]], ex, [True, False])
    dec_x = e_x[CHUNK - 1:CHUNK] if fwd else e_x[0:1]
    cumT, dtT = cum.T, s["dt"].T
    xb = x.astype(BF16)
    for g in range(SSD_GROUPS):
        g0, g1 = g * HPG * HEADDIM, (g + 1) * HPG * HEADDIM
        Bg, Cg = s["B"][:, g * D_STATE:(g + 1) * D_STATE], s["C"][:, g * D_STATE:(g + 1) * D_STATE]
        CBg = _nt(Cg, Bg)
        Hg = state[:, g0:g1]
        Hb = Hg.astype(BF16)
        hin_ref[0, :, g0:g1] = Hb
        yoff = jnp.dot(Cg, Hb, preferred_element_type=F32) * e_x[:, g0:g1]
        ys = []
        for hh in range(HPG):
            h = g * HPG + hh
            Lh = jnp.exp(jnp.where(T, cum[:, h:h + 1] - cumT[h:h + 1, :], NEG))
            Mh = (CBg * Lh * dtT[h:h + 1, :]).astype(BF16)
            ys.append(jnp.dot(Mh, xb[:, h * HEADDIM:(h + 1) * HEADDIM], preferred_element_type=F32))
        y_ref[:, g0:g1] = jnp.concatenate(ys, axis=1) + yoff
        xds = (x[:, g0:g1] * dtds_x[:, g0:g1]).astype(BF16)
        state[:, g0:g1] = Hg * dec_x[:, g0:g1] + _tn(Bg, xds)


def _ssd_scan_fwd(cact, tail, bias, alog, *, name):
    S = cact.shape[0]
    nc = S // CHUNK
    E = D_INNER

    def body(cf_ref, cb_ref, tf_ref, tb_ref, bias_ref, alog_ref, yf_ref, yb_ref, hf_ref, hb_ref, state):
        @pl.when(pl.program_id(0) == 0)
        def _():
            state[...] = jnp.zeros_like(state)

        ex = _head_mask((SSD_HEADS, D_INNER), 0)
        H = SSD_HEADS
        _ssd_fwd_dir(cf_ref[...], tf_ref[:, 0:H], bias_ref[0:1, :], alog_ref[0:1, :], state.at[0], yf_ref, hf_ref, True, ex)
        _ssd_fwd_dir(cb_ref[...], tb_ref[:, H:2 * H], bias_ref[1:2, :], alog_ref[1:2, :], state.at[1], yb_ref, hb_ref, False, ex)

    up, down = (lambda i: (i, 0)), (lambda i: (nc - 1 - i, 0))
    up3, down3 = (lambda i: (i, 0, 0)), (lambda i: (nc - 1 - i, 0, 0))
    W = cact.shape[1]
    return pl.pallas_call(
        body, grid=(nc,),
        out_shape=[SDS((S, E), F32), SDS((S, E), F32), SDS((nc, D_STATE, E), BF16), SDS((nc, D_STATE, E), BF16)],
        in_specs=[pl.BlockSpec((CHUNK, W), up), pl.BlockSpec((CHUNK, W), down),
                  pl.BlockSpec((CHUNK, 128), up), pl.BlockSpec((CHUNK, 128), down),
                  _resident((2, SSD_HEADS)), _resident((2, SSD_HEADS))],
        out_specs=[pl.BlockSpec((CHUNK, E), up), pl.BlockSpec((CHUNK, E), down),
                   pl.BlockSpec((1, D_STATE, E), up3), pl.BlockSpec((1, D_STATE, E), down3)],
        scratch_shapes=[pltpu.VMEM((2, D_STATE, E), F32)],
        compiler_params=_cparams(("arbitrary",)), name=name)(cact, cact, tail, tail, bias, alog)


def _ssd_bwd_dir(c_val, dtraw, bias, alog, dy, hin_ref, dskip_x, G, dc_ref, ddt_ref, accA, accb, fwd, ex, ext):
    s = _ssd_setup(c_val, dtraw, bias, alog, fwd)
    x, cum, T, dt = s["x"], s["cum"], s["T"], s["dt"]
    e_x, dtds_x = _split_dot([s["e"], s["dtds"]], ex, [True, False])
    dec_x = e_x[CHUNK - 1:CHUNK] if fwd else e_x[0:1]
    cumT, dtT = cum.T, dt.T
    xb = x.astype(BF16)
    dyf = dy.astype(F32)
    dx_diag, dx_st, dBs, dCs, gh, yoff_dy = [], [], [], [], [], []
    lane_h = lax.broadcasted_iota(jnp.int32, (CHUNK, SSD_HEADS), 1)
    sub_h = lax.broadcasted_iota(jnp.int32, (SSD_HEADS, CHUNK), 0)
    rows = jnp.zeros((CHUNK, SSD_HEADS), F32)
    colsT = jnp.zeros((SSD_HEADS, CHUNK), F32)
    for g in range(SSD_GROUPS):
        g0, g1 = g * HPG * HEADDIM, (g + 1) * HPG * HEADDIM
        Bg, Cg = s["B"][:, g * D_STATE:(g + 1) * D_STATE], s["C"][:, g * D_STATE:(g + 1) * D_STATE]
        CBg = _nt(Cg, Bg)
        Hb = hin_ref[0, :, g0:g1]
        Gg = G[:, g0:g1]
        Gb = Gg.astype(BF16)
        edyf = dyf[:, g0:g1] * e_x[:, g0:g1]
        edy = edyf.astype(BF16)
        yoff_dy.append(edyf * jnp.dot(Cg, Hb, preferred_element_type=F32))
        dC_g = _nt(edy, Hb)
        dH_g = _tn(Cg, edy)
        xds = (x[:, g0:g1] * dtds_x[:, g0:g1]).astype(BF16)
        dB_g = _nt(xds, Gb)
        dx_st.append(dtds_x[:, g0:g1] * jnp.dot(Bg, Gb, preferred_element_type=F32))
        dCB = jnp.zeros((CHUNK, CHUNK), F32)
        for hh in range(HPG):
            h = g * HPG + hh
            hs = slice(h * HEADDIM, (h + 1) * HEADDIM)
            Lh = jnp.exp(jnp.where(T, cum[:, h:h + 1] - cumT[h:h + 1, :], NEG))
            Mh = (CBg * Lh * dtT[h:h + 1, :]).astype(BF16)
            term = _nt(dy[:, hs], xb[:, hs]) * dtT[h:h + 1, :] * Lh
            dCB = dCB + term
            Wh = term * CBg
            rows = rows + jnp.where(lane_h == h, jnp.sum(Wh, axis=1, keepdims=True), 0.0)
            colsT = colsT + jnp.where(sub_h == h, jnp.sum(Wh, axis=0, keepdims=True), 0.0)
            dx_diag.append(_tn(Mh, dy[:, hs]))
        dCBb = dCB.astype(BF16)
        dCs.append(dC_g + jnp.dot(dCBb, Bg, preferred_element_type=F32))
        dBs.append(dB_g + _tn(dCBb, Cg))
        gh.append(_rowsum8(Gg * Hb.astype(F32)))
        G[:, g0:g1] = Gg * dec_x[:, g0:g1] + dH_g
    dxn = jnp.concatenate(dx_diag, axis=1) + jnp.concatenate(dx_st, axis=1)
    xst = x * jnp.concatenate(dx_st, axis=1)
    RZ, Zx = _split_dot([jnp.concatenate(yoff_dy, axis=1) - xst, x * dxn], ext, [True, False])
    ddec, zs_tot = _split_dot([jnp.concatenate(gh, axis=1), _rowsum8(xst)], ext, [True, True])
    dec = s["e"][CHUNK - 1:CHUNK] if fwd else s["e"][0:1]
    dtotal = jnp.sum(zs_tot + dec * ddec, axis=0, keepdims=True)
    rowi = lax.broadcasted_iota(jnp.int32, (CHUNK, SSD_HEADS), 0)
    dacs = rows - colsT.T + RZ + jnp.where(rowi == (CHUNK - 1 if fwd else 0), dtotal, 0.0)
    da = jnp.dot(s["Tt"].astype(F32), dacs, precision=lax.Precision.HIGHEST, preferred_element_type=F32)
    accA[...] += _rowsum8(da * dt)
    sg = _sig(s["pre"])
    ratio = jnp.where(dt > 1e-30, sg / jnp.maximum(dt, 1e-30), 1.0)
    ddtraw = da * s["A"] * sg + Zx * ratio
    ddt_ref[...] = ddtraw
    accb[...] += _rowsum8(ddtraw)
    dx = dxn + dyf * dskip_x if fwd else dxn
    dact = jnp.concatenate([dx] + dBs + dCs, axis=1)
    dc_ref[...] = (dact * _dsilu(c_val.astype(F32))).astype(BF16)


def _ssd_scan_bwd(cact, tail, bias, alog, dy, hf, hb, dskip_x, *, name, carry=None):
    S = cact.shape[0]
    nc = S // CHUNK
    E, H, W = D_INNER, SSD_HEADS, cact.shape[1]

    def body(cf_ref, cb_ref, tf_ref, tb_ref, bias_ref, alog_ref, dyf_ref, dyb_ref, hf_ref, hb_ref,
             dsk_ref, dcf_ref, dcb_ref, ddtf_ref, ddtb_ref, dalog_ref, dbias_ref, G, accA, accb):
        i = pl.program_id(0)

        @pl.when(i == 0)
        def _():
            G[...] = jnp.zeros_like(G)
            accA[...] = jnp.zeros_like(accA)
            accb[...] = jnp.zeros_like(accb)

        ex = _head_mask((H, E), 0)
        ext = _head_mask((E, H), 1)
        _ssd_bwd_dir(cf_ref[...], tf_ref[:, 0:H], bias_ref[0:1, :], alog_ref[0:1, :], dyf_ref[...], hf_ref,
                     dsk_ref[...], G.at[0], dcf_ref, ddtf_ref, accA.at[0], accb.at[0], True, ex, ext)
        _ssd_bwd_dir(cb_ref[...], tb_ref[:, H:2 * H], bias_ref[1:2, :], alog_ref[1:2, :], dyb_ref[...], hb_ref,
                     dsk_ref[...], G.at[1], dcb_ref, ddtb_ref, accA.at[1], accb.at[1], False, ex, ext)

        @pl.when(i == nc - 1)
        def _():
            for d in range(2):
                A = -jnp.exp(alog_ref[d:d + 1, :])
                dalog_ref[d:d + 1, :] = jnp.sum(accA[d], axis=0, keepdims=True) * A
                dbias_ref[d:d + 1, :] = jnp.sum(accb[d], axis=0, keepdims=True)

    up, down = (lambda i: (i, 0)), (lambda i: (nc - 1 - i, 0))
    up3, down3 = (lambda i: (i, 0, 0)), (lambda i: (nc - 1 - i, 0, 0))
    small = pl.BlockSpec((2, H), lambda i: (0, 0))
    return _call(
        body, grid=(nc,),
        out_shape=[SDS((S, W), BF16), SDS((S, W), BF16), SDS((S, H), F32), SDS((S, H), F32),
                   SDS((2, H), F32), SDS((2, H), F32)],
        in_specs=[pl.BlockSpec((CHUNK, W), down), pl.BlockSpec((CHUNK, W), up),
                  pl.BlockSpec((CHUNK, 128), down), pl.BlockSpec((CHUNK, 128), up),
                  _resident((2, H)), _resident((2, H)),
                  pl.BlockSpec((CHUNK, E), down), pl.BlockSpec((CHUNK, E), up),
                  pl.BlockSpec((1, D_STATE, E), down3), pl.BlockSpec((1, D_STATE, E), up3),
                  _resident((1, E))],
        out_specs=[pl.BlockSpec((CHUNK, W), down), pl.BlockSpec((CHUNK, W), up),
                   pl.BlockSpec((CHUNK, H), down), pl.BlockSpec((CHUNK, H), up), small, small],
        scratch_shapes=[pltpu.VMEM((2, D_STATE, E), F32), pltpu.VMEM((2, 8, H), F32), pltpu.VMEM((2, 8, H), F32)],
        sem=("arbitrary",), name=name, args=(cact, cact, tail, tail, bias, alog, dy, dy, hf, hb, dskip_x), carry=carry)


def _ssd_gate(yf, yb, cx, z, dsk, nw):
    xs = _silu(cx.astype(F32))
    y = yf + yb + xs * dsk
    sz = _silu(z)
    gt = y * sz
    r = lax.rsqrt(jnp.mean(gt * gt, axis=-1, keepdims=True) + EPS)
    return xs, y, sz, r, gt * r


def _ssd_out_fwd(yf, yb, cact, proj, dskip_x, nw, h, wout, *, name):
    S, E = yf.shape
    D = h.shape[1]
    tm = min(GATE_TILE, S)

    def body(yf_ref, yb_ref, cx_ref, z_ref, dsk_ref, nw_ref, h_ref, w_ref, o_ref):
        _, _, _, _, n = _ssd_gate(yf_ref[...], yb_ref[...], cx_ref[...], z_ref[...].astype(F32), dsk_ref[...], nw_ref[...])
        o_ref[...] = h_ref[...] + jnp.dot((n * nw_ref[...]).astype(BF16), w_ref[...], preferred_element_type=F32)

    tile = pl.BlockSpec((tm, E), lambda i: (i, 0))
    return pl.pallas_call(
        body, grid=(S // tm,), out_shape=SDS((S, D), F32),
        in_specs=[tile, tile, tile, tile, _resident((1, E)), _resident((1, E)),
                  pl.BlockSpec((tm, D), lambda i: (i, 0)), _resident((E, D))],
        out_specs=pl.BlockSpec((tm, D), lambda i: (i, 0)),
        compiler_params=_cparams(("parallel",)), name=name)(yf, yb, cact, proj, dskip_x, nw, h, wout)


def _ssd_out_bwd(dh, yf, yb, cact, proj, dskip_x, nw, wout, *, name):
    S, E = yf.shape
    D = dh.shape[1]
    tm = min(GATE_TILE, S)
    nt = S // tm

    def body(dh_ref, yf_ref, yb_ref, cx_ref, z_ref, dsk_ref, nw_ref, w_ref, dw_ref, dy_ref, dz_ref, dnw_ref, dd_ref, anw, ad):
        i = pl.program_id(0)

        @pl.when(i == 0)
        def _():
            anw[...] = jnp.zeros_like(anw)
            ad[...] = jnp.zeros_like(ad)
            dw_ref[...] = jnp.zeros_like(dw_ref)

        z = z_ref[...].astype(F32)
        xs, y, sz, r, n = _ssd_gate(yf_ref[...], yb_ref[...], cx_ref[...], z, dsk_ref[...], nw_ref[...])
        dhb = dh_ref[...].astype(BF16)
        dw_ref[...] += _tn((n * nw_ref[...]).astype(BF16), dhb)
        dyn = _nt(dhb, w_ref[...])
        anw[...] += _rowsum8(dyn * n)
        dn = dyn * nw_ref[...]
        dgt = r * (dn - n * jnp.mean(dn * n, axis=-1, keepdims=True))
        dy = dgt * sz
        dy_ref[...] = dy.astype(BF16)
        dz_ref[...] = (dgt * y * _dsilu(z)).astype(BF16)
        ad[...] += _rowsum8(dy * xs)

        @pl.when(i == nt - 1)
        def _():
            dnw_ref[...] = jnp.sum(anw[...], axis=0, keepdims=True)
            dd_ref[...] = jnp.sum(ad[...], axis=0, keepdims=True)

    tile = pl.BlockSpec((tm, E), lambda i: (i, 0))
    vec = pl.BlockSpec((1, E), lambda i: (0, 0))
    return pl.pallas_call(
        body, grid=(nt,), out_shape=[SDS((E, D), F32)] + [SDS((S, E), BF16)] * 2 + [SDS((1, E), F32)] * 2,
        in_specs=[pl.BlockSpec((tm, D), lambda i: (i, 0)), tile, tile, tile, tile,
                  _resident((1, E)), _resident((1, E)), _resident((E, D))],
        out_specs=[pl.BlockSpec((E, D), lambda i: (0, 0)), tile, tile, vec, vec],
        scratch_shapes=[pltpu.VMEM((8, E), F32), pltpu.VMEM((8, E), F32)],
        compiler_params=_cparams(("arbitrary",)), name=name)(dh, yf, yb, cact, proj, dskip_x, nw, wout)


def _local_step(x, target, p, hooks):
    E, S = D_INNER, x.shape[0]
    depth = p["norm_w"].shape[0]
    names = ["norm_w", "cm_w_in", "cm_dw_w", "cm_dw_b", "cm_ln_w", "cm_ln_b", "cm_w_out", "ssd_w_in", "ssd_conv_w",
             "ssd_conv_b", "ssd_dt_bias", "ssd_A_log", "ssd_D", "ssd_norm_w", "ssd_w_out"]
    g = {n: [None] * (depth if n == "norm_w" else depth // 2) for n in names}

    def carried(key):
        items, done = hooks.get(key, (None, None))
        return (items(g) if items else None), (done if done else lambda got: None)

    h, saved = x, []
    for i in range(depth):
        j, nw = i // 2, p["norm_w"][i][None]
        if i % 2 == 0:
            carry, done = carried(("cm_in_fwd", i))
            proj, got = _inproj_fwd(h, nw, p["cm_w_in"][j], tail=False, wt=False, name="cm_in_fwd", carry=carry)
            done(got)
            carry, done = carried(("cm_conv_fwd", i))
            uc, got = _dwconv_fwd(proj, p["cm_dw_w"][j], p["cm_dw_b"][j][None], col0=0, glu_col0=E, C=E,
                                  name="cm_conv_fwd", carry=carry)
            done(got)
            hn = _conf_out_fwd(uc, proj, h, p["cm_ln_w"][j][None], p["cm_ln_b"][j][None], p["cm_w_out"][j], name="cm_out_fwd")
            saved.append((h, proj, uc))
        else:
            (proj, tail), _ = _inproj_fwd(h, nw, p["ssd_w_in"][j], tail=True, wt=True, name="ssd_in_fwd")
            cact, _ = _dwconv_fwd(proj, p["ssd_conv_w"][j], p["ssd_conv_b"][j][None], col0=E, glu_col0=None,
                                  C=SSD_CONV_DIM, name="ssd_conv_fwd")
            yf, yb, hf, hb = _ssd_scan_fwd(cact, tail, p["ssd_dt_bias"][j], p["ssd_A_log"][j], name="ssd_scan_fwd")
            dsk = jnp.repeat(p["ssd_D"][j], HEADDIM)[None]
            hn = _ssd_out_fwd(yf, yb, cact, proj, dsk, p["ssd_norm_w"][j][None], h, p["ssd_w_out"][j], name="ssd_out_fwd")
            saved.append((h, proj, tail, cact, yf, yb, hf, hb, dsk))
        h = hn
    loss, dh, g_fnw = _loss_head(h, p["final_norm_w"][None], target, name="loss_head")

    for i in reversed(range(depth)):
        j, nw = i // 2, p["norm_w"][i][None]
        if i % 2 == 0:
            h_in, proj, uc = saved[i]
            g["cm_w_out"][j], duc, dz, g_lw, g_lb = _conf_out_bwd(
                dh, uc, proj, p["cm_ln_w"][j][None], p["cm_ln_b"][j][None], p["cm_w_out"][j], name="cm_out_bwd")
            carry, done = carried(("cm_conv_bwd", i))
            (dv, dg, g_dw, g_db), got = _dwconv_bwd([duc], proj, p["cm_dw_w"][j], col0=0, glu_col0=E, C=E,
                                                    name="cm_conv_bwd", carry=carry)
            done(got)
            parts = [dv, dg, dz]
            dh, hnb, g_nw = _inproj_bwd(parts, [0, E, 2 * E], p["cm_w_in"][j], h_in, nw, dh, wt=False, name="cm_in_bwd")
            g["cm_w_in"][j] = jnp.concatenate(
                [_tn_matmul(hnb, q, transpose_out=False, name="cm_win_grad") for q in parts], axis=1)
            g["cm_dw_w"][j], g["cm_dw_b"][j], g["cm_ln_w"][j], g["cm_ln_b"][j] = g_dw, g_db[0], g_lw[0], g_lb[0]
        else:
            h_in, proj, tail, cact, yf, yb, hf, hb, dsk = saved[i]
            g["ssd_w_out"][j], dy, dz, g_sn, g_dx = _ssd_out_bwd(
                dh, yf, yb, cact, proj, dsk, p["ssd_norm_w"][j][None], p["ssd_w_out"][j], name="ssd_out_bwd")
            carry, done = carried(("ssd_scan_bwd", i))
            (dcf, dcb, ddtf, ddtb, g_al, g_bias), got = _ssd_scan_bwd(
                cact, tail, p["ssd_dt_bias"][j], p["ssd_A_log"][j], dy, hf, hb, dsk, name="ssd_scan_bwd", carry=carry)
            done(got)
            (dxbc, g_cw, g_cb), _ = _dwconv_bwd([dcf, dcb], proj, p["ssd_conv_w"][j], col0=E, glu_col0=None,
                                                C=SSD_CONV_DIM, name="ssd_conv_bwd")
            ddt = jnp.concatenate([ddtf, ddtb, jnp.zeros((S, 128 - 2 * SSD_HEADS), F32)], axis=1).astype(BF16)
            parts = [dz, dxbc, ddt]
            dh, hnb, g_nw = _inproj_bwd(parts, [0, E, E + SSD_CONV_DIM], p["ssd_w_in"][j], h_in, nw, dh, wt=True,
                                        name="ssd_in_bwd")
            gw = [_tn_matmul(hnb, q, transpose_out=True, name="ssd_win_grad") for q in parts]
            g["ssd_w_in"][j] = jnp.concatenate([gw[0], gw[1], gw[2][:2 * SSD_HEADS]], axis=0)
            g["ssd_conv_w"][j], g["ssd_conv_b"][j], g["ssd_norm_w"][j] = g_cw, g_cb[0], g_sn[0]
            g["ssd_dt_bias"][j], g["ssd_A_log"][j] = g_bias, g_al
            g["ssd_D"][j] = jnp.sum(g_dx.reshape(SSD_HEADS, HEADDIM), axis=1)
        g["norm_w"][i] = g_nw[0]
    g["final_norm_w"] = g_fnw[0]
    return loss, dh, g


_FLIPS = [(fx, fy, fc) for fx in (0, 1) for fy in (0, 1) for fc in (0, 1)][1:]


def _exchange_copies(items, ins, outs, send_sems, recv_sems, local_sems, arrivals):
    x, y, c = lax.axis_index("x"), lax.axis_index("y"), lax.axis_index("c")
    me = 4 * x + 2 * y + c
    local, pairs = [], []
    for b, (_, scatter) in enumerate(items):
        local.append(pltpu.make_async_copy(ins[b].at[me] if scatter else ins[b], outs[b].at[me], local_sems.at[b]))
        for k, (fx, fy, fc) in enumerate(_FLIPS):
            px, py, pc = x ^ fx, y ^ fy, c ^ fc
            peer = 4 * px + 2 * py + pc
            src = ins[b].at[peer] if scatter else ins[b]
            sems = dict(send_sem=send_sems.at[b * 7 + k], recv_sem=recv_sems.at[b * 7 + k],
                        device_id=(px, py, pc), device_id_type=pl.DeviceIdType.MESH)
            pairs.append((pltpu.make_async_remote_copy(src_ref=src, dst_ref=outs[b].at[me], **sems),
                          pltpu.make_async_remote_copy(src_ref=src, dst_ref=outs[b].at[peer], **sems)
                          if arrivals else None))
    return local, pairs


def _exchange_start(items, ins, outs, *sems):
    local, pairs = _exchange_copies(items, ins, outs, *sems, arrivals=False)
    for cp in local:
        cp.start()
    for cp, _ in pairs:
        cp.start()


def _exchange_wait(items, ins, outs, *sems):
    local, pairs = _exchange_copies(items, ins, outs, *sems, arrivals=True)
    for _, arrival in pairs:
        arrival.wait_recv()
    for cp, _ in pairs:
        cp.wait_send()
    for cp in local:
        cp.wait()


def _exchange_shapes(items):
    n = len(items)
    out_shape = [SDS((N_DEV,) + tuple(a.shape[-2:]), a.dtype) for a, _ in items]
    sems = [pltpu.SemaphoreType.DMA((7 * n,)), pltpu.SemaphoreType.DMA((7 * n,)), pltpu.SemaphoreType.DMA((n,))]
    return out_shape, sems


def _exchange(items, *, name):
    n = len(items)

    def body(*refs):
        _exchange_start(items, refs[:n], refs[n:2 * n], *refs[2 * n:])
        _exchange_wait(items, refs[:n], refs[n:2 * n], *refs[2 * n:])

    hbm = pl.BlockSpec(memory_space=pltpu.HBM)
    out_shape, sems = _exchange_shapes(items)
    return pl.pallas_call(body, out_shape=out_shape, in_specs=[hbm] * n, out_specs=[hbm] * n,
                          scratch_shapes=sems, name=name)(*[a for a, _ in items])


def _call(body, *, grid, out_shape, in_specs, out_specs, scratch_shapes, sem, name, args, carry):
    if not carry:
        return pl.pallas_call(body, grid=grid, out_shape=out_shape, in_specs=in_specs, out_specs=out_specs,
                              scratch_shapes=scratch_shapes, compiler_params=_cparams(sem), name=name)(*args), None
    n, n_in, n_out, n_scr = len(carry), len(in_specs), len(out_specs), len(scratch_shapes)
    last_step = tuple(g - 1 for g in grid)

    def wrapped(*refs):
        ins, cin = refs[:n_in], refs[n_in:n_in + n]
        outs, cout = refs[n_in + n:n_in + n + n_out], refs[n_in + n + n_out:n_in + 2 * n + n_out]
        scr, sems = refs[n_in + 2 * n + n_out:n_in + 2 * n + n_out + n_scr], refs[n_in + 2 * n + n_out + n_scr:]
        first = last = None
        for d, top in enumerate(last_step):
            i = pl.program_id(d)
            first = (i == 0) if first is None else first & (i == 0)
            last = (i == top) if last is None else last & (i == top)

        @pl.when(first)
        def _():
            _exchange_start(carry, cin, cout, *sems)

        body(*ins, *outs, *scr)

        @pl.when(last)
        def _():
            _exchange_wait(carry, cin, cout, *sems)

    hbm = pl.BlockSpec(memory_space=pltpu.HBM)
    x_shape, x_sems = _exchange_shapes(carry)
    res = pl.pallas_call(
        wrapped, grid=grid, out_shape=list(out_shape) + x_shape, in_specs=list(in_specs) + [hbm] * n,
        out_specs=list(out_specs) + [hbm] * n, scratch_shapes=list(scratch_shapes) + x_sems,
        compiler_params=_cparams(("arbitrary",) * len(grid)), name=name)(*args, *[a for a, _ in carry])
    return res[:n_out], res[n_out:]


def _reduce_slots(recv, *, rows, name):
    n, R, C = recv.shape

    def body(r_ref, g_ref):
        g = r_ref[0].astype(F32)
        for s in range(1, n):
            g = g + r_ref[s].astype(F32)
        g_ref[...] = g

    return pl.pallas_call(
        body, grid=(R // rows,), out_shape=SDS((R, C), F32),
        in_specs=[pl.BlockSpec((n, rows, C), lambda i: (0, i, 0))],
        out_specs=pl.BlockSpec((rows, C), lambda i: (i, 0)),
        compiler_params=_cparams(("parallel",)), name=name)(recv)


def _reduce_adamw(recv, w, m, v, *, layer, rows, name):
    _, R, C = w.shape
    n = recv.shape[0]
    c1 = 1.0 / (1.0 - ADAM_B1 ** ADAM_STEP)
    c2 = 1.0 / (1.0 - ADAM_B2 ** ADAM_STEP)

    def body(r_ref, w_ref, m_ref, v_ref, g_ref, d_ref, mo_ref, vo_ref):
        g = r_ref[0].astype(F32)
        for s in range(1, n):
            g = g + r_ref[s].astype(F32)
        mn = ADAM_B1 * m_ref[...] + (1.0 - ADAM_B1) * g
        vn = ADAM_B2 * v_ref[...] + (1.0 - ADAM_B2) * (g * g)
        g_ref[...] = g
        mo_ref[...] = mn
        vo_ref[...] = vn
        d_ref[...] = -ADAM_LR * ((mn * c1) / (jnp.sqrt(vn * c2) + ADAM_EPS) + ADAM_WD * w_ref[...])

    tile = pl.BlockSpec((rows, C), lambda i: (i, 0))
    mine = pl.BlockSpec((None, rows, C), lambda i: (layer, i, 0))
    return pl.pallas_call(
        body, grid=(R // rows,), out_shape=[SDS((R, C), F32)] * 4,
        in_specs=[pl.BlockSpec((n, rows, C), lambda i: (0, i, 0)), mine, mine, mine],
        out_specs=[tile] * 4, compiler_params=_cparams(("parallel",)), name=name)(recv, w, m, v)


_ORDER = ["norm_w", "final_norm_w", "cm_w_in", "cm_dw_w", "cm_dw_b", "cm_ln_w", "cm_ln_b", "cm_w_out", "ssd_w_in",
          "ssd_conv_w", "ssd_conv_b", "ssd_dt_bias", "ssd_A_log", "ssd_D", "ssd_norm_w", "ssd_w_out"]
_SMALL = (["cm_dw_w", "ssd_conv_w", "ssd_conv_b", "ssd_norm_w"], 168)
_REP = (["norm_w", "final_norm_w", "cm_dw_b", "cm_ln_w", "cm_ln_b", "ssd_dt_bias", "ssd_A_log", "ssd_D"], 144)
_SHARD_AXIS = {"cm_w_in": 2, "cm_dw_w": 2, "cm_w_out": 1, "ssd_w_in": 2, "ssd_conv_w": 2, "ssd_conv_b": 1,
               "ssd_norm_w": 1, "ssd_w_out": 1}
OPT_ROWS = 256


def _pack(arrs, rows, lanes, dtype, lead=()):
    nl = len(lead)
    flat = jnp.concatenate([a.reshape(lead + (-1,)).astype(dtype) for a in arrs], axis=nl)
    flat = jnp.pad(flat, [(0, 0)] * nl + [(0, rows * lanes - flat.shape[nl])])
    return flat.reshape(lead + (rows, lanes))


def _unpack(buf, shapes, lead=()):
    nl = len(lead)
    flat = buf.reshape(lead + (-1,))
    out, off = [], 0
    for s in shapes:
        sz = math.prod(s)
        out.append(flat[..., off:off + sz].reshape(lead + tuple(s)))
        off += sz
    return out


def _join_shards(parts, axis):
    full = jnp.moveaxis(parts, 0, axis)
    sh = full.shape
    return full.reshape(sh[:axis] + (sh[axis] * sh[axis + 1],) + sh[axis + 2:])


def _split_shards(full, axis):
    sh = full.shape
    parts = full.reshape(sh[:axis] + (N_DEV, sh[axis] // N_DEV) + sh[axis + 1:])
    return jnp.moveaxis(parts, axis, 0)


def kernel(x, norm_w, final_norm_w, cm_w_in, cm_dw_w, cm_dw_b, cm_ln_w, cm_ln_b, cm_w_out, ssd_w_in, ssd_conv_w, ssd_conv_b, ssd_dt_bias, ssd_A_log, ssd_D, ssd_norm_w, ssd_w_out, loss_target, m_norm_w, m_final_norm_w, m_cm_w_in, m_cm_dw_w, m_cm_dw_b, m_cm_ln_w, m_cm_ln_b, m_cm_w_out, m_ssd_w_in, m_ssd_conv_w, m_ssd_conv_b, m_ssd_dt_bias, m_ssd_A_log, m_ssd_D, m_ssd_norm_w, m_ssd_w_out, v_norm_w, v_final_norm_w, v_cm_w_in, v_cm_dw_w, v_cm_dw_b, v_cm_ln_w, v_cm_ln_b, v_cm_w_out, v_ssd_w_in, v_ssd_conv_w, v_ssd_conv_b, v_ssd_dt_bias, v_ssd_A_log, v_ssd_D, v_ssd_norm_w, v_ssd_w_out):
    w = dict(zip(_ORDER, (norm_w, final_norm_w, cm_w_in, cm_dw_w, cm_dw_b, cm_ln_w, cm_ln_b, cm_w_out, ssd_w_in,
                          ssd_conv_w, ssd_conv_b, ssd_dt_bias, ssd_A_log, ssd_D, ssd_norm_w, ssd_w_out)))
    m = dict(zip(_ORDER, (m_norm_w, m_final_norm_w, m_cm_w_in, m_cm_dw_w, m_cm_dw_b, m_cm_ln_w, m_cm_ln_b, m_cm_w_out,
                          m_ssd_w_in, m_ssd_conv_w, m_ssd_conv_b, m_ssd_dt_bias, m_ssd_A_log, m_ssd_D, m_ssd_norm_w,
                          m_ssd_w_out)))
    v = dict(zip(_ORDER, (v_norm_w, v_final_norm_w, v_cm_w_in, v_cm_dw_w, v_cm_dw_b, v_cm_ln_w, v_cm_ln_b, v_cm_w_out,
                          v_ssd_w_in, v_ssd_conv_w, v_ssd_conv_b, v_ssd_dt_bias, v_ssd_A_log, v_ssd_D, v_ssd_norm_w,
                          v_ssd_w_out)))

    D, nl = D_MODEL, cm_w_in.shape[0]
    (sn, sr), (rn, rr) = _SMALL, _REP
    mats = ["cm_w_in", "cm_w_out", "ssd_w_in", "ssd_w_out"]
    bf = lambda a: a.astype(BF16)
    wt_in = jnp.swapaxes(ssd_w_in, 1, 2)
    t_rows = wt_in.shape[1]
    t_pad = -t_rows % 16

    p = {k: w[k] for k in rn}
    p.update({k: [None] * nl for k in mats})
    first = _exchange([(bf(cm_w_in[0]), False), (_pack([w[k] for k in sn], sr, 128, F32), False)], name="gather_first")
    p["cm_w_in"][0] = _join_shards(first[0], 1)
    for k, parts in zip(sn, _unpack(first[1], [w[k].shape for k in sn], lead=(N_DEV,))):
        p[k] = _join_shards(parts, _SHARD_AXIS[k])

    def rest_in(got):
        p["cm_w_out"][0] = got[0].reshape(-1, D)

    def rest_conv(got):
        p["cm_w_in"][1] = _join_shards(got[0], 1)
        p["cm_w_out"][1] = got[1].reshape(-1, D)
        wt = _join_shards(got[2].astype(F32).reshape(N_DEV, nl, -1, D), 1)
        wt = bf(jnp.pad(wt, ((0, 0), (0, SSD_IN_PAD - SSD_IN), (0, 0))))
        wo = got[3].reshape(N_DEV, nl, -1, D)
        for j in range(nl):
            p["ssd_w_in"][j] = wt[j]
            p["ssd_w_out"][j] = wo[:, j].reshape(-1, D)

    hooks = {("cm_in_fwd", 0): (lambda g: [(bf(cm_w_out[0]), False)], rest_in),
             ("cm_conv_fwd", 0): (lambda g: [(bf(cm_w_in[1]), False), (bf(cm_w_out[1]), False),
                                             (bf(wt_in.reshape(-1, D)), False), (bf(ssd_w_out.reshape(-1, D)), False)],
                                  rest_conv)}

    recv = {}

    def blocks(name, j, g):
        if name == "cm_w_in":
            return bf(_split_shards(g[name][j], 1))
        b = _split_shards(g[name][j], 0)
        if name == "ssd_w_in":
            b = jnp.pad(b, ((0, 0), (0, t_pad), (0, 0)))
        return bf(b)

    def sender(keys):
        def items(g):
            return [(blocks(n, j, g), True) for n, j in keys]

        def done(got):
            recv.update(zip(keys, got))

        return items, done

    hooks["ssd_scan_bwd", 1] = sender([(n, 1) for n in mats])
    hooks["cm_conv_bwd", 0] = sender([("ssd_w_in", 0), ("ssd_w_out", 0)])

    loss, dx, g = _local_step(x[0], loss_target[0], p, hooks)
    loss = lax.psum(loss, ("x", "y", "c"))

    last_keys = [("cm_w_in", 0), ("cm_w_out", 0)]
    stacked = {k: jnp.stack(g[k]) for k in sn + rn if k != "final_norm_w"}
    stacked["final_norm_w"] = g["final_norm_w"]
    last = _exchange([(blocks(n, j, g), True) for n, j in last_keys]
                     + [(_pack([_split_shards(stacked[k], _SHARD_AXIS[k]) for k in sn], sr, 128, F32, lead=(N_DEV,)), True),
                        (_pack([stacked[k] for k in rn], rr, 128, F32), False)], name="exchange_last")
    recv.update(zip(last_keys, last))

    outs = {}
    for name in mats:
        shp = w[name].shape
        w3, m3, v3 = [d[name].reshape(nl, -1, shp[-1]) for d in (w, m, v)]
        per_layer = []
        for j in range(nl):
            r = recv[name, j]
            if name == "ssd_w_in":
                gt = _reduce_slots(r, rows=r.shape[1], name="sum_ssd_w_in")
                r = jnp.swapaxes(gt[:t_rows], 0, 1)[None]
            per_layer.append(_reduce_adamw(r, w3, m3, v3, layer=j, rows=min(OPT_ROWS, r.shape[1]), name="adamw_" + name))
        for kind, parts in zip(("g", "d", "m", "v"), zip(*per_layer)):
            outs[kind, name] = jnp.stack(parts).reshape(shp)
    for (names, rows), r, nm in ((_SMALL, last[2], "adamw_small"), (_REP, last[3], "adamw_rep")):
        res = _reduce_adamw(r, *[_pack([d[k] for k in names], rows, 128, F32)[None] for d in (w, m, v)], layer=0,
                            rows=rows, name=nm)
        for kind, buf in zip(("g", "d", "m", "v"), res):
            for k, a in zip(names, _unpack(buf, [w[k].shape for k in names])):
                outs[kind, k] = a
    return (loss, dx[None], *[outs[kind, k] for kind in ("g", "d", "m", "v") for k in _ORDER])
```

```python
import functools
import math

import jax
import jax.numpy as jnp
from jax import lax
from jax.experimental import pallas as pl
from jax.experimental.pallas import tpu as pltpu

F32, BF16 = jnp.float32, jnp.bfloat16
SDS = jax.ShapeDtypeStruct

D_MODEL = 1024
D_INNER = 2048
CONF_KERNEL = 31
HEADDIM = 64
SSD_HEADS = 32
SSD_GROUPS = 4
HPG = SSD_HEADS // SSD_GROUPS
D_STATE = 128
SSD_CONV = 5
CHUNK = 128
GN = SSD_GROUPS * D_STATE
SSD_CONV_DIM = D_INNER + 2 * GN
SSD_IN = D_INNER + SSD_CONV_DIM + 2 * SSD_HEADS
SSD_IN_PAD = SSD_IN + 64
EPS = 1e-5
N_DEV = 8

ADAM_LR, ADAM_B1, ADAM_B2, ADAM_EPS, ADAM_WD, ADAM_STEP = 0.001, 0.9, 0.999, 1e-08, 0.01, 10

VMEM_LIMIT = 56 * 1024 * 1024
ROW_TILE = 512
GATE_TILE = 256
GRAD_TILE = 1024
CONV_ROWS = 512
CONV_LANES = 512
CONV_HALO = 16
CONV_RB = 16
CONV_LANES_FEW = 1024
CONV_RB_FEW = 16
CONV_DIRECT_TAPS = 8
NEG = -1e30


def _cparams(sem):
    return pltpu.CompilerParams(dimension_semantics=sem, vmem_limit_bytes=VMEM_LIMIT)


def _resident(shape):
    nd = len(shape)
    return pl.BlockSpec(shape, lambda *_: (0,) * nd, pipeline_mode=pl.Buffered(1))


def _sig(x):
    return 1.0 / (1.0 + jnp.exp(-x))


def _silu(x):
    return x * _sig(x)


def _dsilu(x):
    s = _sig(x)
    return s * (1.0 + x * (1.0 - s))


def _softplus(x):
    return jnp.maximum(x, 0.0) + jnp.log(1.0 + jnp.exp(-jnp.abs(x)))


def _col_chunks(n, width):
    out, c = [], 0
    while c < n:
        out.append((c, min(c + width, n)))
        c += width
    return out


def _rowsum8(v):
    acc = v[0:8]
    for j in range(1, v.shape[0] // 8):
        acc = acc + v[8 * j:8 * j + 8]
    return acc


def _inproj_fwd(h, nw, w, *, tail, wt, name, carry=None):
    S, D = h.shape
    N = w.shape[0] if wt else w.shape[1]
    tm = min(ROW_TILE, S)
    chunks = _col_chunks(N, 1024)

    def body(h_ref, nw_ref, w_ref, o_ref, *rest):
        x = h_ref[...]
        r = lax.rsqrt(jnp.mean(x * x, axis=-1, keepdims=True) + EPS)
        hn = (x * r * nw_ref[...]).astype(BF16)
        for c0, c1 in chunks:
            if wt:
                acc = lax.dot_general(hn, w_ref[c0:c1, :], (((1,), (1,)), ((), ())), preferred_element_type=F32)
            else:
                acc = jnp.dot(hn, w_ref[:, c0:c1], preferred_element_type=F32)
            o_ref[:, c0:c1] = acc.astype(BF16)
            if tail and c1 == N:
                rest[0][...] = acc[:, acc.shape[1] - 128:]

    out_shape = [SDS((S, N), BF16)]
    out_specs = [pl.BlockSpec((tm, N), lambda i: (i, 0))]
    if tail:
        out_shape.append(SDS((S, 128), F32))
        out_specs.append(pl.BlockSpec((tm, 128), lambda i: (i, 0)))
    res, got = _call(
        body, grid=(S // tm,), out_shape=out_shape,
        in_specs=[pl.BlockSpec((tm, D), lambda i: (i, 0)), _resident((1, D)), _resident(w.shape)],
        out_specs=out_specs, scratch_shapes=[], sem=("parallel",), name=name, args=(h, nw, w), carry=carry)
    return (res if tail else res[0]), got


def _tap_plan(offsets):
    rs = sorted({s % 8 for s in offsets})
    return rs, [(k, rs.index(s % 8), s // 8) for k, s in enumerate(offsets)]


def _halo_specs(S, rows, lanes, col0):
    per, last = rows // CONV_HALO, S // CONV_HALO - 1
    cb = col0 // lanes
    return [pl.BlockSpec((rows, lanes), lambda j, i: (i, cb + j)),
            pl.BlockSpec((CONV_HALO, lanes), lambda j, i: (jnp.maximum(i * per - 1, 0), cb + j)),
            pl.BlockSpec((CONV_HALO, lanes), lambda j, i: (jnp.minimum((i + 1) * per, last), cb + j))]


def _fill_window(win, prev, main, nxt, i, nt, rows):
    hb = CONV_HALO
    win[0:hb, :] = jnp.where(i > 0, prev, 0.0)
    win[hb:hb + rows, :] = main
    win[hb + rows:hb + rows + hb, :] = jnp.where(i < nt - 1, nxt, 0.0)


def _fill_shifts(win, sh, rs, n_u):
    for idx, r in enumerate(rs):
        sh[idx] = win[pl.ds(r, n_u), :]


def _tap_rows(win, sh, rs, ri, q, r0, n, direct):
    if direct:
        return win[pl.ds(r0 + 8 * q + rs[ri], n), :]
    return sh[ri, pl.ds(pl.multiple_of(r0 + 8 * q, 8), n), :]


def _tap_groups(taps):
    by_copy = {}
    for k, ri, q in taps:
        by_copy.setdefault(ri, []).append((k, q))
    out = []
    for ri, members in sorted(by_copy.items()):
        q0 = min(q for _, q in members)
        out.append((ri, q0, max(q for _, q in members) - q0, [(k, q - q0) for k, q in members]))
    return out


def _tap_sum(w_ref, taps, slab_of, rb, lanes):
    accs = [None] * (rb // 8)
    for ri, q0, extra, members in _tap_groups(taps):
        slab = slab_of(ri, q0, rb + 8 * extra)
        for k, dq in members:
            wk = jnp.broadcast_to(w_ref[k:k + 1, :], (8, lanes))
            for i in range(rb // 8):
                t = wk * slab[8 * (dq + i):8 * (dq + i) + 8]
                accs[i] = t if accs[i] is None else accs[i] + t
    return jnp.concatenate(accs, axis=0)


def _row_loop(n, rb, fn, direct):
    if direct:
        for t in range(n):
            fn(t * rb)
    else:
        def step(t, carry):
            fn(pl.multiple_of(t * rb, rb))
            return carry

        lax.fori_loop(0, n, step, 0)


def _conv_blocking(K):
    return (CONV_LANES_FEW, CONV_RB_FEW) if K <= CONV_DIRECT_TAPS else (CONV_LANES, CONV_RB)


def _dwconv_fwd(src, w, b, *, col0, glu_col0, C, name, carry=None):
    S = src.shape[0]
    K = w.shape[0]
    pad = (K - 1) // 2
    rows, hb = min(CONV_ROWS, S), CONV_HALO
    lanes, rb = _conv_blocking(K)
    nt = S // rows
    rs, taps = _tap_plan([k - pad + hb for k in range(K)])
    direct = K <= CONV_DIRECT_TAPS
    n_u = 8 if direct else rows + 2 * hb - 8
    glu = glu_col0 is not None

    def body(*refs):
        if glu:
            vm, vp, vn, gm, gp, gn, w_ref, b_ref, o_ref, win, sh = refs
        else:
            vm, vp, vn, w_ref, b_ref, o_ref, win, sh = refs
        i = pl.program_id(1)

        def act(k):
            v = (vm, vp, vn)[k][...].astype(F32)
            return v * _sig((gm, gp, gn)[k][...].astype(F32)) if glu else v

        _fill_window(win, act(1), act(0), act(2), i, nt, rows)
        if not direct:
            _fill_shifts(win, sh, rs, n_u)

        def rowblk(r0):
            acc = _tap_sum(w_ref, taps, lambda ri, q, n: _tap_rows(win, sh, rs, ri, q, r0, n, direct), rb, lanes)
            o_ref[pl.ds(r0, rb), :] = (acc + b_ref[...]).astype(BF16)

        _row_loop(rows // rb, rb, rowblk, direct)

    in_specs = _halo_specs(S, rows, lanes, col0)
    args = [src, src, src]
    if glu:
        in_specs += _halo_specs(S, rows, lanes, glu_col0)
        args += [src, src, src]
    in_specs += [pl.BlockSpec((K, lanes), lambda j, i: (0, j)), pl.BlockSpec((1, lanes), lambda j, i: (0, j))]
    res, got = _call(
        body, grid=(C // lanes, nt), out_shape=[SDS((S, C), BF16)], in_specs=in_specs,
        out_specs=[pl.BlockSpec((rows, lanes), lambda j, i: (i, j))],
        scratch_shapes=[pltpu.VMEM((rows + 2 * hb, lanes), F32), pltpu.VMEM((len(rs), n_u, lanes), F32)],
        sem=("parallel", "parallel"), name=name, args=(*args, w, b), carry=carry)
    return res[0], got


def _dwconv_bwd(dsrcs, src, w, *, col0, glu_col0, C, name, carry=None):
    S = src.shape[0]
    K = w.shape[0]
    pad = (K - 1) // 2
    rows, hb = min(CONV_ROWS, S), CONV_HALO
    lanes, rb = _conv_blocking(K)
    nt = S // rows
    rs_u, taps_u = _tap_plan([k - pad + hb for k in range(K)])
    rs_d, taps_d = _tap_plan([hb + pad - k for k in range(K)])
    direct = K <= CONV_DIRECT_TAPS
    n_u = 8 if direct else rows + 2 * hb - 8
    glu = glu_col0 is not None
    nd = len(dsrcs)

    def body(*refs):
        d_refs, refs = refs[:3 * nd], refs[3 * nd:]
        if glu:
            vm, vp, vn, gm, gp, gn, w_ref = refs[:7]
            dv_ref, dg_ref, dw_ref, db_ref, dwin, dsh, uwin, ush, accw, accb = refs[7:]
        else:
            vm, vp, vn, w_ref = refs[:4]
            du_ref, dw_ref, db_ref, dwin, dsh, uwin, ush, accw, accb = refs[4:]
        i = pl.program_id(1)

        def dpiece(k):
            v = d_refs[k][...].astype(F32)
            for e in range(1, nd):
                v = v + d_refs[3 * e + k][...].astype(F32)
            return v

        def act(k):
            v = (vm, vp, vn)[k][...].astype(F32)
            return v * _sig((gm, gp, gn)[k][...].astype(F32)) if glu else v

        _fill_window(dwin, dpiece(1), dpiece(0), dpiece(2), i, nt, rows)
        _fill_window(uwin, act(1), act(0), act(2), i, nt, rows)
        if not direct:
            _fill_shifts(dwin, dsh, rs_d, n_u)
            _fill_shifts(uwin, ush, rs_u, n_u)

        @pl.when(i == 0)
        def _():
            accw[...] = jnp.zeros_like(accw)
            accb[...] = jnp.zeros_like(accb)

        def rowblk(r0):
            du = _tap_sum(w_ref, taps_d, lambda ri, q, n: _tap_rows(dwin, dsh, rs_d, ri, q, r0, n, direct), rb, lanes)
            if glu:
                v = vm[pl.ds(r0, rb), :].astype(F32)
                s = _sig(gm[pl.ds(r0, rb), :].astype(F32))
                dv_ref[pl.ds(r0, rb), :] = (du * s).astype(BF16)
                dg_ref[pl.ds(r0, rb), :] = (du * v * s * (1.0 - s)).astype(BF16)
            else:
                du_ref[pl.ds(r0, rb), :] = du.astype(BF16)
            dmain = dwin[pl.ds(r0 + hb, rb), :] if direct else dwin[pl.ds(pl.multiple_of(r0 + hb, 8), rb), :]
            accb[...] += _rowsum8(dmain)
            for ri, q0, extra, members in _tap_groups(taps_u):
                slab = _tap_rows(uwin, ush, rs_u, ri, q0, r0, rb + 8 * extra, direct)
                for k, dq in members:
                    accw[k] += _rowsum8(dmain * slab[8 * dq:8 * dq + rb])

        _row_loop(rows // rb, rb, rowblk, direct)

        @pl.when(i == nt - 1)
        def _():
            for k in range(K):
                dw_ref[k:k + 1, :] = jnp.sum(accw[k], axis=0, keepdims=True)
            db_ref[...] = jnp.sum(accb[...], axis=0, keepdims=True)

    in_specs, args = [], []
    for d in dsrcs:
        in_specs += _halo_specs(S, rows, lanes, 0)
        args += [d, d, d]
    in_specs += _halo_specs(S, rows, lanes, col0)
    args += [src, src, src]
    if glu:
        in_specs += _halo_specs(S, rows, lanes, glu_col0)
        args += [src, src, src]
    in_specs += [pl.BlockSpec((K, lanes), lambda j, i: (0, j))]
    tile = pl.BlockSpec((rows, lanes), lambda j, i: (i, j))
    n_act = 2 if glu else 1
    out_shape = [SDS((S, C), BF16)] * n_act + [SDS((K, C), F32), SDS((1, C), F32)]
    out_specs = [tile] * n_act + [pl.BlockSpec((K, lanes), lambda j, i: (0, j)),
                                  pl.BlockSpec((1, lanes), lambda j, i: (0, j))]
    return _call(
        body, grid=(C // lanes, nt), out_shape=out_shape, in_specs=in_specs, out_specs=out_specs,
        scratch_shapes=[pltpu.VMEM((rows + 2 * hb, lanes), F32), pltpu.VMEM((len(rs_d), n_u, lanes), F32),
                        pltpu.VMEM((rows + 2 * hb, lanes), F32), pltpu.VMEM((len(rs_u), n_u, lanes), F32),
                        pltpu.VMEM((K, 8, lanes), F32), pltpu.VMEM((8, lanes), F32)],
        sem=("parallel", "arbitrary"), name=name, args=(*args, w), carry=carry)


def _ln_parts(u, ln_w, ln_b):
    mu = jnp.mean(u, axis=-1, keepdims=True)
    xc = u - mu
    rstd = lax.rsqrt(jnp.mean(xc * xc, axis=-1, keepdims=True) + EPS)
    un = xc * rstd
    return un, rstd, un * ln_w + ln_b


def _conf_out_fwd(uc, proj, h, ln_w, ln_b, wout, *, name):
    S, E = uc.shape
    D = h.shape[1]
    tm = min(GATE_TILE, S)

    def body(uc_ref, z_ref, h_ref, lw_ref, lb_ref, w_ref, o_ref):
        _, _, ul = _ln_parts(uc_ref[...].astype(F32), lw_ref[...], lb_ref[...])
        u3 = (_silu(ul) * _silu(z_ref[...].astype(F32))).astype(BF16)
        o_ref[...] = h_ref[...] + jnp.dot(u3, w_ref[...], preferred_element_type=F32)

    return pl.pallas_call(
        body, grid=(S // tm,), out_shape=SDS((S, D), F32),
        in_specs=[pl.BlockSpec((tm, E), lambda i: (i, 0)), pl.BlockSpec((tm, E), lambda i: (i, 2)),
                  pl.BlockSpec((tm, D), lambda i: (i, 0)), _resident((1, E)), _resident((1, E)), _resident((E, D))],
        out_specs=pl.BlockSpec((tm, D), lambda i: (i, 0)),
        compiler_params=_cparams(("parallel",)), name=name)(uc, proj, h, ln_w, ln_b, wout)


def _conf_out_bwd(dh, uc, proj, ln_w, ln_b, wout, *, name):
    S, E = uc.shape
    D = dh.shape[1]
    tm = min(GATE_TILE, S)
    nt = S // tm

    def body(dh_ref, uc_ref, z_ref, lw_ref, lb_ref, w_ref, dw_ref, duc_ref, dz_ref, dlw_ref, dlb_ref, alw, alb):
        i = pl.program_id(0)

        @pl.when(i == 0)
        def _():
            alw[...] = jnp.zeros_like(alw)
            alb[...] = jnp.zeros_like(alb)
            dw_ref[...] = jnp.zeros_like(dw_ref)

        un, rstd, ul = _ln_parts(uc_ref[...].astype(F32), lw_ref[...], lb_ref[...])
        z = z_ref[...].astype(F32)
        su, sz = _silu(ul), _silu(z)
        dhb = dh_ref[...].astype(BF16)
        dw_ref[...] += lax.dot_general((su * sz).astype(BF16), dhb, (((0,), (0,)), ((), ())),
                                       preferred_element_type=F32)
        du3 = lax.dot_general(dhb, w_ref[...], (((1,), (1,)), ((), ())), preferred_element_type=F32)
        dz_ref[...] = (du3 * su * _dsilu(z)).astype(BF16)
        dul = du3 * sz * _dsilu(ul)
        alw[...] += _rowsum8(dul * un)
        alb[...] += _rowsum8(dul)
        dun = dul * lw_ref[...]
        m1 = jnp.mean(dun, axis=-1, keepdims=True)
        m2 = jnp.mean(dun * un, axis=-1, keepdims=True)
        duc_ref[...] = (rstd * (dun - m1 - un * m2)).astype(BF16)

        @pl.when(i == nt - 1)
        def _():
            dlw_ref[...] = jnp.sum(alw[...], axis=0, keepdims=True)
            dlb_ref[...] = jnp.sum(alb[...], axis=0, keepdims=True)

    tile = pl.BlockSpec((tm, E), lambda i: (i, 0))
    vec = pl.BlockSpec((1, E), lambda i: (0, 0))
    return pl.pallas_call(
        body, grid=(nt,),
        out_shape=[SDS((E, D), F32)] + [SDS((S, E), BF16)] * 2 + [SDS((1, E), F32)] * 2,
        in_specs=[pl.BlockSpec((tm, D), lambda i: (i, 0)), tile, pl.BlockSpec((tm, E), lambda i: (i, 2)),
                  _resident((1, E)), _resident((1, E)), _resident((E, D))],
        out_specs=[pl.BlockSpec((E, D), lambda i: (0, 0)), tile, tile, vec, vec],
        scratch_shapes=[pltpu.VMEM((8, E), F32), pltpu.VMEM((8, E), F32)],
        compiler_params=_cparams(("arbitrary",)), name=name)(dh, uc, proj, ln_w, ln_b, wout)


def _inproj_bwd(parts, offs, w, h, nw, dh_in, *, wt, name):
    S, D = h.shape
    tm = min(ROW_TILE, S)
    nt = S // tm
    widths = [p.shape[1] for p in parts]
    npart = len(parts)

    def body(*refs):
        p_refs = refs[:npart]
        w_ref, h_ref, nw_ref, dh_ref, o_ref, hn_ref, dnw_ref, anw = refs[npart:]
        i = pl.program_id(0)

        @pl.when(i == 0)
        def _():
            anw[...] = jnp.zeros_like(anw)

        x = h_ref[...]
        r = lax.rsqrt(jnp.mean(x * x, axis=-1, keepdims=True) + EPS)
        n = x * r
        hn_ref[...] = (n * nw_ref[...]).astype(BF16)
        dhn = jnp.zeros((tm, D), F32)
        for p_ref, off, wd in zip(p_refs, offs, widths):
            for c0, c1 in _col_chunks(wd, 1024):
                if wt:
                    dhn = dhn + jnp.dot(p_ref[:, c0:c1], w_ref[off + c0:off + c1, :], preferred_element_type=F32)
                else:
                    dhn = dhn + lax.dot_general(p_ref[:, c0:c1], w_ref[:, off + c0:off + c1],
                                                (((1,), (1,)), ((), ())), preferred_element_type=F32)
        anw[...] += _rowsum8(dhn * n)
        dn = dhn * nw_ref[...]
        o_ref[...] = dh_ref[...] + r * (dn - n * jnp.mean(dn * n, axis=-1, keepdims=True))

        @pl.when(i == nt - 1)
        def _():
            dnw_ref[...] = jnp.sum(anw[...], axis=0, keepdims=True)

    tile = pl.BlockSpec((tm, D), lambda i: (i, 0))
    return pl.pallas_call(
        body, grid=(nt,),
        out_shape=[SDS((S, D), F32), SDS((S, D), BF16), SDS((1, D), F32)],
        in_specs=[pl.BlockSpec((tm, wd), lambda i: (i, 0)) for wd in widths]
        + [_resident(w.shape), tile, _resident((1, D)), tile],
        out_specs=[tile, tile, pl.BlockSpec((1, D), lambda i: (0, 0))],
        scratch_shapes=[pltpu.VMEM((8, D), F32)],
        compiler_params=_cparams(("arbitrary",)), name=name)(*parts, w, h, nw, dh_in)


def _tn_matmul(a, b, *, transpose_out, name):
    S, Ka = a.shape
    N = b.shape[1]
    tm = min(GRAD_TILE, S)
    bn = min(1024, N)
    nt = S // tm

    def body(a_ref, b_ref, o_ref, acc):
        i = pl.program_id(1)

        @pl.when(i == 0)
        def _():
            acc[...] = jnp.zeros_like(acc)

        acc[...] += lax.dot_general(a_ref[...], b_ref[...].astype(BF16), (((0,), (0,)), ((), ())),
                                    preferred_element_type=F32)

        @pl.when(i == nt - 1)
        def _():
            o_ref[...] = acc[...].T if transpose_out else acc[...]

    if transpose_out:
        out_shape, out_spec = SDS((N, Ka), F32), pl.BlockSpec((bn, Ka), lambda j, i: (j, 0))
    else:
        out_shape, out_spec = SDS((Ka, N), F32), pl.BlockSpec((Ka, bn), lambda j, i: (0, j))
    return pl.pallas_call(
        body, grid=(N // bn, nt), out_shape=out_shape,
        in_specs=[pl.BlockSpec((tm, Ka), lambda j, i: (i, 0)), pl.BlockSpec((tm, bn), lambda j, i: (i, j))],
        out_specs=out_spec, scratch_shapes=[pltpu.VMEM((Ka, bn), F32)],
        compiler_params=_cparams(("parallel", "arbitrary")), name=name)(a, b)


def _loss_head(h, fnw, target, *, name):
    S, D = h.shape
    tm = min(ROW_TILE, S)
    nt = S // tm

    def body(h_ref, w_ref, t_ref, loss_ref, dh_ref, dw_ref, aw, al):
        i = pl.program_id(0)

        @pl.when(i == 0)
        def _():
            aw[...] = jnp.zeros_like(aw)
            al[...] = jnp.zeros_like(al)

        x = h_ref[...]
        r = lax.rsqrt(jnp.mean(x * x, axis=-1, keepdims=True) + EPS)
        n = x * r
        err = n * w_ref[...] - t_ref[...]
        al[...] += _rowsum8(err * err)
        dy = err * (1.0 / D)
        aw[...] += _rowsum8(dy * n)
        dn = dy * w_ref[...]
        dh_ref[...] = r * (dn - n * jnp.mean(dn * n, axis=-1, keepdims=True))

        @pl.when(i == nt - 1)
        def _():
            dw_ref[...] = jnp.sum(aw[...], axis=0, keepdims=True)
            tot = jnp.sum(jnp.sum(al[...], axis=0, keepdims=True), axis=1, keepdims=True) * (0.5 / D)
            loss_ref[...] = jnp.zeros((8, 128), F32) + tot

    tile = pl.BlockSpec((tm, D), lambda i: (i, 0))
    loss, dh, dw = pl.pallas_call(
        body, grid=(nt,), out_shape=[SDS((8, 128), F32), SDS((S, D), F32), SDS((1, D), F32)],
        in_specs=[tile, _resident((1, D)), tile],
        out_specs=[pl.BlockSpec((8, 128), lambda i: (0, 0)), tile, pl.BlockSpec((1, D), lambda i: (0, 0))],
        scratch_shapes=[pltpu.VMEM((8, D), F32), pltpu.VMEM((8, D), F32)],
        compiler_params=_cparams(("arbitrary",)), name=name)(h, fnw, target)
    return loss[0, 0], dh, dw


def _head_mask(shape, head_axis):
    hd = lax.broadcasted_iota(jnp.int32, shape, head_axis)
    ch = lax.broadcasted_iota(jnp.int32, shape, 1 - head_axis)
    return ((ch >= hd * HEADDIM) & (ch < hd * HEADDIM + HEADDIM)).astype(BF16)


def _split_dot(vals, mat, fine):
    R = vals[0].shape[0]
    parts, where = [], []
    for v, f in zip(vals, fine):
        hi = v.astype(BF16)
        where.append((len(parts), f))
        parts += [hi, (v - hi.astype(F32)).astype(BF16)] if f else [hi]
    out = jnp.dot(jnp.concatenate(parts, axis=0), mat, preferred_element_type=F32)
    return [out[R * k:R * k + R] + out[R * k + R:R * k + 2 * R] if f else out[R * k:R * k + R] for k, f in where]


def _nt(a, b):
    return lax.dot_general(a, b, (((1,), (1,)), ((), ())), preferred_element_type=F32)


def _tn(a, b):
    return lax.dot_general(a, b, (((0,), (0,)), ((), ())), preferred_element_type=F32)


def _ssd_setup(c_val, dtraw, bias, alog, fwd):
    xbc = _silu(c_val.astype(F32))
    pre = dtraw + bias
    dt = _softplus(pre)
    A = -jnp.exp(alog)
    row = lax.broadcasted_iota(jnp.int32, (CHUNK, CHUNK), 0)
    col = lax.broadcasted_iota(jnp.int32, (CHUNK, CHUNK), 1)
    T = (col <= row) if fwd else (col >= row)
    cum = jnp.dot(T.astype(F32), dt * A, precision=lax.Precision.HIGHEST, preferred_element_type=F32)
    total = cum[CHUNK - 1:CHUNK] if fwd else cum[0:1]
    return dict(x=xbc[:, :D_INNER], B=xbc[:, D_INNER:D_INNER + GN].astype(BF16),
                C=xbc[:, D_INNER + GN:].astype(BF16), pre=pre, dt=dt, A=A, T=T, Tt=(col >= row) if fwd else (col <= row),
                cum=cum, e=jnp.exp(cum), dtds=dt * jnp.exp(total - cum))


def _ssd_fwd_dir(c_val, dtraw, bias, alog, state, y_ref, hin_ref, fwd, ex):
    s = _ssd_setup(c_val, dtraw, bias, alog, fwd)
    x, cum, T = s["x"], s["cum"], s["T"]
    e_x, dtds_x = _split_dot([s["e"], s["dtds"]], ex, [True, False])
    dec_x = e_x[CHUNK - 1:CHUNK] if fwd else e_x[0:1]
    cumT, dtT = cum.T, s["dt"].T
    xb = x.astype(BF16)
    for g in range(SSD_GROUPS):
        g0, g1 = g * HPG * HEADDIM, (g + 1) * HPG * HEADDIM
        Bg, Cg = s["B"][:, g * D_STATE:(g + 1) * D_STATE], s["C"][:, g * D_STATE:(g + 1) * D_STATE]
        CBg = _nt(Cg, Bg)
        Hg = state[:, g0:g1]
        Hb = Hg.astype(BF16)
        hin_ref[0, :, g0:g1] = Hb
        yoff = jnp.dot(Cg, Hb, preferred_element_type=F32) * e_x[:, g0:g1]
        ys = []
        for hh in range(HPG):
            h = g * HPG + hh
            Lh = jnp.exp(jnp.where(T, cum[:, h:h + 1] - cumT[h:h + 1, :], NEG))
            Mh = (CBg * Lh * dtT[h:h + 1, :]).astype(BF16)
            ys.append(jnp.dot(Mh, xb[:, h * HEADDIM:(h + 1) * HEADDIM], preferred_element_type=F32))
        y_ref[:, g0:g1] = jnp.concatenate(ys, axis=1) + yoff
        xds = (x[:, g0:g1] * dtds_x[:, g0:g1]).astype(BF16)
        state[:, g0:g1] = Hg * dec_x[:, g0:g1] + _tn(Bg, xds)


def _ssd_scan_fwd(cact, tail, bias, alog, *, name):
    S = cact.shape[0]
    nc = S // CHUNK
    E = D_INNER

    def body(cf_ref, cb_ref, tf_ref, tb_ref, bias_ref, alog_ref, yf_ref, yb_ref, hf_ref, hb_ref, state):
        @pl.when(pl.program_id(0) == 0)
        def _():
            state[...] = jnp.zeros_like(state)

        ex = _head_mask((SSD_HEADS, D_INNER), 0)
        H = SSD_HEADS
        _ssd_fwd_dir(cf_ref[...], tf_ref[:, 0:H], bias_ref[0:1, :], alog_ref[0:1, :], state.at[0], yf_ref, hf_ref, True, ex)
        _ssd_fwd_dir(cb_ref[...], tb_ref[:, H:2 * H], bias_ref[1:2, :], alog_ref[1:2, :], state.at[1], yb_ref, hb_ref, False, ex)

    up, down = (lambda i: (i, 0)), (lambda i: (nc - 1 - i, 0))
    up3, down3 = (lambda i: (i, 0, 0)), (lambda i: (nc - 1 - i, 0, 0))
    W = cact.shape[1]
    return pl.pallas_call(
        body, grid=(nc,),
        out_shape=[SDS((S, E), F32), SDS((S, E), F32), SDS((nc, D_STATE, E), BF16), SDS((nc, D_STATE, E), BF16)],
        in_specs=[pl.BlockSpec((CHUNK, W), up), pl.BlockSpec((CHUNK, W), down),
                  pl.BlockSpec((CHUNK, 128), up), pl.BlockSpec((CHUNK, 128), down),
                  _resident((2, SSD_HEADS)), _resident((2, SSD_HEADS))],
        out_specs=[pl.BlockSpec((CHUNK, E), up), pl.BlockSpec((CHUNK, E), down),
                   pl.BlockSpec((1, D_STATE, E), up3), pl.BlockSpec((1, D_STATE, E), down3)],
        scratch_shapes=[pltpu.VMEM((2, D_STATE, E), F32)],
        compiler_params=_cparams(("arbitrary",)), name=name)(cact, cact, tail, tail, bias, alog)


def _ssd_bwd_dir(c_val, dtraw, bias, alog, dy, hin_ref, dskip_x, G, dc_ref, ddt_ref, accA, accb, fwd, ex, ext):
    s = _ssd_setup(c_val, dtraw, bias, alog, fwd)
    x, cum, T, dt = s["x"], s["cum"], s["T"], s["dt"]
    e_x, dtds_x = _split_dot([s["e"], s["dtds"]], ex, [True, False])
    dec_x = e_x[CHUNK - 1:CHUNK] if fwd else e_x[0:1]
    cumT, dtT = cum.T, dt.T
    xb = x.astype(BF16)
    dyf = dy.astype(F32)
    dx_diag, dx_st, dBs, dCs, gh, yoff_dy = [], [], [], [], [], []
    lane_h = lax.broadcasted_iota(jnp.int32, (CHUNK, SSD_HEADS), 1)
    sub_h = lax.broadcasted_iota(jnp.int32, (SSD_HEADS, CHUNK), 0)
    rows = jnp.zeros((CHUNK, SSD_HEADS), F32)
    colsT = jnp.zeros((SSD_HEADS, CHUNK), F32)
    for g in range(SSD_GROUPS):
        g0, g1 = g * HPG * HEADDIM, (g + 1) * HPG * HEADDIM
        Bg, Cg = s["B"][:, g * D_STATE:(g + 1) * D_STATE], s["C"][:, g * D_STATE:(g + 1) * D_STATE]
        CBg = _nt(Cg, Bg)
        Hb = hin_ref[0, :, g0:g1]
        Gg = G[:, g0:g1]
        Gb = Gg.astype(BF16)
        edyf = dyf[:, g0:g1] * e_x[:, g0:g1]
        edy = edyf.astype(BF16)
        yoff_dy.append(edyf * jnp.dot(Cg, Hb, preferred_element_type=F32))
        dC_g = _nt(edy, Hb)
        dH_g = _tn(Cg, edy)
        xds = (x[:, g0:g1] * dtds_x[:, g0:g1]).astype(BF16)
        dB_g = _nt(xds, Gb)
        dx_st.append(dtds_x[:, g0:g1] * jnp.dot(Bg, Gb, preferred_element_type=F32))
        dCB = jnp.zeros((CHUNK, CHUNK), F32)
        for hh in range(HPG):
            h = g * HPG + hh
            hs = slice(h * HEADDIM, (h + 1) * HEADDIM)
            Lh = jnp.exp(jnp.where(T, cum[:, h:h + 1] - cumT[h:h + 1, :], NEG))
            Mh = (CBg * Lh * dtT[h:h + 1, :]).astype(BF16)
            term = _nt(dy[:, hs], xb[:, hs]) * dtT[h:h + 1, :] * Lh
            dCB = dCB + term
            Wh = term * CBg
            rows = rows + jnp.where(lane_h == h, jnp.sum(Wh, axis=1, keepdims=True), 0.0)
            colsT = colsT + jnp.where(sub_h == h, jnp.sum(Wh, axis=0, keepdims=True), 0.0)
            dx_diag.append(_tn(Mh, dy[:, hs]))
        dCBb = dCB.astype(BF16)
        dCs.append(dC_g + jnp.dot(dCBb, Bg, preferred_element_type=F32))
        dBs.append(dB_g + _tn(dCBb, Cg))
        gh.append(_rowsum8(Gg * Hb.astype(F32)))
        G[:, g0:g1] = Gg * dec_x[:, g0:g1] + dH_g
    dxn = jnp.concatenate(dx_diag, axis=1) + jnp.concatenate(dx_st, axis=1)
    xst = x * jnp.concatenate(dx_st, axis=1)
    RZ, Zx = _split_dot([jnp.concatenate(yoff_dy, axis=1) - xst, x * dxn], ext, [True, False])
    ddec, zs_tot = _split_dot([jnp.concatenate(gh, axis=1), _rowsum8(xst)], ext, [True, True])
    dec = s["e"][CHUNK - 1:CHUNK] if fwd else s["e"][0:1]
    dtotal = jnp.sum(zs_tot + dec * ddec, axis=0, keepdims=True)
    rowi = lax.broadcasted_iota(jnp.int32, (CHUNK, SSD_HEADS), 0)
    dacs = rows - colsT.T + RZ + jnp.where(rowi == (CHUNK - 1 if fwd else 0), dtotal, 0.0)
    da = jnp.dot(s["Tt"].astype(F32), dacs, precision=lax.Precision.HIGHEST, preferred_element_type=F32)
    accA[...] += _rowsum8(da * dt)
    sg = _sig(s["pre"])
    ratio = jnp.where(dt > 1e-30, sg / jnp.maximum(dt, 1e-30), 1.0)
    ddtraw = da * s["A"] * sg + Zx * ratio
    ddt_ref[...] = ddtraw
    accb[...] += _rowsum8(ddtraw)
    dx = dxn + dyf * dskip_x if fwd else dxn
    dact = jnp.concatenate([dx] + dBs + dCs, axis=1)
    dc_ref[...] = (dact * _dsilu(c_val.astype(F32))).astype(BF16)


def _ssd_scan_bwd(cact, tail, bias, alog, dy, hf, hb, dskip_x, *, name, carry=None):
    S = cact.shape[0]
    nc = S // CHUNK
    E, H, W = D_INNER, SSD_HEADS, cact.shape[1]

    def body(cf_ref, cb_ref, tf_ref, tb_ref, bias_ref, alog_ref, dyf_ref, dyb_ref, hf_ref, hb_ref,
             dsk_ref, dcf_ref, dcb_ref, ddtf_ref, ddtb_ref, dalog_ref, dbias_ref, G, accA, accb):
        i = pl.program_id(0)

        @pl.when(i == 0)
        def _():
            G[...] = jnp.zeros_like(G)
            accA[...] = jnp.zeros_like(accA)
            accb[...] = jnp.zeros_like(accb)

        ex = _head_mask((H, E), 0)
        ext = _head_mask((E, H), 1)
        _ssd_bwd_dir(cf_ref[...], tf_ref[:, 0:H], bias_ref[0:1, :], alog_ref[0:1, :], dyf_ref[...], hf_ref,
                     dsk_ref[...], G.at[0], dcf_ref, ddtf_ref, accA.at[0], accb.at[0], True, ex, ext)
        _ssd_bwd_dir(cb_ref[...], tb_ref[:, H:2 * H], bias_ref[1:2, :], alog_ref[1:2, :], dyb_ref[...], hb_ref,
                     dsk_ref[...], G.at[1], dcb_ref, ddtb_ref, accA.at[1], accb.at[1], False, ex, ext)

        @pl.when(i == nc - 1)
        def _():
            for d in range(2):
                A = -jnp.exp(alog_ref[d:d + 1, :])
                dalog_ref[d:d + 1, :] = jnp.sum(accA[d], axis=0, keepdims=True) * A
                dbias_ref[d:d + 1, :] = jnp.sum(accb[d], axis=0, keepdims=True)

    up, down = (lambda i: (i, 0)), (lambda i: (nc - 1 - i, 0))
    up3, down3 = (lambda i: (i, 0, 0)), (lambda i: (nc - 1 - i, 0, 0))
    small = pl.BlockSpec((2, H), lambda i: (0, 0))
    return _call(
        body, grid=(nc,),
        out_shape=[SDS((S, W), BF16), SDS((S, W), BF16), SDS((S, H), F32), SDS((S, H), F32),
                   SDS((2, H), F32), SDS((2, H), F32)],
        in_specs=[pl.BlockSpec((CHUNK, W), down), pl.BlockSpec((CHUNK, W), up),
                  pl.BlockSpec((CHUNK, 128), down), pl.BlockSpec((CHUNK, 128), up),
                  _resident((2, H)), _resident((2, H)),
                  pl.BlockSpec((CHUNK, E), down), pl.BlockSpec((CHUNK, E), up),
                  pl.BlockSpec((1, D_STATE, E), down3), pl.BlockSpec((1, D_STATE, E), up3),
                  _resident((1, E))],
        out_specs=[pl.BlockSpec((CHUNK, W), down), pl.BlockSpec((CHUNK, W), up),
                   pl.BlockSpec((CHUNK, H), down), pl.BlockSpec((CHUNK, H), up), small, small],
        scratch_shapes=[pltpu.VMEM((2, D_STATE, E), F32), pltpu.VMEM((2, 8, H), F32), pltpu.VMEM((2, 8, H), F32)],
        sem=("arbitrary",), name=name, args=(cact, cact, tail, tail, bias, alog, dy, dy, hf, hb, dskip_x), carry=carry)


def _ssd_gate(yf, yb, cx, z, dsk, nw):
    xs = _silu(cx.astype(F32))
    y = yf + yb + xs * dsk
    sz = _silu(z)
    gt = y * sz
    r = lax.rsqrt(jnp.mean(gt * gt, axis=-1, keepdims=True) + EPS)
    return xs, y, sz, r, gt * r


def _ssd_out_fwd(yf, yb, cact, proj, dskip_x, nw, h, wout, *, name):
    S, E = yf.shape
    D = h.shape[1]
    tm = min(GATE_TILE, S)

    def body(yf_ref, yb_ref, cx_ref, z_ref, dsk_ref, nw_ref, h_ref, w_ref, o_ref):
        _, _, _, _, n = _ssd_gate(yf_ref[...], yb_ref[...], cx_ref[...], z_ref[...].astype(F32), dsk_ref[...], nw_ref[...])
        o_ref[...] = h_ref[...] + jnp.dot((n * nw_ref[...]).astype(BF16), w_ref[...], preferred_element_type=F32)

    tile = pl.BlockSpec((tm, E), lambda i: (i, 0))
    return pl.pallas_call(
        body, grid=(S // tm,), out_shape=SDS((S, D), F32),
        in_specs=[tile, tile, tile, tile, _resident((1, E)), _resident((1, E)),
                  pl.BlockSpec((tm, D), lambda i: (i, 0)), _resident((E, D))],
        out_specs=pl.BlockSpec((tm, D), lambda i: (i, 0)),
        compiler_params=_cparams(("parallel",)), name=name)(yf, yb, cact, proj, dskip_x, nw, h, wout)


def _ssd_out_bwd(dh, yf, yb, cact, proj, dskip_x, nw, wout, *, name):
    S, E = yf.shape
    D = dh.shape[1]
    tm = min(GATE_TILE, S)
    nt = S // tm

    def body(dh_ref, yf_ref, yb_ref, cx_ref, z_ref, dsk_ref, nw_ref, w_ref, dw_ref, dy_ref, dz_ref, dnw_ref, dd_ref, anw, ad):
        i = pl.program_id(0)

        @pl.when(i == 0)
        def _():
            anw[...] = jnp.zeros_like(anw)
            ad[...] = jnp.zeros_like(ad)
            dw_ref[...] = jnp.zeros_like(dw_ref)

        z = z_ref[...].astype(F32)
        xs, y, sz, r, n = _ssd_gate(yf_ref[...], yb_ref[...], cx_ref[...], z, dsk_ref[...], nw_ref[...])
        dhb = dh_ref[...].astype(BF16)
        dw_ref[...] += _tn((n * nw_ref[...]).astype(BF16), dhb)
        dyn = _nt(dhb, w_ref[...])
        anw[...] += _rowsum8(dyn * n)
        dn = dyn * nw_ref[...]
        dgt = r * (dn - n * jnp.mean(dn * n, axis=-1, keepdims=True))
        dy = dgt * sz
        dy_ref[...] = dy.astype(BF16)
        dz_ref[...] = (dgt * y * _dsilu(z)).astype(BF16)
        ad[...] += _rowsum8(dy * xs)

        @pl.when(i == nt - 1)
        def _():
            dnw_ref[...] = jnp.sum(anw[...], axis=0, keepdims=True)
            dd_ref[...] = jnp.sum(ad[...], axis=0, keepdims=True)

    tile = pl.BlockSpec((tm, E), lambda i: (i, 0))
    vec = pl.BlockSpec((1, E), lambda i: (0, 0))
    return pl.pallas_call(
        body, grid=(nt,), out_shape=[SDS((E, D), F32)] + [SDS((S, E), BF16)] * 2 + [SDS((1, E), F32)] * 2,
        in_specs=[pl.BlockSpec((tm, D), lambda i: (i, 0)), tile, tile, tile, tile,
                  _resident((1, E)), _resident((1, E)), _resident((E, D))],
        out_specs=[pl.BlockSpec((E, D), lambda i: (0, 0)), tile, tile, vec, vec],
        scratch_shapes=[pltpu.VMEM((8, E), F32), pltpu.VMEM((8, E), F32)],
        compiler_params=_cparams(("arbitrary",)), name=name)(dh, yf, yb, cact, proj, dskip_x, nw, wout)


def _local_step(x, target, p, hooks):
    E, S = D_INNER, x.shape[0]
    depth = p["norm_w"].shape[0]
    names = ["norm_w", "cm_w_in", "cm_dw_w", "cm_dw_b", "cm_ln_w", "cm_ln_b", "cm_w_out", "ssd_w_in", "ssd_conv_w",
             "ssd_conv_b", "ssd_dt_bias", "ssd_A_log", "ssd_D", "ssd_norm_w", "ssd_w_out"]
    g = {n: [None] * (depth if n == "norm_w" else depth // 2) for n in names}

    def carried(key):
        items, done = hooks.get(key, (None, None))
        return (items(g) if items else None), (done if done else lambda got: None)

    h, saved = x, []
    for i in range(depth):
        j, nw = i // 2, p["norm_w"][i][None]
        if i % 2 == 0:
            carry, done = carried(("cm_in_fwd", i))
            proj, got = _inproj_fwd(h, nw, p["cm_w_in"][j], tail=False, wt=False, name="cm_in_fwd", carry=carry)
            done(got)
            carry, done = carried(("cm_conv_fwd", i))
            uc, got = _dwconv_fwd(proj, p["cm_dw_w"][j], p["cm_dw_b"][j][None], col0=0, glu_col0=E, C=E,
                                  name="cm_conv_fwd", carry=carry)
            done(got)
            hn = _conf_out_fwd(uc, proj, h, p["cm_ln_w"][j][None], p["cm_ln_b"][j][None], p["cm_w_out"][j], name="cm_out_fwd")
            saved.append((h, proj, uc))
        else:
            (proj, tail), _ = _inproj_fwd(h, nw, p["ssd_w_in"][j], tail=True, wt=True, name="ssd_in_fwd")
            cact, _ = _dwconv_fwd(proj, p["ssd_conv_w"][j], p["ssd_conv_b"][j][None], col0=E, glu_col0=None,
                                  C=SSD_CONV_DIM, name="ssd_conv_fwd")
            yf, yb, hf, hb = _ssd_scan_fwd(cact, tail, p["ssd_dt_bias"][j], p["ssd_A_log"][j], name="ssd_scan_fwd")
            dsk = jnp.repeat(p["ssd_D"][j], HEADDIM)[None]
            hn = _ssd_out_fwd(yf, yb, cact, proj, dsk, p["ssd_norm_w"][j][None], h, p["ssd_w_out"][j], name="ssd_out_fwd")
            saved.append((h, proj, tail, cact, yf, yb, hf, hb, dsk))
        h = hn
    loss, dh, g_fnw = _loss_head(h, p["final_norm_w"][None], target, name="loss_head")

    for i in reversed(range(depth)):
        j, nw = i // 2, p["norm_w"][i][None]
        if i % 2 == 0:
            h_in, proj, uc = saved[i]
            g["cm_w_out"][j], duc, dz, g_lw, g_lb = _conf_out_bwd(
                dh, uc, proj, p["cm_ln_w"][j][None], p["cm_ln_b"][j][None], p["cm_w_out"][j], name="cm_out_bwd")
            carry, done = carried(("cm_conv_bwd", i))
            (dv, dg, g_dw, g_db), got = _dwconv_bwd([duc], proj, p["cm_dw_w"][j], col0=0, glu_col0=E, C=E,
                                                    name="cm_conv_bwd", carry=carry)
            done(got)
            parts = [dv, dg, dz]
            dh, hnb, g_nw = _inproj_bwd(parts, [0, E, 2 * E], p["cm_w_in"][j], h_in, nw, dh, wt=False, name="cm_in_bwd")
            g["cm_w_in"][j] = jnp.concatenate(
                [_tn_matmul(hnb, q, transpose_out=False, name="cm_win_grad") for q in parts], axis=1)
            g["cm_dw_w"][j], g["cm_dw_b"][j], g["cm_ln_w"][j], g["cm_ln_b"][j] = g_dw, g_db[0], g_lw[0], g_lb[0]
        else:
            h_in, proj, tail, cact, yf, yb, hf, hb, dsk = saved[i]
            g["ssd_w_out"][j], dy, dz, g_sn, g_dx = _ssd_out_bwd(
                dh, yf, yb, cact, proj, dsk, p["ssd_norm_w"][j][None], p["ssd_w_out"][j], name="ssd_out_bwd")
            carry, done = carried(("ssd_scan_bwd", i))
            (dcf, dcb, ddtf, ddtb, g_al, g_bias), got = _ssd_scan_bwd(
                cact, tail, p["ssd_dt_bias"][j], p["ssd_A_log"][j], dy, hf, hb, dsk, name="ssd_scan_bwd", carry=carry)
            done(got)
            (dxbc, g_cw, g_cb), _ = _dwconv_bwd([dcf, dcb], proj, p["ssd_conv_w"][j], col0=E, glu_col0=None,
                                                C=SSD_CONV_DIM, name="ssd_conv_bwd")
            ddt = jnp.concatenate([ddtf, ddtb, jnp.zeros((S, 128 - 2 * SSD_HEADS), F32)], axis=1).astype(BF16)
            parts = [dz, dxbc, ddt]
            dh, hnb, g_nw = _inproj_bwd(parts, [0, E, E + SSD_CONV_DIM], p["ssd_w_in"][j], h_in, nw, dh, wt=True,
                                        name="ssd_in_bwd")
            gw = [_tn_matmul(hnb, q, transpose_out=True, name="ssd_win_grad") for q in parts]
            g["ssd_w_in"][j] = jnp.concatenate([gw[0], gw[1], gw[2][:2 * SSD_HEADS]], axis=0)
            g["ssd_conv_w"][j], g["ssd_conv_b"][j], g["ssd_norm_w"][j] = g_cw, g_cb[0], g_sn[0]
            g["ssd_dt_bias"][j], g["ssd_A_log"][j] = g_bias, g_al
            g["ssd_D"][j] = jnp.sum(g_dx.reshape(SSD_HEADS, HEADDIM), axis=1)
        g["norm_w"][i] = g_nw[0]
    g["final_norm_w"] = g_fnw[0]
    return loss, dh, g


_FLIPS = [(fx, fy, fc) for fx in (0, 1) for fy in (0, 1) for fc in (0, 1)][1:]


def _exchange_copies(items, ins, outs, send_sems, recv_sems, local_sems, arrivals):
    x, y, c = lax.axis_index("x"), lax.axis_index("y"), lax.axis_index("c")
    me = 4 * x + 2 * y + c
    local, pairs = [], []
    for b, (_, scatter) in enumerate(items):
        local.append(pltpu.make_async_copy(ins[b].at[me] if scatter else ins[b], outs[b].at[me], local_sems.at[b]))
        for k, (fx, fy, fc) in enumerate(_FLIPS):
            px, py, pc = x ^ fx, y ^ fy, c ^ fc
            peer = 4 * px + 2 * py + pc
            src = ins[b].at[peer] if scatter else ins[b]
            sems = dict(send_sem=send_sems.at[b * 7 + k], recv_sem=recv_sems.at[b * 7 + k],
                        device_id=(px, py, pc), device_id_type=pl.DeviceIdType.MESH)
            pairs.append((pltpu.make_async_remote_copy(src_ref=src, dst_ref=outs[b].at[me], **sems),
                          pltpu.make_async_remote_copy(src_ref=src, dst_ref=outs[b].at[peer], **sems)
                          if arrivals else None))
    return local, pairs


def _exchange_start(items, ins, outs, *sems):
    local, pairs = _exchange_copies(items, ins, outs, *sems, arrivals=False)
    for cp in local:
        cp.start()
    for cp, _ in pairs:
        cp.start()


def _exchange_wait(items, ins, outs, *sems):
    local, pairs = _exchange_copies(items, ins, outs, *sems, arrivals=True)
    for _, arrival in pairs:
        arrival.wait_recv()
    for cp, _ in pairs:
        cp.wait_send()
    for cp in local:
        cp.wait()


def _exchange_shapes(items):
    n = len(items)
    out_shape = [SDS((N_DEV,) + tuple(a.shape[-2:]), a.dtype) for a, _ in items]
    sems = [pltpu.SemaphoreType.DMA((7 * n,)), pltpu.SemaphoreType.DMA((7 * n,)), pltpu.SemaphoreType.DMA((n,))]
    return out_shape, sems


def _exchange(items, *, name):
    n = len(items)

    def body(*refs):
        _exchange_start(items, refs[:n], refs[n:2 * n], *refs[2 * n:])
        _exchange_wait(items, refs[:n], refs[n:2 * n], *refs[2 * n:])

    hbm = pl.BlockSpec(memory_space=pltpu.HBM)
    out_shape, sems = _exchange_shapes(items)
    return pl.pallas_call(body, out_shape=out_shape, in_specs=[hbm] * n, out_specs=[hbm] * n,
                          scratch_shapes=sems, name=name)(*[a for a, _ in items])


def _call(body, *, grid, out_shape, in_specs, out_specs, scratch_shapes, sem, name, args, carry):
    if not carry:
        return pl.pallas_call(body, grid=grid, out_shape=out_shape, in_specs=in_specs, out_specs=out_specs,
                              scratch_shapes=scratch_shapes, compiler_params=_cparams(sem), name=name)(*args), None
    n, n_in, n_out, n_scr = len(carry), len(in_specs), len(out_specs), len(scratch_shapes)
    last_step = tuple(g - 1 for g in grid)

    def wrapped(*refs):
        ins, cin = refs[:n_in], refs[n_in:n_in + n]
        outs, cout = refs[n_in + n:n_in + n + n_out], refs[n_in + n + n_out:n_in + 2 * n + n_out]
        scr, sems = refs[n_in + 2 * n + n_out:n_in + 2 * n + n_out + n_scr], refs[n_in + 2 * n + n_out + n_scr:]
        first = last = None
        for d, top in enumerate(last_step):
            i = pl.program_id(d)
            first = (i == 0) if first is None else first & (i == 0)
            last = (i == top) if last is None else last & (i == top)

        @pl.when(first)
        def _():
            _exchange_start(carry, cin, cout, *sems)

        body(*ins, *outs, *scr)

        @pl.when(last)
        def _():
            _exchange_wait(carry, cin, cout, *sems)

    hbm = pl.BlockSpec(memory_space=pltpu.HBM)
    x_shape, x_sems = _exchange_shapes(carry)
    res = pl.pallas_call(
        wrapped, grid=grid, out_shape=list(out_shape) + x_shape, in_specs=list(in_specs) + [hbm] * n,
        out_specs=list(out_specs) + [hbm] * n, scratch_shapes=list(scratch_shapes) + x_sems,
        compiler_params=_cparams(("arbitrary",) * len(grid)), name=name)(*args, *[a for a, _ in carry])
    return res[:n_out], res[n_out:]


def _reduce_slots(recv, *, rows, name):
    n, R, C = recv.shape

    def body(r_ref, g_ref):
        g = r_ref[0].astype(F32)
        for s in range(1, n):
            g = g + r_ref[s].astype(F32)
        g_ref[...] = g

    return pl.pallas_call(
        body, grid=(R // rows,), out_shape=SDS((R, C), F32),
        in_specs=[pl.BlockSpec((n, rows, C), lambda i: (0, i, 0))],
        out_specs=pl.BlockSpec((rows, C), lambda i: (i, 0)),
        compiler_params=_cparams(("parallel",)), name=name)(recv)


def _reduce_adamw(recv, w, m, v, *, layer, rows, name):
    _, R, C = w.shape
    n = recv.shape[0]
    c1 = 1.0 / (1.0 - ADAM_B1 ** ADAM_STEP)
    c2 = 1.0 / (1.0 - ADAM_B2 ** ADAM_STEP)

    def body(r_ref, w_ref, m_ref, v_ref, g_ref, d_ref, mo_ref, vo_ref):
        g = r_ref[0].astype(F32)
        for s in range(1, n):
            g = g + r_ref[s].astype(F32)
        mn = ADAM_B1 * m_ref[...] + (1.0 - ADAM_B1) * g
        vn = ADAM_B2 * v_ref[...] + (1.0 - ADAM_B2) * (g * g)
        g_ref[...] = g
        mo_ref[...] = mn
        vo_ref[...] = vn
        d_ref[...] = -ADAM_LR * ((mn * c1) / (jnp.sqrt(vn * c2) + ADAM_EPS) + ADAM_WD * w_ref[...])

    tile = pl.BlockSpec((rows, C), lambda i: (i, 0))
    mine = pl.BlockSpec((None, rows, C), lambda i: (layer, i, 0))
    return pl.pallas_call(
        body, grid=(R // rows,), out_shape=[SDS((R, C), F32)] * 4,
        in_specs=[pl.BlockSpec((n, rows, C), lambda i: (0, i, 0)), mine, mine, mine],
        out_specs=[tile] * 4, compiler_params=_cparams(("parallel",)), name=name)(recv, w, m, v)


_ORDER = ["norm_w", "final_norm_w", "cm_w_in", "cm_dw_w", "cm_dw_b", "cm_ln_w", "cm_ln_b", "cm_w_out", "ssd_w_in",
          "ssd_conv_w", "ssd_conv_b", "ssd_dt_bias", "ssd_A_log", "ssd_D", "ssd_norm_w", "ssd_w_out"]
_SMALL = (["cm_dw_w", "ssd_conv_w", "ssd_conv_b", "ssd_norm_w"], 168)
_REP = (["norm_w", "final_norm_w", "cm_dw_b", "cm_ln_w", "cm_ln_b", "ssd_dt_bias", "ssd_A_log", "ssd_D"], 144)
_SHARD_AXIS = {"cm_w_in": 2, "cm_dw_w": 2, "cm_w_out": 1, "ssd_w_in": 2, "ssd_conv_w": 2, "ssd_conv_b": 1,
               "ssd_norm_w": 1, "ssd_w_out": 1}
OPT_ROWS = 256


def _pack(arrs, rows, lanes, dtype, lead=()):
    nl = len(lead)
    flat = jnp.concatenate([a.reshape(lead + (-1,)).astype(dtype) for a in arrs], axis=nl)
    flat = jnp.pad(flat, [(0, 0)] * nl + [(0, rows * lanes - flat.shape[nl])])
    return flat.reshape(lead + (rows, lanes))


def _unpack(buf, shapes, lead=()):
    nl = len(lead)
    flat = buf.reshape(lead + (-1,))
    out, off = [], 0
    for s in shapes:
        sz = math.prod(s)
        out.append(flat[..., off:off + sz].reshape(lead + tuple(s)))
        off += sz
    return out


def _join_shards(parts, axis):
    full = jnp.moveaxis(parts, 0, axis)
    sh = full.shape
    return full.reshape(sh[:axis] + (sh[axis] * sh[axis + 1],) + sh[axis + 2:])


def _split_shards(full, axis):
    sh = full.shape
    parts = full.reshape(sh[:axis] + (N_DEV, sh[axis] // N_DEV) + sh[axis + 1:])
    return jnp.moveaxis(parts, axis, 0)


def kernel(x, norm_w, final_norm_w, cm_w_in, cm_dw_w, cm_dw_b, cm_ln_w, cm_ln_b, cm_w_out, ssd_w_in, ssd_conv_w, ssd_conv_b, ssd_dt_bias, ssd_A_log, ssd_D, ssd_norm_w, ssd_w_out, loss_target, m_norm_w, m_final_norm_w, m_cm_w_in, m_cm_dw_w, m_cm_dw_b, m_cm_ln_w, m_cm_ln_b, m_cm_w_out, m_ssd_w_in, m_ssd_conv_w, m_ssd_conv_b, m_ssd_dt_bias, m_ssd_A_log, m_ssd_D, m_ssd_norm_w, m_ssd_w_out, v_norm_w, v_final_norm_w, v_cm_w_in, v_cm_dw_w, v_cm_dw_b, v_cm_ln_w, v_cm_ln_b, v_cm_w_out, v_ssd_w_in, v_ssd_conv_w, v_ssd_conv_b, v_ssd_dt_bias, v_ssd_A_log, v_ssd_D, v_ssd_norm_w, v_ssd_w_out):
    w = dict(zip(_ORDER, (norm_w, final_norm_w, cm_w_in, cm_dw_w, cm_dw_b, cm_ln_w, cm_ln_b, cm_w_out, ssd_w_in,
                          ssd_conv_w, ssd_conv_b, ssd_dt_bias, ssd_A_log, ssd_D, ssd_norm_w, ssd_w_out)))
    m = dict(zip(_ORDER, (m_norm_w, m_final_norm_w, m_cm_w_in, m_cm_dw_w, m_cm_dw_b, m_cm_ln_w, m_cm_ln_b, m_cm_w_out,
                          m_ssd_w_in, m_ssd_conv_w, m_ssd_conv_b, m_ssd_dt_bias, m_ssd_A_log, m_ssd_D, m_ssd_norm_w,
                          m_ssd_w_out)))
    v = dict(zip(_ORDER, (v_norm_w, v_final_norm_w, v_cm_w_in, v_cm_dw_w, v_cm_dw_b, v_cm_ln_w, v_cm_ln_b, v_cm_w_out,
                          v_ssd_w_in, v_ssd_conv_w, v_ssd_conv_b, v_ssd_dt_bias, v_ssd_A_log, v_ssd_D, v_ssd_norm_w,
                          v_ssd_w_out)))

    D, nl = D_MODEL, cm_w_in.shape[0]
    (sn, sr), (rn, rr) = _SMALL, _REP
    mats = ["cm_w_in", "cm_w_out", "ssd_w_in", "ssd_w_out"]
    bf = lambda a: a.astype(BF16)
    wt_in = jnp.swapaxes(ssd_w_in, 1, 2)
    t_rows = wt_in.shape[1]
    t_pad = -t_rows % 16

    p = {k: w[k] for k in rn}
    p.update({k: [None] * nl for k in mats})
    first = _exchange([(bf(cm_w_in[0]), False), (_pack([w[k] for k in sn], sr, 128, F32), False)], name="gather_first")
    p["cm_w_in"][0] = _join_shards(first[0], 1)
    for k, parts in zip(sn, _unpack(first[1], [w[k].shape for k in sn], lead=(N_DEV,))):
        p[k] = _join_shards(parts, _SHARD_AXIS[k])

    def rest_in(got):
        p["cm_w_out"][0] = got[0].reshape(-1, D)

    def rest_conv(got):
        p["cm_w_in"][1] = _join_shards(got[0], 1)
        p["cm_w_out"][1] = got[1].reshape(-1, D)
        wt = _join_shards(got[2].astype(F32).reshape(N_DEV, nl, -1, D), 1)
        wt = bf(jnp.pad(wt, ((0, 0), (0, SSD_IN_PAD - SSD_IN), (0, 0))))
        wo = got[3].reshape(N_DEV, nl, -1, D)
        for j in range(nl):
            p["ssd_w_in"][j] = wt[j]
            p["ssd_w_out"][j] = wo[:, j].reshape(-1, D)

    hooks = {("cm_in_fwd", 0): (lambda g: [(bf(cm_w_out[0]), False)], rest_in),
             ("cm_conv_fwd", 0): (lambda g: [(bf(cm_w_in[1]), False), (bf(cm_w_out[1]), False),
                                             (bf(wt_in.reshape(-1, D)), False), (bf(ssd_w_out.reshape(-1, D)), False)],
                                  rest_conv)}

    recv = {}

    def blocks(name, j, g):
        if name == "cm_w_in":
            return bf(_split_shards(g[name][j], 1))
        b = _split_shards(g[name][j], 0)
        if name == "ssd_w_in":
            b = jnp.pad(b, ((0, 0), (0, t_pad), (0, 0)))
        return bf(b)

    def sender(keys):
        def items(g):
            return [(blocks(n, j, g), True) for n, j in keys]

        def done(got):
            recv.update(zip(keys, got))

        return items, done

    hooks["ssd_scan_bwd", 1] = sender([(n, 1) for n in mats])
    hooks["cm_conv_bwd", 0] = sender([("ssd_w_in", 0), ("ssd_w_out", 0), ("cm_w_out", 0)])

    loss, dx, g = _local_step(x[0], loss_target[0], p, hooks)
    loss = lax.psum(loss, ("x", "y", "c"))

    last_keys = [("cm_w_in", 0)]
    stacked = {k: jnp.stack(g[k]) for k in sn + rn if k != "final_norm_w"}
    stacked["final_norm_w"] = g["final_norm_w"]
    last = _exchange([(blocks(n, j, g), True) for n, j in last_keys]
                     + [(_pack([_split_shards(stacked[k], _SHARD_AXIS[k]) for k in sn], sr, 128, F32, lead=(N_DEV,)), True),
                        (_pack([stacked[k] for k in rn], rr, 128, F32), False)], name="exchange_last")
    recv.update(zip(last_keys, last))

    outs = {}
    for name in mats:
        shp = w[name].shape
        w3, m3, v3 = [d[name].reshape(nl, -1, shp[-1]) for d in (w, m, v)]
        per_layer = []
        for j in range(nl):
            r = recv[name, j]
            if name == "ssd_w_in":
                gt = _reduce_slots(r, rows=r.shape[1], name="sum_ssd_w_in")
                r = jnp.swapaxes(gt[:t_rows], 0, 1)[None]
            per_layer.append(_reduce_adamw(r, w3, m3, v3, layer=j, rows=min(OPT_ROWS, r.shape[1]), name="adamw_" + name))
        for kind, parts in zip(("g", "d", "m", "v"), zip(*per_layer)):
            outs[kind, name] = jnp.stack(parts).reshape(shp)
    for (names, rows), r, nm in ((_SMALL, last[-2], "adamw_small"), (_REP, last[-1], "adamw_rep")):
        res = _reduce_adamw(r, *[_pack([d[k] for k in names], rows, 128, F32)[None] for d in (w, m, v)], layer=0,
                            rows=rows, name=nm)
        for kind, buf in zip(("g", "d", "m", "v"), res):
            for k, a in zip(names, _unpack(buf, [w[k].shape for k in names])):
                outs[kind, k] = a
    return (loss, dx[None], *[outs[kind, k] for kind in ("g", "d", "m", "v") for k in _ORDER])
```

```python
import functools
import math

import jax
import jax.numpy as jnp
from jax import lax
from jax.experimental import pallas as pl
from jax.experimental.pallas import tpu as pltpu

F32, BF16 = jnp.float32, jnp.bfloat16
SDS = jax.ShapeDtypeStruct

D_MODEL = 1024
D_INNER = 2048
CONF_KERNEL = 31
HEADDIM = 64
SSD_HEADS = 32
SSD_GROUPS = 4
HPG = SSD_HEADS // SSD_GROUPS
D_STATE = 128
SSD_CONV = 5
CHUNK = 128
GN = SSD_GROUPS * D_STATE
SSD_CONV_DIM = D_INNER + 2 * GN
SSD_IN = D_INNER + SSD_CONV_DIM + 2 * SSD_HEADS
SSD_IN_PAD = SSD_IN + 64
EPS = 1e-5
N_DEV = 8

ADAM_LR, ADAM_B1, ADAM_B2, ADAM_EPS, ADAM_WD, ADAM_STEP = 0.001, 0.9, 0.999, 1e-08, 0.01, 10

VMEM_LIMIT = 56 * 1024 * 1024
ROW_TILE = 512
GATE_TILE = 256
GRAD_TILE = 1024
CONV_ROWS = 512
CONV_LANES = 512
CONV_HALO = 16
CONV_RB = 16
CONV_LANES_FEW = 1024
CONV_RB_FEW = 16
CONV_DIRECT_TAPS = 8
NEG = -1e30


def _cparams(sem):
    return pltpu.CompilerParams(dimension_semantics=sem, vmem_limit_bytes=VMEM_LIMIT)


def _resident(shape):
    nd = len(shape)
    return pl.BlockSpec(shape, lambda *_: (0,) * nd, pipeline_mode=pl.Buffered(1))


def _sig(x):
    return 1.0 / (1.0 + jnp.exp(-x))


def _silu(x):
    return x * _sig(x)


def _dsilu(x):
    s = _sig(x)
    return s * (1.0 + x * (1.0 - s))


def _softplus(x):
    return jnp.maximum(x, 0.0) + jnp.log(1.0 + jnp.exp(-jnp.abs(x)))


def _col_chunks(n, width):
    out, c = [], 0
    while c < n:
        out.append((c, min(c + width, n)))
        c += width
    return out


def _rowsum8(v):
    acc = v[0:8]
    for j in range(1, v.shape[0] // 8):
        acc = acc + v[8 * j:8 * j + 8]
    return acc


def _inproj_fwd(h, nw, w, *, tail, wt, name, carry=None):
    S, D = h.shape
    N = w.shape[0] if wt else w.shape[1]
    tm = min(ROW_TILE, S)
    chunks = _col_chunks(N, 1024)

    def body(h_ref, nw_ref, w_ref, o_ref, *rest):
        x = h_ref[...]
        r = lax.rsqrt(jnp.mean(x * x, axis=-1, keepdims=True) + EPS)
        hn = (x * r * nw_ref[...]).astype(BF16)
        for c0, c1 in chunks:
            if wt:
                acc = lax.dot_general(hn, w_ref[c0:c1, :], (((1,), (1,)), ((), ())), preferred_element_type=F32)
            else:
                acc = jnp.dot(hn, w_ref[:, c0:c1], preferred_element_type=F32)
            o_ref[:, c0:c1] = acc.astype(BF16)
            if tail and c1 == N:
                rest[0][...] = acc[:, acc.shape[1] - 128:]

    out_shape = [SDS((S, N), BF16)]
    out_specs = [pl.BlockSpec((tm, N), lambda i: (i, 0))]
    if tail:
        out_shape.append(SDS((S, 128), F32))
        out_specs.append(pl.BlockSpec((tm, 128), lambda i: (i, 0)))
    res, got = _call(
        body, grid=(S // tm,), out_shape=out_shape,
        in_specs=[pl.BlockSpec((tm, D), lambda i: (i, 0)), _resident((1, D)), _resident(w.shape)],
        out_specs=out_specs, scratch_shapes=[], sem=("parallel",), name=name, args=(h, nw, w), carry=carry)
    return (res if tail else res[0]), got


def _tap_plan(offsets):
    rs = sorted({s % 8 for s in offsets})
    return rs, [(k, rs.index(s % 8), s // 8) for k, s in enumerate(offsets)]


def _halo_specs(S, rows, lanes, col0):
    per, last = rows // CONV_HALO, S // CONV_HALO - 1
    cb = col0 // lanes
    return [pl.BlockSpec((rows, lanes), lambda j, i: (i, cb + j)),
            pl.BlockSpec((CONV_HALO, lanes), lambda j, i: (jnp.maximum(i * per - 1, 0), cb + j)),
            pl.BlockSpec((CONV_HALO, lanes), lambda j, i: (jnp.minimum((i + 1) * per, last), cb + j))]


def _fill_window(win, prev, main, nxt, i, nt, rows):
    hb = CONV_HALO
    win[0:hb, :] = jnp.where(i > 0, prev, 0.0)
    win[hb:hb + rows, :] = main
    win[hb + rows:hb + rows + hb, :] = jnp.where(i < nt - 1, nxt, 0.0)


def _fill_shifts(win, sh, rs, n_u):
    for idx, r in enumerate(rs):
        sh[idx] = win[pl.ds(r, n_u), :]


def _tap_rows(win, sh, rs, ri, q, r0, n, direct):
    if direct:
        return win[pl.ds(r0 + 8 * q + rs[ri], n), :]
    return sh[ri, pl.ds(pl.multiple_of(r0 + 8 * q, 8), n), :]


def _tap_groups(taps):
    by_copy = {}
    for k, ri, q in taps:
        by_copy.setdefault(ri, []).append((k, q))
    out = []
    for ri, members in sorted(by_copy.items()):
        q0 = min(q for _, q in members)
        out.append((ri, q0, max(q for _, q in members) - q0, [(k, q - q0) for k, q in members]))
    return out


def _tap_sum(w_ref, taps, slab_of, rb, lanes):
    accs = [None] * (rb // 8)
    for ri, q0, extra, members in _tap_groups(taps):
        slab = slab_of(ri, q0, rb + 8 * extra)
        for k, dq in members:
            wk = jnp.broadcast_to(w_ref[k:k + 1, :], (8, lanes))
            for i in range(rb // 8):
                t = wk * slab[8 * (dq + i):8 * (dq + i) + 8]
                accs[i] = t if accs[i] is None else accs[i] + t
    return jnp.concatenate(accs, axis=0)


def _row_loop(n, rb, fn, direct):
    if direct:
        for t in range(n):
            fn(t * rb)
    else:
        def step(t, carry):
            fn(pl.multiple_of(t * rb, rb))
            return carry

        lax.fori_loop(0, n, step, 0)


def _conv_blocking(K):
    return (CONV_LANES_FEW, CONV_RB_FEW) if K <= CONV_DIRECT_TAPS else (CONV_LANES, CONV_RB)


def _dwconv_fwd(src, w, b, *, col0, glu_col0, C, name, carry=None):
    S = src.shape[0]
    K = w.shape[0]
    pad = (K - 1) // 2
    rows, hb = min(CONV_ROWS, S), CONV_HALO
    lanes, rb = _conv_blocking(K)
    nt = S // rows
    rs, taps = _tap_plan([k - pad + hb for k in range(K)])
    direct = K <= CONV_DIRECT_TAPS
    n_u = 8 if direct else rows + 2 * hb - 8
    glu = glu_col0 is not None

    def body(*refs):
        if glu:
            vm, vp, vn, gm, gp, gn, w_ref, b_ref, o_ref, win, sh = refs
        else:
            vm, vp, vn, w_ref, b_ref, o_ref, win, sh = refs
        i = pl.program_id(1)

        def act(k):
            v = (vm, vp, vn)[k][...].astype(F32)
            return v * _sig((gm, gp, gn)[k][...].astype(F32)) if glu else v

        _fill_window(win, act(1), act(0), act(2), i, nt, rows)
        if not direct:
            _fill_shifts(win, sh, rs, n_u)

        def rowblk(r0):
            acc = _tap_sum(w_ref, taps, lambda ri, q, n: _tap_rows(win, sh, rs, ri, q, r0, n, direct), rb, lanes)
            o_ref[pl.ds(r0, rb), :] = (acc + b_ref[...]).astype(BF16)

        _row_loop(rows // rb, rb, rowblk, direct)

    in_specs = _halo_specs(S, rows, lanes, col0)
    args = [src, src, src]
    if glu:
        in_specs += _halo_specs(S, rows, lanes, glu_col0)
        args += [src, src, src]
    in_specs += [pl.BlockSpec((K, lanes), lambda j, i: (0, j)), pl.BlockSpec((1, lanes), lambda j, i: (0, j))]
    res, got = _call(
        body, grid=(C // lanes, nt), out_shape=[SDS((S, C), BF16)], in_specs=in_specs,
        out_specs=[pl.BlockSpec((rows, lanes), lambda j, i: (i, j))],
        scratch_shapes=[pltpu.VMEM((rows + 2 * hb, lanes), F32), pltpu.VMEM((len(rs), n_u, lanes), F32)],
        sem=("parallel", "parallel"), name=name, args=(*args, w, b), carry=carry)
    return res[0], got


def _dwconv_bwd(dsrcs, src, w, *, col0, glu_col0, C, name, carry=None):
    S = src.shape[0]
    K = w.shape[0]
    pad = (K - 1) // 2
    rows, hb = min(CONV_ROWS, S), CONV_HALO
    lanes, rb = _conv_blocking(K)
    nt = S // rows
    rs_u, taps_u = _tap_plan([k - pad + hb for k in range(K)])
    rs_d, taps_d = _tap_plan([hb + pad - k for k in range(K)])
    direct = K <= CONV_DIRECT_TAPS
    n_u = 8 if direct else rows + 2 * hb - 8
    glu = glu_col0 is not None
    nd = len(dsrcs)

    def body(*refs):
        d_refs, refs = refs[:3 * nd], refs[3 * nd:]
        if glu:
            vm, vp, vn, gm, gp, gn, w_ref = refs[:7]
            dv_ref, dg_ref, dw_ref, db_ref, dwin, dsh, uwin, ush, accw, accb = refs[7:]
        else:
            vm, vp, vn, w_ref = refs[:4]
            du_ref, dw_ref, db_ref, dwin, dsh, uwin, ush, accw, accb = refs[4:]
        i = pl.program_id(1)

        def dpiece(k):
            v = d_refs[k][...].astype(F32)
            for e in range(1, nd):
                v = v + d_refs[3 * e + k][...].astype(F32)
            return v

        def act(k):
            v = (vm, vp, vn)[k][...].astype(F32)
            return v * _sig((gm, gp, gn)[k][...].astype(F32)) if glu else v

        _fill_window(dwin, dpiece(1), dpiece(0), dpiece(2), i, nt, rows)
        _fill_window(uwin, act(1), act(0), act(2), i, nt, rows)
        if not direct:
            _fill_shifts(dwin, dsh, rs_d, n_u)
            _fill_shifts(uwin, ush, rs_u, n_u)

        @pl.when(i == 0)
        def _():
            accw[...] = jnp.zeros_like(accw)
            accb[...] = jnp.zeros_like(accb)

        def rowblk(r0):
            du = _tap_sum(w_ref, taps_d, lambda ri, q, n: _tap_rows(dwin, dsh, rs_d, ri, q, r0, n, direct), rb, lanes)
            if glu:
                v = vm[pl.ds(r0, rb), :].astype(F32)
                s = _sig(gm[pl.ds(r0, rb), :].astype(F32))
                dv_ref[pl.ds(r0, rb), :] = (du * s).astype(BF16)
                dg_ref[pl.ds(r0, rb), :] = (du * v * s * (1.0 - s)).astype(BF16)
            else:
                du_ref[pl.ds(r0, rb), :] = du.astype(BF16)
            dmain = dwin[pl.ds(r0 + hb, rb), :] if direct else dwin[pl.ds(pl.multiple_of(r0 + hb, 8), rb), :]
            accb[...] += _rowsum8(dmain)
            for ri, q0, extra, members in _tap_groups(taps_u):
                slab = _tap_rows(uwin, ush, rs_u, ri, q0, r0, rb + 8 * extra, direct)
                for k, dq in members:
                    accw[k] += _rowsum8(dmain * slab[8 * dq:8 * dq + rb])

        _row_loop(rows // rb, rb, rowblk, direct)

        @pl.when(i == nt - 1)
        def _():
            for k in range(K):
                dw_ref[k:k + 1, :] = jnp.sum(accw[k], axis=0, keepdims=True)
            db_ref[...] = jnp.sum(accb[...], axis=0, keepdims=True)

    in_specs, args = [], []
    for d in dsrcs:
        in_specs += _halo_specs(S, rows, lanes, 0)
        args += [d, d, d]
    in_specs += _halo_specs(S, rows, lanes, col0)
    args += [src, src, src]
    if glu:
        in_specs += _halo_specs(S, rows, lanes, glu_col0)
        args += [src, src, src]
    in_specs += [pl.BlockSpec((K, lanes), lambda j, i: (0, j))]
    tile = pl.BlockSpec((rows, lanes), lambda j, i: (i, j))
    n_act = 2 if glu else 1
    out_shape = [SDS((S, C), BF16)] * n_act + [SDS((K, C), F32), SDS((1, C), F32)]
    out_specs = [tile] * n_act + [pl.BlockSpec((K, lanes), lambda j, i: (0, j)),
                                  pl.BlockSpec((1, lanes), lambda j, i: (0, j))]
    return _call(
        body, grid=(C // lanes, nt), out_shape=out_shape, in_specs=in_specs, out_specs=out_specs,
        scratch_shapes=[pltpu.VMEM((rows + 2 * hb, lanes), F32), pltpu.VMEM((len(rs_d), n_u, lanes), F32),
                        pltpu.VMEM((rows + 2 * hb, lanes), F32), pltpu.VMEM((len(rs_u), n_u, lanes), F32),
                        pltpu.VMEM((K, 8, lanes), F32), pltpu.VMEM((8, lanes), F32)],
        sem=("parallel", "arbitrary"), name=name, args=(*args, w), carry=carry)


def _ln_parts(u, ln_w, ln_b):
    mu = jnp.mean(u, axis=-1, keepdims=True)
    xc = u - mu
    rstd = lax.rsqrt(jnp.mean(xc * xc, axis=-1, keepdims=True) + EPS)
    un = xc * rstd
    return un, rstd, un * ln_w + ln_b


def _conf_out_fwd(uc, proj, h, ln_w, ln_b, wout, *, name):
    S, E = uc.shape
    D = h.shape[1]
    tm = min(GATE_TILE, S)

    def body(uc_ref, z_ref, h_ref, lw_ref, lb_ref, w_ref, o_ref):
        _, _, ul = _ln_parts(uc_ref[...].astype(F32), lw_ref[...], lb_ref[...])
        u3 = (_silu(ul) * _silu(z_ref[...].astype(F32))).astype(BF16)
        o_ref[...] = h_ref[...] + jnp.dot(u3, w_ref[...], preferred_element_type=F32)

    return pl.pallas_call(
        body, grid=(S // tm,), out_shape=SDS((S, D), F32),
        in_specs=[pl.BlockSpec((tm, E), lambda i: (i, 0)), pl.BlockSpec((tm, E), lambda i: (i, 2)),
                  pl.BlockSpec((tm, D), lambda i: (i, 0)), _resident((1, E)), _resident((1, E)), _resident((E, D))],
        out_specs=pl.BlockSpec((tm, D), lambda i: (i, 0)),
        compiler_params=_cparams(("parallel",)), name=name)(uc, proj, h, ln_w, ln_b, wout)


def _conf_out_bwd(dh, uc, proj, ln_w, ln_b, wout, *, name):
    S, E = uc.shape
    D = dh.shape[1]
    tm = min(GATE_TILE, S)
    nt = S // tm

    def body(dh_ref, uc_ref, z_ref, lw_ref, lb_ref, w_ref, dw_ref, duc_ref, dz_ref, dlw_ref, dlb_ref, alw, alb):
        i = pl.program_id(0)

        @pl.when(i == 0)
        def _():
            alw[...] = jnp.zeros_like(alw)
            alb[...] = jnp.zeros_like(alb)
            dw_ref[...] = jnp.zeros_like(dw_ref)

        un, rstd, ul = _ln_parts(uc_ref[...].astype(F32), lw_ref[...], lb_ref[...])
        z = z_ref[...].astype(F32)
        su, sz = _silu(ul), _silu(z)
        dhb = dh_ref[...].astype(BF16)
        dw_ref[...] += lax.dot_general((su * sz).astype(BF16), dhb, (((0,), (0,)), ((), ())),
                                       preferred_element_type=F32)
        du3 = lax.dot_general(dhb, w_ref[...], (((1,), (1,)), ((), ())), preferred_element_type=F32)
        dz_ref[...] = (du3 * su * _dsilu(z)).astype(BF16)
        dul = du3 * sz * _dsilu(ul)
        alw[...] += _rowsum8(dul * un)
        alb[...] += _rowsum8(dul)
        dun = dul * lw_ref[...]
        m1 = jnp.mean(dun, axis=-1, keepdims=True)
        m2 = jnp.mean(dun * un, axis=-1, keepdims=True)
        duc_ref[...] = (rstd * (dun - m1 - un * m2)).astype(BF16)

        @pl.when(i == nt - 1)
        def _():
            dlw_ref[...] = jnp.sum(alw[...], axis=0, keepdims=True)
            dlb_ref[...] = jnp.sum(alb[...], axis=0, keepdims=True)

    tile = pl.BlockSpec((tm, E), lambda i: (i, 0))
    vec = pl.BlockSpec((1, E), lambda i: (0, 0))
    return pl.pallas_call(
        body, grid=(nt,),
        out_shape=[SDS((E, D), F32)] + [SDS((S, E), BF16)] * 2 + [SDS((1, E), F32)] * 2,
        in_specs=[pl.BlockSpec((tm, D), lambda i: (i, 0)), tile, pl.BlockSpec((tm, E), lambda i: (i, 2)),
                  _resident((1, E)), _resident((1, E)), _resident((E, D))],
        out_specs=[pl.BlockSpec((E, D), lambda i: (0, 0)), tile, tile, vec, vec],
        scratch_shapes=[pltpu.VMEM((8, E), F32), pltpu.VMEM((8, E), F32)],
        compiler_params=_cparams(("arbitrary",)), name=name)(dh, uc, proj, ln_w, ln_b, wout)


def _inproj_bwd(parts, offs, w, h, nw, dh_in, *, wt, name):
    S, D = h.shape
    tm = min(ROW_TILE, S)
    nt = S // tm
    widths = [p.shape[1] for p in parts]
    npart = len(parts)

    def body(*refs):
        p_refs = refs[:npart]
        w_ref, h_ref, nw_ref, dh_ref, o_ref, hn_ref, dnw_ref, anw = refs[npart:]
        i = pl.program_id(0)

        @pl.when(i == 0)
        def _():
            anw[...] = jnp.zeros_like(anw)

        x = h_ref[...]
        r = lax.rsqrt(jnp.mean(x * x, axis=-1, keepdims=True) + EPS)
        n = x * r
        hn_ref[...] = (n * nw_ref[...]).astype(BF16)
        dhn = jnp.zeros((tm, D), F32)
        for p_ref, off, wd in zip(p_refs, offs, widths):
            for c0, c1 in _col_chunks(wd, 1024):
                if wt:
                    dhn = dhn + jnp.dot(p_ref[:, c0:c1], w_ref[off + c0:off + c1, :], preferred_element_type=F32)
                else:
                    dhn = dhn + lax.dot_general(p_ref[:, c0:c1], w_ref[:, off + c0:off + c1],
                                                (((1,), (1,)), ((), ())), preferred_element_type=F32)
        anw[...] += _rowsum8(dhn * n)
        dn = dhn * nw_ref[...]
        o_ref[...] = dh_ref[...] + r * (dn - n * jnp.mean(dn * n, axis=-1, keepdims=True))

        @pl.when(i == nt - 1)
        def _():
            dnw_ref[...] = jnp.sum(anw[...], axis=0, keepdims=True)

    tile = pl.BlockSpec((tm, D), lambda i: (i, 0))
    return pl.pallas_call(
        body, grid=(nt,),
        out_shape=[SDS((S, D), F32), SDS((S, D), BF16), SDS((1, D), F32)],
        in_specs=[pl.BlockSpec((tm, wd), lambda i: (i, 0)) for wd in widths]
        + [_resident(w.shape), tile, _resident((1, D)), tile],
        out_specs=[tile, tile, pl.BlockSpec((1, D), lambda i: (0, 0))],
        scratch_shapes=[pltpu.VMEM((8, D), F32)],
        compiler_params=_cparams(("arbitrary",)), name=name)(*parts, w, h, nw, dh_in)


def _tn_matmul(a, b, *, transpose_out, name):
    S, Ka = a.shape
    N = b.shape[1]
    tm = min(GRAD_TILE, S)
    bn = min(1024, N)
    nt = S // tm

    def body(a_ref, b_ref, o_ref, acc):
        i = pl.program_id(1)

        @pl.when(i == 0)
        def _():
            acc[...] = jnp.zeros_like(acc)

        acc[...] += lax.dot_general(a_ref[...], b_ref[...].astype(BF16), (((0,), (0,)), ((), ())),
                                    preferred_element_type=F32)

        @pl.when(i == nt - 1)
        def _():
            o_ref[...] = acc[...].T if transpose_out else acc[...]

    if transpose_out:
        out_shape, out_spec = SDS((N, Ka), F32), pl.BlockSpec((bn, Ka), lambda j, i: (j, 0))
    else:
        out_shape, out_spec = SDS((Ka, N), F32), pl.BlockSpec((Ka, bn), lambda j, i: (0, j))
    return pl.pallas_call(
        body, grid=(N // bn, nt), out_shape=out_shape,
        in_specs=[pl.BlockSpec((tm, Ka), lambda j, i: (i, 0)), pl.BlockSpec((tm, bn), lambda j, i: (i, j))],
        out_specs=out_spec, scratch_shapes=[pltpu.VMEM((Ka, bn), F32)],
        compiler_params=_cparams(("parallel", "arbitrary")), name=name)(a, b)


def _loss_head(h, fnw, target, *, name):
    S, D = h.shape
    tm = min(ROW_TILE, S)
    nt = S // tm

    def body(h_ref, w_ref, t_ref, loss_ref, dh_ref, dw_ref, aw, al):
        i = pl.program_id(0)

        @pl.when(i == 0)
        def _():
            aw[...] = jnp.zeros_like(aw)
            al[...] = jnp.zeros_like(al)

        x = h_ref[...]
        r = lax.rsqrt(jnp.mean(x * x, axis=-1, keepdims=True) + EPS)
        n = x * r
        err = n * w_ref[...] - t_ref[...]
        al[...] += _rowsum8(err * err)
        dy = err * (1.0 / D)
        aw[...] += _rowsum8(dy * n)
        dn = dy * w_ref[...]
        dh_ref[...] = r * (dn - n * jnp.mean(dn * n, axis=-1, keepdims=True))

        @pl.when(i == nt - 1)
        def _():
            dw_ref[...] = jnp.sum(aw[...], axis=0, keepdims=True)
            tot = jnp.sum(jnp.sum(al[...], axis=0, keepdims=True), axis=1, keepdims=True) * (0.5 / D)
            loss_ref[...] = jnp.zeros((8, 128), F32) + tot

    tile = pl.BlockSpec((tm, D), lambda i: (i, 0))
    loss, dh, dw = pl.pallas_call(
        body, grid=(nt,), out_shape=[SDS((8, 128), F32), SDS((S, D), F32), SDS((1, D), F32)],
        in_specs=[tile, _resident((1, D)), tile],
        out_specs=[pl.BlockSpec((8, 128), lambda i: (0, 0)), tile, pl.BlockSpec((1, D), lambda i: (0, 0))],
        scratch_shapes=[pltpu.VMEM((8, D), F32), pltpu.VMEM((8, D), F32)],
        compiler_params=_cparams(("arbitrary",)), name=name)(h, fnw, target)
    return loss[0, 0], dh, dw


def _head_mask(shape, head_axis):
    hd = lax.broadcasted_iota(jnp.int32, shape, head_axis)
    ch = lax.broadcasted_iota(jnp.int32, shape, 1 - head_axis)
    return ((ch >= hd * HEADDIM) & (ch < hd * HEADDIM + HEADDIM)).astype(BF16)


def _split_dot(vals, mat, fine):
    R = vals[0].shape[0]
    parts, where = [], []
    for v, f in zip(vals, fine):
        hi = v.astype(BF16)
        where.append((len(parts), f))
        parts += [hi, (v - hi.astype(F32)).astype(BF16)] if f else [hi]
    out = jnp.dot(jnp.concatenate(parts, axis=0), mat, preferred_element_type=F32)
    return [out[R * k:R * k + R] + out[R * k + R:R * k + 2 * R] if f else out[R * k:R * k + R] for k, f in where]


def _nt(a, b):
    return lax.dot_general(a, b, (((1,), (1,)), ((), ())), preferred_element_type=F32)


def _tn(a, b):
    return lax.dot_general(a, b, (((0,), (0,)), ((), ())), preferred_element_type=F32)


def _pair_blockdiag(v):
    lane = lax.broadcasted_iota(jnp.int32, v.shape, 1)
    zero = jnp.zeros_like(v)
    return jnp.concatenate([jnp.where(lane < HEADDIM, v, zero), jnp.where(lane >= HEADDIM, v, zero)], axis=0)


def _ssd_setup(c_val, dtraw, bias, alog, fwd):
    xbc = _silu(c_val.astype(F32))
    pre = dtraw + bias
    dt = _softplus(pre)
    A = -jnp.exp(alog)
    row = lax.broadcasted_iota(jnp.int32, (CHUNK, CHUNK), 0)
    col = lax.broadcasted_iota(jnp.int32, (CHUNK, CHUNK), 1)
    T = (col <= row) if fwd else (col >= row)
    cum = jnp.dot(T.astype(F32), dt * A, precision=lax.Precision.HIGHEST, preferred_element_type=F32)
    total = cum[CHUNK - 1:CHUNK] if fwd else cum[0:1]
    return dict(x=xbc[:, :D_INNER], B=xbc[:, D_INNER:D_INNER + GN].astype(BF16),
                C=xbc[:, D_INNER + GN:].astype(BF16), pre=pre, dt=dt, A=A, T=T, Tt=(col >= row) if fwd else (col <= row),
                cum=cum, e=jnp.exp(cum), dtds=dt * jnp.exp(total - cum))


def _ssd_fwd_dir(c_val, dtraw, bias, alog, state, y_ref, hin_ref, fwd, ex):
    s = _ssd_setup(c_val, dtraw, bias, alog, fwd)
    x, cum, T = s["x"], s["cum"], s["T"]
    e_x, dtds_x = _split_dot([s["e"], s["dtds"]], ex, [True, False])
    dec_x = e_x[CHUNK - 1:CHUNK] if fwd else e_x[0:1]
    cumT, dtT = cum.T, s["dt"].T
    xb = x.astype(BF16)
    for g in range(SSD_GROUPS):
        g0, g1 = g * HPG * HEADDIM, (g + 1) * HPG * HEADDIM
        Bg, Cg = s["B"][:, g * D_STATE:(g + 1) * D_STATE], s["C"][:, g * D_STATE:(g + 1) * D_STATE]
        CBg = _nt(Cg, Bg)
        Hg = state[:, g0:g1]
        Hb = Hg.astype(BF16)
        hin_ref[0, :, g0:g1] = Hb
        yoff = jnp.dot(Cg, Hb, preferred_element_type=F32) * e_x[:, g0:g1]
        ys = []
        for h in range(g * HPG, (g + 1) * HPG, 2):
            Ms = []
            for hh in (h, h + 1):
                Lh = jnp.exp(jnp.where(T, cum[:, hh:hh + 1] - cumT[hh:hh + 1, :], NEG))
                Ms.append((CBg * Lh * dtT[hh:hh + 1, :]).astype(BF16))
            ys.append(jnp.dot(jnp.concatenate(Ms, axis=1), _pair_blockdiag(xb[:, h * HEADDIM:(h + 2) * HEADDIM]),
                              preferred_element_type=F32))
        y_ref[:, g0:g1] = jnp.concatenate(ys, axis=1) + yoff
        xds = (x[:, g0:g1] * dtds_x[:, g0:g1]).astype(BF16)
        state[:, g0:g1] = Hg * dec_x[:, g0:g1] + _tn(Bg, xds)


def _ssd_scan_fwd(cact, tail, bias, alog, *, name):
    S = cact.shape[0]
    nc = S // CHUNK
    E = D_INNER

    def body(cf_ref, cb_ref, tf_ref, tb_ref, bias_ref, alog_ref, yf_ref, yb_ref, hf_ref, hb_ref, state):
        @pl.when(pl.program_id(0) == 0)
        def _():
            state[...] = jnp.zeros_like(state)

        ex = _head_mask((SSD_HEADS, D_INNER), 0)
        H = SSD_HEADS
        _ssd_fwd_dir(cf_ref[...], tf_ref[:, 0:H], bias_ref[0:1, :], alog_ref[0:1, :], state.at[0], yf_ref, hf_ref, True, ex)
        _ssd_fwd_dir(cb_ref[...], tb_ref[:, H:2 * H], bias_ref[1:2, :], alog_ref[1:2, :], state.at[1], yb_ref, hb_ref, False, ex)

    up, down = (lambda i: (i, 0)), (lambda i: (nc - 1 - i, 0))
    up3, down3 = (lambda i: (i, 0, 0)), (lambda i: (nc - 1 - i, 0, 0))
    W = cact.shape[1]
    return pl.pallas_call(
        body, grid=(nc,),
        out_shape=[SDS((S, E), F32), SDS((S, E), F32), SDS((nc, D_STATE, E), BF16), SDS((nc, D_STATE, E), BF16)],
        in_specs=[pl.BlockSpec((CHUNK, W), up), pl.BlockSpec((CHUNK, W), down),
                  pl.BlockSpec((CHUNK, 128), up), pl.BlockSpec((CHUNK, 128), down),
                  _resident((2, SSD_HEADS)), _resident((2, SSD_HEADS))],
        out_specs=[pl.BlockSpec((CHUNK, E), up), pl.BlockSpec((CHUNK, E), down),
                   pl.BlockSpec((1, D_STATE, E), up3), pl.BlockSpec((1, D_STATE, E), down3)],
        scratch_shapes=[pltpu.VMEM((2, D_STATE, E), F32)],
        compiler_params=_cparams(("arbitrary",)), name=name)(cact, cact, tail, tail, bias, alog)


def _ssd_bwd_dir(c_val, dtraw, bias, alog, dy, hin_ref, dskip_x, G, dc_ref, ddt_ref, accA, accb, fwd, ex, ext):
    s = _ssd_setup(c_val, dtraw, bias, alog, fwd)
    x, cum, T, dt = s["x"], s["cum"], s["T"], s["dt"]
    e_x, dtds_x = _split_dot([s["e"], s["dtds"]], ex, [True, False])
    dec_x = e_x[CHUNK - 1:CHUNK] if fwd else e_x[0:1]
    cumT, dtT = cum.T, dt.T
    xb = x.astype(BF16)
    dyf = dy.astype(F32)
    dx_diag, dx_st, dBs, dCs, gh, yoff_dy = [], [], [], [], [], []
    lane_h = lax.broadcasted_iota(jnp.int32, (CHUNK, SSD_HEADS), 1)
    sub_h = lax.broadcasted_iota(jnp.int32, (SSD_HEADS, CHUNK), 0)
    rows = jnp.zeros((CHUNK, SSD_HEADS), F32)
    colsT = jnp.zeros((SSD_HEADS, CHUNK), F32)
    for g in range(SSD_GROUPS):
        g0, g1 = g * HPG * HEADDIM, (g + 1) * HPG * HEADDIM
        Bg, Cg = s["B"][:, g * D_STATE:(g + 1) * D_STATE], s["C"][:, g * D_STATE:(g + 1) * D_STATE]
        CBg = _nt(Cg, Bg)
        Hb = hin_ref[0, :, g0:g1]
        Gg = G[:, g0:g1]
        Gb = Gg.astype(BF16)
        edyf = dyf[:, g0:g1] * e_x[:, g0:g1]
        edy = edyf.astype(BF16)
        yoff_dy.append(edyf * jnp.dot(Cg, Hb, preferred_element_type=F32))
        dC_g = _nt(edy, Hb)
        dH_g = _tn(Cg, edy)
        xds = (x[:, g0:g1] * dtds_x[:, g0:g1]).astype(BF16)
        dB_g = _nt(xds, Gb)
        dx_st.append(dtds_x[:, g0:g1] * jnp.dot(Bg, Gb, preferred_element_type=F32))
        dCB = jnp.zeros((CHUNK, CHUNK), F32)
        for h in range(g * HPG, (g + 1) * HPG, 2):
            ps = slice(h * HEADDIM, (h + 2) * HEADDIM)
            dM2 = _nt(dy[:, ps], _pair_blockdiag(xb[:, ps]))
            Ms = []
            for j, hh in enumerate((h, h + 1)):
                Lh = jnp.exp(jnp.where(T, cum[:, hh:hh + 1] - cumT[hh:hh + 1, :], NEG))
                Ms.append((CBg * Lh * dtT[hh:hh + 1, :]).astype(BF16))
                term = dM2[:, j * CHUNK:(j + 1) * CHUNK] * dtT[hh:hh + 1, :] * Lh
                dCB = dCB + term
                Wh = term * CBg
                rows = rows + jnp.where(lane_h == hh, jnp.sum(Wh, axis=1, keepdims=True), 0.0)
                colsT = colsT + jnp.where(sub_h == hh, jnp.sum(Wh, axis=0, keepdims=True), 0.0)
            dx_diag.append(_tn(jnp.concatenate(Ms, axis=0), _pair_blockdiag(dy[:, ps])))
        dCBb = dCB.astype(BF16)
        dCs.append(dC_g + jnp.dot(dCBb, Bg, preferred_element_type=F32))
        dBs.append(dB_g + _tn(dCBb, Cg))
        gh.append(_rowsum8(Gg * Hb.astype(F32)))
        G[:, g0:g1] = Gg * dec_x[:, g0:g1] + dH_g
    dxn = jnp.concatenate(dx_diag, axis=1) + jnp.concatenate(dx_st, axis=1)
    xst = x * jnp.concatenate(dx_st, axis=1)
    RZ, Zx = _split_dot([jnp.concatenate(yoff_dy, axis=1) - xst, x * dxn], ext, [True, False])
    ddec, zs_tot = _split_dot([jnp.concatenate(gh, axis=1), _rowsum8(xst)], ext, [True, True])
    dec = s["e"][CHUNK - 1:CHUNK] if fwd else s["e"][0:1]
    dtotal = jnp.sum(zs_tot + dec * ddec, axis=0, keepdims=True)
    rowi = lax.broadcasted_iota(jnp.int32, (CHUNK, SSD_HEADS), 0)
    dacs = rows - colsT.T + RZ + jnp.where(rowi == (CHUNK - 1 if fwd else 0), dtotal, 0.0)
    da = jnp.dot(s["Tt"].astype(F32), dacs, precision=lax.Precision.HIGHEST, preferred_element_type=F32)
    accA[...] += _rowsum8(da * dt)
    sg = _sig(s["pre"])
    ratio = jnp.where(dt > 1e-30, sg / jnp.maximum(dt, 1e-30), 1.0)
    ddtraw = da * s["A"] * sg + Zx * ratio
    ddt_ref[...] = ddtraw
    accb[...] += _rowsum8(ddtraw)
    dx = dxn + dyf * dskip_x if fwd else dxn
    dact = jnp.concatenate([dx] + dBs + dCs, axis=1)
    dc_ref[...] = (dact * _dsilu(c_val.astype(F32))).astype(BF16)


def _ssd_scan_bwd(cact, tail, bias, alog, dy, hf, hb, dskip_x, *, name, carry=None):
    S = cact.shape[0]
    nc = S // CHUNK
    E, H, W = D_INNER, SSD_HEADS, cact.shape[1]

    def body(cf_ref, cb_ref, tf_ref, tb_ref, bias_ref, alog_ref, dyf_ref, dyb_ref, hf_ref, hb_ref,
             dsk_ref, dcf_ref, dcb_ref, ddtf_ref, ddtb_ref, dalog_ref, dbias_ref, G, accA, accb):
        i = pl.program_id(0)

        @pl.when(i == 0)
        def _():
            G[...] = jnp.zeros_like(G)
            accA[...] = jnp.zeros_like(accA)
            accb[...] = jnp.zeros_like(accb)

        ex = _head_mask((H, E), 0)
        ext = _head_mask((E, H), 1)
        _ssd_bwd_dir(cf_ref[...], tf_ref[:, 0:H], bias_ref[0:1, :], alog_ref[0:1, :], dyf_ref[...], hf_ref,
                     dsk_ref[...], G.at[0], dcf_ref, ddtf_ref, accA.at[0], accb.at[0], True, ex, ext)
        _ssd_bwd_dir(cb_ref[...], tb_ref[:, H:2 * H], bias_ref[1:2, :], alog_ref[1:2, :], dyb_ref[...], hb_ref,
                     dsk_ref[...], G.at[1], dcb_ref, ddtb_ref, accA.at[1], accb.at[1], False, ex, ext)

        @pl.when(i == nc - 1)
        def _():
            for d in range(2):
                A = -jnp.exp(alog_ref[d:d + 1, :])
                dalog_ref[d:d + 1, :] = jnp.sum(accA[d], axis=0, keepdims=True) * A
                dbias_ref[d:d + 1, :] = jnp.sum(accb[d], axis=0, keepdims=True)

    up, down = (lambda i: (i, 0)), (lambda i: (nc - 1 - i, 0))
    up3, down3 = (lambda i: (i, 0, 0)), (lambda i: (nc - 1 - i, 0, 0))
    small = pl.BlockSpec((2, H), lambda i: (0, 0))
    return _call(
        body, grid=(nc,),
        out_shape=[SDS((S, W), BF16), SDS((S, W), BF16), SDS((S, H), F32), SDS((S, H), F32),
                   SDS((2, H), F32), SDS((2, H), F32)],
        in_specs=[pl.BlockSpec((CHUNK, W), down), pl.BlockSpec((CHUNK, W), up),
                  pl.BlockSpec((CHUNK, 128), down), pl.BlockSpec((CHUNK, 128), up),
                  _resident((2, H)), _resident((2, H)),
                  pl.BlockSpec((CHUNK, E), down), pl.BlockSpec((CHUNK, E), up),
                  pl.BlockSpec((1, D_STATE, E), down3), pl.BlockSpec((1, D_STATE, E), up3),
                  _resident((1, E))],
        out_specs=[pl.BlockSpec((CHUNK, W), down), pl.BlockSpec((CHUNK, W), up),
                   pl.BlockSpec((CHUNK, H), down), pl.BlockSpec((CHUNK, H), up), small, small],
        scratch_shapes=[pltpu.VMEM((2, D_STATE, E), F32), pltpu.VMEM((2, 8, H), F32), pltpu.VMEM((2, 8, H), F32)],
        sem=("arbitrary",), name=name, args=(cact, cact, tail, tail, bias, alog, dy, dy, hf, hb, dskip_x), carry=carry)


def _ssd_gate(yf, yb, cx, z, dsk, nw):
    xs = _silu(cx.astype(F32))
    y = yf + yb + xs * dsk
    sz = _silu(z)
    gt = y * sz
    r = lax.rsqrt(jnp.mean(gt * gt, axis=-1, keepdims=True) + EPS)
    return xs, y, sz, r, gt * r


def _ssd_out_fwd(yf, yb, cact, proj, dskip_x, nw, h, wout, *, name):
    S, E = yf.shape
    D = h.shape[1]
    tm = min(GATE_TILE, S)

    def body(yf_ref, yb_ref, cx_ref, z_ref, dsk_ref, nw_ref, h_ref, w_ref, o_ref):
        _, _, _, _, n = _ssd_gate(yf_ref[...], yb_ref[...], cx_ref[...], z_ref[...].astype(F32), dsk_ref[...], nw_ref[...])
        o_ref[...] = h_ref[...] + jnp.dot((n * nw_ref[...]).astype(BF16), w_ref[...], preferred_element_type=F32)

    tile = pl.BlockSpec((tm, E), lambda i: (i, 0))
    return pl.pallas_call(
        body, grid=(S // tm,), out_shape=SDS((S, D), F32),
        in_specs=[tile, tile, tile, tile, _resident((1, E)), _resident((1, E)),
                  pl.BlockSpec((tm, D), lambda i: (i, 0)), _resident((E, D))],
        out_specs=pl.BlockSpec((tm, D), lambda i: (i, 0)),
        compiler_params=_cparams(("parallel",)), name=name)(yf, yb, cact, proj, dskip_x, nw, h, wout)


def _ssd_out_bwd(dh, yf, yb, cact, proj, dskip_x, nw, wout, *, name):
    S, E = yf.shape
    D = dh.shape[1]
    tm = min(GATE_TILE, S)
    nt = S // tm

    def body(dh_ref, yf_ref, yb_ref, cx_ref, z_ref, dsk_ref, nw_ref, w_ref, dw_ref, dy_ref, dz_ref, dnw_ref, dd_ref, anw, ad):
        i = pl.program_id(0)

        @pl.when(i == 0)
        def _():
            anw[...] = jnp.zeros_like(anw)
            ad[...] = jnp.zeros_like(ad)
            dw_ref[...] = jnp.zeros_like(dw_ref)

        z = z_ref[...].astype(F32)
        xs, y, sz, r, n = _ssd_gate(yf_ref[...], yb_ref[...], cx_ref[...], z, dsk_ref[...], nw_ref[...])
        dhb = dh_ref[...].astype(BF16)
        dw_ref[...] += _tn((n * nw_ref[...]).astype(BF16), dhb)
        dyn = _nt(dhb, w_ref[...])
        anw[...] += _rowsum8(dyn * n)
        dn = dyn * nw_ref[...]
        dgt = r * (dn - n * jnp.mean(dn * n, axis=-1, keepdims=True))
        dy = dgt * sz
        dy_ref[...] = dy.astype(BF16)
        dz_ref[...] = (dgt * y * _dsilu(z)).astype(BF16)
        ad[...] += _rowsum8(dy * xs)

        @pl.when(i == nt - 1)
        def _():
            dnw_ref[...] = jnp.sum(anw[...], axis=0, keepdims=True)
            dd_ref[...] = jnp.sum(ad[...], axis=0, keepdims=True)

    tile = pl.BlockSpec((tm, E), lambda i: (i, 0))
    vec = pl.BlockSpec((1, E), lambda i: (0, 0))
    return pl.pallas_call(
        body, grid=(nt,), out_shape=[SDS((E, D), F32)] + [SDS((S, E), BF16)] * 2 + [SDS((1, E), F32)] * 2,
        in_specs=[pl.BlockSpec((tm, D), lambda i: (i, 0)), tile, tile, tile, tile,
                  _resident((1, E)), _resident((1, E)), _resident((E, D))],
        out_specs=[pl.BlockSpec((E, D), lambda i: (0, 0)), tile, tile, vec, vec],
        scratch_shapes=[pltpu.VMEM((8, E), F32), pltpu.VMEM((8, E), F32)],
        compiler_params=_cparams(("arbitrary",)), name=name)(dh, yf, yb, cact, proj, dskip_x, nw, wout)


def _local_step(x, target, p, hooks):
    E, S = D_INNER, x.shape[0]
    depth = p["norm_w"].shape[0]
    names = ["norm_w", "cm_w_in", "cm_dw_w", "cm_dw_b", "cm_ln_w", "cm_ln_b", "cm_w_out", "ssd_w_in", "ssd_conv_w",
             "ssd_conv_b", "ssd_dt_bias", "ssd_A_log", "ssd_D", "ssd_norm_w", "ssd_w_out"]
    g = {n: [None] * (depth if n == "norm_w" else depth // 2) for n in names}

    def carried(key):
        items, done = hooks.get(key, (None, None))
        return (items(g) if items else None), (done if done else lambda got: None)

    h, saved = x, []
    for i in range(depth):
        j, nw = i // 2, p["norm_w"][i][None]
        if i % 2 == 0:
            carry, done = carried(("cm_in_fwd", i))
            proj, got = _inproj_fwd(h, nw, p["cm_w_in"][j], tail=False, wt=False, name="cm_in_fwd", carry=carry)
            done(got)
            carry, done = carried(("cm_conv_fwd", i))
            uc, got = _dwconv_fwd(proj, p["cm_dw_w"][j], p["cm_dw_b"][j][None], col0=0, glu_col0=E, C=E,
                                  name="cm_conv_fwd", carry=carry)
            done(got)
            hn = _conf_out_fwd(uc, proj, h, p["cm_ln_w"][j][None], p["cm_ln_b"][j][None], p["cm_w_out"][j], name="cm_out_fwd")
            saved.append((h, proj, uc))
        else:
            (proj, tail), _ = _inproj_fwd(h, nw, p["ssd_w_in"][j], tail=True, wt=True, name="ssd_in_fwd")
            cact, _ = _dwconv_fwd(proj, p["ssd_conv_w"][j], p["ssd_conv_b"][j][None], col0=E, glu_col0=None,
                                  C=SSD_CONV_DIM, name="ssd_conv_fwd")
            yf, yb, hf, hb = _ssd_scan_fwd(cact, tail, p["ssd_dt_bias"][j], p["ssd_A_log"][j], name="ssd_scan_fwd")
            dsk = jnp.repeat(p["ssd_D"][j], HEADDIM)[None]
            hn = _ssd_out_fwd(yf, yb, cact, proj, dsk, p["ssd_norm_w"][j][None], h, p["ssd_w_out"][j], name="ssd_out_fwd")
            saved.append((h, proj, tail, cact, yf, yb, hf, hb, dsk))
        h = hn
    loss, dh, g_fnw = _loss_head(h, p["final_norm_w"][None], target, name="loss_head")

    for i in reversed(range(depth)):
        j, nw = i // 2, p["norm_w"][i][None]
        if i % 2 == 0:
            h_in, proj, uc = saved[i]
            g["cm_w_out"][j], duc, dz, g_lw, g_lb = _conf_out_bwd(
                dh, uc, proj, p["cm_ln_w"][j][None], p["cm_ln_b"][j][None], p["cm_w_out"][j], name="cm_out_bwd")
            carry, done = carried(("cm_conv_bwd", i))
            (dv, dg, g_dw, g_db), got = _dwconv_bwd([duc], proj, p["cm_dw_w"][j], col0=0, glu_col0=E, C=E,
                                                    name="cm_conv_bwd", carry=carry)
            done(got)
            parts = [dv, dg, dz]
            dh, hnb, g_nw = _inproj_bwd(parts, [0, E, 2 * E], p["cm_w_in"][j], h_in, nw, dh, wt=False, name="cm_in_bwd")
            g["cm_w_in"][j] = jnp.concatenate(
                [_tn_matmul(hnb, q, transpose_out=False, name="cm_win_grad") for q in parts], axis=1)
            g["cm_dw_w"][j], g["cm_dw_b"][j], g["cm_ln_w"][j], g["cm_ln_b"][j] = g_dw, g_db[0], g_lw[0], g_lb[0]
        else:
            h_in, proj, tail, cact, yf, yb, hf, hb, dsk = saved[i]
            g["ssd_w_out"][j], dy, dz, g_sn, g_dx = _ssd_out_bwd(
                dh, yf, yb, cact, proj, dsk, p["ssd_norm_w"][j][None], p["ssd_w_out"][j], name="ssd_out_bwd")
            carry, done = carried(("ssd_scan_bwd", i))
            (dcf, dcb, ddtf, ddtb, g_al, g_bias), got = _ssd_scan_bwd(
                cact, tail, p["ssd_dt_bias"][j], p["ssd_A_log"][j], dy, hf, hb, dsk, name="ssd_scan_bwd", carry=carry)
            done(got)
            (dxbc, g_cw, g_cb), _ = _dwconv_bwd([dcf, dcb], proj, p["ssd_conv_w"][j], col0=E, glu_col0=None,
                                                C=SSD_CONV_DIM, name="ssd_conv_bwd")
            ddt = jnp.concatenate([ddtf, ddtb, jnp.zeros((S, 128 - 2 * SSD_HEADS), F32)], axis=1).astype(BF16)
            parts = [dz, dxbc, ddt]
            dh, hnb, g_nw = _inproj_bwd(parts, [0, E, E + SSD_CONV_DIM], p["ssd_w_in"][j], h_in, nw, dh, wt=True,
                                        name="ssd_in_bwd")
            gw = [_tn_matmul(hnb, q, transpose_out=True, name="ssd_win_grad") for q in parts]
            g["ssd_w_in"][j] = jnp.concatenate([gw[0], gw[1], gw[2][:2 * SSD_HEADS]], axis=0)
            g["ssd_conv_w"][j], g["ssd_conv_b"][j], g["ssd_norm_w"][j] = g_cw, g_cb[0], g_sn[0]
            g["ssd_dt_bias"][j], g["ssd_A_log"][j] = g_bias, g_al
            g["ssd_D"][j] = jnp.sum(g_dx.reshape(SSD_HEADS, HEADDIM), axis=1)
        g["norm_w"][i] = g_nw[0]
    g["final_norm_w"] = g_fnw[0]
    return loss, dh, g


_FLIPS = [(fx, fy, fc) for fx in (0, 1) for fy in (0, 1) for fc in (0, 1)][1:]


def _exchange_copies(items, ins, outs, send_sems, recv_sems, local_sems, arrivals):
    x, y, c = lax.axis_index("x"), lax.axis_index("y"), lax.axis_index("c")
    me = 4 * x + 2 * y + c
    local, pairs = [], []
    for b, (_, scatter) in enumerate(items):
        local.append(pltpu.make_async_copy(ins[b].at[me] if scatter else ins[b], outs[b].at[me], local_sems.at[b]))
        for k, (fx, fy, fc) in enumerate(_FLIPS):
            px, py, pc = x ^ fx, y ^ fy, c ^ fc
            peer = 4 * px + 2 * py + pc
            src = ins[b].at[peer] if scatter else ins[b]
            sems = dict(send_sem=send_sems.at[b * 7 + k], recv_sem=recv_sems.at[b * 7 + k],
                        device_id=(px, py, pc), device_id_type=pl.DeviceIdType.MESH)
            pairs.append((pltpu.make_async_remote_copy(src_ref=src, dst_ref=outs[b].at[me], **sems),
                          pltpu.make_async_remote_copy(src_ref=src, dst_ref=outs[b].at[peer], **sems)
                          if arrivals else None))
    return local, pairs


def _exchange_start(items, ins, outs, *sems):
    local, pairs = _exchange_copies(items, ins, outs, *sems, arrivals=False)
    for cp in local:
        cp.start()
    for cp, _ in pairs:
        cp.start()


def _exchange_wait(items, ins, outs, *sems):
    local, pairs = _exchange_copies(items, ins, outs, *sems, arrivals=True)
    for _, arrival in pairs:
        arrival.wait_recv()
    for cp, _ in pairs:
        cp.wait_send()
    for cp in local:
        cp.wait()


def _exchange_shapes(items):
    n = len(items)
    out_shape = [SDS((N_DEV,) + tuple(a.shape[-2:]), a.dtype) for a, _ in items]
    sems = [pltpu.SemaphoreType.DMA((7 * n,)), pltpu.SemaphoreType.DMA((7 * n,)), pltpu.SemaphoreType.DMA((n,))]
    return out_shape, sems


def _exchange(items, *, name):
    n = len(items)

    def body(*refs):
        _exchange_start(items, refs[:n], refs[n:2 * n], *refs[2 * n:])
        _exchange_wait(items, refs[:n], refs[n:2 * n], *refs[2 * n:])

    hbm = pl.BlockSpec(memory_space=pltpu.HBM)
    out_shape, sems = _exchange_shapes(items)
    return pl.pallas_call(body, out_shape=out_shape, in_specs=[hbm] * n, out_specs=[hbm] * n,
                          scratch_shapes=sems, name=name)(*[a for a, _ in items])


def _call(body, *, grid, out_shape, in_specs, out_specs, scratch_shapes, sem, name, args, carry):
    if not carry:
        return pl.pallas_call(body, grid=grid, out_shape=out_shape, in_specs=in_specs, out_specs=out_specs,
                              scratch_shapes=scratch_shapes, compiler_params=_cparams(sem), name=name)(*args), None
    n, n_in, n_out, n_scr = len(carry), len(in_specs), len(out_specs), len(scratch_shapes)
    last_step = tuple(g - 1 for g in grid)

    def wrapped(*refs):
        ins, cin = refs[:n_in], refs[n_in:n_in + n]
        outs, cout = refs[n_in + n:n_in + n + n_out], refs[n_in + n + n_out:n_in + 2 * n + n_out]
        scr, sems = refs[n_in + 2 * n + n_out:n_in + 2 * n + n_out + n_scr], refs[n_in + 2 * n + n_out + n_scr:]
        first = last = None
        for d, top in enumerate(last_step):
            i = pl.program_id(d)
            first = (i == 0) if first is None else first & (i == 0)
            last = (i == top) if last is None else last & (i == top)

        @pl.when(first)
        def _():
            _exchange_start(carry, cin, cout, *sems)

        body(*ins, *outs, *scr)

        @pl.when(last)
        def _():
            _exchange_wait(carry, cin, cout, *sems)

    hbm = pl.BlockSpec(memory_space=pltpu.HBM)
    x_shape, x_sems = _exchange_shapes(carry)
    res = pl.pallas_call(
        wrapped, grid=grid, out_shape=list(out_shape) + x_shape, in_specs=list(in_specs) + [hbm] * n,
        out_specs=list(out_specs) + [hbm] * n, scratch_shapes=list(scratch_shapes) + x_sems,
        compiler_params=_cparams(("arbitrary",) * len(grid)), name=name)(*args, *[a for a, _ in carry])
    return res[:n_out], res[n_out:]


def _reduce_slots(recv, *, rows, name):
    n, R, C = recv.shape

    def body(r_ref, g_ref):
        g = r_ref[0].astype(F32)
        for s in range(1, n):
            g = g + r_ref[s].astype(F32)
        g_ref[...] = g

    return pl.pallas_call(
        body, grid=(R // rows,), out_shape=SDS((R, C), F32),
        in_specs=[pl.BlockSpec((n, rows, C), lambda i: (0, i, 0))],
        out_specs=pl.BlockSpec((rows, C), lambda i: (i, 0)),
        compiler_params=_cparams(("parallel",)), name=name)(recv)


def _reduce_adamw(recv, w, m, v, *, layer, rows, name):
    _, R, C = w.shape
    n = recv.shape[0]
    c1 = 1.0 / (1.0 - ADAM_B1 ** ADAM_STEP)
    c2 = 1.0 / (1.0 - ADAM_B2 ** ADAM_STEP)

    def body(r_ref, w_ref, m_ref, v_ref, g_ref, d_ref, mo_ref, vo_ref):
        g = r_ref[0].astype(F32)
        for s in range(1, n):
            g = g + r_ref[s].astype(F32)
        mn = ADAM_B1 * m_ref[...] + (1.0 - ADAM_B1) * g
        vn = ADAM_B2 * v_ref[...] + (1.0 - ADAM_B2) * (g * g)
        g_ref[...] = g
        mo_ref[...] = mn
        vo_ref[...] = vn
        d_ref[...] = -ADAM_LR * ((mn * c1) / (jnp.sqrt(vn * c2) + ADAM_EPS) + ADAM_WD * w_ref[...])

    tile = pl.BlockSpec((rows, C), lambda i: (i, 0))
    mine = pl.BlockSpec((None, rows, C), lambda i: (layer, i, 0))
    return pl.pallas_call(
        body, grid=(R // rows,), out_shape=[SDS((R, C), F32)] * 4,
        in_specs=[pl.BlockSpec((n, rows, C), lambda i: (0, i, 0)), mine, mine, mine],
        out_specs=[tile] * 4, compiler_params=_cparams(("parallel",)), name=name)(recv, w, m, v)


_ORDER = ["norm_w", "final_norm_w", "cm_w_in", "cm_dw_w", "cm_dw_b", "cm_ln_w", "cm_ln_b", "cm_w_out", "ssd_w_in",
          "ssd_conv_w", "ssd_conv_b", "ssd_dt_bias", "ssd_A_log", "ssd_D", "ssd_norm_w", "ssd_w_out"]
_SMALL = (["cm_dw_w", "ssd_conv_w", "ssd_conv_b", "ssd_norm_w"], 168)
_REP = (["norm_w", "final_norm_w", "cm_dw_b", "cm_ln_w", "cm_ln_b", "ssd_dt_bias", "ssd_A_log", "ssd_D"], 144)
_SHARD_AXIS = {"cm_w_in": 2, "cm_dw_w": 2, "cm_w_out": 1, "ssd_w_in": 2, "ssd_conv_w": 2, "ssd_conv_b": 1,
               "ssd_norm_w": 1, "ssd_w_out": 1}
OPT_ROWS = 256


def _pack(arrs, rows, lanes, dtype, lead=()):
    nl = len(lead)
    flat = jnp.concatenate([a.reshape(lead + (-1,)).astype(dtype) for a in arrs], axis=nl)
    flat = jnp.pad(flat, [(0, 0)] * nl + [(0, rows * lanes - flat.shape[nl])])
    return flat.reshape(lead + (rows, lanes))


def _unpack(buf, shapes, lead=()):
    nl = len(lead)
    flat = buf.reshape(lead + (-1,))
    out, off = [], 0
    for s in shapes:
        sz = math.prod(s)
        out.append(flat[..., off:off + sz].reshape(lead + tuple(s)))
        off += sz
    return out


def _join_shards(parts, axis):
    full = jnp.moveaxis(parts, 0, axis)
    sh = full.shape
    return full.reshape(sh[:axis] + (sh[axis] * sh[axis + 1],) + sh[axis + 2:])


def _split_shards(full, axis):
    sh = full.shape
    parts = full.reshape(sh[:axis] + (N_DEV, sh[axis] // N_DEV) + sh[axis + 1:])
    return jnp.moveaxis(parts, axis, 0)


def kernel(x, norm_w, final_norm_w, cm_w_in, cm_dw_w, cm_dw_b, cm_ln_w, cm_ln_b, cm_w_out, ssd_w_in, ssd_conv_w, ssd_conv_b, ssd_dt_bias, ssd_A_log, ssd_D, ssd_norm_w, ssd_w_out, loss_target, m_norm_w, m_final_norm_w, m_cm_w_in, m_cm_dw_w, m_cm_dw_b, m_cm_ln_w, m_cm_ln_b, m_cm_w_out, m_ssd_w_in, m_ssd_conv_w, m_ssd_conv_b, m_ssd_dt_bias, m_ssd_A_log, m_ssd_D, m_ssd_norm_w, m_ssd_w_out, v_norm_w, v_final_norm_w, v_cm_w_in, v_cm_dw_w, v_cm_dw_b, v_cm_ln_w, v_cm_ln_b, v_cm_w_out, v_ssd_w_in, v_ssd_conv_w, v_ssd_conv_b, v_ssd_dt_bias, v_ssd_A_log, v_ssd_D, v_ssd_norm_w, v_ssd_w_out):
    w = dict(zip(_ORDER, (norm_w, final_norm_w, cm_w_in, cm_dw_w, cm_dw_b, cm_ln_w, cm_ln_b, cm_w_out, ssd_w_in,
                          ssd_conv_w, ssd_conv_b, ssd_dt_bias, ssd_A_log, ssd_D, ssd_norm_w, ssd_w_out)))
    m = dict(zip(_ORDER, (m_norm_w, m_final_norm_w, m_cm_w_in, m_cm_dw_w, m_cm_dw_b, m_cm_ln_w, m_cm_ln_b, m_cm_w_out,
                          m_ssd_w_in, m_ssd_conv_w, m_ssd_conv_b, m_ssd_dt_bias, m_ssd_A_log, m_ssd_D, m_ssd_norm_w,
                          m_ssd_w_out)))
    v = dict(zip(_ORDER, (v_norm_w, v_final_norm_w, v_cm_w_in, v_cm_dw_w, v_cm_dw_b, v_cm_ln_w, v_cm_ln_b, v_cm_w_out,
                          v_ssd_w_in, v_ssd_conv_w, v_ssd_conv_b, v_ssd_dt_bias, v_ssd_A_log, v_ssd_D, v_ssd_norm_w,
                          v_ssd_w_out)))

    D, nl = D_MODEL, cm_w_in.shape[0]
    (sn, sr), (rn, rr) = _SMALL, _REP
    mats = ["cm_w_in", "cm_w_out", "ssd_w_in", "ssd_w_out"]
    bf = lambda a: a.astype(BF16)
    wt_in = jnp.swapaxes(ssd_w_in, 1, 2)
    t_rows = wt_in.shape[1]
    t_pad = -t_rows % 16

    p = {k: w[k] for k in rn}
    p.update({k: [None] * nl for k in mats})
    first = _exchange([(bf(cm_w_in[0]), False), (_pack([w[k] for k in sn], sr, 128, F32), False)], name="gather_first")
    p["cm_w_in"][0] = _join_shards(first[0], 1)
    for k, parts in zip(sn, _unpack(first[1], [w[k].shape for k in sn], lead=(N_DEV,))):
        p[k] = _join_shards(parts, _SHARD_AXIS[k])

    def rest_in(got):
        p["cm_w_out"][0] = got[0].reshape(-1, D)

    def rest_conv(got):
        p["cm_w_in"][1] = _join_shards(got[0], 1)
        p["cm_w_out"][1] = got[1].reshape(-1, D)
        wt = _join_shards(got[2].astype(F32).reshape(N_DEV, nl, -1, D), 1)
        wt = bf(jnp.pad(wt, ((0, 0), (0, SSD_IN_PAD - SSD_IN), (0, 0))))
        wo = got[3].reshape(N_DEV, nl, -1, D)
        for j in range(nl):
            p["ssd_w_in"][j] = wt[j]
            p["ssd_w_out"][j] = wo[:, j].reshape(-1, D)

    hooks = {("cm_in_fwd", 0): (lambda g: [(bf(cm_w_out[0]), False)], rest_in),
             ("cm_conv_fwd", 0): (lambda g: [(bf(cm_w_in[1]), False), (bf(cm_w_out[1]), False),
                                             (bf(wt_in.reshape(-1, D)), False), (bf(ssd_w_out.reshape(-1, D)), False)],
                                  rest_conv)}

    recv = {}

    def blocks(name, j, g):
        if name == "cm_w_in":
            return bf(_split_shards(g[name][j], 1))
        b = _split_shards(g[name][j], 0)
        if name == "ssd_w_in":
            b = jnp.pad(b, ((0, 0), (0, t_pad), (0, 0)))
        return bf(b)

    def sender(keys):
        def items(g):
            return [(blocks(n, j, g), True) for n, j in keys]

        def done(got):
            recv.update(zip(keys, got))

        return items, done

    hooks["ssd_scan_bwd", 1] = sender([(n, 1) for n in mats])
    hooks["cm_conv_bwd", 0] = sender([("ssd_w_in", 0), ("ssd_w_out", 0), ("cm_w_out", 0)])

    loss, dx, g = _local_step(x[0], loss_target[0], p, hooks)
    loss = lax.psum(loss, ("x", "y", "c"))

    last_keys = [("cm_w_in", 0)]
    stacked = {k: jnp.stack(g[k]) for k in sn + rn if k != "final_norm_w"}
    stacked["final_norm_w"] = g["final_norm_w"]
    last = _exchange([(blocks(n, j, g), True) for n, j in last_keys]
                     + [(_pack([_split_shards(stacked[k], _SHARD_AXIS[k]) for k in sn], sr, 128, F32, lead=(N_DEV,)), True),
                        (_pack([stacked[k] for k in rn], rr, 128, F32), False)], name="exchange_last")
    recv.update(zip(last_keys, last))

    outs = {}
    for name in mats:
        shp = w[name].shape
        w3, m3, v3 = [d[name].reshape(nl, -1, shp[-1]) for d in (w, m, v)]
        per_layer = []
        for j in range(nl):
            r = recv[name, j]
            if name == "ssd_w_in":
                gt = _reduce_slots(r, rows=r.shape[1], name="sum_ssd_w_in")
                r = jnp.swapaxes(gt[:t_rows], 0, 1)[None]
            per_layer.append(_reduce_adamw(r, w3, m3, v3, layer=j, rows=min(OPT_ROWS, r.shape[1]), name="adamw_" + name))
        for kind, parts in zip(("g", "d", "m", "v"), zip(*per_layer)):
            outs[kind, name] = jnp.stack(parts).reshape(shp)
    for (names, rows), r, nm in ((_SMALL, last[-2], "adamw_small"), (_REP, last[-1], "adamw_rep")):
        res = _reduce_adamw(r, *[_pack([d[k] for k in names], rows, 128, F32)[None] for d in (w, m, v)], layer=0,
                            rows=rows, name=nm)
        for kind, buf in zip(("g", "d", "m", "v"), res):
            for k, a in zip(names, _unpack(buf, [w[k].shape for k in names])):
                outs[kind, k] = a
    return (loss, dx[None], *[outs[kind, k] for kind in ("g", "d", "m", "v") for k in _ORDER])
```

```python
import functools
import math

import jax
import jax.numpy as jnp
from jax import lax
from jax.experimental import pallas as pl
from jax.experimental.pallas import tpu as pltpu

F32, BF16 = jnp.float32, jnp.bfloat16
SDS = jax.ShapeDtypeStruct

D_MODEL = 1024
D_INNER = 2048
CONF_KERNEL = 31
HEADDIM = 64
SSD_HEADS = 32
SSD_GROUPS = 4
HPG = SSD_HEADS // SSD_GROUPS
D_STATE = 128
SSD_CONV = 5
CHUNK = 128
GN = SSD_GROUPS * D_STATE
SSD_CONV_DIM = D_INNER + 2 * GN
SSD_IN = D_INNER + SSD_CONV_DIM + 2 * SSD_HEADS
SSD_IN_PAD = SSD_IN + 64
EPS = 1e-5
N_DEV = 8

ADAM_LR, ADAM_B1, ADAM_B2, ADAM_EPS, ADAM_WD, ADAM_STEP = 0.001, 0.9, 0.999, 1e-08, 0.01, 10

VMEM_LIMIT = 56 * 1024 * 1024
ROW_TILE = 512
GATE_TILE = 256
GRAD_TILE = 1024
CONV_ROWS = 512
CONV_LANES = 512
CONV_HALO = 16
CONV_RB = 32
CONV_SUB_LANES = 256
CONV_LANES_FEW = 1024
CONV_RB_FEW = 16
CONV_DIRECT_TAPS = 8
NEG = -1e30


def _cparams(sem):
    return pltpu.CompilerParams(dimension_semantics=sem, vmem_limit_bytes=VMEM_LIMIT)


def _resident(shape):
    nd = len(shape)
    return pl.BlockSpec(shape, lambda *_: (0,) * nd, pipeline_mode=pl.Buffered(1))


def _sig(x):
    return 1.0 / (1.0 + jnp.exp(-x))


def _silu(x):
    return x * _sig(x)


def _dsilu(x):
    s = _sig(x)
    return s * (1.0 + x * (1.0 - s))


def _softplus(x):
    return jnp.maximum(x, 0.0) + jnp.log(1.0 + jnp.exp(-jnp.abs(x)))


def _col_chunks(n, width):
    out, c = [], 0
    while c < n:
        out.append((c, min(c + width, n)))
        c += width
    return out


def _rowsum8(v):
    acc = v[0:8]
    for j in range(1, v.shape[0] // 8):
        acc = acc + v[8 * j:8 * j + 8]
    return acc


def _inproj_fwd(h, nw, w, *, tail, wt, name, carry=None):
    S, D = h.shape
    N = w.shape[0] if wt else w.shape[1]
    tm = min(ROW_TILE, S)
    chunks = _col_chunks(N, 1024)

    def body(h_ref, nw_ref, w_ref, o_ref, *rest):
        x = h_ref[...]
        r = lax.rsqrt(jnp.mean(x * x, axis=-1, keepdims=True) + EPS)
        hn = (x * r * nw_ref[...]).astype(BF16)
        for c0, c1 in chunks:
            if wt:
                acc = lax.dot_general(hn, w_ref[c0:c1, :], (((1,), (1,)), ((), ())), preferred_element_type=F32)
            else:
                acc = jnp.dot(hn, w_ref[:, c0:c1], preferred_element_type=F32)
            o_ref[:, c0:c1] = acc.astype(BF16)
            if tail and c1 == N:
                rest[0][...] = acc[:, acc.shape[1] - 128:]

    out_shape = [SDS((S, N), BF16)]
    out_specs = [pl.BlockSpec((tm, N), lambda i: (i, 0))]
    if tail:
        out_shape.append(SDS((S, 128), F32))
        out_specs.append(pl.BlockSpec((tm, 128), lambda i: (i, 0)))
    res, got = _call(
        body, grid=(S // tm,), out_shape=out_shape,
        in_specs=[pl.BlockSpec((tm, D), lambda i: (i, 0)), _resident((1, D)), _resident(w.shape)],
        out_specs=out_specs, scratch_shapes=[], sem=("parallel",), name=name, args=(h, nw, w), carry=carry)
    return (res if tail else res[0]), got


def _tap_plan(offsets):
    rs = sorted({s % 8 for s in offsets})
    return rs, [(k, rs.index(s % 8), s // 8) for k, s in enumerate(offsets)]


def _halo_specs(S, rows, lanes, col0):
    per, last = rows // CONV_HALO, S // CONV_HALO - 1
    cb = col0 // lanes
    return [pl.BlockSpec((rows, lanes), lambda j, i: (i, cb + j)),
            pl.BlockSpec((CONV_HALO, lanes), lambda j, i: (jnp.maximum(i * per - 1, 0), cb + j)),
            pl.BlockSpec((CONV_HALO, lanes), lambda j, i: (jnp.minimum((i + 1) * per, last), cb + j))]


def _fill_window(win, prev, main, nxt, i, nt, rows):
    hb = CONV_HALO
    win[0:hb, :] = jnp.where(i > 0, prev, 0.0)
    win[hb:hb + rows, :] = main
    win[hb + rows:hb + rows + hb, :] = jnp.where(i < nt - 1, nxt, 0.0)


def _fill_shifts(win, sh, rs, n_u):
    for idx, r in enumerate(rs):
        sh[idx] = win[pl.ds(r, n_u), :]


def _tap_rows(win, sh, rs, ri, q, r0, n, direct, ls):
    if direct:
        return win[pl.ds(r0 + 8 * q + rs[ri], n), ls]
    return sh[ri, pl.ds(pl.multiple_of(r0 + 8 * q, 8), n), ls]


def _lane_slices(lanes, direct):
    sub = lanes if direct else min(CONV_SUB_LANES, lanes)
    return [pl.ds(l0, sub) for l0 in range(0, lanes, sub)]


def _tap_groups(taps):
    by_copy = {}
    for k, ri, q in taps:
        by_copy.setdefault(ri, []).append((k, q))
    out = []
    for ri, members in sorted(by_copy.items()):
        q0 = min(q for _, q in members)
        out.append((ri, q0, max(q for _, q in members) - q0, [(k, q - q0) for k, q in members]))
    return out


def _tap_sum(w_ref, taps, slab_of, rb, ls):
    accs = [None] * (rb // 8)
    for ri, q0, extra, members in _tap_groups(taps):
        slab = slab_of(ri, q0, rb + 8 * extra)
        for k, dq in members:
            wk = jnp.broadcast_to(w_ref[k:k + 1, ls], (8, ls.size))
            for i in range(rb // 8):
                t = wk * slab[8 * (dq + i):8 * (dq + i) + 8]
                accs[i] = t if accs[i] is None else accs[i] + t
    return jnp.concatenate(accs, axis=0)


def _row_loop(n, rb, fn, direct):
    if direct:
        for t in range(n):
            fn(t * rb)
    else:
        def step(t, carry):
            fn(pl.multiple_of(t * rb, rb))
            return carry

        lax.fori_loop(0, n, step, 0)


def _conv_blocking(K):
    return (CONV_LANES_FEW, CONV_RB_FEW) if K <= CONV_DIRECT_TAPS else (CONV_LANES, CONV_RB)


def _dwconv_fwd(src, w, b, *, col0, glu_col0, C, name, carry=None):
    S = src.shape[0]
    K = w.shape[0]
    pad = (K - 1) // 2
    rows, hb = min(CONV_ROWS, S), CONV_HALO
    lanes, rb = _conv_blocking(K)
    nt = S // rows
    rs, taps = _tap_plan([k - pad + hb for k in range(K)])
    direct = K <= CONV_DIRECT_TAPS
    n_u = 8 if direct else rows + 2 * hb - 8
    glu = glu_col0 is not None

    def body(*refs):
        if glu:
            vm, vp, vn, gm, gp, gn, w_ref, b_ref, o_ref, win, sh = refs
        else:
            vm, vp, vn, w_ref, b_ref, o_ref, win, sh = refs
        i = pl.program_id(1)

        def act(k):
            v = (vm, vp, vn)[k][...].astype(F32)
            return v * _sig((gm, gp, gn)[k][...].astype(F32)) if glu else v

        _fill_window(win, act(1), act(0), act(2), i, nt, rows)
        if not direct:
            _fill_shifts(win, sh, rs, n_u)

        def rowblk(r0):
            for ls in _lane_slices(lanes, direct):
                acc = _tap_sum(w_ref, taps, lambda ri, q, n: _tap_rows(win, sh, rs, ri, q, r0, n, direct, ls), rb, ls)
                o_ref[pl.ds(r0, rb), ls] = (acc + b_ref[:, ls]).astype(BF16)

        _row_loop(rows // rb, rb, rowblk, direct)

    in_specs = _halo_specs(S, rows, lanes, col0)
    args = [src, src, src]
    if glu:
        in_specs += _halo_specs(S, rows, lanes, glu_col0)
        args += [src, src, src]
    in_specs += [pl.BlockSpec((K, lanes), lambda j, i: (0, j)), pl.BlockSpec((1, lanes), lambda j, i: (0, j))]
    res, got = _call(
        body, grid=(C // lanes, nt), out_shape=[SDS((S, C), BF16)], in_specs=in_specs,
        out_specs=[pl.BlockSpec((rows, lanes), lambda j, i: (i, j))],
        scratch_shapes=[pltpu.VMEM((rows + 2 * hb, lanes), F32), pltpu.VMEM((len(rs), n_u, lanes), F32)],
        sem=("parallel", "parallel"), name=name, args=(*args, w, b), carry=carry)
    return res[0], got


def _dwconv_bwd(dsrcs, src, w, *, col0, glu_col0, C, name, carry=None):
    S = src.shape[0]
    K = w.shape[0]
    pad = (K - 1) // 2
    rows, hb = min(CONV_ROWS, S), CONV_HALO
    lanes, rb = _conv_blocking(K)
    nt = S // rows
    rs_u, taps_u = _tap_plan([k - pad + hb for k in range(K)])
    rs_d, taps_d = _tap_plan([hb + pad - k for k in range(K)])
    direct = K <= CONV_DIRECT_TAPS
    n_u = 8 if direct else rows + 2 * hb - 8
    glu = glu_col0 is not None
    nd = len(dsrcs)

    def body(*refs):
        d_refs, refs = refs[:3 * nd], refs[3 * nd:]
        if glu:
            vm, vp, vn, gm, gp, gn, w_ref = refs[:7]
            dv_ref, dg_ref, dw_ref, db_ref, dwin, dsh, uwin, ush, accw, accb = refs[7:]
        else:
            vm, vp, vn, w_ref = refs[:4]
            du_ref, dw_ref, db_ref, dwin, dsh, uwin, ush, accw, accb = refs[4:]
        i = pl.program_id(1)

        def dpiece(k):
            v = d_refs[k][...].astype(F32)
            for e in range(1, nd):
                v = v + d_refs[3 * e + k][...].astype(F32)
            return v

        def act(k):
            v = (vm, vp, vn)[k][...].astype(F32)
            return v * _sig((gm, gp, gn)[k][...].astype(F32)) if glu else v

        _fill_window(dwin, dpiece(1), dpiece(0), dpiece(2), i, nt, rows)
        _fill_window(uwin, act(1), act(0), act(2), i, nt, rows)
        if not direct:
            _fill_shifts(dwin, dsh, rs_d, n_u)
            _fill_shifts(uwin, ush, rs_u, n_u)

        @pl.when(i == 0)
        def _():
            accw[...] = jnp.zeros_like(accw)
            accb[...] = jnp.zeros_like(accb)

        def rowblk(r0):
            out_rows = pl.ds(r0, rb)
            for ls in _lane_slices(lanes, direct):
                du = _tap_sum(w_ref, taps_d, lambda ri, q, n: _tap_rows(dwin, dsh, rs_d, ri, q, r0, n, direct, ls),
                              rb, ls)
                if glu:
                    v = vm[out_rows, ls].astype(F32)
                    s = _sig(gm[out_rows, ls].astype(F32))
                    dv_ref[out_rows, ls] = (du * s).astype(BF16)
                    dg_ref[out_rows, ls] = (du * v * s * (1.0 - s)).astype(BF16)
                else:
                    du_ref[out_rows, ls] = du.astype(BF16)
                dmain = dwin[pl.ds(r0 + hb, rb) if direct else pl.ds(pl.multiple_of(r0 + hb, 8), rb), ls]
                accb[:, ls] += _rowsum8(dmain)
                for ri, q0, extra, members in _tap_groups(taps_u):
                    slab = _tap_rows(uwin, ush, rs_u, ri, q0, r0, rb + 8 * extra, direct, ls)
                    for k, dq in members:
                        accw[k, :, ls] += _rowsum8(dmain * slab[8 * dq:8 * dq + rb])

        _row_loop(rows // rb, rb, rowblk, direct)

        @pl.when(i == nt - 1)
        def _():
            for k in range(K):
                dw_ref[k:k + 1, :] = jnp.sum(accw[k], axis=0, keepdims=True)
            db_ref[...] = jnp.sum(accb[...], axis=0, keepdims=True)

    in_specs, args = [], []
    for d in dsrcs:
        in_specs += _halo_specs(S, rows, lanes, 0)
        args += [d, d, d]
    in_specs += _halo_specs(S, rows, lanes, col0)
    args += [src, src, src]
    if glu:
        in_specs += _halo_specs(S, rows, lanes, glu_col0)
        args += [src, src, src]
    in_specs += [pl.BlockSpec((K, lanes), lambda j, i: (0, j))]
    tile = pl.BlockSpec((rows, lanes), lambda j, i: (i, j))
    n_act = 2 if glu else 1
    out_shape = [SDS((S, C), BF16)] * n_act + [SDS((K, C), F32), SDS((1, C), F32)]
    out_specs = [tile] * n_act + [pl.BlockSpec((K, lanes), lambda j, i: (0, j)),
                                  pl.BlockSpec((1, lanes), lambda j, i: (0, j))]
    return _call(
        body, grid=(C // lanes, nt), out_shape=out_shape, in_specs=in_specs, out_specs=out_specs,
        scratch_shapes=[pltpu.VMEM((rows + 2 * hb, lanes), F32), pltpu.VMEM((len(rs_d), n_u, lanes), F32),
                        pltpu.VMEM((rows + 2 * hb, lanes), F32), pltpu.VMEM((len(rs_u), n_u, lanes), F32),
                        pltpu.VMEM((K, 8, lanes), F32), pltpu.VMEM((8, lanes), F32)],
        sem=("parallel", "arbitrary"), name=name, args=(*args, w), carry=carry)


def _ln_parts(u, ln_w, ln_b):
    mu = jnp.mean(u, axis=-1, keepdims=True)
    xc = u - mu
    rstd = lax.rsqrt(jnp.mean(xc * xc, axis=-1, keepdims=True) + EPS)
    un = xc * rstd
    return un, rstd, un * ln_w + ln_b


def _conf_out_fwd(uc, proj, h, ln_w, ln_b, wout, *, name):
    S, E = uc.shape
    D = h.shape[1]
    tm = min(GATE_TILE, S)

    def body(uc_ref, z_ref, h_ref, lw_ref, lb_ref, w_ref, o_ref):
        _, _, ul = _ln_parts(uc_ref[...].astype(F32), lw_ref[...], lb_ref[...])
        u3 = (_silu(ul) * _silu(z_ref[...].astype(F32))).astype(BF16)
        o_ref[...] = h_ref[...] + jnp.dot(u3, w_ref[...], preferred_element_type=F32)

    return pl.pallas_call(
        body, grid=(S // tm,), out_shape=SDS((S, D), F32),
        in_specs=[pl.BlockSpec((tm, E), lambda i: (i, 0)), pl.BlockSpec((tm, E), lambda i: (i, 2)),
                  pl.BlockSpec((tm, D), lambda i: (i, 0)), _resident((1, E)), _resident((1, E)), _resident((E, D))],
        out_specs=pl.BlockSpec((tm, D), lambda i: (i, 0)),
        compiler_params=_cparams(("parallel",)), name=name)(uc, proj, h, ln_w, ln_b, wout)


def _conf_out_bwd(dh, uc, proj, ln_w, ln_b, wout, *, name):
    S, E = uc.shape
    D = dh.shape[1]
    tm = min(GATE_TILE, S)
    nt = S // tm

    def body(dh_ref, uc_ref, z_ref, lw_ref, lb_ref, w_ref, dw_ref, duc_ref, dz_ref, dlw_ref, dlb_ref, alw, alb):
        i = pl.program_id(0)

        @pl.when(i == 0)
        def _():
            alw[...] = jnp.zeros_like(alw)
            alb[...] = jnp.zeros_like(alb)
            dw_ref[...] = jnp.zeros_like(dw_ref)

        un, rstd, ul = _ln_parts(uc_ref[...].astype(F32), lw_ref[...], lb_ref[...])
        z = z_ref[...].astype(F32)
        su, sz = _silu(ul), _silu(z)
        dhb = dh_ref[...].astype(BF16)
        dw_ref[...] += lax.dot_general((su * sz).astype(BF16), dhb, (((0,), (0,)), ((), ())),
                                       preferred_element_type=F32)
        du3 = lax.dot_general(dhb, w_ref[...], (((1,), (1,)), ((), ())), preferred_element_type=F32)
        dz_ref[...] = (du3 * su * _dsilu(z)).astype(BF16)
        dul = du3 * sz * _dsilu(ul)
        alw[...] += _rowsum8(dul * un)
        alb[...] += _rowsum8(dul)
        dun = dul * lw_ref[...]
        m1 = jnp.mean(dun, axis=-1, keepdims=True)
        m2 = jnp.mean(dun * un, axis=-1, keepdims=True)
        duc_ref[...] = (rstd * (dun - m1 - un * m2)).astype(BF16)

        @pl.when(i == nt - 1)
        def _():
            dlw_ref[...] = jnp.sum(alw[...], axis=0, keepdims=True)
            dlb_ref[...] = jnp.sum(alb[...], axis=0, keepdims=True)

    tile = pl.BlockSpec((tm, E), lambda i: (i, 0))
    vec = pl.BlockSpec((1, E), lambda i: (0, 0))
    return pl.pallas_call(
        body, grid=(nt,),
        out_shape=[SDS((E, D), F32)] + [SDS((S, E), BF16)] * 2 + [SDS((1, E), F32)] * 2,
        in_specs=[pl.BlockSpec((tm, D), lambda i: (i, 0)), tile, pl.BlockSpec((tm, E), lambda i: (i, 2)),
                  _resident((1, E)), _resident((1, E)), _resident((E, D))],
        out_specs=[pl.BlockSpec((E, D), lambda i: (0, 0)), tile, tile, vec, vec],
        scratch_shapes=[pltpu.VMEM((8, E), F32), pltpu.VMEM((8, E), F32)],
        compiler_params=_cparams(("arbitrary",)), name=name)(dh, uc, proj, ln_w, ln_b, wout)


def _inproj_bwd(parts, offs, w, h, nw, dh_in, *, wt, name):
    S, D = h.shape
    tm = min(ROW_TILE, S)
    nt = S // tm
    widths = [p.shape[1] for p in parts]
    npart = len(parts)

    def body(*refs):
        p_refs = refs[:npart]
        w_ref, h_ref, nw_ref, dh_ref, o_ref, hn_ref, dnw_ref, anw = refs[npart:]
        i = pl.program_id(0)

        @pl.when(i == 0)
        def _():
            anw[...] = jnp.zeros_like(anw)

        x = h_ref[...]
        r = lax.rsqrt(jnp.mean(x * x, axis=-1, keepdims=True) + EPS)
        n = x * r
        hn_ref[...] = (n * nw_ref[...]).astype(BF16)
        dhn = jnp.zeros((tm, D), F32)
        for p_ref, off, wd in zip(p_refs, offs, widths):
            for c0, c1 in _col_chunks(wd, 1024):
                if wt:
                    dhn = dhn + jnp.dot(p_ref[:, c0:c1], w_ref[off + c0:off + c1, :], preferred_element_type=F32)
                else:
                    dhn = dhn + lax.dot_general(p_ref[:, c0:c1], w_ref[:, off + c0:off + c1],
                                                (((1,), (1,)), ((), ())), preferred_element_type=F32)
        anw[...] += _rowsum8(dhn * n)
        dn = dhn * nw_ref[...]
        o_ref[...] = dh_ref[...] + r * (dn - n * jnp.mean(dn * n, axis=-1, keepdims=True))

        @pl.when(i == nt - 1)
        def _():
            dnw_ref[...] = jnp.sum(anw[...], axis=0, keepdims=True)

    tile = pl.BlockSpec((tm, D), lambda i: (i, 0))
    return pl.pallas_call(
        body, grid=(nt,),
        out_shape=[SDS((S, D), F32), SDS((S, D), BF16), SDS((1, D), F32)],
        in_specs=[pl.BlockSpec((tm, wd), lambda i: (i, 0)) for wd in widths]
        + [_resident(w.shape), tile, _resident((1, D)), tile],
        out_specs=[tile, tile, pl.BlockSpec((1, D), lambda i: (0, 0))],
        scratch_shapes=[pltpu.VMEM((8, D), F32)],
        compiler_params=_cparams(("arbitrary",)), name=name)(*parts, w, h, nw, dh_in)


def _tn_matmul(a, b, *, transpose_out, name):
    S, Ka = a.shape
    N = b.shape[1]
    tm = min(GRAD_TILE, S)
    bn = min(1024, N)
    nt = S // tm

    def body(a_ref, b_ref, o_ref, acc):
        i = pl.program_id(1)

        @pl.when(i == 0)
        def _():
            acc[...] = jnp.zeros_like(acc)

        acc[...] += lax.dot_general(a_ref[...], b_ref[...].astype(BF16), (((0,), (0,)), ((), ())),
                                    preferred_element_type=F32)

        @pl.when(i == nt - 1)
        def _():
            o_ref[...] = acc[...].T if transpose_out else acc[...]

    if transpose_out:
        out_shape, out_spec = SDS((N, Ka), F32), pl.BlockSpec((bn, Ka), lambda j, i: (j, 0))
    else:
        out_shape, out_spec = SDS((Ka, N), F32), pl.BlockSpec((Ka, bn), lambda j, i: (0, j))
    return pl.pallas_call(
        body, grid=(N // bn, nt), out_shape=out_shape,
        in_specs=[pl.BlockSpec((tm, Ka), lambda j, i: (i, 0)), pl.BlockSpec((tm, bn), lambda j, i: (i, j))],
        out_specs=out_spec, scratch_shapes=[pltpu.VMEM((Ka, bn), F32)],
        compiler_params=_cparams(("parallel", "arbitrary")), name=name)(a, b)


def _loss_head(h, fnw, target, *, name):
    S, D = h.shape
    tm = min(ROW_TILE, S)
    nt = S // tm

    def body(h_ref, w_ref, t_ref, loss_ref, dh_ref, dw_ref, aw, al):
        i = pl.program_id(0)

        @pl.when(i == 0)
        def _():
            aw[...] = jnp.zeros_like(aw)
            al[...] = jnp.zeros_like(al)

        x = h_ref[...]
        r = lax.rsqrt(jnp.mean(x * x, axis=-1, keepdims=True) + EPS)
        n = x * r
        err = n * w_ref[...] - t_ref[...]
        al[...] += _rowsum8(err * err)
        dy = err * (1.0 / D)
        aw[...] += _rowsum8(dy * n)
        dn = dy * w_ref[...]
        dh_ref[...] = r * (dn - n * jnp.mean(dn * n, axis=-1, keepdims=True))

        @pl.when(i == nt - 1)
        def _():
            dw_ref[...] = jnp.sum(aw[...], axis=0, keepdims=True)
            tot = jnp.sum(jnp.sum(al[...], axis=0, keepdims=True), axis=1, keepdims=True) * (0.5 / D)
            loss_ref[...] = jnp.zeros((8, 128), F32) + tot

    tile = pl.BlockSpec((tm, D), lambda i: (i, 0))
    loss, dh, dw = pl.pallas_call(
        body, grid=(nt,), out_shape=[SDS((8, 128), F32), SDS((S, D), F32), SDS((1, D), F32)],
        in_specs=[tile, _resident((1, D)), tile],
        out_specs=[pl.BlockSpec((8, 128), lambda i: (0, 0)), tile, pl.BlockSpec((1, D), lambda i: (0, 0))],
        scratch_shapes=[pltpu.VMEM((8, D), F32), pltpu.VMEM((8, D), F32)],
        compiler_params=_cparams(("arbitrary",)), name=name)(h, fnw, target)
    return loss[0, 0], dh, dw


def _head_mask(shape, head_axis):
    hd = lax.broadcasted_iota(jnp.int32, shape, head_axis)
    ch = lax.broadcasted_iota(jnp.int32, shape, 1 - head_axis)
    return ((ch >= hd * HEADDIM) & (ch < hd * HEADDIM + HEADDIM)).astype(BF16)


def _split_dot(vals, mat, fine):
    R = vals[0].shape[0]
    parts, where = [], []
    for v, f in zip(vals, fine):
        hi = v.astype(BF16)
        where.append((len(parts), f))
        parts += [hi, (v - hi.astype(F32)).astype(BF16)] if f else [hi]
    out = jnp.dot(jnp.concatenate(parts, axis=0), mat, preferred_element_type=F32)
    return [out[R * k:R * k + R] + out[R * k + R:R * k + 2 * R] if f else out[R * k:R * k + R] for k, f in where]


def _nt(a, b):
    return lax.dot_general(a, b, (((1,), (1,)), ((), ())), preferred_element_type=F32)


def _tn(a, b):
    return lax.dot_general(a, b, (((0,), (0,)), ((), ())), preferred_element_type=F32)


def _pair_blockdiag(v):
    lane = lax.broadcasted_iota(jnp.int32, v.shape, 1)
    zero = jnp.zeros_like(v)
    return jnp.concatenate([jnp.where(lane < HEADDIM, v, zero), jnp.where(lane >= HEADDIM, v, zero)], axis=0)


def _ssd_setup(c_val, dtraw, bias, alog, fwd):
    xbc = _silu(c_val.astype(F32))
    pre = dtraw + bias
    dt = _softplus(pre)
    A = -jnp.exp(alog)
    row = lax.broadcasted_iota(jnp.int32, (CHUNK, CHUNK), 0)
    col = lax.broadcasted_iota(jnp.int32, (CHUNK, CHUNK), 1)
    T = (col <= row) if fwd else (col >= row)
    cum = jnp.dot(T.astype(F32), dt * A, precision=lax.Precision.HIGHEST, preferred_element_type=F32)
    total = cum[CHUNK - 1:CHUNK] if fwd else cum[0:1]
    return dict(x=xbc[:, :D_INNER], B=xbc[:, D_INNER:D_INNER + GN].astype(BF16),
                C=xbc[:, D_INNER + GN:].astype(BF16), pre=pre, dt=dt, A=A, T=T, Tt=(col >= row) if fwd else (col <= row),
                cum=cum, e=jnp.exp(cum), dtds=dt * jnp.exp(total - cum))


def _ssd_fwd_dir(c_val, dtraw, bias, alog, state, y_ref, hin_ref, fwd, ex):
    s = _ssd_setup(c_val, dtraw, bias, alog, fwd)
    x, cum, T = s["x"], s["cum"], s["T"]
    e_x, dtds_x = _split_dot([s["e"], s["dtds"]], ex, [True, False])
    dec_x = e_x[CHUNK - 1:CHUNK] if fwd else e_x[0:1]
    cumT, dtT = cum.T, s["dt"].T
    xb = x.astype(BF16)
    for g in range(SSD_GROUPS):
        g0, g1 = g * HPG * HEADDIM, (g + 1) * HPG * HEADDIM
        Bg, Cg = s["B"][:, g * D_STATE:(g + 1) * D_STATE], s["C"][:, g * D_STATE:(g + 1) * D_STATE]
        CBg = _nt(Cg, Bg)
        Hg = state[:, g0:g1]
        Hb = Hg.astype(BF16)
        hin_ref[0, :, g0:g1] = Hb
        yoff = jnp.dot(Cg, Hb, preferred_element_type=F32) * e_x[:, g0:g1]
        ys = []
        for h in range(g * HPG, (g + 1) * HPG, 2):
            Ms = []
            for hh in (h, h + 1):
                Lh = jnp.exp(jnp.where(T, cum[:, hh:hh + 1] - cumT[hh:hh + 1, :], NEG))
                Ms.append((CBg * Lh * dtT[hh:hh + 1, :]).astype(BF16))
            ys.append(jnp.dot(jnp.concatenate(Ms, axis=1), _pair_blockdiag(xb[:, h * HEADDIM:(h + 2) * HEADDIM]),
                              preferred_element_type=F32))
        y_ref[:, g0:g1] = jnp.concatenate(ys, axis=1) + yoff
        xds = (x[:, g0:g1] * dtds_x[:, g0:g1]).astype(BF16)
        state[:, g0:g1] = Hg * dec_x[:, g0:g1] + _tn(Bg, xds)


def _ssd_scan_fwd(cact, tail, bias, alog, *, name):
    S = cact.shape[0]
    nc = S // CHUNK
    E = D_INNER

    def body(cf_ref, cb_ref, tf_ref, tb_ref, bias_ref, alog_ref, yf_ref, yb_ref, hf_ref, hb_ref, state):
        @pl.when(pl.program_id(0) == 0)
        def _():
            state[...] = jnp.zeros_like(state)

        ex = _head_mask((SSD_HEADS, D_INNER), 0)
        H = SSD_HEADS
        _ssd_fwd_dir(cf_ref[...], tf_ref[:, 0:H], bias_ref[0:1, :], alog_ref[0:1, :], state.at[0], yf_ref, hf_ref, True, ex)
        _ssd_fwd_dir(cb_ref[...], tb_ref[:, H:2 * H], bias_ref[1:2, :], alog_ref[1:2, :], state.at[1], yb_ref, hb_ref, False, ex)

    up, down = (lambda i: (i, 0)), (lambda i: (nc - 1 - i, 0))
    up3, down3 = (lambda i: (i, 0, 0)), (lambda i: (nc - 1 - i, 0, 0))
    W = cact.shape[1]
    return pl.pallas_call(
        body, grid=(nc,),
        out_shape=[SDS((S, E), F32), SDS((S, E), F32), SDS((nc, D_STATE, E), BF16), SDS((nc, D_STATE, E), BF16)],
        in_specs=[pl.BlockSpec((CHUNK, W), up), pl.BlockSpec((CHUNK, W), down),
                  pl.BlockSpec((CHUNK, 128), up), pl.BlockSpec((CHUNK, 128), down),
                  _resident((2, SSD_HEADS)), _resident((2, SSD_HEADS))],
        out_specs=[pl.BlockSpec((CHUNK, E), up), pl.BlockSpec((CHUNK, E), down),
                   pl.BlockSpec((1, D_STATE, E), up3), pl.BlockSpec((1, D_STATE, E), down3)],
        scratch_shapes=[pltpu.VMEM((2, D_STATE, E), F32)],
        compiler_params=_cparams(("arbitrary",)), name=name)(cact, cact, tail, tail, bias, alog)


def _ssd_bwd_dir(c_val, dtraw, bias, alog, dy, hin_ref, dskip_x, G, dc_ref, ddt_ref, accA, accb, fwd, ex, ext):
    s = _ssd_setup(c_val, dtraw, bias, alog, fwd)
    x, cum, T, dt = s["x"], s["cum"], s["T"], s["dt"]
    e_x, dtds_x = _split_dot([s["e"], s["dtds"]], ex, [True, False])
    dec_x = e_x[CHUNK - 1:CHUNK] if fwd else e_x[0:1]
    cumT, dtT = cum.T, dt.T
    xb = x.astype(BF16)
    dyf = dy.astype(F32)
    dx_diag, dx_st, dBs, dCs, gh, yoff_dy = [], [], [], [], [], []
    lane_h = lax.broadcasted_iota(jnp.int32, (CHUNK, SSD_HEADS), 1)
    sub_h = lax.broadcasted_iota(jnp.int32, (SSD_HEADS, CHUNK), 0)
    rows = jnp.zeros((CHUNK, SSD_HEADS), F32)
    colsT = jnp.zeros((SSD_HEADS, CHUNK), F32)
    for g in range(SSD_GROUPS):
        g0, g1 = g * HPG * HEADDIM, (g + 1) * HPG * HEADDIM
        Bg, Cg = s["B"][:, g * D_STATE:(g + 1) * D_STATE], s["C"][:, g * D_STATE:(g + 1) * D_STATE]
        CBg = _nt(Cg, Bg)
        Hb = hin_ref[0, :, g0:g1]
        Gg = G[:, g0:g1]
        Gb = Gg.astype(BF16)
        edyf = dyf[:, g0:g1] * e_x[:, g0:g1]
        edy = edyf.astype(BF16)
        yoff_dy.append(edyf * jnp.dot(Cg, Hb, preferred_element_type=F32))
        dC_g = _nt(edy, Hb)
        dH_g = _tn(Cg, edy)
        xds = (x[:, g0:g1] * dtds_x[:, g0:g1]).astype(BF16)
        dB_g = _nt(xds, Gb)
        dx_st.append(dtds_x[:, g0:g1] * jnp.dot(Bg, Gb, preferred_element_type=F32))
        dCB = jnp.zeros((CHUNK, CHUNK), F32)
        for h in range(g * HPG, (g + 1) * HPG, 2):
            ps = slice(h * HEADDIM, (h + 2) * HEADDIM)
            dM2 = _nt(dy[:, ps], _pair_blockdiag(xb[:, ps]))
            Ms = []
            for j, hh in enumerate((h, h + 1)):
                Lh = jnp.exp(jnp.where(T, cum[:, hh:hh + 1] - cumT[hh:hh + 1, :], NEG))
                Ms.append((CBg * Lh * dtT[hh:hh + 1, :]).astype(BF16))
                term = dM2[:, j * CHUNK:(j + 1) * CHUNK] * dtT[hh:hh + 1, :] * Lh
                dCB = dCB + term
                Wh = term * CBg
                rows = rows + jnp.where(lane_h == hh, jnp.sum(Wh, axis=1, keepdims=True), 0.0)
                colsT = colsT + jnp.where(sub_h == hh, jnp.sum(Wh, axis=0, keepdims=True), 0.0)
            dx_diag.append(_tn(jnp.concatenate(Ms, axis=0), _pair_blockdiag(dy[:, ps])))
        dCBb = dCB.astype(BF16)
        dCs.append(dC_g + jnp.dot(dCBb, Bg, preferred_element_type=F32))
        dBs.append(dB_g + _tn(dCBb, Cg))
        gh.append(_rowsum8(Gg * Hb.astype(F32)))
        G[:, g0:g1] = Gg * dec_x[:, g0:g1] + dH_g
    dxn = jnp.concatenate(dx_diag, axis=1) + jnp.concatenate(dx_st, axis=1)
    xst = x * jnp.concatenate(dx_st, axis=1)
    RZ, Zx = _split_dot([jnp.concatenate(yoff_dy, axis=1) - xst, x * dxn], ext, [True, False])
    ddec, zs_tot = _split_dot([jnp.concatenate(gh, axis=1), _rowsum8(xst)], ext, [True, True])
    dec = s["e"][CHUNK - 1:CHUNK] if fwd else s["e"][0:1]
    dtotal = jnp.sum(zs_tot + dec * ddec, axis=0, keepdims=True)
    rowi = lax.broadcasted_iota(jnp.int32, (CHUNK, SSD_HEADS), 0)
    dacs = rows - colsT.T + RZ + jnp.where(rowi == (CHUNK - 1 if fwd else 0), dtotal, 0.0)
    da = jnp.dot(s["Tt"].astype(F32), dacs, precision=lax.Precision.HIGHEST, preferred_element_type=F32)
    accA[...] += _rowsum8(da * dt)
    sg = _sig(s["pre"])
    ratio = jnp.where(dt > 1e-30, sg / jnp.maximum(dt, 1e-30), 1.0)
    ddtraw = da * s["A"] * sg + Zx * ratio
    ddt_ref[...] = ddtraw
    accb[...] += _rowsum8(ddtraw)
    dx = dxn + dyf * dskip_x if fwd else dxn
    dact = jnp.concatenate([dx] + dBs + dCs, axis=1)
    dc_ref[...] = (dact * _dsilu(c_val.astype(F32))).astype(BF16)


def _ssd_scan_bwd(cact, tail, bias, alog, dy, hf, hb, dskip_x, *, name, carry=None):
    S = cact.shape[0]
    nc = S // CHUNK
    E, H, W = D_INNER, SSD_HEADS, cact.shape[1]

    def body(cf_ref, cb_ref, tf_ref, tb_ref, bias_ref, alog_ref, dyf_ref, dyb_ref, hf_ref, hb_ref,
             dsk_ref, dcf_ref, dcb_ref, ddtf_ref, ddtb_ref, dalog_ref, dbias_ref, G, accA, accb):
        i = pl.program_id(0)

        @pl.when(i == 0)
        def _():
            G[...] = jnp.zeros_like(G)
            accA[...] = jnp.zeros_like(accA)
            accb[...] = jnp.zeros_like(accb)

        ex = _head_mask((H, E), 0)
        ext = _head_mask((E, H), 1)
        _ssd_bwd_dir(cf_ref[...], tf_ref[:, 0:H], bias_ref[0:1, :], alog_ref[0:1, :], dyf_ref[...], hf_ref,
                     dsk_ref[...], G.at[0], dcf_ref, ddtf_ref, accA.at[0], accb.at[0], True, ex, ext)
        _ssd_bwd_dir(cb_ref[...], tb_ref[:, H:2 * H], bias_ref[1:2, :], alog_ref[1:2, :], dyb_ref[...], hb_ref,
                     dsk_ref[...], G.at[1], dcb_ref, ddtb_ref, accA.at[1], accb.at[1], False, ex, ext)

        @pl.when(i == nc - 1)
        def _():
            for d in range(2):
                A = -jnp.exp(alog_ref[d:d + 1, :])
                dalog_ref[d:d + 1, :] = jnp.sum(accA[d], axis=0, keepdims=True) * A
                dbias_ref[d:d + 1, :] = jnp.sum(accb[d], axis=0, keepdims=True)

    up, down = (lambda i: (i, 0)), (lambda i: (nc - 1 - i, 0))
    up3, down3 = (lambda i: (i, 0, 0)), (lambda i: (nc - 1 - i, 0, 0))
    small = pl.BlockSpec((2, H), lambda i: (0, 0))
    return _call(
        body, grid=(nc,),
        out_shape=[SDS((S, W), BF16), SDS((S, W), BF16), SDS((S, H), F32), SDS((S, H), F32),
                   SDS((2, H), F32), SDS((2, H), F32)],
        in_specs=[pl.BlockSpec((CHUNK, W), down), pl.BlockSpec((CHUNK, W), up),
                  pl.BlockSpec((CHUNK, 128), down), pl.BlockSpec((CHUNK, 128), up),
                  _resident((2, H)), _resident((2, H)),
                  pl.BlockSpec((CHUNK, E), down), pl.BlockSpec((CHUNK, E), up),
                  pl.BlockSpec((1, D_STATE, E), down3), pl.BlockSpec((1, D_STATE, E), up3),
                  _resident((1, E))],
        out_specs=[pl.BlockSpec((CHUNK, W), down), pl.BlockSpec((CHUNK, W), up),
                   pl.BlockSpec((CHUNK, H), down), pl.BlockSpec((CHUNK, H), up), small, small],
        scratch_shapes=[pltpu.VMEM((2, D_STATE, E), F32), pltpu.VMEM((2, 8, H), F32), pltpu.VMEM((2, 8, H), F32)],
        sem=("arbitrary",), name=name, args=(cact, cact, tail, tail, bias, alog, dy, dy, hf, hb, dskip_x), carry=carry)


def _ssd_gate(yf, yb, cx, z, dsk, nw):
    xs = _silu(cx.astype(F32))
    y = yf + yb + xs * dsk
    sz = _silu(z)
    gt = y * sz
    r = lax.rsqrt(jnp.mean(gt * gt, axis=-1, keepdims=True) + EPS)
    return xs, y, sz, r, gt * r


def _ssd_out_fwd(yf, yb, cact, proj, dskip_x, nw, h, wout, *, name):
    S, E = yf.shape
    D = h.shape[1]
    tm = min(GATE_TILE, S)

    def body(yf_ref, yb_ref, cx_ref, z_ref, dsk_ref, nw_ref, h_ref, w_ref, o_ref):
        _, _, _, _, n = _ssd_gate(yf_ref[...], yb_ref[...], cx_ref[...], z_ref[...].astype(F32), dsk_ref[...], nw_ref[...])
        o_ref[...] = h_ref[...] + jnp.dot((n * nw_ref[...]).astype(BF16), w_ref[...], preferred_element_type=F32)

    tile = pl.BlockSpec((tm, E), lambda i: (i, 0))
    return pl.pallas_call(
        body, grid=(S // tm,), out_shape=SDS((S, D), F32),
        in_specs=[tile, tile, tile, tile, _resident((1, E)), _resident((1, E)),
                  pl.BlockSpec((tm, D), lambda i: (i, 0)), _resident((E, D))],
        out_specs=pl.BlockSpec((tm, D), lambda i: (i, 0)),
        compiler_params=_cparams(("parallel",)), name=name)(yf, yb, cact, proj, dskip_x, nw, h, wout)


def _ssd_out_bwd(dh, yf, yb, cact, proj, dskip_x, nw, wout, *, name):
    S, E = yf.shape
    D = dh.shape[1]
    tm = min(GATE_TILE, S)
    nt = S // tm

    def body(dh_ref, yf_ref, yb_ref, cx_ref, z_ref, dsk_ref, nw_ref, w_ref, dw_ref, dy_ref, dz_ref, dnw_ref, dd_ref, anw, ad):
        i = pl.program_id(0)

        @pl.when(i == 0)
        def _():
            anw[...] = jnp.zeros_like(anw)
            ad[...] = jnp.zeros_like(ad)
            dw_ref[...] = jnp.zeros_like(dw_ref)

        z = z_ref[...].astype(F32)
        xs, y, sz, r, n = _ssd_gate(yf_ref[...], yb_ref[...], cx_ref[...], z, dsk_ref[...], nw_ref[...])
        dhb = dh_ref[...].astype(BF16)
        dw_ref[...] += _tn((n * nw_ref[...]).astype(BF16), dhb)
        dyn = _nt(dhb, w_ref[...])
        anw[...] += _rowsum8(dyn * n)
        dn = dyn * nw_ref[...]
        dgt = r * (dn - n * jnp.mean(dn * n, axis=-1, keepdims=True))
        dy = dgt * sz
        dy_ref[...] = dy.astype(BF16)
        dz_ref[...] = (dgt * y * _dsilu(z)).astype(BF16)
        ad[...] += _rowsum8(dy * xs)

        @pl.when(i == nt - 1)
        def _():
            dnw_ref[...] = jnp.sum(anw[...], axis=0, keepdims=True)
            dd_ref[...] = jnp.sum(ad[...], axis=0, keepdims=True)

    tile = pl.BlockSpec((tm, E), lambda i: (i, 0))
    vec = pl.BlockSpec((1, E), lambda i: (0, 0))
    return pl.pallas_call(
        body, grid=(nt,), out_shape=[SDS((E, D), F32)] + [SDS((S, E), BF16)] * 2 + [SDS((1, E), F32)] * 2,
        in_specs=[pl.BlockSpec((tm, D), lambda i: (i, 0)), tile, tile, tile, tile,
                  _resident((1, E)), _resident((1, E)), _resident((E, D))],
        out_specs=[pl.BlockSpec((E, D), lambda i: (0, 0)), tile, tile, vec, vec],
        scratch_shapes=[pltpu.VMEM((8, E), F32), pltpu.VMEM((8, E), F32)],
        compiler_params=_cparams(("arbitrary",)), name=name)(dh, yf, yb, cact, proj, dskip_x, nw, wout)


def _local_step(x, target, p, hooks):
    E, S = D_INNER, x.shape[0]
    depth = p["norm_w"].shape[0]
    names = ["norm_w", "cm_w_in", "cm_dw_w", "cm_dw_b", "cm_ln_w", "cm_ln_b", "cm_w_out", "ssd_w_in", "ssd_conv_w",
             "ssd_conv_b", "ssd_dt_bias", "ssd_A_log", "ssd_D", "ssd_norm_w", "ssd_w_out"]
    g = {n: [None] * (depth if n == "norm_w" else depth // 2) for n in names}

    def carried(key):
        items, done = hooks.get(key, (None, None))
        return (items(g) if items else None), (done if done else lambda got: None)

    h, saved = x, []
    for i in range(depth):
        j, nw = i // 2, p["norm_w"][i][None]
        if i % 2 == 0:
            carry, done = carried(("cm_in_fwd", i))
            proj, got = _inproj_fwd(h, nw, p["cm_w_in"][j], tail=False, wt=False, name="cm_in_fwd", carry=carry)
            done(got)
            carry, done = carried(("cm_conv_fwd", i))
            uc, got = _dwconv_fwd(proj, p["cm_dw_w"][j], p["cm_dw_b"][j][None], col0=0, glu_col0=E, C=E,
                                  name="cm_conv_fwd", carry=carry)
            done(got)
            hn = _conf_out_fwd(uc, proj, h, p["cm_ln_w"][j][None], p["cm_ln_b"][j][None], p["cm_w_out"][j], name="cm_out_fwd")
            saved.append((h, proj, uc))
        else:
            (proj, tail), _ = _inproj_fwd(h, nw, p["ssd_w_in"][j], tail=True, wt=True, name="ssd_in_fwd")
            cact, _ = _dwconv_fwd(proj, p["ssd_conv_w"][j], p["ssd_conv_b"][j][None], col0=E, glu_col0=None,
                                  C=SSD_CONV_DIM, name="ssd_conv_fwd")
            yf, yb, hf, hb = _ssd_scan_fwd(cact, tail, p["ssd_dt_bias"][j], p["ssd_A_log"][j], name="ssd_scan_fwd")
            dsk = jnp.repeat(p["ssd_D"][j], HEADDIM)[None]
            hn = _ssd_out_fwd(yf, yb, cact, proj, dsk, p["ssd_norm_w"][j][None], h, p["ssd_w_out"][j], name="ssd_out_fwd")
            saved.append((h, proj, tail, cact, yf, yb, hf, hb, dsk))
        h = hn
    loss, dh, g_fnw = _loss_head(h, p["final_norm_w"][None], target, name="loss_head")

    for i in reversed(range(depth)):
        j, nw = i // 2, p["norm_w"][i][None]
        if i % 2 == 0:
            h_in, proj, uc = saved[i]
            g["cm_w_out"][j], duc, dz, g_lw, g_lb = _conf_out_bwd(
                dh, uc, proj, p["cm_ln_w"][j][None], p["cm_ln_b"][j][None], p["cm_w_out"][j], name="cm_out_bwd")
            carry, done = carried(("cm_conv_bwd", i))
            (dv, dg, g_dw, g_db), got = _dwconv_bwd([duc], proj, p["cm_dw_w"][j], col0=0, glu_col0=E, C=E,
                                                    name="cm_conv_bwd", carry=carry)
            done(got)
            parts = [dv, dg, dz]
            dh, hnb, g_nw = _inproj_bwd(parts, [0, E, 2 * E], p["cm_w_in"][j], h_in, nw, dh, wt=False, name="cm_in_bwd")
            g["cm_w_in"][j] = jnp.concatenate(
                [_tn_matmul(hnb, q, transpose_out=False, name="cm_win_grad") for q in parts], axis=1)
            g["cm_dw_w"][j], g["cm_dw_b"][j], g["cm_ln_w"][j], g["cm_ln_b"][j] = g_dw, g_db[0], g_lw[0], g_lb[0]
        else:
            h_in, proj, tail, cact, yf, yb, hf, hb, dsk = saved[i]
            g["ssd_w_out"][j], dy, dz, g_sn, g_dx = _ssd_out_bwd(
                dh, yf, yb, cact, proj, dsk, p["ssd_norm_w"][j][None], p["ssd_w_out"][j], name="ssd_out_bwd")
            carry, done = carried(("ssd_scan_bwd", i))
            (dcf, dcb, ddtf, ddtb, g_al, g_bias), got = _ssd_scan_bwd(
                cact, tail, p["ssd_dt_bias"][j], p["ssd_A_log"][j], dy, hf, hb, dsk, name="ssd_scan_bwd", carry=carry)
            done(got)
            (dxbc, g_cw, g_cb), _ = _dwconv_bwd([dcf, dcb], proj, p["ssd_conv_w"][j], col0=E, glu_col0=None,
                                                C=SSD_CONV_DIM, name="ssd_conv_bwd")
            ddt = jnp.concatenate([ddtf, ddtb, jnp.zeros((S, 128 - 2 * SSD_HEADS), F32)], axis=1).astype(BF16)
            parts = [dz, dxbc, ddt]
            dh, hnb, g_nw = _inproj_bwd(parts, [0, E, E + SSD_CONV_DIM], p["ssd_w_in"][j], h_in, nw, dh, wt=True,
                                        name="ssd_in_bwd")
            gw = [_tn_matmul(hnb, q, transpose_out=True, name="ssd_win_grad") for q in parts]
            g["ssd_w_in"][j] = jnp.concatenate([gw[0], gw[1], gw[2][:2 * SSD_HEADS]], axis=0)
            g["ssd_conv_w"][j], g["ssd_conv_b"][j], g["ssd_norm_w"][j] = g_cw, g_cb[0], g_sn[0]
            g["ssd_dt_bias"][j], g["ssd_A_log"][j] = g_bias, g_al
            g["ssd_D"][j] = jnp.sum(g_dx.reshape(SSD_HEADS, HEADDIM), axis=1)
        g["norm_w"][i] = g_nw[0]
    g["final_norm_w"] = g_fnw[0]
    return loss, dh, g


_FLIPS = [(fx, fy, fc) for fx in (0, 1) for fy in (0, 1) for fc in (0, 1)][1:]


def _exchange_copies(items, ins, outs, send_sems, recv_sems, local_sems, arrivals):
    x, y, c = lax.axis_index("x"), lax.axis_index("y"), lax.axis_index("c")
    me = 4 * x + 2 * y + c
    local, pairs = [], []
    for b, (_, scatter) in enumerate(items):
        local.append(pltpu.make_async_copy(ins[b].at[me] if scatter else ins[b], outs[b].at[me], local_sems.at[b]))
        for k, (fx, fy, fc) in enumerate(_FLIPS):
            px, py, pc = x ^ fx, y ^ fy, c ^ fc
            peer = 4 * px + 2 * py + pc
            src = ins[b].at[peer] if scatter else ins[b]
            sems = dict(send_sem=send_sems.at[b * 7 + k], recv_sem=recv_sems.at[b * 7 + k],
                        device_id=(px, py, pc), device_id_type=pl.DeviceIdType.MESH)
            pairs.append((pltpu.make_async_remote_copy(src_ref=src, dst_ref=outs[b].at[me], **sems),
                          pltpu.make_async_remote_copy(src_ref=src, dst_ref=outs[b].at[peer], **sems)
                          if arrivals else None))
    return local, pairs


def _exchange_start(items, ins, outs, *sems):
    local, pairs = _exchange_copies(items, ins, outs, *sems, arrivals=False)
    for cp in local:
        cp.start()
    for cp, _ in pairs:
        cp.start()


def _exchange_wait(items, ins, outs, *sems):
    local, pairs = _exchange_copies(items, ins, outs, *sems, arrivals=True)
    for _, arrival in pairs:
        arrival.wait_recv()
    for cp, _ in pairs:
        cp.wait_send()
    for cp in local:
        cp.wait()


def _exchange_shapes(items):
    n = len(items)
    out_shape = [SDS((N_DEV,) + tuple(a.shape[-2:]), a.dtype) for a, _ in items]
    sems = [pltpu.SemaphoreType.DMA((7 * n,)), pltpu.SemaphoreType.DMA((7 * n,)), pltpu.SemaphoreType.DMA((n,))]
    return out_shape, sems


def _exchange(items, *, name):
    n = len(items)

    def body(*refs):
        _exchange_start(items, refs[:n], refs[n:2 * n], *refs[2 * n:])
        _exchange_wait(items, refs[:n], refs[n:2 * n], *refs[2 * n:])

    hbm = pl.BlockSpec(memory_space=pltpu.HBM)
    out_shape, sems = _exchange_shapes(items)
    return pl.pallas_call(body, out_shape=out_shape, in_specs=[hbm] * n, out_specs=[hbm] * n,
                          scratch_shapes=sems, name=name)(*[a for a, _ in items])


def _call(body, *, grid, out_shape, in_specs, out_specs, scratch_shapes, sem, name, args, carry):
    if not carry:
        return pl.pallas_call(body, grid=grid, out_shape=out_shape, in_specs=in_specs, out_specs=out_specs,
                              scratch_shapes=scratch_shapes, compiler_params=_cparams(sem), name=name)(*args), None
    n, n_in, n_out, n_scr = len(carry), len(in_specs), len(out_specs), len(scratch_shapes)
    last_step = tuple(g - 1 for g in grid)

    def wrapped(*refs):
        ins, cin = refs[:n_in], refs[n_in:n_in + n]
        outs, cout = refs[n_in + n:n_in + n + n_out], refs[n_in + n + n_out:n_in + 2 * n + n_out]
        scr, sems = refs[n_in + 2 * n + n_out:n_in + 2 * n + n_out + n_scr], refs[n_in + 2 * n + n_out + n_scr:]
        first = last = None
        for d, top in enumerate(last_step):
            i = pl.program_id(d)
            first = (i == 0) if first is None else first & (i == 0)
            last = (i == top) if last is None else last & (i == top)

        @pl.when(first)
        def _():
            _exchange_start(carry, cin, cout, *sems)

        body(*ins, *outs, *scr)

        @pl.when(last)
        def _():
            _exchange_wait(carry, cin, cout, *sems)

    hbm = pl.BlockSpec(memory_space=pltpu.HBM)
    x_shape, x_sems = _exchange_shapes(carry)
    res = pl.pallas_call(
        wrapped, grid=grid, out_shape=list(out_shape) + x_shape, in_specs=list(in_specs) + [hbm] * n,
        out_specs=list(out_specs) + [hbm] * n, scratch_shapes=list(scratch_shapes) + x_sems,
        compiler_params=_cparams(("arbitrary",) * len(grid)), name=name)(*args, *[a for a, _ in carry])
    return res[:n_out], res[n_out:]


def _reduce_slots(recv, *, rows, name):
    n, R, C = recv.shape

    def body(r_ref, g_ref):
        g = r_ref[0].astype(F32)
        for s in range(1, n):
            g = g + r_ref[s].astype(F32)
        g_ref[...] = g

    return pl.pallas_call(
        body, grid=(R // rows,), out_shape=SDS((R, C), F32),
        in_specs=[pl.BlockSpec((n, rows, C), lambda i: (0, i, 0))],
        out_specs=pl.BlockSpec((rows, C), lambda i: (i, 0)),
        compiler_params=_cparams(("parallel",)), name=name)(recv)


def _reduce_adamw(recv, w, m, v, *, layer, rows, name):
    _, R, C = w.shape
    n = recv.shape[0]
    c1 = 1.0 / (1.0 - ADAM_B1 ** ADAM_STEP)
    c2 = 1.0 / (1.0 - ADAM_B2 ** ADAM_STEP)

    def body(r_ref, w_ref, m_ref, v_ref, g_ref, d_ref, mo_ref, vo_ref):
        g = r_ref[0].astype(F32)
        for s in range(1, n):
            g = g + r_ref[s].astype(F32)
        mn = ADAM_B1 * m_ref[...] + (1.0 - ADAM_B1) * g
        vn = ADAM_B2 * v_ref[...] + (1.0 - ADAM_B2) * (g * g)
        g_ref[...] = g
        mo_ref[...] = mn
        vo_ref[...] = vn
        d_ref[...] = -ADAM_LR * ((mn * c1) / (jnp.sqrt(vn * c2) + ADAM_EPS) + ADAM_WD * w_ref[...])

    tile = pl.BlockSpec((rows, C), lambda i: (i, 0))
    mine = pl.BlockSpec((None, rows, C), lambda i: (layer, i, 0))
    return pl.pallas_call(
        body, grid=(R // rows,), out_shape=[SDS((R, C), F32)] * 4,
        in_specs=[pl.BlockSpec((n, rows, C), lambda i: (0, i, 0)), mine, mine, mine],
        out_specs=[tile] * 4, compiler_params=_cparams(("parallel",)), name=name)(recv, w, m, v)


_ORDER = ["norm_w", "final_norm_w", "cm_w_in", "cm_dw_w", "cm_dw_b", "cm_ln_w", "cm_ln_b", "cm_w_out", "ssd_w_in",
          "ssd_conv_w", "ssd_conv_b", "ssd_dt_bias", "ssd_A_log", "ssd_D", "ssd_norm_w", "ssd_w_out"]
_SMALL = (["cm_dw_w", "ssd_conv_w", "ssd_conv_b", "ssd_norm_w"], 168)
_REP = (["norm_w", "final_norm_w", "cm_dw_b", "cm_ln_w", "cm_ln_b", "ssd_dt_bias", "ssd_A_log", "ssd_D"], 144)
_SHARD_AXIS = {"cm_w_in": 2, "cm_dw_w": 2, "cm_w_out": 1, "ssd_w_in": 2, "ssd_conv_w": 2, "ssd_conv_b": 1,
               "ssd_norm_w": 1, "ssd_w_out": 1}
OPT_ROWS = 256


def _pack(arrs, rows, lanes, dtype, lead=()):
    nl = len(lead)
    flat = jnp.concatenate([a.reshape(lead + (-1,)).astype(dtype) for a in arrs], axis=nl)
    flat = jnp.pad(flat, [(0, 0)] * nl + [(0, rows * lanes - flat.shape[nl])])
    return flat.reshape(lead + (rows, lanes))


def _unpack(buf, shapes, lead=()):
    nl = len(lead)
    flat = buf.reshape(lead + (-1,))
    out, off = [], 0
    for s in shapes:
        sz = math.prod(s)
        out.append(flat[..., off:off + sz].reshape(lead + tuple(s)))
        off += sz
    return out


def _join_shards(parts, axis):
    full = jnp.moveaxis(parts, 0, axis)
    sh = full.shape
    return full.reshape(sh[:axis] + (sh[axis] * sh[axis + 1],) + sh[axis + 2:])


def _split_shards(full, axis):
    sh = full.shape
    parts = full.reshape(sh[:axis] + (N_DEV, sh[axis] // N_DEV) + sh[axis + 1:])
    return jnp.moveaxis(parts, axis, 0)


def kernel(x, norm_w, final_norm_w, cm_w_in, cm_dw_w, cm_dw_b, cm_ln_w, cm_ln_b, cm_w_out, ssd_w_in, ssd_conv_w, ssd_conv_b, ssd_dt_bias, ssd_A_log, ssd_D, ssd_norm_w, ssd_w_out, loss_target, m_norm_w, m_final_norm_w, m_cm_w_in, m_cm_dw_w, m_cm_dw_b, m_cm_ln_w, m_cm_ln_b, m_cm_w_out, m_ssd_w_in, m_ssd_conv_w, m_ssd_conv_b, m_ssd_dt_bias, m_ssd_A_log, m_ssd_D, m_ssd_norm_w, m_ssd_w_out, v_norm_w, v_final_norm_w, v_cm_w_in, v_cm_dw_w, v_cm_dw_b, v_cm_ln_w, v_cm_ln_b, v_cm_w_out, v_ssd_w_in, v_ssd_conv_w, v_ssd_conv_b, v_ssd_dt_bias, v_ssd_A_log, v_ssd_D, v_ssd_norm_w, v_ssd_w_out):
    w = dict(zip(_ORDER, (norm_w, final_norm_w, cm_w_in, cm_dw_w, cm_dw_b, cm_ln_w, cm_ln_b, cm_w_out, ssd_w_in,
                          ssd_conv_w, ssd_conv_b, ssd_dt_bias, ssd_A_log, ssd_D, ssd_norm_w, ssd_w_out)))
    m = dict(zip(_ORDER, (m_norm_w, m_final_norm_w, m_cm_w_in, m_cm_dw_w, m_cm_dw_b, m_cm_ln_w, m_cm_ln_b, m_cm_w_out,
                          m_ssd_w_in, m_ssd_conv_w, m_ssd_conv_b, m_ssd_dt_bias, m_ssd_A_log, m_ssd_D, m_ssd_norm_w,
                          m_ssd_w_out)))
    v = dict(zip(_ORDER, (v_norm_w, v_final_norm_w, v_cm_w_in, v_cm_dw_w, v_cm_dw_b, v_cm_ln_w, v_cm_ln_b, v_cm_w_out,
                          v_ssd_w_in, v_ssd_conv_w, v_ssd_conv_b, v_ssd_dt_bias, v_ssd_A_log, v_ssd_D, v_ssd_norm_w,
                          v_ssd_w_out)))

    D, nl = D_MODEL, cm_w_in.shape[0]
    (sn, sr), (rn, rr) = _SMALL, _REP
    mats = ["cm_w_in", "cm_w_out", "ssd_w_in", "ssd_w_out"]
    bf = lambda a: a.astype(BF16)
    wt_in = jnp.swapaxes(ssd_w_in, 1, 2)
    t_rows = wt_in.shape[1]
    t_pad = -t_rows % 16

    p = {k: w[k] for k in rn}
    p.update({k: [None] * nl for k in mats})
    first = _exchange([(bf(cm_w_in[0]), False), (_pack([w[k] for k in sn], sr, 128, F32), False)], name="gather_first")
    p["cm_w_in"][0] = _join_shards(first[0], 1)
    for k, parts in zip(sn, _unpack(first[1], [w[k].shape for k in sn], lead=(N_DEV,))):
        p[k] = _join_shards(parts, _SHARD_AXIS[k])

    def rest_in(got):
        p["cm_w_out"][0] = got[0].reshape(-1, D)

    def rest_conv(got):
        p["cm_w_in"][1] = _join_shards(got[0], 1)
        p["cm_w_out"][1] = got[1].reshape(-1, D)
        wt = _join_shards(got[2].astype(F32).reshape(N_DEV, nl, -1, D), 1)
        wt = bf(jnp.pad(wt, ((0, 0), (0, SSD_IN_PAD - SSD_IN), (0, 0))))
        wo = got[3].reshape(N_DEV, nl, -1, D)
        for j in range(nl):
            p["ssd_w_in"][j] = wt[j]
            p["ssd_w_out"][j] = wo[:, j].reshape(-1, D)

    hooks = {("cm_in_fwd", 0): (lambda g: [(bf(cm_w_out[0]), False)], rest_in),
             ("cm_conv_fwd", 0): (lambda g: [(bf(cm_w_in[1]), False), (bf(cm_w_out[1]), False),
                                             (bf(wt_in.reshape(-1, D)), False), (bf(ssd_w_out.reshape(-1, D)), False)],
                                  rest_conv)}

    recv = {}

    def blocks(name, j, g):
        if name == "cm_w_in":
            return bf(_split_shards(g[name][j], 1))
        b = _split_shards(g[name][j], 0)
        if name == "ssd_w_in":
            b = jnp.pad(b, ((0, 0), (0, t_pad), (0, 0)))
        return bf(b)

    def sender(keys):
        def items(g):
            return [(blocks(n, j, g), True) for n, j in keys]

        def done(got):
            recv.update(zip(keys, got))

        return items, done

    hooks["ssd_scan_bwd", 1] = sender([(n, 1) for n in mats])
    hooks["cm_conv_bwd", 0] = sender([("ssd_w_in", 0), ("ssd_w_out", 0), ("cm_w_out", 0)])

    loss, dx, g = _local_step(x[0], loss_target[0], p, hooks)
    loss = lax.psum(loss, ("x", "y", "c"))

    last_keys = [("cm_w_in", 0)]
    stacked = {k: jnp.stack(g[k]) for k in sn + rn if k != "final_norm_w"}
    stacked["final_norm_w"] = g["final_norm_w"]
    last = _exchange([(blocks(n, j, g), True) for n, j in last_keys]
                     + [(_pack([_split_shards(stacked[k], _SHARD_AXIS[k]) for k in sn], sr, 128, F32, lead=(N_DEV,)), True),
                        (_pack([stacked[k] for k in rn], rr, 128, F32), False)], name="exchange_last")
    recv.update(zip(last_keys, last))

    outs = {}
    for name in mats:
        shp = w[name].shape
        w3, m3, v3 = [d[name].reshape(nl, -1, shp[-1]) for d in (w, m, v)]
        per_layer = []
        for j in range(nl):
            r = recv[name, j]
            if name == "ssd_w_in":
                gt = _reduce_slots(r, rows=r.shape[1], name="sum_ssd_w_in")
                r = jnp.swapaxes(gt[:t_rows], 0, 1)[None]
            per_layer.append(_reduce_adamw(r, w3, m3, v3, layer=j, rows=min(OPT_ROWS, r.shape[1]), name="adamw_" + name))
        for kind, parts in zip(("g", "d", "m", "v"), zip(*per_layer)):
            outs[kind, name] = jnp.stack(parts).reshape(shp)
    for (names, rows), r, nm in ((_SMALL, last[-2], "adamw_small"), (_REP, last[-1], "adamw_rep")):
        res = _reduce_adamw(r, *[_pack([d[k] for k in names], rows, 128, F32)[None] for d in (w, m, v)], layer=0,
                            rows=rows, name=nm)
        for kind, buf in zip(("g", "d", "m", "v"), res):
            for k, a in zip(names, _unpack(buf, [w[k].shape for k in names])):
                outs[kind, k] = a
    return (loss, dx[None], *[outs[kind, k] for kind in ("g", "d", "m", "v") for k in _ORDER])
```

```python
import functools
import math

import jax
import jax.numpy as jnp
from jax import lax
from jax.experimental import pallas as pl
from jax.experimental.pallas import tpu as pltpu

F32, BF16 = jnp.float32, jnp.bfloat16
SDS = jax.ShapeDtypeStruct

D_MODEL = 1024
D_INNER = 2048
CONF_KERNEL = 31
HEADDIM = 64
SSD_HEADS = 32
SSD_GROUPS = 4
HPG = SSD_HEADS // SSD_GROUPS
D_STATE = 128
SSD_CONV = 5
CHUNK = 128
GN = SSD_GROUPS * D_STATE
SSD_CONV_DIM = D_INNER + 2 * GN
SSD_IN = D_INNER + SSD_CONV_DIM + 2 * SSD_HEADS
SSD_IN_PAD = SSD_IN + 64
EPS = 1e-5
N_DEV = 8

ADAM_LR, ADAM_B1, ADAM_B2, ADAM_EPS, ADAM_WD, ADAM_STEP = 0.001, 0.9, 0.999, 1e-08, 0.01, 10

VMEM_LIMIT = 56 * 1024 * 1024
ROW_TILE = 512
GATE_TILE = 256
GRAD_TILE = 1024
CONV_ROWS = 512
CONV_LANES = 512
CONV_HALO = 16
CONV_RB = 32
CONV_SUB_LANES = 256
CONV_LANES_FEW = 1024
CONV_RB_FEW = 16
CONV_DIRECT_TAPS = 8
NEG = -1e30


def _cparams(sem):
    return pltpu.CompilerParams(dimension_semantics=sem, vmem_limit_bytes=VMEM_LIMIT)


def _resident(shape):
    nd = len(shape)
    return pl.BlockSpec(shape, lambda *_: (0,) * nd, pipeline_mode=pl.Buffered(1))


def _sig(x):
    return 1.0 / (1.0 + jnp.exp(-x))


def _silu(x):
    return x * _sig(x)


def _dsilu(x):
    s = _sig(x)
    return s * (1.0 + x * (1.0 - s))


def _softplus(x):
    return jnp.maximum(x, 0.0) + jnp.log(1.0 + jnp.exp(-jnp.abs(x)))


def _col_chunks(n, width):
    out, c = [], 0
    while c < n:
        out.append((c, min(c + width, n)))
        c += width
    return out


def _rowsum8(v):
    acc = v[0:8]
    for j in range(1, v.shape[0] // 8):
        acc = acc + v[8 * j:8 * j + 8]
    return acc


def _inproj_fwd(h, nw, w, *, tail, wt, name, carry=None):
    S, D = h.shape
    N = w.shape[0] if wt else w.shape[1]
    tm = min(ROW_TILE, S)
    chunks = _col_chunks(N, 1024)

    def body(h_ref, nw_ref, w_ref, o_ref, *rest):
        x = h_ref[...]
        r = lax.rsqrt(jnp.mean(x * x, axis=-1, keepdims=True) + EPS)
        hn = (x * r * nw_ref[...]).astype(BF16)
        for c0, c1 in chunks:
            if wt:
                acc = lax.dot_general(hn, w_ref[c0:c1, :], (((1,), (1,)), ((), ())), preferred_element_type=F32)
            else:
                acc = jnp.dot(hn, w_ref[:, c0:c1], preferred_element_type=F32)
            o_ref[:, c0:c1] = acc.astype(BF16)
            if tail and c1 == N:
                rest[0][...] = acc[:, acc.shape[1] - 128:]

    out_shape = [SDS((S, N), BF16)]
    out_specs = [pl.BlockSpec((tm, N), lambda i: (i, 0))]
    if tail:
        out_shape.append(SDS((S, 128), F32))
        out_specs.append(pl.BlockSpec((tm, 128), lambda i: (i, 0)))
    res, got = _call(
        body, grid=(S // tm,), out_shape=out_shape,
        in_specs=[pl.BlockSpec((tm, D), lambda i: (i, 0)), _resident((1, D)), _resident(w.shape)],
        out_specs=out_specs, scratch_shapes=[], sem=("parallel",), name=name, args=(h, nw, w), carry=carry)
    return (res if tail else res[0]), got


def _tap_plan(offsets):
    rs = sorted({s % 8 for s in offsets})
    return rs, [(k, rs.index(s % 8), s // 8) for k, s in enumerate(offsets)]


def _halo_specs(S, rows, lanes, col0):
    per, last = rows // CONV_HALO, S // CONV_HALO - 1
    cb = col0 // lanes
    return [pl.BlockSpec((rows, lanes), lambda j, i: (i, cb + j)),
            pl.BlockSpec((CONV_HALO, lanes), lambda j, i: (jnp.maximum(i * per - 1, 0), cb + j)),
            pl.BlockSpec((CONV_HALO, lanes), lambda j, i: (jnp.minimum((i + 1) * per, last), cb + j))]


def _fill_window(win, prev, main, nxt, i, nt, rows):
    hb = CONV_HALO
    win[0:hb, :] = jnp.where(i > 0, prev, 0.0)
    win[hb:hb + rows, :] = main
    win[hb + rows:hb + rows + hb, :] = jnp.where(i < nt - 1, nxt, 0.0)


def _fill_shifts(win, sh, rs, n_u):
    for idx, r in enumerate(rs):
        if r:
            sh[idx] = win[pl.ds(r, n_u), :]


def _tap_rows(win, sh, rs, ri, q, r0, n, direct, ls):
    if direct:
        return win[pl.ds(r0 + 8 * q + rs[ri], n), ls]
    rows = pl.ds(pl.multiple_of(r0 + 8 * q, 8), n)
    return sh[ri, rows, ls] if rs[ri] else win[rows, ls]


def _lane_slices(lanes, direct):
    sub = lanes if direct else min(CONV_SUB_LANES, lanes)
    return [pl.ds(l0, sub) for l0 in range(0, lanes, sub)]


def _tap_groups(taps):
    by_copy = {}
    for k, ri, q in taps:
        by_copy.setdefault(ri, []).append((k, q))
    out = []
    for ri, members in sorted(by_copy.items()):
        q0 = min(q for _, q in members)
        out.append((ri, q0, max(q for _, q in members) - q0, [(k, q - q0) for k, q in members]))
    return out


def _tap_sum(w_ref, taps, slab_of, rb, ls):
    accs = [None] * (rb // 8)
    for ri, q0, extra, members in _tap_groups(taps):
        slab = slab_of(ri, q0, rb + 8 * extra)
        for k, dq in members:
            wk = jnp.broadcast_to(w_ref[k:k + 1, ls], (8, ls.size))
            for i in range(rb // 8):
                t = wk * slab[8 * (dq + i):8 * (dq + i) + 8]
                accs[i] = t if accs[i] is None else accs[i] + t
    return jnp.concatenate(accs, axis=0)


def _row_loop(n, rb, fn, direct):
    if direct:
        for t in range(n):
            fn(t * rb)
    else:
        def step(t, carry):
            fn(pl.multiple_of(t * rb, rb))
            return carry

        lax.fori_loop(0, n, step, 0)


def _conv_blocking(K):
    return (CONV_LANES_FEW, CONV_RB_FEW) if K <= CONV_DIRECT_TAPS else (CONV_LANES, CONV_RB)


def _dwconv_fwd(src, w, b, *, col0, glu_col0, C, name, carry=None):
    S = src.shape[0]
    K = w.shape[0]
    pad = (K - 1) // 2
    rows, hb = min(CONV_ROWS, S), CONV_HALO
    lanes, rb = _conv_blocking(K)
    nt = S // rows
    rs, taps = _tap_plan([k - pad + hb for k in range(K)])
    direct = K <= CONV_DIRECT_TAPS
    n_u = 8 if direct else rows + 2 * hb - 8
    glu = glu_col0 is not None

    def body(*refs):
        if glu:
            vm, vp, vn, gm, gp, gn, w_ref, b_ref, o_ref, win, sh = refs
        else:
            vm, vp, vn, w_ref, b_ref, o_ref, win, sh = refs
        i = pl.program_id(1)

        def act(k):
            v = (vm, vp, vn)[k][...].astype(F32)
            return v * _sig((gm, gp, gn)[k][...].astype(F32)) if glu else v

        _fill_window(win, act(1), act(0), act(2), i, nt, rows)
        if not direct:
            _fill_shifts(win, sh, rs, n_u)

        def rowblk(r0):
            for ls in _lane_slices(lanes, direct):
                acc = _tap_sum(w_ref, taps, lambda ri, q, n: _tap_rows(win, sh, rs, ri, q, r0, n, direct, ls), rb, ls)
                o_ref[pl.ds(r0, rb), ls] = (acc + b_ref[:, ls]).astype(BF16)

        _row_loop(rows // rb, rb, rowblk, direct)

    in_specs = _halo_specs(S, rows, lanes, col0)
    args = [src, src, src]
    if glu:
        in_specs += _halo_specs(S, rows, lanes, glu_col0)
        args += [src, src, src]
    in_specs += [pl.BlockSpec((K, lanes), lambda j, i: (0, j)), pl.BlockSpec((1, lanes), lambda j, i: (0, j))]
    res, got = _call(
        body, grid=(C // lanes, nt), out_shape=[SDS((S, C), BF16)], in_specs=in_specs,
        out_specs=[pl.BlockSpec((rows, lanes), lambda j, i: (i, j))],
        scratch_shapes=[pltpu.VMEM((rows + 2 * hb, lanes), F32), pltpu.VMEM((len(rs), n_u, lanes), F32)],
        sem=("parallel", "parallel"), name=name, args=(*args, w, b), carry=carry)
    return res[0], got


def _dwconv_bwd(dsrcs, src, w, *, col0, glu_col0, C, name, carry=None):
    S = src.shape[0]
    K = w.shape[0]
    pad = (K - 1) // 2
    rows, hb = min(CONV_ROWS, S), CONV_HALO
    lanes, rb = _conv_blocking(K)
    nt = S // rows
    rs_u, taps_u = _tap_plan([k - pad + hb for k in range(K)])
    rs_d, taps_d = _tap_plan([hb + pad - k for k in range(K)])
    direct = K <= CONV_DIRECT_TAPS
    n_u = 8 if direct else rows + 2 * hb - 8
    glu = glu_col0 is not None
    nd = len(dsrcs)

    def body(*refs):
        d_refs, refs = refs[:3 * nd], refs[3 * nd:]
        if glu:
            vm, vp, vn, gm, gp, gn, w_ref = refs[:7]
            dv_ref, dg_ref, dw_ref, db_ref, dwin, dsh, uwin, ush, accw, accb = refs[7:]
        else:
            vm, vp, vn, w_ref = refs[:4]
            du_ref, dw_ref, db_ref, dwin, dsh, uwin, ush, accw, accb = refs[4:]
        i = pl.program_id(1)

        def dpiece(k):
            v = d_refs[k][...].astype(F32)
            for e in range(1, nd):
                v = v + d_refs[3 * e + k][...].astype(F32)
            return v

        def act(k):
            v = (vm, vp, vn)[k][...].astype(F32)
            return v * _sig((gm, gp, gn)[k][...].astype(F32)) if glu else v

        _fill_window(dwin, dpiece(1), dpiece(0), dpiece(2), i, nt, rows)
        _fill_window(uwin, act(1), act(0), act(2), i, nt, rows)
        if not direct:
            _fill_shifts(dwin, dsh, rs_d, n_u)
            _fill_shifts(uwin, ush, rs_u, n_u)

        @pl.when(i == 0)
        def _():
            accw[...] = jnp.zeros_like(accw)
            accb[...] = jnp.zeros_like(accb)

        def rowblk(r0):
            out_rows = pl.ds(r0, rb)
            for ls in _lane_slices(lanes, direct):
                du = _tap_sum(w_ref, taps_d, lambda ri, q, n: _tap_rows(dwin, dsh, rs_d, ri, q, r0, n, direct, ls),
                              rb, ls)
                if glu:
                    v = vm[out_rows, ls].astype(F32)
                    s = _sig(gm[out_rows, ls].astype(F32))
                    dv_ref[out_rows, ls] = (du * s).astype(BF16)
                    dg_ref[out_rows, ls] = (du * v * s * (1.0 - s)).astype(BF16)
                else:
                    du_ref[out_rows, ls] = du.astype(BF16)
                dmain = dwin[pl.ds(r0 + hb, rb) if direct else pl.ds(pl.multiple_of(r0 + hb, 8), rb), ls]
                accb[:, ls] += _rowsum8(dmain)
                for ri, q0, extra, members in _tap_groups(taps_u):
                    slab = _tap_rows(uwin, ush, rs_u, ri, q0, r0, rb + 8 * extra, direct, ls)
                    for k, dq in members:
                        accw[k, :, ls] += _rowsum8(dmain * slab[8 * dq:8 * dq + rb])

        _row_loop(rows // rb, rb, rowblk, direct)

        @pl.when(i == nt - 1)
        def _():
            for k in range(K):
                dw_ref[k:k + 1, :] = jnp.sum(accw[k], axis=0, keepdims=True)
            db_ref[...] = jnp.sum(accb[...], axis=0, keepdims=True)

    in_specs, args = [], []
    for d in dsrcs:
        in_specs += _halo_specs(S, rows, lanes, 0)
        args += [d, d, d]
    in_specs += _halo_specs(S, rows, lanes, col0)
    args += [src, src, src]
    if glu:
        in_specs += _halo_specs(S, rows, lanes, glu_col0)
        args += [src, src, src]
    in_specs += [pl.BlockSpec((K, lanes), lambda j, i: (0, j))]
    tile = pl.BlockSpec((rows, lanes), lambda j, i: (i, j))
    n_act = 2 if glu else 1
    out_shape = [SDS((S, C), BF16)] * n_act + [SDS((K, C), F32), SDS((1, C), F32)]
    out_specs = [tile] * n_act + [pl.BlockSpec((K, lanes), lambda j, i: (0, j)),
                                  pl.BlockSpec((1, lanes), lambda j, i: (0, j))]
    return _call(
        body, grid=(C // lanes, nt), out_shape=out_shape, in_specs=in_specs, out_specs=out_specs,
        scratch_shapes=[pltpu.VMEM((rows + 2 * hb, lanes), F32), pltpu.VMEM((len(rs_d), n_u, lanes), F32),
                        pltpu.VMEM((rows + 2 * hb, lanes), F32), pltpu.VMEM((len(rs_u), n_u, lanes), F32),
                        pltpu.VMEM((K, 8, lanes), F32), pltpu.VMEM((8, lanes), F32)],
        sem=("parallel", "arbitrary"), name=name, args=(*args, w), carry=carry)


def _ln_parts(u, ln_w, ln_b):
    mu = jnp.mean(u, axis=-1, keepdims=True)
    xc = u - mu
    rstd = lax.rsqrt(jnp.mean(xc * xc, axis=-1, keepdims=True) + EPS)
    un = xc * rstd
    return un, rstd, un * ln_w + ln_b


def _conf_out_fwd(uc, proj, h, ln_w, ln_b, wout, *, name):
    S, E = uc.shape
    D = h.shape[1]
    tm = min(GATE_TILE, S)

    def body(uc_ref, z_ref, h_ref, lw_ref, lb_ref, w_ref, o_ref):
        _, _, ul = _ln_parts(uc_ref[...].astype(F32), lw_ref[...], lb_ref[...])
        u3 = (_silu(ul) * _silu(z_ref[...].astype(F32))).astype(BF16)
        o_ref[...] = h_ref[...] + jnp.dot(u3, w_ref[...], preferred_element_type=F32)

    return pl.pallas_call(
        body, grid=(S // tm,), out_shape=SDS((S, D), F32),
        in_specs=[pl.BlockSpec((tm, E), lambda i: (i, 0)), pl.BlockSpec((tm, E), lambda i: (i, 2)),
                  pl.BlockSpec((tm, D), lambda i: (i, 0)), _resident((1, E)), _resident((1, E)), _resident((E, D))],
        out_specs=pl.BlockSpec((tm, D), lambda i: (i, 0)),
        compiler_params=_cparams(("parallel",)), name=name)(uc, proj, h, ln_w, ln_b, wout)


def _conf_out_bwd(dh, uc, proj, ln_w, ln_b, wout, *, name):
    S, E = uc.shape
    D = dh.shape[1]
    tm = min(GATE_TILE, S)
    nt = S // tm

    def body(dh_ref, uc_ref, z_ref, lw_ref, lb_ref, w_ref, dw_ref, duc_ref, dz_ref, dlw_ref, dlb_ref, alw, alb):
        i = pl.program_id(0)

        @pl.when(i == 0)
        def _():
            alw[...] = jnp.zeros_like(alw)
            alb[...] = jnp.zeros_like(alb)
            dw_ref[...] = jnp.zeros_like(dw_ref)

        un, rstd, ul = _ln_parts(uc_ref[...].astype(F32), lw_ref[...], lb_ref[...])
        z = z_ref[...].astype(F32)
        su, sz = _silu(ul), _silu(z)
        dhb = dh_ref[...].astype(BF16)
        dw_ref[...] += lax.dot_general((su * sz).astype(BF16), dhb, (((0,), (0,)), ((), ())),
                                       preferred_element_type=F32)
        du3 = lax.dot_general(dhb, w_ref[...], (((1,), (1,)), ((), ())), preferred_element_type=F32)
        dz_ref[...] = (du3 * su * _dsilu(z)).astype(BF16)
        dul = du3 * sz * _dsilu(ul)
        alw[...] += _rowsum8(dul * un)
        alb[...] += _rowsum8(dul)
        dun = dul * lw_ref[...]
        m1 = jnp.mean(dun, axis=-1, keepdims=True)
        m2 = jnp.mean(dun * un, axis=-1, keepdims=True)
        duc_ref[...] = (rstd * (dun - m1 - un * m2)).astype(BF16)

        @pl.when(i == nt - 1)
        def _():
            dlw_ref[...] = jnp.sum(alw[...], axis=0, keepdims=True)
            dlb_ref[...] = jnp.sum(alb[...], axis=0, keepdims=True)

    tile = pl.BlockSpec((tm, E), lambda i: (i, 0))
    vec = pl.BlockSpec((1, E), lambda i: (0, 0))
    return pl.pallas_call(
        body, grid=(nt,),
        out_shape=[SDS((E, D), F32)] + [SDS((S, E), BF16)] * 2 + [SDS((1, E), F32)] * 2,
        in_specs=[pl.BlockSpec((tm, D), lambda i: (i, 0)), tile, pl.BlockSpec((tm, E), lambda i: (i, 2)),
                  _resident((1, E)), _resident((1, E)), _resident((E, D))],
        out_specs=[pl.BlockSpec((E, D), lambda i: (0, 0)), tile, tile, vec, vec],
        scratch_shapes=[pltpu.VMEM((8, E), F32), pltpu.VMEM((8, E), F32)],
        compiler_params=_cparams(("arbitrary",)), name=name)(dh, uc, proj, ln_w, ln_b, wout)


def _inproj_bwd(parts, offs, w, h, nw, dh_in, *, wt, name):
    S, D = h.shape
    tm = min(ROW_TILE, S)
    nt = S // tm
    widths = [p.shape[1] for p in parts]
    npart = len(parts)

    def body(*refs):
        p_refs = refs[:npart]
        w_ref, h_ref, nw_ref, dh_ref, o_ref, hn_ref, dnw_ref, anw = refs[npart:]
        i = pl.program_id(0)

        @pl.when(i == 0)
        def _():
            anw[...] = jnp.zeros_like(anw)

        x = h_ref[...]
        r = lax.rsqrt(jnp.mean(x * x, axis=-1, keepdims=True) + EPS)
        n = x * r
        hn_ref[...] = (n * nw_ref[...]).astype(BF16)
        dhn = jnp.zeros((tm, D), F32)
        for p_ref, off, wd in zip(p_refs, offs, widths):
            for c0, c1 in _col_chunks(wd, 1024):
                if wt:
                    dhn = dhn + jnp.dot(p_ref[:, c0:c1], w_ref[off + c0:off + c1, :], preferred_element_type=F32)
                else:
                    dhn = dhn + lax.dot_general(p_ref[:, c0:c1], w_ref[:, off + c0:off + c1],
                                                (((1,), (1,)), ((), ())), preferred_element_type=F32)
        anw[...] += _rowsum8(dhn * n)
        dn = dhn * nw_ref[...]
        o_ref[...] = dh_ref[...] + r * (dn - n * jnp.mean(dn * n, axis=-1, keepdims=True))

        @pl.when(i == nt - 1)
        def _():
            dnw_ref[...] = jnp.sum(anw[...], axis=0, keepdims=True)

    tile = pl.BlockSpec((tm, D), lambda i: (i, 0))
    return pl.pallas_call(
        body, grid=(nt,),
        out_shape=[SDS((S, D), F32), SDS((S, D), BF16), SDS((1, D), F32)],
        in_specs=[pl.BlockSpec((tm, wd), lambda i: (i, 0)) for wd in widths]
        + [_resident(w.shape), tile, _resident((1, D)), tile],
        out_specs=[tile, tile, pl.BlockSpec((1, D), lambda i: (0, 0))],
        scratch_shapes=[pltpu.VMEM((8, D), F32)],
        compiler_params=_cparams(("arbitrary",)), name=name)(*parts, w, h, nw, dh_in)


def _tn_matmul(a, b, *, transpose_out, name):
    S, Ka = a.shape
    N = b.shape[1]
    tm = min(GRAD_TILE, S)
    bn = min(1024, N)
    nt = S // tm

    def body(a_ref, b_ref, o_ref, acc):
        i = pl.program_id(1)

        @pl.when(i == 0)
        def _():
            acc[...] = jnp.zeros_like(acc)

        acc[...] += lax.dot_general(a_ref[...], b_ref[...].astype(BF16), (((0,), (0,)), ((), ())),
                                    preferred_element_type=F32)

        @pl.when(i == nt - 1)
        def _():
            o_ref[...] = acc[...].T if transpose_out else acc[...]

    if transpose_out:
        out_shape, out_spec = SDS((N, Ka), F32), pl.BlockSpec((bn, Ka), lambda j, i: (j, 0))
    else:
        out_shape, out_spec = SDS((Ka, N), F32), pl.BlockSpec((Ka, bn), lambda j, i: (0, j))
    return pl.pallas_call(
        body, grid=(N // bn, nt), out_shape=out_shape,
        in_specs=[pl.BlockSpec((tm, Ka), lambda j, i: (i, 0)), pl.BlockSpec((tm, bn), lambda j, i: (i, j))],
        out_specs=out_spec, scratch_shapes=[pltpu.VMEM((Ka, bn), F32)],
        compiler_params=_cparams(("parallel", "arbitrary")), name=name)(a, b)


def _loss_head(h, fnw, target, *, name):
    S, D = h.shape
    tm = min(ROW_TILE, S)
    nt = S // tm

    def body(h_ref, w_ref, t_ref, loss_ref, dh_ref, dw_ref, aw, al):
        i = pl.program_id(0)

        @pl.when(i == 0)
        def _():
            aw[...] = jnp.zeros_like(aw)
            al[...] = jnp.zeros_like(al)

        x = h_ref[...]
        r = lax.rsqrt(jnp.mean(x * x, axis=-1, keepdims=True) + EPS)
        n = x * r
        err = n * w_ref[...] - t_ref[...]
        al[...] += _rowsum8(err * err)
        dy = err * (1.0 / D)
        aw[...] += _rowsum8(dy * n)
        dn = dy * w_ref[...]
        dh_ref[...] = r * (dn - n * jnp.mean(dn * n, axis=-1, keepdims=True))

        @pl.when(i == nt - 1)
        def _():
            dw_ref[...] = jnp.sum(aw[...], axis=0, keepdims=True)
            tot = jnp.sum(jnp.sum(al[...], axis=0, keepdims=True), axis=1, keepdims=True) * (0.5 / D)
            loss_ref[...] = jnp.zeros((8, 128), F32) + tot

    tile = pl.BlockSpec((tm, D), lambda i: (i, 0))
    loss, dh, dw = pl.pallas_call(
        body, grid=(nt,), out_shape=[SDS((8, 128), F32), SDS((S, D), F32), SDS((1, D), F32)],
        in_specs=[tile, _resident((1, D)), tile],
        out_specs=[pl.BlockSpec((8, 128), lambda i: (0, 0)), tile, pl.BlockSpec((1, D), lambda i: (0, 0))],
        scratch_shapes=[pltpu.VMEM((8, D), F32), pltpu.VMEM((8, D), F32)],
        compiler_params=_cparams(("arbitrary",)), name=name)(h, fnw, target)
    return loss[0, 0], dh, dw


def _head_mask(shape, head_axis):
    hd = lax.broadcasted_iota(jnp.int32, shape, head_axis)
    ch = lax.broadcasted_iota(jnp.int32, shape, 1 - head_axis)
    return ((ch >= hd * HEADDIM) & (ch < hd * HEADDIM + HEADDIM)).astype(BF16)


def _split_dot(vals, mat, fine):
    R = vals[0].shape[0]
    parts, where = [], []
    for v, f in zip(vals, fine):
        hi = v.astype(BF16)
        where.append((len(parts), f))
        parts += [hi, (v - hi.astype(F32)).astype(BF16)] if f else [hi]
    out = jnp.dot(jnp.concatenate(parts, axis=0), mat, preferred_element_type=F32)
    return [out[R * k:R * k + R] + out[R * k + R:R * k + 2 * R] if f else out[R * k:R * k + R] for k, f in where]


def _nt(a, b):
    return lax.dot_general(a, b, (((1,), (1,)), ((), ())), preferred_element_type=F32)


def _tn(a, b):
    return lax.dot_general(a, b, (((0,), (0,)), ((), ())), preferred_element_type=F32)


def _pair_blockdiag(v):
    lane = lax.broadcasted_iota(jnp.int32, v.shape, 1)
    zero = jnp.zeros_like(v)
    return jnp.concatenate([jnp.where(lane < HEADDIM, v, zero), jnp.where(lane >= HEADDIM, v, zero)], axis=0)


def _ssd_setup(c_val, dtraw, bias, alog, fwd):
    xbc = _silu(c_val.astype(F32))
    pre = dtraw + bias
    dt = _softplus(pre)
    A = -jnp.exp(alog)
    row = lax.broadcasted_iota(jnp.int32, (CHUNK, CHUNK), 0)
    col = lax.broadcasted_iota(jnp.int32, (CHUNK, CHUNK), 1)
    T = (col <= row) if fwd else (col >= row)
    cum = jnp.dot(T.astype(F32), dt * A, precision=lax.Precision.HIGHEST, preferred_element_type=F32)
    total = cum[CHUNK - 1:CHUNK] if fwd else cum[0:1]
    return dict(x=xbc[:, :D_INNER], B=xbc[:, D_INNER:D_INNER + GN].astype(BF16),
                C=xbc[:, D_INNER + GN:].astype(BF16), pre=pre, dt=dt, A=A, T=T, Tt=(col >= row) if fwd else (col <= row),
                cum=cum, e=jnp.exp(cum), dtds=dt * jnp.exp(total - cum))


def _ssd_fwd_dir(c_val, dtraw, bias, alog, state, y_ref, hin_ref, fwd, ex):
    s = _ssd_setup(c_val, dtraw, bias, alog, fwd)
    x, cum, T = s["x"], s["cum"], s["T"]
    e_x, dtds_x = _split_dot([s["e"], s["dtds"]], ex, [True, False])
    dec_x = e_x[CHUNK - 1:CHUNK] if fwd else e_x[0:1]
    cumT, dtT = cum.T, s["dt"].T
    xb = x.astype(BF16)
    for g in range(SSD_GROUPS):
        g0, g1 = g * HPG * HEADDIM, (g + 1) * HPG * HEADDIM
        Bg, Cg = s["B"][:, g * D_STATE:(g + 1) * D_STATE], s["C"][:, g * D_STATE:(g + 1) * D_STATE]
        CBg = _nt(Cg, Bg)
        Hg = state[:, g0:g1]
        Hb = Hg.astype(BF16)
        hin_ref[0, :, g0:g1] = Hb
        yoff = jnp.dot(Cg, Hb, preferred_element_type=F32) * e_x[:, g0:g1]
        ys = []
        for h in range(g * HPG, (g + 1) * HPG, 2):
            Ms = []
            for hh in (h, h + 1):
                Lh = jnp.exp(jnp.where(T, cum[:, hh:hh + 1] - cumT[hh:hh + 1, :], NEG))
                Ms.append((CBg * Lh * dtT[hh:hh + 1, :]).astype(BF16))
            ys.append(jnp.dot(jnp.concatenate(Ms, axis=1), _pair_blockdiag(xb[:, h * HEADDIM:(h + 2) * HEADDIM]),
                              preferred_element_type=F32))
        y_ref[:, g0:g1] = jnp.concatenate(ys, axis=1) + yoff
        xds = (x[:, g0:g1] * dtds_x[:, g0:g1]).astype(BF16)
        state[:, g0:g1] = Hg * dec_x[:, g0:g1] + _tn(Bg, xds)


def _ssd_scan_fwd(cact, tail, bias, alog, *, name):
    S = cact.shape[0]
    nc = S // CHUNK
    E = D_INNER

    def body(cf_ref, cb_ref, tf_ref, tb_ref, bias_ref, alog_ref, yf_ref, yb_ref, hf_ref, hb_ref, state):
        @pl.when(pl.program_id(0) == 0)
        def _():
            state[...] = jnp.zeros_like(state)

        ex = _head_mask((SSD_HEADS, D_INNER), 0)
        H = SSD_HEADS
        _ssd_fwd_dir(cf_ref[...], tf_ref[:, 0:H], bias_ref[0:1, :], alog_ref[0:1, :], state.at[0], yf_ref, hf_ref, True, ex)
        _ssd_fwd_dir(cb_ref[...], tb_ref[:, H:2 * H], bias_ref[1:2, :], alog_ref[1:2, :], state.at[1], yb_ref, hb_ref, False, ex)

    up, down = (lambda i: (i, 0)), (lambda i: (nc - 1 - i, 0))
    up3, down3 = (lambda i: (i, 0, 0)), (lambda i: (nc - 1 - i, 0, 0))
    W = cact.shape[1]
    return pl.pallas_call(
        body, grid=(nc,),
        out_shape=[SDS((S, E), F32), SDS((S, E), F32), SDS((nc, D_STATE, E), BF16), SDS((nc, D_STATE, E), BF16)],
        in_specs=[pl.BlockSpec((CHUNK, W), up), pl.BlockSpec((CHUNK, W), down),
                  pl.BlockSpec((CHUNK, 128), up), pl.BlockSpec((CHUNK, 128), down),
                  _resident((2, SSD_HEADS)), _resident((2, SSD_HEADS))],
        out_specs=[pl.BlockSpec((CHUNK, E), up), pl.BlockSpec((CHUNK, E), down),
                   pl.BlockSpec((1, D_STATE, E), up3), pl.BlockSpec((1, D_STATE, E), down3)],
        scratch_shapes=[pltpu.VMEM((2, D_STATE, E), F32)],
        compiler_params=_cparams(("arbitrary",)), name=name)(cact, cact, tail, tail, bias, alog)


def _ssd_bwd_dir(c_val, dtraw, bias, alog, dy, hin_ref, dskip_x, G, dc_ref, ddt_ref, accA, accb, fwd, ex, ext):
    s = _ssd_setup(c_val, dtraw, bias, alog, fwd)
    x, cum, T, dt = s["x"], s["cum"], s["T"], s["dt"]
    e_x, dtds_x = _split_dot([s["e"], s["dtds"]], ex, [True, False])
    dec_x = e_x[CHUNK - 1:CHUNK] if fwd else e_x[0:1]
    cumT, dtT = cum.T, dt.T
    xb = x.astype(BF16)
    dyf = dy.astype(F32)
    dx_diag, dx_st, dBs, dCs, gh, yoff_dy = [], [], [], [], [], []
    lane_h = lax.broadcasted_iota(jnp.int32, (CHUNK, SSD_HEADS), 1)
    sub_h = lax.broadcasted_iota(jnp.int32, (SSD_HEADS, CHUNK), 0)
    rows = jnp.zeros((CHUNK, SSD_HEADS), F32)
    colsT = jnp.zeros((SSD_HEADS, CHUNK), F32)
    for g in range(SSD_GROUPS):
        g0, g1 = g * HPG * HEADDIM, (g + 1) * HPG * HEADDIM
        Bg, Cg = s["B"][:, g * D_STATE:(g + 1) * D_STATE], s["C"][:, g * D_STATE:(g + 1) * D_STATE]
        CBg = _nt(Cg, Bg)
        Hb = hin_ref[0, :, g0:g1]
        Gg = G[:, g0:g1]
        Gb = Gg.astype(BF16)
        edyf = dyf[:, g0:g1] * e_x[:, g0:g1]
        edy = edyf.astype(BF16)
        yoff_dy.append(edyf * jnp.dot(Cg, Hb, preferred_element_type=F32))
        dC_g = _nt(edy, Hb)
        dH_g = _tn(Cg, edy)
        xds = (x[:, g0:g1] * dtds_x[:, g0:g1]).astype(BF16)
        dB_g = _nt(xds, Gb)
        dx_st.append(dtds_x[:, g0:g1] * jnp.dot(Bg, Gb, preferred_element_type=F32))
        dCB = jnp.zeros((CHUNK, CHUNK), F32)
        for h in range(g * HPG, (g + 1) * HPG, 2):
            ps = slice(h * HEADDIM, (h + 2) * HEADDIM)
            dM2 = _nt(dy[:, ps], _pair_blockdiag(xb[:, ps]))
            Ms = []
            for j, hh in enumerate((h, h + 1)):
                Lh = jnp.exp(jnp.where(T, cum[:, hh:hh + 1] - cumT[hh:hh + 1, :], NEG))
                Ms.append((CBg * Lh * dtT[hh:hh + 1, :]).astype(BF16))
                term = dM2[:, j * CHUNK:(j + 1) * CHUNK] * dtT[hh:hh + 1, :] * Lh
                dCB = dCB + term
                Wh = term * CBg
                rows = rows + jnp.where(lane_h == hh, jnp.sum(Wh, axis=1, keepdims=True), 0.0)
                colsT = colsT + jnp.where(sub_h == hh, jnp.sum(Wh, axis=0, keepdims=True), 0.0)
            dx_diag.append(_tn(jnp.concatenate(Ms, axis=0), _pair_blockdiag(dy[:, ps])))
        dCBb = dCB.astype(BF16)
        dCs.append(dC_g + jnp.dot(dCBb, Bg, preferred_element_type=F32))
        dBs.append(dB_g + _tn(dCBb, Cg))
        gh.append(_rowsum8(Gg * Hb.astype(F32)))
        G[:, g0:g1] = Gg * dec_x[:, g0:g1] + dH_g
    dxn = jnp.concatenate(dx_diag, axis=1) + jnp.concatenate(dx_st, axis=1)
    xst = x * jnp.concatenate(dx_st, axis=1)
    RZ, Zx = _split_dot([jnp.concatenate(yoff_dy, axis=1) - xst, x * dxn], ext, [True, False])
    ddec, zs_tot = _split_dot([jnp.concatenate(gh, axis=1), _rowsum8(xst)], ext, [True, True])
    dec = s["e"][CHUNK - 1:CHUNK] if fwd else s["e"][0:1]
    dtotal = jnp.sum(zs_tot + dec * ddec, axis=0, keepdims=True)
    rowi = lax.broadcasted_iota(jnp.int32, (CHUNK, SSD_HEADS), 0)
    dacs = rows - colsT.T + RZ + jnp.where(rowi == (CHUNK - 1 if fwd else 0), dtotal, 0.0)
    da = jnp.dot(s["Tt"].astype(F32), dacs, precision=lax.Precision.HIGHEST, preferred_element_type=F32)
    accA[...] += _rowsum8(da * dt)
    sg = _sig(s["pre"])
    ratio = jnp.where(dt > 1e-30, sg / jnp.maximum(dt, 1e-30), 1.0)
    ddtraw = da * s["A"] * sg + Zx * ratio
    ddt_ref[...] = ddtraw
    accb[...] += _rowsum8(ddtraw)
    dx = dxn + dyf * dskip_x if fwd else dxn
    dact = jnp.concatenate([dx] + dBs + dCs, axis=1)
    dc_ref[...] = (dact * _dsilu(c_val.astype(F32))).astype(BF16)


def _ssd_scan_bwd(cact, tail, bias, alog, dy, hf, hb, dskip_x, *, name, carry=None):
    S = cact.shape[0]
    nc = S // CHUNK
    E, H, W = D_INNER, SSD_HEADS, cact.shape[1]

    def body(cf_ref, cb_ref, tf_ref, tb_ref, bias_ref, alog_ref, dyf_ref, dyb_ref, hf_ref, hb_ref,
             dsk_ref, dcf_ref, dcb_ref, ddtf_ref, ddtb_ref, dalog_ref, dbias_ref, G, accA, accb):
        i = pl.program_id(0)

        @pl.when(i == 0)
        def _():
            G[...] = jnp.zeros_like(G)
            accA[...] = jnp.zeros_like(accA)
            accb[...] = jnp.zeros_like(accb)

        ex = _head_mask((H, E), 0)
        ext = _head_mask((E, H), 1)
        _ssd_bwd_dir(cf_ref[...], tf_ref[:, 0:H], bias_ref[0:1, :], alog_ref[0:1, :], dyf_ref[...], hf_ref,
                     dsk_ref[...], G.at[0], dcf_ref, ddtf_ref, accA.at[0], accb.at[0], True, ex, ext)
        _ssd_bwd_dir(cb_ref[...], tb_ref[:, H:2 * H], bias_ref[1:2, :], alog_ref[1:2, :], dyb_ref[...], hb_ref,
                     dsk_ref[...], G.at[1], dcb_ref, ddtb_ref, accA.at[1], accb.at[1], False, ex, ext)

        @pl.when(i == nc - 1)
        def _():
            for d in range(2):
                A = -jnp.exp(alog_ref[d:d + 1, :])
                dalog_ref[d:d + 1, :] = jnp.sum(accA[d], axis=0, keepdims=True) * A
                dbias_ref[d:d + 1, :] = jnp.sum(accb[d], axis=0, keepdims=True)

    up, down = (lambda i: (i, 0)), (lambda i: (nc - 1 - i, 0))
    up3, down3 = (lambda i: (i, 0, 0)), (lambda i: (nc - 1 - i, 0, 0))
    small = pl.BlockSpec((2, H), lambda i: (0, 0))
    return _call(
        body, grid=(nc,),
        out_shape=[SDS((S, W), BF16), SDS((S, W), BF16), SDS((S, H), F32), SDS((S, H), F32),
                   SDS((2, H), F32), SDS((2, H), F32)],
        in_specs=[pl.BlockSpec((CHUNK, W), down), pl.BlockSpec((CHUNK, W), up),
                  pl.BlockSpec((CHUNK, 128), down), pl.BlockSpec((CHUNK, 128), up),
                  _resident((2, H)), _resident((2, H)),
                  pl.BlockSpec((CHUNK, E), down), pl.BlockSpec((CHUNK, E), up),
                  pl.BlockSpec((1, D_STATE, E), down3), pl.BlockSpec((1, D_STATE, E), up3),
                  _resident((1, E))],
        out_specs=[pl.BlockSpec((CHUNK, W), down), pl.BlockSpec((CHUNK, W), up),
                   pl.BlockSpec((CHUNK, H), down), pl.BlockSpec((CHUNK, H), up), small, small],
        scratch_shapes=[pltpu.VMEM((2, D_STATE, E), F32), pltpu.VMEM((2, 8, H), F32), pltpu.VMEM((2, 8, H), F32)],
        sem=("arbitrary",), name=name, args=(cact, cact, tail, tail, bias, alog, dy, dy, hf, hb, dskip_x), carry=carry)


def _ssd_gate(yf, yb, cx, z, dsk, nw):
    xs = _silu(cx.astype(F32))
    y = yf + yb + xs * dsk
    sz = _silu(z)
    gt = y * sz
    r = lax.rsqrt(jnp.mean(gt * gt, axis=-1, keepdims=True) + EPS)
    return xs, y, sz, r, gt * r


def _ssd_out_fwd(yf, yb, cact, proj, dskip_x, nw, h, wout, *, name):
    S, E = yf.shape
    D = h.shape[1]
    tm = min(GATE_TILE, S)

    def body(yf_ref, yb_ref, cx_ref, z_ref, dsk_ref, nw_ref, h_ref, w_ref, o_ref):
        _, _, _, _, n = _ssd_gate(yf_ref[...], yb_ref[...], cx_ref[...], z_ref[...].astype(F32), dsk_ref[...], nw_ref[...])
        o_ref[...] = h_ref[...] + jnp.dot((n * nw_ref[...]).astype(BF16), w_ref[...], preferred_element_type=F32)

    tile = pl.BlockSpec((tm, E), lambda i: (i, 0))
    return pl.pallas_call(
        body, grid=(S // tm,), out_shape=SDS((S, D), F32),
        in_specs=[tile, tile, tile, tile, _resident((1, E)), _resident((1, E)),
                  pl.BlockSpec((tm, D), lambda i: (i, 0)), _resident((E, D))],
        out_specs=pl.BlockSpec((tm, D), lambda i: (i, 0)),
        compiler_params=_cparams(("parallel",)), name=name)(yf, yb, cact, proj, dskip_x, nw, h, wout)


def _ssd_out_bwd(dh, yf, yb, cact, proj, dskip_x, nw, wout, *, name):
    S, E = yf.shape
    D = dh.shape[1]
    tm = min(GATE_TILE, S)
    nt = S // tm

    def body(dh_ref, yf_ref, yb_ref, cx_ref, z_ref, dsk_ref, nw_ref, w_ref, dw_ref, dy_ref, dz_ref, dnw_ref, dd_ref, anw, ad):
        i = pl.program_id(0)

        @pl.when(i == 0)
        def _():
            anw[...] = jnp.zeros_like(anw)
            ad[...] = jnp.zeros_like(ad)
            dw_ref[...] = jnp.zeros_like(dw_ref)

        z = z_ref[...].astype(F32)
        xs, y, sz, r, n = _ssd_gate(yf_ref[...], yb_ref[...], cx_ref[...], z, dsk_ref[...], nw_ref[...])
        dhb = dh_ref[...].astype(BF16)
        dw_ref[...] += _tn((n * nw_ref[...]).astype(BF16), dhb)
        dyn = _nt(dhb, w_ref[...])
        anw[...] += _rowsum8(dyn * n)
        dn = dyn * nw_ref[...]
        dgt = r * (dn - n * jnp.mean(dn * n, axis=-1, keepdims=True))
        dy = dgt * sz
        dy_ref[...] = dy.astype(BF16)
        dz_ref[...] = (dgt * y * _dsilu(z)).astype(BF16)
        ad[...] += _rowsum8(dy * xs)

        @pl.when(i == nt - 1)
        def _():
            dnw_ref[...] = jnp.sum(anw[...], axis=0, keepdims=True)
            dd_ref[...] = jnp.sum(ad[...], axis=0, keepdims=True)

    tile = pl.BlockSpec((tm, E), lambda i: (i, 0))
    vec = pl.BlockSpec((1, E), lambda i: (0, 0))
    return pl.pallas_call(
        body, grid=(nt,), out_shape=[SDS((E, D), F32)] + [SDS((S, E), BF16)] * 2 + [SDS((1, E), F32)] * 2,
        in_specs=[pl.BlockSpec((tm, D), lambda i: (i, 0)), tile, tile, tile, tile,
                  _resident((1, E)), _resident((1, E)), _resident((E, D))],
        out_specs=[pl.BlockSpec((E, D), lambda i: (0, 0)), tile, tile, vec, vec],
        scratch_shapes=[pltpu.VMEM((8, E), F32), pltpu.VMEM((8, E), F32)],
        compiler_params=_cparams(("arbitrary",)), name=name)(dh, yf, yb, cact, proj, dskip_x, nw, wout)


def _local_step(x, target, p, hooks):
    E, S = D_INNER, x.shape[0]
    depth = p["norm_w"].shape[0]
    names = ["norm_w", "cm_w_in", "cm_dw_w", "cm_dw_b", "cm_ln_w", "cm_ln_b", "cm_w_out", "ssd_w_in", "ssd_conv_w",
             "ssd_conv_b", "ssd_dt_bias", "ssd_A_log", "ssd_D", "ssd_norm_w", "ssd_w_out"]
    g = {n: [None] * (depth if n == "norm_w" else depth // 2) for n in names}

    def carried(key):
        items, done = hooks.get(key, (None, None))
        return (items(g) if items else None), (done if done else lambda got: None)

    h, saved = x, []
    for i in range(depth):
        j, nw = i // 2, p["norm_w"][i][None]
        if i % 2 == 0:
            carry, done = carried(("cm_in_fwd", i))
            proj, got = _inproj_fwd(h, nw, p["cm_w_in"][j], tail=False, wt=False, name="cm_in_fwd", carry=carry)
            done(got)
            carry, done = carried(("cm_conv_fwd", i))
            uc, got = _dwconv_fwd(proj, p["cm_dw_w"][j], p["cm_dw_b"][j][None], col0=0, glu_col0=E, C=E,
                                  name="cm_conv_fwd", carry=carry)
            done(got)
            hn = _conf_out_fwd(uc, proj, h, p["cm_ln_w"][j][None], p["cm_ln_b"][j][None], p["cm_w_out"][j], name="cm_out_fwd")
            saved.append((h, proj, uc))
        else:
            (proj, tail), _ = _inproj_fwd(h, nw, p["ssd_w_in"][j], tail=True, wt=True, name="ssd_in_fwd")
            cact, _ = _dwconv_fwd(proj, p["ssd_conv_w"][j], p["ssd_conv_b"][j][None], col0=E, glu_col0=None,
                                  C=SSD_CONV_DIM, name="ssd_conv_fwd")
            yf, yb, hf, hb = _ssd_scan_fwd(cact, tail, p["ssd_dt_bias"][j], p["ssd_A_log"][j], name="ssd_scan_fwd")
            dsk = jnp.repeat(p["ssd_D"][j], HEADDIM)[None]
            hn = _ssd_out_fwd(yf, yb, cact, proj, dsk, p["ssd_norm_w"][j][None], h, p["ssd_w_out"][j], name="ssd_out_fwd")
            saved.append((h, proj, tail, cact, yf, yb, hf, hb, dsk))
        h = hn
    loss, dh, g_fnw = _loss_head(h, p["final_norm_w"][None], target, name="loss_head")

    for i in reversed(range(depth)):
        j, nw = i // 2, p["norm_w"][i][None]
        if i % 2 == 0:
            h_in, proj, uc = saved[i]
            g["cm_w_out"][j], duc, dz, g_lw, g_lb = _conf_out_bwd(
                dh, uc, proj, p["cm_ln_w"][j][None], p["cm_ln_b"][j][None], p["cm_w_out"][j], name="cm_out_bwd")
            carry, done = carried(("cm_conv_bwd", i))
            (dv, dg, g_dw, g_db), got = _dwconv_bwd([duc], proj, p["cm_dw_w"][j], col0=0, glu_col0=E, C=E,
                                                    name="cm_conv_bwd", carry=carry)
            done(got)
            parts = [dv, dg, dz]
            dh, hnb, g_nw = _inproj_bwd(parts, [0, E, 2 * E], p["cm_w_in"][j], h_in, nw, dh, wt=False, name="cm_in_bwd")
            g["cm_w_in"][j] = jnp.concatenate(
                [_tn_matmul(hnb, q, transpose_out=False, name="cm_win_grad") for q in parts], axis=1)
            g["cm_dw_w"][j], g["cm_dw_b"][j], g["cm_ln_w"][j], g["cm_ln_b"][j] = g_dw, g_db[0], g_lw[0], g_lb[0]
        else:
            h_in, proj, tail, cact, yf, yb, hf, hb, dsk = saved[i]
            g["ssd_w_out"][j], dy, dz, g_sn, g_dx = _ssd_out_bwd(
                dh, yf, yb, cact, proj, dsk, p["ssd_norm_w"][j][None], p["ssd_w_out"][j], name="ssd_out_bwd")
            carry, done = carried(("ssd_scan_bwd", i))
            (dcf, dcb, ddtf, ddtb, g_al, g_bias), got = _ssd_scan_bwd(
                cact, tail, p["ssd_dt_bias"][j], p["ssd_A_log"][j], dy, hf, hb, dsk, name="ssd_scan_bwd", carry=carry)
            done(got)
            (dxbc, g_cw, g_cb), _ = _dwconv_bwd([dcf, dcb], proj, p["ssd_conv_w"][j], col0=E, glu_col0=None,
                                                C=SSD_CONV_DIM, name="ssd_conv_bwd")
            ddt = jnp.concatenate([ddtf, ddtb, jnp.zeros((S, 128 - 2 * SSD_HEADS), F32)], axis=1).astype(BF16)
            parts = [dz, dxbc, ddt]
            dh, hnb, g_nw = _inproj_bwd(parts, [0, E, E + SSD_CONV_DIM], p["ssd_w_in"][j], h_in, nw, dh, wt=True,
                                        name="ssd_in_bwd")
            gw = [_tn_matmul(hnb, q, transpose_out=True, name="ssd_win_grad") for q in parts]
            g["ssd_w_in"][j] = jnp.concatenate([gw[0], gw[1], gw[2][:2 * SSD_HEADS]], axis=0)
            g["ssd_conv_w"][j], g["ssd_conv_b"][j], g["ssd_norm_w"][j] = g_cw, g_cb[0], g_sn[0]
            g["ssd_dt_bias"][j], g["ssd_A_log"][j] = g_bias, g_al
            g["ssd_D"][j] = jnp.sum(g_dx.reshape(SSD_HEADS, HEADDIM), axis=1)
        g["norm_w"][i] = g_nw[0]
    g["final_norm_w"] = g_fnw[0]
    return loss, dh, g


_FLIPS = [(fx, fy, fc) for fx in (0, 1) for fy in (0, 1) for fc in (0, 1)][1:]


def _exchange_copies(items, ins, outs, send_sems, recv_sems, local_sems, arrivals):
    x, y, c = lax.axis_index("x"), lax.axis_index("y"), lax.axis_index("c")
    me = 4 * x + 2 * y + c
    local, pairs = [], []
    for b, (_, scatter) in enumerate(items):
        local.append(pltpu.make_async_copy(ins[b].at[me] if scatter else ins[b], outs[b].at[me], local_sems.at[b]))
        for k, (fx, fy, fc) in enumerate(_FLIPS):
            px, py, pc = x ^ fx, y ^ fy, c ^ fc
            peer = 4 * px + 2 * py + pc
            src = ins[b].at[peer] if scatter else ins[b]
            sems = dict(send_sem=send_sems.at[b * 7 + k], recv_sem=recv_sems.at[b * 7 + k],
                        device_id=(px, py, pc), device_id_type=pl.DeviceIdType.MESH)
            pairs.append((pltpu.make_async_remote_copy(src_ref=src, dst_ref=outs[b].at[me], **sems),
                          pltpu.make_async_remote_copy(src_ref=src, dst_ref=outs[b].at[peer], **sems)
                          if arrivals else None))
    return local, pairs


def _exchange_start(items, ins, outs, *sems):
    local, pairs = _exchange_copies(items, ins, outs, *sems, arrivals=False)
    for cp in local:
        cp.start()
    for cp, _ in pairs:
        cp.start()


def _exchange_wait(items, ins, outs, *sems):
    local, pairs = _exchange_copies(items, ins, outs, *sems, arrivals=True)
    for _, arrival in pairs:
        arrival.wait_recv()
    for cp, _ in pairs:
        cp.wait_send()
    for cp in local:
        cp.wait()


def _exchange_shapes(items):
    n = len(items)
    out_shape = [SDS((N_DEV,) + tuple(a.shape[-2:]), a.dtype) for a, _ in items]
    sems = [pltpu.SemaphoreType.DMA((7 * n,)), pltpu.SemaphoreType.DMA((7 * n,)), pltpu.SemaphoreType.DMA((n,))]
    return out_shape, sems


def _exchange(items, *, name):
    n = len(items)

    def body(*refs):
        _exchange_start(items, refs[:n], refs[n:2 * n], *refs[2 * n:])
        _exchange_wait(items, refs[:n], refs[n:2 * n], *refs[2 * n:])

    hbm = pl.BlockSpec(memory_space=pltpu.HBM)
    out_shape, sems = _exchange_shapes(items)
    return pl.pallas_call(body, out_shape=out_shape, in_specs=[hbm] * n, out_specs=[hbm] * n,
                          scratch_shapes=sems, name=name)(*[a for a, _ in items])


def _call(body, *, grid, out_shape, in_specs, out_specs, scratch_shapes, sem, name, args, carry):
    if not carry:
        return pl.pallas_call(body, grid=grid, out_shape=out_shape, in_specs=in_specs, out_specs=out_specs,
                              scratch_shapes=scratch_shapes, compiler_params=_cparams(sem), name=name)(*args), None
    n, n_in, n_out, n_scr = len(carry), len(in_specs), len(out_specs), len(scratch_shapes)
    last_step = tuple(g - 1 for g in grid)

    def wrapped(*refs):
        ins, cin = refs[:n_in], refs[n_in:n_in + n]
        outs, cout = refs[n_in + n:n_in + n + n_out], refs[n_in + n + n_out:n_in + 2 * n + n_out]
        scr, sems = refs[n_in + 2 * n + n_out:n_in + 2 * n + n_out + n_scr], refs[n_in + 2 * n + n_out + n_scr:]
        first = last = None
        for d, top in enumerate(last_step):
            i = pl.program_id(d)
            first = (i == 0) if first is None else first & (i == 0)
            last = (i == top) if last is None else last & (i == top)

        @pl.when(first)
        def _():
            _exchange_start(carry, cin, cout, *sems)

        body(*ins, *outs, *scr)

        @pl.when(last)
        def _():
            _exchange_wait(carry, cin, cout, *sems)

    hbm = pl.BlockSpec(memory_space=pltpu.HBM)
    x_shape, x_sems = _exchange_shapes(carry)
    res = pl.pallas_call(
        wrapped, grid=grid, out_shape=list(out_shape) + x_shape, in_specs=list(in_specs) + [hbm] * n,
        out_specs=list(out_specs) + [hbm] * n, scratch_shapes=list(scratch_shapes) + x_sems,
        compiler_params=_cparams(("arbitrary",) * len(grid)), name=name)(*args, *[a for a, _ in carry])
    return res[:n_out], res[n_out:]


def _reduce_slots(recv, *, rows, name):
    n, R, C = recv.shape

    def body(r_ref, g_ref):
        g = r_ref[0].astype(F32)
        for s in range(1, n):
            g = g + r_ref[s].astype(F32)
        g_ref[...] = g

    return pl.pallas_call(
        body, grid=(R // rows,), out_shape=SDS((R, C), F32),
        in_specs=[pl.BlockSpec((n, rows, C), lambda i: (0, i, 0))],
        out_specs=pl.BlockSpec((rows, C), lambda i: (i, 0)),
        compiler_params=_cparams(("parallel",)), name=name)(recv)


def _reduce_adamw(recv, w, m, v, *, layer, rows, name):
    _, R, C = w.shape
    n = recv.shape[0]
    c1 = 1.0 / (1.0 - ADAM_B1 ** ADAM_STEP)
    c2 = 1.0 / (1.0 - ADAM_B2 ** ADAM_STEP)

    def body(r_ref, w_ref, m_ref, v_ref, g_ref, d_ref, mo_ref, vo_ref):
        g = r_ref[0].astype(F32)
        for s in range(1, n):
            g = g + r_ref[s].astype(F32)
        mn = ADAM_B1 * m_ref[...] + (1.0 - ADAM_B1) * g
        vn = ADAM_B2 * v_ref[...] + (1.0 - ADAM_B2) * (g * g)
        g_ref[...] = g
        mo_ref[...] = mn
        vo_ref[...] = vn
        d_ref[...] = -ADAM_LR * ((mn * c1) / (jnp.sqrt(vn * c2) + ADAM_EPS) + ADAM_WD * w_ref[...])

    tile = pl.BlockSpec((rows, C), lambda i: (i, 0))
    mine = pl.BlockSpec((None, rows, C), lambda i: (layer, i, 0))
    return pl.pallas_call(
        body, grid=(R // rows,), out_shape=[SDS((R, C), F32)] * 4,
        in_specs=[pl.BlockSpec((n, rows, C), lambda i: (0, i, 0)), mine, mine, mine],
        out_specs=[tile] * 4, compiler_params=_cparams(("parallel",)), name=name)(recv, w, m, v)


_ORDER = ["norm_w", "final_norm_w", "cm_w_in", "cm_dw_w", "cm_dw_b", "cm_ln_w", "cm_ln_b", "cm_w_out", "ssd_w_in",
          "ssd_conv_w", "ssd_conv_b", "ssd_dt_bias", "ssd_A_log", "ssd_D", "ssd_norm_w", "ssd_w_out"]
_SMALL = (["cm_dw_w", "ssd_conv_w", "ssd_conv_b", "ssd_norm_w"], 168)
_REP = (["norm_w", "final_norm_w", "cm_dw_b", "cm_ln_w", "cm_ln_b", "ssd_dt_bias", "ssd_A_log", "ssd_D"], 144)
_SHARD_AXIS = {"cm_w_in": 2, "cm_dw_w": 2, "cm_w_out": 1, "ssd_w_in": 2, "ssd_conv_w": 2, "ssd_conv_b": 1,
               "ssd_norm_w": 1, "ssd_w_out": 1}
OPT_ROWS = 256


def _pack(arrs, rows, lanes, dtype, lead=()):
    nl = len(lead)
    flat = jnp.concatenate([a.reshape(lead + (-1,)).astype(dtype) for a in arrs], axis=nl)
    flat = jnp.pad(flat, [(0, 0)] * nl + [(0, rows * lanes - flat.shape[nl])])
    return flat.reshape(lead + (rows, lanes))


def _unpack(buf, shapes, lead=()):
    nl = len(lead)
    flat = buf.reshape(lead + (-1,))
    out, off = [], 0
    for s in shapes:
        sz = math.prod(s)
        out.append(flat[..., off:off + sz].reshape(lead + tuple(s)))
        off += sz
    return out


def _join_shards(parts, axis):
    full = jnp.moveaxis(parts, 0, axis)
    sh = full.shape
    return full.reshape(sh[:axis] + (sh[axis] * sh[axis + 1],) + sh[axis + 2:])


def _split_shards(full, axis):
    sh = full.shape
    parts = full.reshape(sh[:axis] + (N_DEV, sh[axis] // N_DEV) + sh[axis + 1:])
    return jnp.moveaxis(parts, axis, 0)


def kernel(x, norm_w, final_norm_w, cm_w_in, cm_dw_w, cm_dw_b, cm_ln_w, cm_ln_b, cm_w_out, ssd_w_in, ssd_conv_w, ssd_conv_b, ssd_dt_bias, ssd_A_log, ssd_D, ssd_norm_w, ssd_w_out, loss_target, m_norm_w, m_final_norm_w, m_cm_w_in, m_cm_dw_w, m_cm_dw_b, m_cm_ln_w, m_cm_ln_b, m_cm_w_out, m_ssd_w_in, m_ssd_conv_w, m_ssd_conv_b, m_ssd_dt_bias, m_ssd_A_log, m_ssd_D, m_ssd_norm_w, m_ssd_w_out, v_norm_w, v_final_norm_w, v_cm_w_in, v_cm_dw_w, v_cm_dw_b, v_cm_ln_w, v_cm_ln_b, v_cm_w_out, v_ssd_w_in, v_ssd_conv_w, v_ssd_conv_b, v_ssd_dt_bias, v_ssd_A_log, v_ssd_D, v_ssd_norm_w, v_ssd_w_out):
    w = dict(zip(_ORDER, (norm_w, final_norm_w, cm_w_in, cm_dw_w, cm_dw_b, cm_ln_w, cm_ln_b, cm_w_out, ssd_w_in,
                          ssd_conv_w, ssd_conv_b, ssd_dt_bias, ssd_A_log, ssd_D, ssd_norm_w, ssd_w_out)))
    m = dict(zip(_ORDER, (m_norm_w, m_final_norm_w, m_cm_w_in, m_cm_dw_w, m_cm_dw_b, m_cm_ln_w, m_cm_ln_b, m_cm_w_out,
                          m_ssd_w_in, m_ssd_conv_w, m_ssd_conv_b, m_ssd_dt_bias, m_ssd_A_log, m_ssd_D, m_ssd_norm_w,
                          m_ssd_w_out)))
    v = dict(zip(_ORDER, (v_norm_w, v_final_norm_w, v_cm_w_in, v_cm_dw_w, v_cm_dw_b, v_cm_ln_w, v_cm_ln_b, v_cm_w_out,
                          v_ssd_w_in, v_ssd_conv_w, v_ssd_conv_b, v_ssd_dt_bias, v_ssd_A_log, v_ssd_D, v_ssd_norm_w,
                          v_ssd_w_out)))

    D, nl = D_MODEL, cm_w_in.shape[0]
    (sn, sr), (rn, rr) = _SMALL, _REP
    mats = ["cm_w_in", "cm_w_out", "ssd_w_in", "ssd_w_out"]
    bf = lambda a: a.astype(BF16)
    wt_in = jnp.swapaxes(ssd_w_in, 1, 2)
    t_rows = wt_in.shape[1]
    t_pad = -t_rows % 16

    p = {k: w[k] for k in rn}
    p.update({k: [None] * nl for k in mats})
    first = _exchange([(bf(cm_w_in[0]), False), (_pack([w[k] for k in sn], sr, 128, F32), False)], name="gather_first")
    p["cm_w_in"][0] = _join_shards(first[0], 1)
    for k, parts in zip(sn, _unpack(first[1], [w[k].shape for k in sn], lead=(N_DEV,))):
        p[k] = _join_shards(parts, _SHARD_AXIS[k])

    def rest_in(got):
        p["cm_w_out"][0] = got[0].reshape(-1, D)

    def rest_conv(got):
        p["cm_w_in"][1] = _join_shards(got[0], 1)
        p["cm_w_out"][1] = got[1].reshape(-1, D)
        wt = _join_shards(got[2].astype(F32).reshape(N_DEV, nl, -1, D), 1)
        wt = bf(jnp.pad(wt, ((0, 0), (0, SSD_IN_PAD - SSD_IN), (0, 0))))
        wo = got[3].reshape(N_DEV, nl, -1, D)
        for j in range(nl):
            p["ssd_w_in"][j] = wt[j]
            p["ssd_w_out"][j] = wo[:, j].reshape(-1, D)

    hooks = {("cm_in_fwd", 0): (lambda g: [(bf(cm_w_out[0]), False)], rest_in),
             ("cm_conv_fwd", 0): (lambda g: [(bf(cm_w_in[1]), False), (bf(cm_w_out[1]), False),
                                             (bf(wt_in.reshape(-1, D)), False), (bf(ssd_w_out.reshape(-1, D)), False)],
                                  rest_conv)}

    recv = {}

    def blocks(name, j, g):
        if name == "cm_w_in":
            return bf(_split_shards(g[name][j], 1))
        b = _split_shards(g[name][j], 0)
        if name == "ssd_w_in":
            b = jnp.pad(b, ((0, 0), (0, t_pad), (0, 0)))
        return bf(b)

    def sender(keys):
        def items(g):
            return [(blocks(n, j, g), True) for n, j in keys]

        def done(got):
            recv.update(zip(keys, got))

        return items, done

    hooks["ssd_scan_bwd", 1] = sender([(n, 1) for n in mats])
    hooks["cm_conv_bwd", 0] = sender([("ssd_w_in", 0), ("ssd_w_out", 0), ("cm_w_out", 0)])

    loss, dx, g = _local_step(x[0], loss_target[0], p, hooks)
    loss = lax.psum(loss, ("x", "y", "c"))

    last_keys = [("cm_w_in", 0)]
    stacked = {k: jnp.stack(g[k]) for k in sn + rn if k != "final_norm_w"}
    stacked["final_norm_w"] = g["final_norm_w"]
    last = _exchange([(blocks(n, j, g), True) for n, j in last_keys]
                     + [(_pack([_split_shards(stacked[k], _SHARD_AXIS[k]) for k in sn], sr, 128, F32, lead=(N_DEV,)), True),
                        (_pack([stacked[k] for k in rn], rr, 128, F32), False)], name="exchange_last")
    recv.update(zip(last_keys, last))

    outs = {}
    for name in mats:
        shp = w[name].shape
        w3, m3, v3 = [d[name].reshape(nl, -1, shp[-1]) for d in (w, m, v)]
        per_layer = []
        for j in range(nl):
            r = recv[name, j]
            if name == "ssd_w_in":
                gt = _reduce_slots(r, rows=r.shape[1], name="sum_ssd_w_in")
                r = jnp.swapaxes(gt[:t_rows], 0, 1)[None]
            per_layer.append(_reduce_adamw(r, w3, m3, v3, layer=j, rows=min(OPT_ROWS, r.shape[1]), name="adamw_" + name))
        for kind, parts in zip(("g", "d", "m", "v"), zip(*per_layer)):
            outs[kind, name] = jnp.stack(parts).reshape(shp)
    for (names, rows), r, nm in ((_SMALL, last[-2], "adamw_small"), (_REP, last[-1], "adamw_rep")):
        res = _reduce_adamw(r, *[_pack([d[k] for k in names], rows, 128, F32)[None] for d in (w, m, v)], layer=0,
                            rows=rows, name=nm)
        for kind, buf in zip(("g", "d", "m", "v"), res):
            for k, a in zip(names, _unpack(buf, [w[k].shape for k in names])):
                outs[kind, k] = a
    return (loss, dx[None], *[outs[kind, k] for kind in ("g", "d", "m", "v") for k in _ORDER])
```

```python
import functools
import math

import jax
import jax.numpy as jnp
from jax import lax
from jax.experimental import pallas as pl
from jax.experimental.pallas import tpu as pltpu

F32, BF16 = jnp.float32, jnp.bfloat16
SDS = jax.ShapeDtypeStruct

D_MODEL = 1024
D_INNER = 2048
CONF_KERNEL = 31
HEADDIM = 64
SSD_HEADS = 32
SSD_GROUPS = 4
HPG = SSD_HEADS // SSD_GROUPS
D_STATE = 128
SSD_CONV = 5
CHUNK = 128
GN = SSD_GROUPS * D_STATE
SSD_CONV_DIM = D_INNER + 2 * GN
SSD_IN = D_INNER + SSD_CONV_DIM + 2 * SSD_HEADS
SSD_IN_PAD = SSD_IN + 64
EPS = 1e-5
N_DEV = 8

ADAM_LR, ADAM_B1, ADAM_B2, ADAM_EPS, ADAM_WD, ADAM_STEP = 0.001, 0.9, 0.999, 1e-08, 0.01, 10

VMEM_LIMIT = 56 * 1024 * 1024
ROW_TILE = 512
GATE_TILE = 256
GRAD_TILE = 2048
CONV_ROWS = 512
CONV_LANES = 512
CONV_HALO = 16
CONV_RB = 32
CONV_SUB_LANES = 256
CONV_LANES_FEW = 1024
CONV_RB_FEW = 16
CONV_DIRECT_TAPS = 8
NEG = -1e30


def _cparams(sem):
    return pltpu.CompilerParams(dimension_semantics=sem, vmem_limit_bytes=VMEM_LIMIT)


def _resident(shape):
    nd = len(shape)
    return pl.BlockSpec(shape, lambda *_: (0,) * nd, pipeline_mode=pl.Buffered(1))


def _sig(x):
    return 1.0 / (1.0 + jnp.exp(-x))


def _silu(x):
    return x * _sig(x)


def _dsilu(x):
    s = _sig(x)
    return s * (1.0 + x * (1.0 - s))


def _softplus(x):
    return jnp.maximum(x, 0.0) + jnp.log(1.0 + jnp.exp(-jnp.abs(x)))


def _col_chunks(n, width):
    out, c = [], 0
    while c < n:
        out.append((c, min(c + width, n)))
        c += width
    return out


def _rowsum8(v):
    acc = v[0:8]
    for j in range(1, v.shape[0] // 8):
        acc = acc + v[8 * j:8 * j + 8]
    return acc


def _inproj_fwd(h, nw, w, *, tail, wt, name, carry=None):
    S, D = h.shape
    N = w.shape[0] if wt else w.shape[1]
    tm = min(ROW_TILE, S)
    chunks = _col_chunks(N, 1024)

    def body(h_ref, nw_ref, w_ref, o_ref, *rest):
        x = h_ref[...]
        r = lax.rsqrt(jnp.mean(x * x, axis=-1, keepdims=True) + EPS)
        hn = (x * r * nw_ref[...]).astype(BF16)
        for c0, c1 in chunks:
            if wt:
                acc = lax.dot_general(hn, w_ref[c0:c1, :], (((1,), (1,)), ((), ())), preferred_element_type=F32)
            else:
                acc = jnp.dot(hn, w_ref[:, c0:c1], preferred_element_type=F32)
            o_ref[:, c0:c1] = acc.astype(BF16)
            if tail and c1 == N:
                rest[0][...] = acc[:, acc.shape[1] - 128:]

    out_shape = [SDS((S, N), BF16)]
    out_specs = [pl.BlockSpec((tm, N), lambda i: (i, 0))]
    if tail:
        out_shape.append(SDS((S, 128), F32))
        out_specs.append(pl.BlockSpec((tm, 128), lambda i: (i, 0)))
    res, got = _call(
        body, grid=(S // tm,), out_shape=out_shape,
        in_specs=[pl.BlockSpec((tm, D), lambda i: (i, 0)), _resident((1, D)), _resident(w.shape)],
        out_specs=out_specs, scratch_shapes=[], sem=("parallel",), name=name, args=(h, nw, w), carry=carry)
    return (res if tail else res[0]), got


def _tap_plan(offsets):
    rs = sorted({s % 8 for s in offsets})
    return rs, [(k, rs.index(s % 8), s // 8) for k, s in enumerate(offsets)]


def _halo_specs(S, rows, lanes, col0):
    per, last = rows // CONV_HALO, S // CONV_HALO - 1
    cb = col0 // lanes
    return [pl.BlockSpec((rows, lanes), lambda j, i: (i, cb + j)),
            pl.BlockSpec((CONV_HALO, lanes), lambda j, i: (jnp.maximum(i * per - 1, 0), cb + j)),
            pl.BlockSpec((CONV_HALO, lanes), lambda j, i: (jnp.minimum((i + 1) * per, last), cb + j))]


def _fill_window(win, prev, main, nxt, i, nt, rows):
    hb = CONV_HALO
    win[0:hb, :] = jnp.where(i > 0, prev, 0.0)
    win[hb:hb + rows, :] = main
    win[hb + rows:hb + rows + hb, :] = jnp.where(i < nt - 1, nxt, 0.0)


def _fill_shifts(win, sh, rs, n_u):
    for idx, r in enumerate(rs):
        if r:
            sh[idx] = win[pl.ds(r, n_u), :]


def _tap_rows(win, sh, rs, ri, q, r0, n, direct, ls):
    if direct:
        return win[pl.ds(r0 + 8 * q + rs[ri], n), ls]
    rows = pl.ds(pl.multiple_of(r0 + 8 * q, 8), n)
    return sh[ri, rows, ls] if rs[ri] else win[rows, ls]


def _lane_slices(lanes, direct):
    sub = lanes if direct else min(CONV_SUB_LANES, lanes)
    return [pl.ds(l0, sub) for l0 in range(0, lanes, sub)]


def _tap_groups(taps):
    by_copy = {}
    for k, ri, q in taps:
        by_copy.setdefault(ri, []).append((k, q))
    out = []
    for ri, members in sorted(by_copy.items()):
        q0 = min(q for _, q in members)
        out.append((ri, q0, max(q for _, q in members) - q0, [(k, q - q0) for k, q in members]))
    return out


def _tap_sum(w_ref, taps, slab_of, rb, ls):
    accs = [None] * (rb // 8)
    for ri, q0, extra, members in _tap_groups(taps):
        slab = slab_of(ri, q0, rb + 8 * extra)
        for k, dq in members:
            wk = jnp.broadcast_to(w_ref[k:k + 1, ls], (8, ls.size))
            for i in range(rb // 8):
                t = wk * slab[8 * (dq + i):8 * (dq + i) + 8]
                accs[i] = t if accs[i] is None else accs[i] + t
    return jnp.concatenate(accs, axis=0)


def _row_loop(n, rb, fn, direct):
    if direct:
        for t in range(n):
            fn(t * rb)
    else:
        def step(t, carry):
            fn(pl.multiple_of(t * rb, rb))
            return carry

        lax.fori_loop(0, n, step, 0)


def _conv_blocking(K):
    return (CONV_LANES_FEW, CONV_RB_FEW) if K <= CONV_DIRECT_TAPS else (CONV_LANES, CONV_RB)


def _dwconv_fwd(src, w, b, *, col0, glu_col0, C, name, carry=None):
    S = src.shape[0]
    K = w.shape[0]
    pad = (K - 1) // 2
    rows, hb = min(CONV_ROWS, S), CONV_HALO
    lanes, rb = _conv_blocking(K)
    nt = S // rows
    rs, taps = _tap_plan([k - pad + hb for k in range(K)])
    direct = K <= CONV_DIRECT_TAPS
    n_u = 8 if direct else rows + 2 * hb - 8
    glu = glu_col0 is not None

    def body(*refs):
        if glu:
            vm, vp, vn, gm, gp, gn, w_ref, b_ref, o_ref, win, sh = refs
        else:
            vm, vp, vn, w_ref, b_ref, o_ref, win, sh = refs
        i = pl.program_id(1)

        def act(k):
            v = (vm, vp, vn)[k][...].astype(F32)
            return v * _sig((gm, gp, gn)[k][...].astype(F32)) if glu else v

        _fill_window(win, act(1), act(0), act(2), i, nt, rows)
        if not direct:
            _fill_shifts(win, sh, rs, n_u)

        def rowblk(r0):
            for ls in _lane_slices(lanes, direct):
                acc = _tap_sum(w_ref, taps, lambda ri, q, n: _tap_rows(win, sh, rs, ri, q, r0, n, direct, ls), rb, ls)
                o_ref[pl.ds(r0, rb), ls] = (acc + b_ref[:, ls]).astype(BF16)

        _row_loop(rows // rb, rb, rowblk, direct)

    in_specs = _halo_specs(S, rows, lanes, col0)
    args = [src, src, src]
    if glu:
        in_specs += _halo_specs(S, rows, lanes, glu_col0)
        args += [src, src, src]
    in_specs += [pl.BlockSpec((K, lanes), lambda j, i: (0, j)), pl.BlockSpec((1, lanes), lambda j, i: (0, j))]
    res, got = _call(
        body, grid=(C // lanes, nt), out_shape=[SDS((S, C), BF16)], in_specs=in_specs,
        out_specs=[pl.BlockSpec((rows, lanes), lambda j, i: (i, j))],
        scratch_shapes=[pltpu.VMEM((rows + 2 * hb, lanes), F32), pltpu.VMEM((len(rs), n_u, lanes), F32)],
        sem=("parallel", "parallel"), name=name, args=(*args, w, b), carry=carry)
    return res[0], got


def _dwconv_bwd(dsrcs, src, w, *, col0, glu_col0, C, name, carry=None):
    S = src.shape[0]
    K = w.shape[0]
    pad = (K - 1) // 2
    rows, hb = min(CONV_ROWS, S), CONV_HALO
    lanes, rb = _conv_blocking(K)
    nt = S // rows
    rs_u, taps_u = _tap_plan([k - pad + hb for k in range(K)])
    rs_d, taps_d = _tap_plan([hb + pad - k for k in range(K)])
    direct = K <= CONV_DIRECT_TAPS
    n_u = 8 if direct else rows + 2 * hb - 8
    glu = glu_col0 is not None
    nd = len(dsrcs)

    def body(*refs):
        d_refs, refs = refs[:3 * nd], refs[3 * nd:]
        if glu:
            vm, vp, vn, gm, gp, gn, w_ref = refs[:7]
            dv_ref, dg_ref, dw_ref, db_ref, dwin, dsh, uwin, ush, accw, accb = refs[7:]
        else:
            vm, vp, vn, w_ref = refs[:4]
            du_ref, dw_ref, db_ref, dwin, dsh, uwin, ush, accw, accb = refs[4:]
        i = pl.program_id(1)

        def dpiece(k):
            v = d_refs[k][...].astype(F32)
            for e in range(1, nd):
                v = v + d_refs[3 * e + k][...].astype(F32)
            return v

        def act(k):
            v = (vm, vp, vn)[k][...].astype(F32)
            return v * _sig((gm, gp, gn)[k][...].astype(F32)) if glu else v

        _fill_window(dwin, dpiece(1), dpiece(0), dpiece(2), i, nt, rows)
        _fill_window(uwin, act(1), act(0), act(2), i, nt, rows)
        if not direct:
            _fill_shifts(dwin, dsh, rs_d, n_u)
            _fill_shifts(uwin, ush, rs_u, n_u)

        @pl.when(i == 0)
        def _():
            accw[...] = jnp.zeros_like(accw)
            accb[...] = jnp.zeros_like(accb)

        def rowblk(r0):
            out_rows = pl.ds(r0, rb)
            for ls in _lane_slices(lanes, direct):
                du = _tap_sum(w_ref, taps_d, lambda ri, q, n: _tap_rows(dwin, dsh, rs_d, ri, q, r0, n, direct, ls),
                              rb, ls)
                if glu:
                    v = vm[out_rows, ls].astype(F32)
                    s = _sig(gm[out_rows, ls].astype(F32))
                    dv_ref[out_rows, ls] = (du * s).astype(BF16)
                    dg_ref[out_rows, ls] = (du * v * s * (1.0 - s)).astype(BF16)
                else:
                    du_ref[out_rows, ls] = du.astype(BF16)
                dmain = dwin[pl.ds(r0 + hb, rb) if direct else pl.ds(pl.multiple_of(r0 + hb, 8), rb), ls]
                accb[:, ls] += _rowsum8(dmain)
                for ri, q0, extra, members in _tap_groups(taps_u):
                    slab = _tap_rows(uwin, ush, rs_u, ri, q0, r0, rb + 8 * extra, direct, ls)
                    for k, dq in members:
                        accw[k, :, ls] += _rowsum8(dmain * slab[8 * dq:8 * dq + rb])

        _row_loop(rows // rb, rb, rowblk, direct)

        @pl.when(i == nt - 1)
        def _():
            for k in range(K):
                dw_ref[k:k + 1, :] = jnp.sum(accw[k], axis=0, keepdims=True)
            db_ref[...] = jnp.sum(accb[...], axis=0, keepdims=True)

    in_specs, args = [], []
    for d in dsrcs:
        in_specs += _halo_specs(S, rows, lanes, 0)
        args += [d, d, d]
    in_specs += _halo_specs(S, rows, lanes, col0)
    args += [src, src, src]
    if glu:
        in_specs += _halo_specs(S, rows, lanes, glu_col0)
        args += [src, src, src]
    in_specs += [pl.BlockSpec((K, lanes), lambda j, i: (0, j))]
    tile = pl.BlockSpec((rows, lanes), lambda j, i: (i, j))
    n_act = 2 if glu else 1
    out_shape = [SDS((S, C), BF16)] * n_act + [SDS((K, C), F32), SDS((1, C), F32)]
    out_specs = [tile] * n_act + [pl.BlockSpec((K, lanes), lambda j, i: (0, j)),
                                  pl.BlockSpec((1, lanes), lambda j, i: (0, j))]
    return _call(
        body, grid=(C // lanes, nt), out_shape=out_shape, in_specs=in_specs, out_specs=out_specs,
        scratch_shapes=[pltpu.VMEM((rows + 2 * hb, lanes), F32), pltpu.VMEM((len(rs_d), n_u, lanes), F32),
                        pltpu.VMEM((rows + 2 * hb, lanes), F32), pltpu.VMEM((len(rs_u), n_u, lanes), F32),
                        pltpu.VMEM((K, 8, lanes), F32), pltpu.VMEM((8, lanes), F32)],
        sem=("parallel", "arbitrary"), name=name, args=(*args, w), carry=carry)


def _ln_parts(u, ln_w, ln_b):
    mu = jnp.mean(u, axis=-1, keepdims=True)
    xc = u - mu
    rstd = lax.rsqrt(jnp.mean(xc * xc, axis=-1, keepdims=True) + EPS)
    un = xc * rstd
    return un, rstd, un * ln_w + ln_b


def _conf_out_fwd(uc, proj, h, ln_w, ln_b, wout, *, name):
    S, E = uc.shape
    D = h.shape[1]
    tm = min(GATE_TILE, S)

    def body(uc_ref, z_ref, h_ref, lw_ref, lb_ref, w_ref, o_ref):
        _, _, ul = _ln_parts(uc_ref[...].astype(F32), lw_ref[...], lb_ref[...])
        u3 = (_silu(ul) * _silu(z_ref[...].astype(F32))).astype(BF16)
        o_ref[...] = h_ref[...] + jnp.dot(u3, w_ref[...], preferred_element_type=F32)

    return pl.pallas_call(
        body, grid=(S // tm,), out_shape=SDS((S, D), F32),
        in_specs=[pl.BlockSpec((tm, E), lambda i: (i, 0)), pl.BlockSpec((tm, E), lambda i: (i, 2)),
                  pl.BlockSpec((tm, D), lambda i: (i, 0)), _resident((1, E)), _resident((1, E)), _resident((E, D))],
        out_specs=pl.BlockSpec((tm, D), lambda i: (i, 0)),
        compiler_params=_cparams(("parallel",)), name=name)(uc, proj, h, ln_w, ln_b, wout)


def _conf_out_bwd(dh, uc, proj, ln_w, ln_b, wout, *, name):
    S, E = uc.shape
    D = dh.shape[1]
    tm = min(GATE_TILE, S)
    nt = S // tm

    def body(dh_ref, uc_ref, z_ref, lw_ref, lb_ref, w_ref, dw_ref, duc_ref, dz_ref, dlw_ref, dlb_ref, alw, alb):
        i = pl.program_id(0)

        @pl.when(i == 0)
        def _():
            alw[...] = jnp.zeros_like(alw)
            alb[...] = jnp.zeros_like(alb)
            dw_ref[...] = jnp.zeros_like(dw_ref)

        un, rstd, ul = _ln_parts(uc_ref[...].astype(F32), lw_ref[...], lb_ref[...])
        z = z_ref[...].astype(F32)
        su, sz = _silu(ul), _silu(z)
        dhb = dh_ref[...].astype(BF16)
        dw_ref[...] += lax.dot_general((su * sz).astype(BF16), dhb, (((0,), (0,)), ((), ())),
                                       preferred_element_type=F32)
        du3 = lax.dot_general(dhb, w_ref[...], (((1,), (1,)), ((), ())), preferred_element_type=F32)
        dz_ref[...] = (du3 * su * _dsilu(z)).astype(BF16)
        dul = du3 * sz * _dsilu(ul)
        alw[...] += _rowsum8(dul * un)
        alb[...] += _rowsum8(dul)
        dun = dul * lw_ref[...]
        m1 = jnp.mean(dun, axis=-1, keepdims=True)
        m2 = jnp.mean(dun * un, axis=-1, keepdims=True)
        duc_ref[...] = (rstd * (dun - m1 - un * m2)).astype(BF16)

        @pl.when(i == nt - 1)
        def _():
            dlw_ref[...] = jnp.sum(alw[...], axis=0, keepdims=True)
            dlb_ref[...] = jnp.sum(alb[...], axis=0, keepdims=True)

    tile = pl.BlockSpec((tm, E), lambda i: (i, 0))
    vec = pl.BlockSpec((1, E), lambda i: (0, 0))
    return pl.pallas_call(
        body, grid=(nt,),
        out_shape=[SDS((E, D), F32)] + [SDS((S, E), BF16)] * 2 + [SDS((1, E), F32)] * 2,
        in_specs=[pl.BlockSpec((tm, D), lambda i: (i, 0)), tile, pl.BlockSpec((tm, E), lambda i: (i, 2)),
                  _resident((1, E)), _resident((1, E)), _resident((E, D))],
        out_specs=[pl.BlockSpec((E, D), lambda i: (0, 0)), tile, tile, vec, vec],
        scratch_shapes=[pltpu.VMEM((8, E), F32), pltpu.VMEM((8, E), F32)],
        compiler_params=_cparams(("arbitrary",)), name=name)(dh, uc, proj, ln_w, ln_b, wout)


def _inproj_bwd(parts, offs, w, h, nw, dh_in, *, wt, name):
    S, D = h.shape
    tm = min(ROW_TILE, S)
    nt = S // tm
    widths = [p.shape[1] for p in parts]
    npart = len(parts)

    def body(*refs):
        p_refs = refs[:npart]
        w_ref, h_ref, nw_ref, dh_ref, o_ref, hn_ref, dnw_ref, anw = refs[npart:]
        i = pl.program_id(0)

        @pl.when(i == 0)
        def _():
            anw[...] = jnp.zeros_like(anw)

        x = h_ref[...]
        r = lax.rsqrt(jnp.mean(x * x, axis=-1, keepdims=True) + EPS)
        n = x * r
        hn_ref[...] = (n * nw_ref[...]).astype(BF16)
        dhn = jnp.zeros((tm, D), F32)
        for p_ref, off, wd in zip(p_refs, offs, widths):
            for c0, c1 in _col_chunks(wd, 1024):
                if wt:
                    dhn = dhn + jnp.dot(p_ref[:, c0:c1], w_ref[off + c0:off + c1, :], preferred_element_type=F32)
                else:
                    dhn = dhn + lax.dot_general(p_ref[:, c0:c1], w_ref[:, off + c0:off + c1],
                                                (((1,), (1,)), ((), ())), preferred_element_type=F32)
        anw[...] += _rowsum8(dhn * n)
        dn = dhn * nw_ref[...]
        o_ref[...] = dh_ref[...] + r * (dn - n * jnp.mean(dn * n, axis=-1, keepdims=True))

        @pl.when(i == nt - 1)
        def _():
            dnw_ref[...] = jnp.sum(anw[...], axis=0, keepdims=True)

    tile = pl.BlockSpec((tm, D), lambda i: (i, 0))
    return pl.pallas_call(
        body, grid=(nt,),
        out_shape=[SDS((S, D), F32), SDS((S, D), BF16), SDS((1, D), F32)],
        in_specs=[pl.BlockSpec((tm, wd), lambda i: (i, 0)) for wd in widths]
        + [_resident(w.shape), tile, _resident((1, D)), tile],
        out_specs=[tile, tile, pl.BlockSpec((1, D), lambda i: (0, 0))],
        scratch_shapes=[pltpu.VMEM((8, D), F32)],
        compiler_params=_cparams(("arbitrary",)), name=name)(*parts, w, h, nw, dh_in)


def _tn_matmul(a, b, *, transpose_out, name):
    S, Ka = a.shape
    N = b.shape[1]
    tm = min(GRAD_TILE, S)
    bn = min(1024, N)
    nt = S // tm

    def body(a_ref, b_ref, o_ref, acc):
        i = pl.program_id(1)

        @pl.when(i == 0)
        def _():
            acc[...] = jnp.zeros_like(acc)

        acc[...] += lax.dot_general(a_ref[...], b_ref[...].astype(BF16), (((0,), (0,)), ((), ())),
                                    preferred_element_type=F32)

        @pl.when(i == nt - 1)
        def _():
            o_ref[...] = acc[...].T if transpose_out else acc[...]

    if transpose_out:
        out_shape, out_spec = SDS((N, Ka), F32), pl.BlockSpec((bn, Ka), lambda j, i: (j, 0))
    else:
        out_shape, out_spec = SDS((Ka, N), F32), pl.BlockSpec((Ka, bn), lambda j, i: (0, j))
    return pl.pallas_call(
        body, grid=(N // bn, nt), out_shape=out_shape,
        in_specs=[pl.BlockSpec((tm, Ka), lambda j, i: (i, 0)), pl.BlockSpec((tm, bn), lambda j, i: (i, j))],
        out_specs=out_spec, scratch_shapes=[pltpu.VMEM((Ka, bn), F32)],
        compiler_params=_cparams(("parallel", "arbitrary")), name=name)(a, b)


def _loss_head(h, fnw, target, *, name):
    S, D = h.shape
    tm = min(ROW_TILE, S)
    nt = S // tm

    def body(h_ref, w_ref, t_ref, loss_ref, dh_ref, dw_ref, aw, al):
        i = pl.program_id(0)

        @pl.when(i == 0)
        def _():
            aw[...] = jnp.zeros_like(aw)
            al[...] = jnp.zeros_like(al)

        x = h_ref[...]
        r = lax.rsqrt(jnp.mean(x * x, axis=-1, keepdims=True) + EPS)
        n = x * r
        err = n * w_ref[...] - t_ref[...]
        al[...] += _rowsum8(err * err)
        dy = err * (1.0 / D)
        aw[...] += _rowsum8(dy * n)
        dn = dy * w_ref[...]
        dh_ref[...] = r * (dn - n * jnp.mean(dn * n, axis=-1, keepdims=True))

        @pl.when(i == nt - 1)
        def _():
            dw_ref[...] = jnp.sum(aw[...], axis=0, keepdims=True)
            tot = jnp.sum(jnp.sum(al[...], axis=0, keepdims=True), axis=1, keepdims=True) * (0.5 / D)
            loss_ref[...] = jnp.zeros((8, 128), F32) + tot

    tile = pl.BlockSpec((tm, D), lambda i: (i, 0))
    loss, dh, dw = pl.pallas_call(
        body, grid=(nt,), out_shape=[SDS((8, 128), F32), SDS((S, D), F32), SDS((1, D), F32)],
        in_specs=[tile, _resident((1, D)), tile],
        out_specs=[pl.BlockSpec((8, 128), lambda i: (0, 0)), tile, pl.BlockSpec((1, D), lambda i: (0, 0))],
        scratch_shapes=[pltpu.VMEM((8, D), F32), pltpu.VMEM((8, D), F32)],
        compiler_params=_cparams(("arbitrary",)), name=name)(h, fnw, target)
    return loss[0, 0], dh, dw


def _head_mask(shape, head_axis):
    hd = lax.broadcasted_iota(jnp.int32, shape, head_axis)
    ch = lax.broadcasted_iota(jnp.int32, shape, 1 - head_axis)
    return ((ch >= hd * HEADDIM) & (ch < hd * HEADDIM + HEADDIM)).astype(BF16)


def _split_dot(vals, mat, fine):
    R = vals[0].shape[0]
    parts, where = [], []
    for v, f in zip(vals, fine):
        hi = v.astype(BF16)
        where.append((len(parts), f))
        parts += [hi, (v - hi.astype(F32)).astype(BF16)] if f else [hi]
    out = jnp.dot(jnp.concatenate(parts, axis=0), mat, preferred_element_type=F32)
    return [out[R * k:R * k + R] + out[R * k + R:R * k + 2 * R] if f else out[R * k:R * k + R] for k, f in where]


def _nt(a, b):
    return lax.dot_general(a, b, (((1,), (1,)), ((), ())), preferred_element_type=F32)


def _tn(a, b):
    return lax.dot_general(a, b, (((0,), (0,)), ((), ())), preferred_element_type=F32)


def _pair_blockdiag(v):
    lane = lax.broadcasted_iota(jnp.int32, v.shape, 1)
    zero = jnp.zeros_like(v)
    return jnp.concatenate([jnp.where(lane < HEADDIM, v, zero), jnp.where(lane >= HEADDIM, v, zero)], axis=0)


def _ssd_setup(c_val, dtraw, bias, alog, fwd):
    xbc = _silu(c_val.astype(F32))
    pre = dtraw + bias
    dt = _softplus(pre)
    A = -jnp.exp(alog)
    row = lax.broadcasted_iota(jnp.int32, (CHUNK, CHUNK), 0)
    col = lax.broadcasted_iota(jnp.int32, (CHUNK, CHUNK), 1)
    T = (col <= row) if fwd else (col >= row)
    cum = jnp.dot(T.astype(F32), dt * A, precision=lax.Precision.HIGHEST, preferred_element_type=F32)
    total = cum[CHUNK - 1:CHUNK] if fwd else cum[0:1]
    return dict(x=xbc[:, :D_INNER], B=xbc[:, D_INNER:D_INNER + GN].astype(BF16),
                C=xbc[:, D_INNER + GN:].astype(BF16), pre=pre, dt=dt, A=A, T=T, Tt=(col >= row) if fwd else (col <= row),
                cum=cum, e=jnp.exp(cum), dtds=dt * jnp.exp(total - cum))


def _ssd_fwd_dir(c_val, dtraw, bias, alog, state, y_ref, hin_ref, fwd, ex):
    s = _ssd_setup(c_val, dtraw, bias, alog, fwd)
    x, cum, T = s["x"], s["cum"], s["T"]
    e_x, dtds_x = _split_dot([s["e"], s["dtds"]], ex, [True, False])
    dec_x = e_x[CHUNK - 1:CHUNK] if fwd else e_x[0:1]
    cumT, dtT = cum.T, s["dt"].T
    xb = x.astype(BF16)
    for g in range(SSD_GROUPS):
        g0, g1 = g * HPG * HEADDIM, (g + 1) * HPG * HEADDIM
        Bg, Cg = s["B"][:, g * D_STATE:(g + 1) * D_STATE], s["C"][:, g * D_STATE:(g + 1) * D_STATE]
        CBg = _nt(Cg, Bg)
        Hg = state[:, g0:g1]
        Hb = Hg.astype(BF16)
        hin_ref[0, :, g0:g1] = Hb
        yoff = jnp.dot(Cg, Hb, preferred_element_type=F32) * e_x[:, g0:g1]
        ys = []
        for h in range(g * HPG, (g + 1) * HPG, 2):
            Ms = []
            for hh in (h, h + 1):
                Lh = jnp.exp(jnp.where(T, cum[:, hh:hh + 1] - cumT[hh:hh + 1, :], NEG))
                Ms.append((CBg * Lh * dtT[hh:hh + 1, :]).astype(BF16))
            ys.append(jnp.dot(jnp.concatenate(Ms, axis=1), _pair_blockdiag(xb[:, h * HEADDIM:(h + 2) * HEADDIM]),
                              preferred_element_type=F32))
        y_ref[:, g0:g1] = jnp.concatenate(ys, axis=1) + yoff
        xds = (x[:, g0:g1] * dtds_x[:, g0:g1]).astype(BF16)
        state[:, g0:g1] = Hg * dec_x[:, g0:g1] + _tn(Bg, xds)


def _ssd_scan_fwd(cact, tail, bias, alog, *, name):
    S = cact.shape[0]
    nc = S // CHUNK
    E = D_INNER

    def body(cf_ref, cb_ref, tf_ref, tb_ref, bias_ref, alog_ref, yf_ref, yb_ref, hf_ref, hb_ref, state):
        @pl.when(pl.program_id(0) == 0)
        def _():
            state[...] = jnp.zeros_like(state)

        ex = _head_mask((SSD_HEADS, D_INNER), 0)
        H = SSD_HEADS
        _ssd_fwd_dir(cf_ref[...], tf_ref[:, 0:H], bias_ref[0:1, :], alog_ref[0:1, :], state.at[0], yf_ref, hf_ref, True, ex)
        _ssd_fwd_dir(cb_ref[...], tb_ref[:, H:2 * H], bias_ref[1:2, :], alog_ref[1:2, :], state.at[1], yb_ref, hb_ref, False, ex)

    up, down = (lambda i: (i, 0)), (lambda i: (nc - 1 - i, 0))
    up3, down3 = (lambda i: (i, 0, 0)), (lambda i: (nc - 1 - i, 0, 0))
    W = cact.shape[1]
    return pl.pallas_call(
        body, grid=(nc,),
        out_shape=[SDS((S, E), F32), SDS((S, E), F32), SDS((nc, D_STATE, E), BF16), SDS((nc, D_STATE, E), BF16)],
        in_specs=[pl.BlockSpec((CHUNK, W), up), pl.BlockSpec((CHUNK, W), down),
                  pl.BlockSpec((CHUNK, 128), up), pl.BlockSpec((CHUNK, 128), down),
                  _resident((2, SSD_HEADS)), _resident((2, SSD_HEADS))],
        out_specs=[pl.BlockSpec((CHUNK, E), up), pl.BlockSpec((CHUNK, E), down),
                   pl.BlockSpec((1, D_STATE, E), up3), pl.BlockSpec((1, D_STATE, E), down3)],
        scratch_shapes=[pltpu.VMEM((2, D_STATE, E), F32)],
        compiler_params=_cparams(("arbitrary",)), name=name)(cact, cact, tail, tail, bias, alog)


def _ssd_bwd_dir(c_val, dtraw, bias, alog, dy, hin_ref, dskip_x, G, dc_ref, ddt_ref, accA, accb, fwd, ex, ext):
    s = _ssd_setup(c_val, dtraw, bias, alog, fwd)
    x, cum, T, dt = s["x"], s["cum"], s["T"], s["dt"]
    e_x, dtds_x = _split_dot([s["e"], s["dtds"]], ex, [True, False])
    dec_x = e_x[CHUNK - 1:CHUNK] if fwd else e_x[0:1]
    cumT, dtT = cum.T, dt.T
    xb = x.astype(BF16)
    dyf = dy.astype(F32)
    dx_diag, dx_st, dBs, dCs, gh, yoff_dy = [], [], [], [], [], []
    lane_h = lax.broadcasted_iota(jnp.int32, (CHUNK, SSD_HEADS), 1)
    sub_h = lax.broadcasted_iota(jnp.int32, (SSD_HEADS, CHUNK), 0)
    rows = jnp.zeros((CHUNK, SSD_HEADS), F32)
    colsT = jnp.zeros((SSD_HEADS, CHUNK), F32)
    for g in range(SSD_GROUPS):
        g0, g1 = g * HPG * HEADDIM, (g + 1) * HPG * HEADDIM
        Bg, Cg = s["B"][:, g * D_STATE:(g + 1) * D_STATE], s["C"][:, g * D_STATE:(g + 1) * D_STATE]
        CBg = _nt(Cg, Bg)
        Hb = hin_ref[0, :, g0:g1]
        Gg = G[:, g0:g1]
        Gb = Gg.astype(BF16)
        edyf = dyf[:, g0:g1] * e_x[:, g0:g1]
        edy = edyf.astype(BF16)
        yoff_dy.append(edyf * jnp.dot(Cg, Hb, preferred_element_type=F32))
        dC_g = _nt(edy, Hb)
        dH_g = _tn(Cg, edy)
        xds = (x[:, g0:g1] * dtds_x[:, g0:g1]).astype(BF16)
        dB_g = _nt(xds, Gb)
        dx_st.append(dtds_x[:, g0:g1] * jnp.dot(Bg, Gb, preferred_element_type=F32))
        dCB = jnp.zeros((CHUNK, CHUNK), F32)
        for h in range(g * HPG, (g + 1) * HPG, 2):
            ps = slice(h * HEADDIM, (h + 2) * HEADDIM)
            dM2 = _nt(dy[:, ps], _pair_blockdiag(xb[:, ps]))
            Ms = []
            for j, hh in enumerate((h, h + 1)):
                Lh = jnp.exp(jnp.where(T, cum[:, hh:hh + 1] - cumT[hh:hh + 1, :], NEG))
                Ms.append((CBg * Lh * dtT[hh:hh + 1, :]).astype(BF16))
                term = dM2[:, j * CHUNK:(j + 1) * CHUNK] * dtT[hh:hh + 1, :] * Lh
                dCB = dCB + term
                Wh = term * CBg
                rows = rows + jnp.where(lane_h == hh, jnp.sum(Wh, axis=1, keepdims=True), 0.0)
                colsT = colsT + jnp.where(sub_h == hh, jnp.sum(Wh, axis=0, keepdims=True), 0.0)
            dx_diag.append(_tn(jnp.concatenate(Ms, axis=0), _pair_blockdiag(dy[:, ps])))
        dCBb = dCB.astype(BF16)
        dCs.append(dC_g + jnp.dot(dCBb, Bg, preferred_element_type=F32))
        dBs.append(dB_g + _tn(dCBb, Cg))
        gh.append(_rowsum8(Gg * Hb.astype(F32)))
        G[:, g0:g1] = Gg * dec_x[:, g0:g1] + dH_g
    dxn = jnp.concatenate(dx_diag, axis=1) + jnp.concatenate(dx_st, axis=1)
    xst = x * jnp.concatenate(dx_st, axis=1)
    RZ, Zx = _split_dot([jnp.concatenate(yoff_dy, axis=1) - xst, x * dxn], ext, [True, False])
    ddec, zs_tot = _split_dot([jnp.concatenate(gh, axis=1), _rowsum8(xst)], ext, [True, True])
    dec = s["e"][CHUNK - 1:CHUNK] if fwd else s["e"][0:1]
    dtotal = jnp.sum(zs_tot + dec * ddec, axis=0, keepdims=True)
    rowi = lax.broadcasted_iota(jnp.int32, (CHUNK, SSD_HEADS), 0)
    dacs = rows - colsT.T + RZ + jnp.where(rowi == (CHUNK - 1 if fwd else 0), dtotal, 0.0)
    da = jnp.dot(s["Tt"].astype(F32), dacs, precision=lax.Precision.HIGHEST, preferred_element_type=F32)
    accA[...] += _rowsum8(da * dt)
    sg = _sig(s["pre"])
    ratio = jnp.where(dt > 1e-30, sg / jnp.maximum(dt, 1e-30), 1.0)
    ddtraw = da * s["A"] * sg + Zx * ratio
    ddt_ref[...] = ddtraw
    accb[...] += _rowsum8(ddtraw)
    dx = dxn + dyf * dskip_x if fwd else dxn
    dact = jnp.concatenate([dx] + dBs + dCs, axis=1)
    dc_ref[...] = (dact * _dsilu(c_val.astype(F32))).astype(BF16)


def _ssd_scan_bwd(cact, tail, bias, alog, dy, hf, hb, dskip_x, *, name, carry=None):
    S = cact.shape[0]
    nc = S // CHUNK
    E, H, W = D_INNER, SSD_HEADS, cact.shape[1]

    def body(cf_ref, cb_ref, tf_ref, tb_ref, bias_ref, alog_ref, dyf_ref, dyb_ref, hf_ref, hb_ref,
             dsk_ref, dcf_ref, dcb_ref, ddtf_ref, ddtb_ref, dalog_ref, dbias_ref, G, accA, accb):
        i = pl.program_id(0)

        @pl.when(i == 0)
        def _():
            G[...] = jnp.zeros_like(G)
            accA[...] = jnp.zeros_like(accA)
            accb[...] = jnp.zeros_like(accb)

        ex = _head_mask((H, E), 0)
        ext = _head_mask((E, H), 1)
        _ssd_bwd_dir(cf_ref[...], tf_ref[:, 0:H], bias_ref[0:1, :], alog_ref[0:1, :], dyf_ref[...], hf_ref,
                     dsk_ref[...], G.at[0], dcf_ref, ddtf_ref, accA.at[0], accb.at[0], True, ex, ext)
        _ssd_bwd_dir(cb_ref[...], tb_ref[:, H:2 * H], bias_ref[1:2, :], alog_ref[1:2, :], dyb_ref[...], hb_ref,
                     dsk_ref[...], G.at[1], dcb_ref, ddtb_ref, accA.at[1], accb.at[1], False, ex, ext)

        @pl.when(i == nc - 1)
        def _():
            for d in range(2):
                A = -jnp.exp(alog_ref[d:d + 1, :])
                dalog_ref[d:d + 1, :] = jnp.sum(accA[d], axis=0, keepdims=True) * A
                dbias_ref[d:d + 1, :] = jnp.sum(accb[d], axis=0, keepdims=True)

    up, down = (lambda i: (i, 0)), (lambda i: (nc - 1 - i, 0))
    up3, down3 = (lambda i: (i, 0, 0)), (lambda i: (nc - 1 - i, 0, 0))
    small = pl.BlockSpec((2, H), lambda i: (0, 0))
    return _call(
        body, grid=(nc,),
        out_shape=[SDS((S, W), BF16), SDS((S, W), BF16), SDS((S, H), F32), SDS((S, H), F32),
                   SDS((2, H), F32), SDS((2, H), F32)],
        in_specs=[pl.BlockSpec((CHUNK, W), down), pl.BlockSpec((CHUNK, W), up),
                  pl.BlockSpec((CHUNK, 128), down), pl.BlockSpec((CHUNK, 128), up),
                  _resident((2, H)), _resident((2, H)),
                  pl.BlockSpec((CHUNK, E), down), pl.BlockSpec((CHUNK, E), up),
                  pl.BlockSpec((1, D_STATE, E), down3), pl.BlockSpec((1, D_STATE, E), up3),
                  _resident((1, E))],
        out_specs=[pl.BlockSpec((CHUNK, W), down), pl.BlockSpec((CHUNK, W), up),
                   pl.BlockSpec((CHUNK, H), down), pl.BlockSpec((CHUNK, H), up), small, small],
        scratch_shapes=[pltpu.VMEM((2, D_STATE, E), F32), pltpu.VMEM((2, 8, H), F32), pltpu.VMEM((2, 8, H), F32)],
        sem=("arbitrary",), name=name, args=(cact, cact, tail, tail, bias, alog, dy, dy, hf, hb, dskip_x), carry=carry)


def _ssd_gate(yf, yb, cx, z, dsk, nw):
    xs = _silu(cx.astype(F32))
    y = yf + yb + xs * dsk
    sz = _silu(z)
    gt = y * sz
    r = lax.rsqrt(jnp.mean(gt * gt, axis=-1, keepdims=True) + EPS)
    return xs, y, sz, r, gt * r


def _ssd_out_fwd(yf, yb, cact, proj, dskip_x, nw, h, wout, *, name):
    S, E = yf.shape
    D = h.shape[1]
    tm = min(GATE_TILE, S)

    def body(yf_ref, yb_ref, cx_ref, z_ref, dsk_ref, nw_ref, h_ref, w_ref, o_ref):
        _, _, _, _, n = _ssd_gate(yf_ref[...], yb_ref[...], cx_ref[...], z_ref[...].astype(F32), dsk_ref[...], nw_ref[...])
        o_ref[...] = h_ref[...] + jnp.dot((n * nw_ref[...]).astype(BF16), w_ref[...], preferred_element_type=F32)

    tile = pl.BlockSpec((tm, E), lambda i: (i, 0))
    return pl.pallas_call(
        body, grid=(S // tm,), out_shape=SDS((S, D), F32),
        in_specs=[tile, tile, tile, tile, _resident((1, E)), _resident((1, E)),
                  pl.BlockSpec((tm, D), lambda i: (i, 0)), _resident((E, D))],
        out_specs=pl.BlockSpec((tm, D), lambda i: (i, 0)),
        compiler_params=_cparams(("parallel",)), name=name)(yf, yb, cact, proj, dskip_x, nw, h, wout)


def _ssd_out_bwd(dh, yf, yb, cact, proj, dskip_x, nw, wout, *, name):
    S, E = yf.shape
    D = dh.shape[1]
    tm = min(GATE_TILE, S)
    nt = S // tm

    def body(dh_ref, yf_ref, yb_ref, cx_ref, z_ref, dsk_ref, nw_ref, w_ref, dw_ref, dy_ref, dz_ref, dnw_ref, dd_ref, anw, ad):
        i = pl.program_id(0)

        @pl.when(i == 0)
        def _():
            anw[...] = jnp.zeros_like(anw)
            ad[...] = jnp.zeros_like(ad)
            dw_ref[...] = jnp.zeros_like(dw_ref)

        z = z_ref[...].astype(F32)
        xs, y, sz, r, n = _ssd_gate(yf_ref[...], yb_ref[...], cx_ref[...], z, dsk_ref[...], nw_ref[...])
        dhb = dh_ref[...].astype(BF16)
        dw_ref[...] += _tn((n * nw_ref[...]).astype(BF16), dhb)
        dyn = _nt(dhb, w_ref[...])
        anw[...] += _rowsum8(dyn * n)
        dn = dyn * nw_ref[...]
        dgt = r * (dn - n * jnp.mean(dn * n, axis=-1, keepdims=True))
        dy = dgt * sz
        dy_ref[...] = dy.astype(BF16)
        dz_ref[...] = (dgt * y * _dsilu(z)).astype(BF16)
        ad[...] += _rowsum8(dy * xs)

        @pl.when(i == nt - 1)
        def _():
            dnw_ref[...] = jnp.sum(anw[...], axis=0, keepdims=True)
            dd_ref[...] = jnp.sum(ad[...], axis=0, keepdims=True)

    tile = pl.BlockSpec((tm, E), lambda i: (i, 0))
    vec = pl.BlockSpec((1, E), lambda i: (0, 0))
    return pl.pallas_call(
        body, grid=(nt,), out_shape=[SDS((E, D), F32)] + [SDS((S, E), BF16)] * 2 + [SDS((1, E), F32)] * 2,
        in_specs=[pl.BlockSpec((tm, D), lambda i: (i, 0)), tile, tile, tile, tile,
                  _resident((1, E)), _resident((1, E)), _resident((E, D))],
        out_specs=[pl.BlockSpec((E, D), lambda i: (0, 0)), tile, tile, vec, vec],
        scratch_shapes=[pltpu.VMEM((8, E), F32), pltpu.VMEM((8, E), F32)],
        compiler_params=_cparams(("arbitrary",)), name=name)(dh, yf, yb, cact, proj, dskip_x, nw, wout)


def _local_step(x, target, p, hooks):
    E, S = D_INNER, x.shape[0]
    depth = p["norm_w"].shape[0]
    names = ["norm_w", "cm_w_in", "cm_dw_w", "cm_dw_b", "cm_ln_w", "cm_ln_b", "cm_w_out", "ssd_w_in", "ssd_conv_w",
             "ssd_conv_b", "ssd_dt_bias", "ssd_A_log", "ssd_D", "ssd_norm_w", "ssd_w_out"]
    g = {n: [None] * (depth if n == "norm_w" else depth // 2) for n in names}

    def carried(key):
        items, done = hooks.get(key, (None, None))
        return (items(g) if items else None), (done if done else lambda got: None)

    h, saved = x, []
    for i in range(depth):
        j, nw = i // 2, p["norm_w"][i][None]
        if i % 2 == 0:
            carry, done = carried(("cm_in_fwd", i))
            proj, got = _inproj_fwd(h, nw, p["cm_w_in"][j], tail=False, wt=False, name="cm_in_fwd", carry=carry)
            done(got)
            carry, done = carried(("cm_conv_fwd", i))
            uc, got = _dwconv_fwd(proj, p["cm_dw_w"][j], p["cm_dw_b"][j][None], col0=0, glu_col0=E, C=E,
                                  name="cm_conv_fwd", carry=carry)
            done(got)
            hn = _conf_out_fwd(uc, proj, h, p["cm_ln_w"][j][None], p["cm_ln_b"][j][None], p["cm_w_out"][j], name="cm_out_fwd")
            saved.append((h, proj, uc))
        else:
            (proj, tail), _ = _inproj_fwd(h, nw, p["ssd_w_in"][j], tail=True, wt=True, name="ssd_in_fwd")
            cact, _ = _dwconv_fwd(proj, p["ssd_conv_w"][j], p["ssd_conv_b"][j][None], col0=E, glu_col0=None,
                                  C=SSD_CONV_DIM, name="ssd_conv_fwd")
            yf, yb, hf, hb = _ssd_scan_fwd(cact, tail, p["ssd_dt_bias"][j], p["ssd_A_log"][j], name="ssd_scan_fwd")
            dsk = jnp.repeat(p["ssd_D"][j], HEADDIM)[None]
            hn = _ssd_out_fwd(yf, yb, cact, proj, dsk, p["ssd_norm_w"][j][None], h, p["ssd_w_out"][j], name="ssd_out_fwd")
            saved.append((h, proj, tail, cact, yf, yb, hf, hb, dsk))
        h = hn
    loss, dh, g_fnw = _loss_head(h, p["final_norm_w"][None], target, name="loss_head")

    for i in reversed(range(depth)):
        j, nw = i // 2, p["norm_w"][i][None]
        if i % 2 == 0:
            h_in, proj, uc = saved[i]
            g["cm_w_out"][j], duc, dz, g_lw, g_lb = _conf_out_bwd(
                dh, uc, proj, p["cm_ln_w"][j][None], p["cm_ln_b"][j][None], p["cm_w_out"][j], name="cm_out_bwd")
            carry, done = carried(("cm_conv_bwd", i))
            (dv, dg, g_dw, g_db), got = _dwconv_bwd([duc], proj, p["cm_dw_w"][j], col0=0, glu_col0=E, C=E,
                                                    name="cm_conv_bwd", carry=carry)
            done(got)
            parts = [dv, dg, dz]
            dh, hnb, g_nw = _inproj_bwd(parts, [0, E, 2 * E], p["cm_w_in"][j], h_in, nw, dh, wt=False, name="cm_in_bwd")
            g["cm_w_in"][j] = jnp.concatenate(
                [_tn_matmul(hnb, q, transpose_out=False, name="cm_win_grad") for q in parts], axis=1)
            g["cm_dw_w"][j], g["cm_dw_b"][j], g["cm_ln_w"][j], g["cm_ln_b"][j] = g_dw, g_db[0], g_lw[0], g_lb[0]
        else:
            h_in, proj, tail, cact, yf, yb, hf, hb, dsk = saved[i]
            g["ssd_w_out"][j], dy, dz, g_sn, g_dx = _ssd_out_bwd(
                dh, yf, yb, cact, proj, dsk, p["ssd_norm_w"][j][None], p["ssd_w_out"][j], name="ssd_out_bwd")
            carry, done = carried(("ssd_scan_bwd", i))
            (dcf, dcb, ddtf, ddtb, g_al, g_bias), got = _ssd_scan_bwd(
                cact, tail, p["ssd_dt_bias"][j], p["ssd_A_log"][j], dy, hf, hb, dsk, name="ssd_scan_bwd", carry=carry)
            done(got)
            (dxbc, g_cw, g_cb), _ = _dwconv_bwd([dcf, dcb], proj, p["ssd_conv_w"][j], col0=E, glu_col0=None,
                                                C=SSD_CONV_DIM, name="ssd_conv_bwd")
            ddt = jnp.concatenate([ddtf, ddtb, jnp.zeros((S, 128 - 2 * SSD_HEADS), F32)], axis=1).astype(BF16)
            parts = [dz, dxbc, ddt]
            dh, hnb, g_nw = _inproj_bwd(parts, [0, E, E + SSD_CONV_DIM], p["ssd_w_in"][j], h_in, nw, dh, wt=True,
                                        name="ssd_in_bwd")
            gw = [_tn_matmul(hnb, q, transpose_out=True, name="ssd_win_grad") for q in parts]
            g["ssd_w_in"][j] = jnp.concatenate([gw[0], gw[1], gw[2][:2 * SSD_HEADS]], axis=0)
            g["ssd_conv_w"][j], g["ssd_conv_b"][j], g["ssd_norm_w"][j] = g_cw, g_cb[0], g_sn[0]
            g["ssd_dt_bias"][j], g["ssd_A_log"][j] = g_bias, g_al
            g["ssd_D"][j] = jnp.sum(g_dx.reshape(SSD_HEADS, HEADDIM), axis=1)
        g["norm_w"][i] = g_nw[0]
    g["final_norm_w"] = g_fnw[0]
    return loss, dh, g


_FLIPS = [(fx, fy, fc) for fx in (0, 1) for fy in (0, 1) for fc in (0, 1)][1:]


def _exchange_copies(items, ins, outs, send_sems, recv_sems, local_sems, arrivals):
    x, y, c = lax.axis_index("x"), lax.axis_index("y"), lax.axis_index("c")
    me = 4 * x + 2 * y + c
    local, pairs = [], []
    for b, (_, scatter) in enumerate(items):
        local.append(pltpu.make_async_copy(ins[b].at[me] if scatter else ins[b], outs[b].at[me], local_sems.at[b]))
        for k, (fx, fy, fc) in enumerate(_FLIPS):
            px, py, pc = x ^ fx, y ^ fy, c ^ fc
            peer = 4 * px + 2 * py + pc
            src = ins[b].at[peer] if scatter else ins[b]
            sems = dict(send_sem=send_sems.at[b * 7 + k], recv_sem=recv_sems.at[b * 7 + k],
                        device_id=(px, py, pc), device_id_type=pl.DeviceIdType.MESH)
            pairs.append((pltpu.make_async_remote_copy(src_ref=src, dst_ref=outs[b].at[me], **sems),
                          pltpu.make_async_remote_copy(src_ref=src, dst_ref=outs[b].at[peer], **sems)
                          if arrivals else None))
    return local, pairs


def _exchange_start(items, ins, outs, *sems):
    local, pairs = _exchange_copies(items, ins, outs, *sems, arrivals=False)
    for cp in local:
        cp.start()
    for cp, _ in pairs:
        cp.start()


def _exchange_wait(items, ins, outs, *sems):
    local, pairs = _exchange_copies(items, ins, outs, *sems, arrivals=True)
    for _, arrival in pairs:
        arrival.wait_recv()
    for cp, _ in pairs:
        cp.wait_send()
    for cp in local:
        cp.wait()


def _exchange_shapes(items):
    n = len(items)
    out_shape = [SDS((N_DEV,) + tuple(a.shape[-2:]), a.dtype) for a, _ in items]
    sems = [pltpu.SemaphoreType.DMA((7 * n,)), pltpu.SemaphoreType.DMA((7 * n,)), pltpu.SemaphoreType.DMA((n,))]
    return out_shape, sems


def _exchange(items, *, name):
    n = len(items)

    def body(*refs):
        _exchange_start(items, refs[:n], refs[n:2 * n], *refs[2 * n:])
        _exchange_wait(items, refs[:n], refs[n:2 * n], *refs[2 * n:])

    hbm = pl.BlockSpec(memory_space=pltpu.HBM)
    out_shape, sems = _exchange_shapes(items)
    return pl.pallas_call(body, out_shape=out_shape, in_specs=[hbm] * n, out_specs=[hbm] * n,
                          scratch_shapes=sems, name=name)(*[a for a, _ in items])


def _call(body, *, grid, out_shape, in_specs, out_specs, scratch_shapes, sem, name, args, carry):
    if not carry:
        return pl.pallas_call(body, grid=grid, out_shape=out_shape, in_specs=in_specs, out_specs=out_specs,
                              scratch_shapes=scratch_shapes, compiler_params=_cparams(sem), name=name)(*args), None
    n, n_in, n_out, n_scr = len(carry), len(in_specs), len(out_specs), len(scratch_shapes)
    last_step = tuple(g - 1 for g in grid)

    def wrapped(*refs):
        ins, cin = refs[:n_in], refs[n_in:n_in + n]
        outs, cout = refs[n_in + n:n_in + n + n_out], refs[n_in + n + n_out:n_in + 2 * n + n_out]
        scr, sems = refs[n_in + 2 * n + n_out:n_in + 2 * n + n_out + n_scr], refs[n_in + 2 * n + n_out + n_scr:]
        first = last = None
        for d, top in enumerate(last_step):
            i = pl.program_id(d)
            first = (i == 0) if first is None else first & (i == 0)
            last = (i == top) if last is None else last & (i == top)

        @pl.when(first)
        def _():
            _exchange_start(carry, cin, cout, *sems)

        body(*ins, *outs, *scr)

        @pl.when(last)
        def _():
            _exchange_wait(carry, cin, cout, *sems)

    hbm = pl.BlockSpec(memory_space=pltpu.HBM)
    x_shape, x_sems = _exchange_shapes(carry)
    res = pl.pallas_call(
        wrapped, grid=grid, out_shape=list(out_shape) + x_shape, in_specs=list(in_specs) + [hbm] * n,
        out_specs=list(out_specs) + [hbm] * n, scratch_shapes=list(scratch_shapes) + x_sems,
        compiler_params=_cparams(("arbitrary",) * len(grid)), name=name)(*args, *[a for a, _ in carry])
    return res[:n_out], res[n_out:]


def _reduce_slots(recv, *, rows, name):
    n, R, C = recv.shape

    def body(r_ref, g_ref):
        g = r_ref[0].astype(F32)
        for s in range(1, n):
            g = g + r_ref[s].astype(F32)
        g_ref[...] = g

    return pl.pallas_call(
        body, grid=(R // rows,), out_shape=SDS((R, C), F32),
        in_specs=[pl.BlockSpec((n, rows, C), lambda i: (0, i, 0))],
        out_specs=pl.BlockSpec((rows, C), lambda i: (i, 0)),
        compiler_params=_cparams(("parallel",)), name=name)(recv)


def _reduce_adamw(recv, w, m, v, *, layer, rows, name):
    _, R, C = w.shape
    n = recv.shape[0]
    c1 = 1.0 / (1.0 - ADAM_B1 ** ADAM_STEP)
    c2 = 1.0 / (1.0 - ADAM_B2 ** ADAM_STEP)

    def body(r_ref, w_ref, m_ref, v_ref, g_ref, d_ref, mo_ref, vo_ref):
        g = r_ref[0].astype(F32)
        for s in range(1, n):
            g = g + r_ref[s].astype(F32)
        mn = ADAM_B1 * m_ref[...] + (1.0 - ADAM_B1) * g
        vn = ADAM_B2 * v_ref[...] + (1.0 - ADAM_B2) * (g * g)
        g_ref[...] = g
        mo_ref[...] = mn
        vo_ref[...] = vn
        d_ref[...] = -ADAM_LR * ((mn * c1) / (jnp.sqrt(vn * c2) + ADAM_EPS) + ADAM_WD * w_ref[...])

    tile = pl.BlockSpec((rows, C), lambda i: (i, 0))
    mine = pl.BlockSpec((None, rows, C), lambda i: (layer, i, 0))
    return pl.pallas_call(
        body, grid=(R // rows,), out_shape=[SDS((R, C), F32)] * 4,
        in_specs=[pl.BlockSpec((n, rows, C), lambda i: (0, i, 0)), mine, mine, mine],
        out_specs=[tile] * 4, compiler_params=_cparams(("parallel",)), name=name)(recv, w, m, v)


_ORDER = ["norm_w", "final_norm_w", "cm_w_in", "cm_dw_w", "cm_dw_b", "cm_ln_w", "cm_ln_b", "cm_w_out", "ssd_w_in",
          "ssd_conv_w", "ssd_conv_b", "ssd_dt_bias", "ssd_A_log", "ssd_D", "ssd_norm_w", "ssd_w_out"]
_SMALL = (["cm_dw_w", "ssd_conv_w", "ssd_conv_b", "ssd_norm_w"], 168)
_REP = (["norm_w", "final_norm_w", "cm_dw_b", "cm_ln_w", "cm_ln_b", "ssd_dt_bias", "ssd_A_log", "ssd_D"], 144)
_SHARD_AXIS = {"cm_w_in": 2, "cm_dw_w": 2, "cm_w_out": 1, "ssd_w_in": 2, "ssd_conv_w": 2, "ssd_conv_b": 1,
               "ssd_norm_w": 1, "ssd_w_out": 1}
OPT_ROWS = 256


def _pack(arrs, rows, lanes, dtype, lead=()):
    nl = len(lead)
    flat = jnp.concatenate([a.reshape(lead + (-1,)).astype(dtype) for a in arrs], axis=nl)
    flat = jnp.pad(flat, [(0, 0)] * nl + [(0, rows * lanes - flat.shape[nl])])
    return flat.reshape(lead + (rows, lanes))


def _unpack(buf, shapes, lead=()):
    nl = len(lead)
    flat = buf.reshape(lead + (-1,))
    out, off = [], 0
    for s in shapes:
        sz = math.prod(s)
        out.append(flat[..., off:off + sz].reshape(lead + tuple(s)))
        off += sz
    return out


def _join_shards(parts, axis):
    full = jnp.moveaxis(parts, 0, axis)
    sh = full.shape
    return full.reshape(sh[:axis] + (sh[axis] * sh[axis + 1],) + sh[axis + 2:])


def _split_shards(full, axis):
    sh = full.shape
    parts = full.reshape(sh[:axis] + (N_DEV, sh[axis] // N_DEV) + sh[axis + 1:])
    return jnp.moveaxis(parts, axis, 0)


def kernel(x, norm_w, final_norm_w, cm_w_in, cm_dw_w, cm_dw_b, cm_ln_w, cm_ln_b, cm_w_out, ssd_w_in, ssd_conv_w, ssd_conv_b, ssd_dt_bias, ssd_A_log, ssd_D, ssd_norm_w, ssd_w_out, loss_target, m_norm_w, m_final_norm_w, m_cm_w_in, m_cm_dw_w, m_cm_dw_b, m_cm_ln_w, m_cm_ln_b, m_cm_w_out, m_ssd_w_in, m_ssd_conv_w, m_ssd_conv_b, m_ssd_dt_bias, m_ssd_A_log, m_ssd_D, m_ssd_norm_w, m_ssd_w_out, v_norm_w, v_final_norm_w, v_cm_w_in, v_cm_dw_w, v_cm_dw_b, v_cm_ln_w, v_cm_ln_b, v_cm_w_out, v_ssd_w_in, v_ssd_conv_w, v_ssd_conv_b, v_ssd_dt_bias, v_ssd_A_log, v_ssd_D, v_ssd_norm_w, v_ssd_w_out):
    w = dict(zip(_ORDER, (norm_w, final_norm_w, cm_w_in, cm_dw_w, cm_dw_b, cm_ln_w, cm_ln_b, cm_w_out, ssd_w_in,
                          ssd_conv_w, ssd_conv_b, ssd_dt_bias, ssd_A_log, ssd_D, ssd_norm_w, ssd_w_out)))
    m = dict(zip(_ORDER, (m_norm_w, m_final_norm_w, m_cm_w_in, m_cm_dw_w, m_cm_dw_b, m_cm_ln_w, m_cm_ln_b, m_cm_w_out,
                          m_ssd_w_in, m_ssd_conv_w, m_ssd_conv_b, m_ssd_dt_bias, m_ssd_A_log, m_ssd_D, m_ssd_norm_w,
                          m_ssd_w_out)))
    v = dict(zip(_ORDER, (v_norm_w, v_final_norm_w, v_cm_w_in, v_cm_dw_w, v_cm_dw_b, v_cm_ln_w, v_cm_ln_b, v_cm_w_out,
                          v_ssd_w_in, v_ssd_conv_w, v_ssd_conv_b, v_ssd_dt_bias, v_ssd_A_log, v_ssd_D, v_ssd_norm_w,
                          v_ssd_w_out)))

    D, nl = D_MODEL, cm_w_in.shape[0]
    (sn, sr), (rn, rr) = _SMALL, _REP
    mats = ["cm_w_in", "cm_w_out", "ssd_w_in", "ssd_w_out"]
    bf = lambda a: a.astype(BF16)
    wt_in = jnp.swapaxes(ssd_w_in, 1, 2)
    t_rows = wt_in.shape[1]
    t_pad = -t_rows % 16

    p = {k: w[k] for k in rn}
    p.update({k: [None] * nl for k in mats})
    first = _exchange([(bf(cm_w_in[0]), False), (_pack([w[k] for k in sn], sr, 128, F32), False)], name="gather_first")
    p["cm_w_in"][0] = _join_shards(first[0], 1)
    for k, parts in zip(sn, _unpack(first[1], [w[k].shape for k in sn], lead=(N_DEV,))):
        p[k] = _join_shards(parts, _SHARD_AXIS[k])

    def rest_in(got):
        p["cm_w_out"][0] = got[0].reshape(-1, D)

    def rest_conv(got):
        p["cm_w_in"][1] = _join_shards(got[0], 1)
        p["cm_w_out"][1] = got[1].reshape(-1, D)
        wt = _join_shards(got[2].astype(F32).reshape(N_DEV, nl, -1, D), 1)
        wt = bf(jnp.pad(wt, ((0, 0), (0, SSD_IN_PAD - SSD_IN), (0, 0))))
        wo = got[3].reshape(N_DEV, nl, -1, D)
        for j in range(nl):
            p["ssd_w_in"][j] = wt[j]
            p["ssd_w_out"][j] = wo[:, j].reshape(-1, D)

    hooks = {("cm_in_fwd", 0): (lambda g: [(bf(cm_w_out[0]), False)], rest_in),
             ("cm_conv_fwd", 0): (lambda g: [(bf(cm_w_in[1]), False), (bf(cm_w_out[1]), False),
                                             (bf(wt_in.reshape(-1, D)), False), (bf(ssd_w_out.reshape(-1, D)), False)],
                                  rest_conv)}

    recv = {}

    def blocks(name, j, g):
        if name == "cm_w_in":
            return bf(_split_shards(g[name][j], 1))
        b = _split_shards(g[name][j], 0)
        if name == "ssd_w_in":
            b = jnp.pad(b, ((0, 0), (0, t_pad), (0, 0)))
        return bf(b)

    def sender(keys):
        def items(g):
            return [(blocks(n, j, g), True) for n, j in keys]

        def done(got):
            recv.update(zip(keys, got))

        return items, done

    hooks["ssd_scan_bwd", 1] = sender([(n, 1) for n in mats])
    hooks["cm_conv_bwd", 0] = sender([("ssd_w_in", 0), ("ssd_w_out", 0), ("cm_w_out", 0)])

    loss, dx, g = _local_step(x[0], loss_target[0], p, hooks)
    loss = lax.psum(loss, ("x", "y", "c"))

    last_keys = [("cm_w_in", 0)]
    stacked = {k: jnp.stack(g[k]) for k in sn + rn if k != "final_norm_w"}
    stacked["final_norm_w"] = g["final_norm_w"]
    last = _exchange([(blocks(n, j, g), True) for n, j in last_keys]
                     + [(_pack([_split_shards(stacked[k], _SHARD_AXIS[k]) for k in sn], sr, 128, F32, lead=(N_DEV,)), True),
                        (_pack([stacked[k] for k in rn], rr, 128, F32), False)], name="exchange_last")
    recv.update(zip(last_keys, last))

    outs = {}
    for name in mats:
        shp = w[name].shape
        w3, m3, v3 = [d[name].reshape(nl, -1, shp[-1]) for d in (w, m, v)]
        per_layer = []
        for j in range(nl):
            r = recv[name, j]
            if name == "ssd_w_in":
                gt = _reduce_slots(r, rows=r.shape[1], name="sum_ssd_w_in")
                r = jnp.swapaxes(gt[:t_rows], 0, 1)[None]
            per_layer.append(_reduce_adamw(r, w3, m3, v3, layer=j, rows=min(OPT_ROWS, r.shape[1]), name="adamw_" + name))
        for kind, parts in zip(("g", "d", "m", "v"), zip(*per_layer)):
            outs[kind, name] = jnp.stack(parts).reshape(shp)
    for (names, rows), r, nm in ((_SMALL, last[-2], "adamw_small"), (_REP, last[-1], "adamw_rep")):
        res = _reduce_adamw(r, *[_pack([d[k] for k in names], rows, 128, F32)[None] for d in (w, m, v)], layer=0,
                            rows=rows, name=nm)
        for kind, buf in zip(("g", "d", "m", "v"), res):
            for k, a in zip(names, _unpack(buf, [w[k].shape for k in names])):
                outs[kind, k] = a
    return (loss, dx[None], *[outs[kind, k] for kind in ("g", "d", "m", "v") for k in _ORDER])
```
